```python
import jax, jax.numpy as jnp
from jax import lax
import numpy as np

D_MODEL = 1024
BATCH = 2
SEQ = 16384
DEPTH = 1
DEC_BATCH = 8
DEC_SEQ = 16
PAST_LEN = 2048

CHUNK = 64
D_MIX = D_MODEL
A_WIDTH = D_MIX // 2
A_GROUPS = 8
A_GDIM = A_WIDTH // A_GROUPS
MLP_CHUNK = 128
B_WIDTH = D_MIX - A_WIDTH
B_HEADS = 8
B_HDIM = B_WIDTH // B_HEADS
LEFT_CHUNKS = 8
KV_WIN = LEFT_CHUNKS * CHUNK
BAND = KV_WIN + CHUNK
REL_CLIP = 128
N_REL = 2 * REL_CLIP + 1
EPS = 1e-6
D_IN = 3 * A_WIDTH + 4 * B_WIDTH
SPLITS = [A_WIDTH, 2 * A_WIDTH, 3 * A_WIDTH, 3 * A_WIDTH + B_WIDTH,
          3 * A_WIDTH + 2 * B_WIDTH, 3 * A_WIDTH + 3 * B_WIDTH]
NEG = -1e30

kernel_name = "hymba_gmlp_chunkband_stream_step"


def rmsnorm(x, g):
    xf = x.astype(jnp.float32)
    y = xf * lax.rsqrt(jnp.mean(xf * xf, -1, keepdims=True) + EPS)
    return (y * g.astype(jnp.float32)).astype(x.dtype)


def layernorm(x, g, b):
    xf = x.astype(jnp.float32)
    mu = jnp.mean(xf, -1, keepdims=True)
    xc = xf - mu
    var = jnp.mean(xc * xc, -1, keepdims=True)
    y = xc * lax.rsqrt(var + EPS) * g.astype(jnp.float32) + b.astype(jnp.float32)
    return y.astype(x.dtype)


def rel_bias_lookup(rel_bias, dist):
    idx = jnp.clip(dist, -REL_CLIP, REL_CLIP) + REL_CLIP
    return jnp.take(rel_bias, idx, axis=1).astype(jnp.float32)


def mixer_inputs(x, c, g_pre, w_ada, b_ada, w_in, ln_g, ln_b):
    mod = jax.nn.silu(c) @ w_ada + b_ada
    shift, scale, gate = jnp.split(mod[:, None, :], 3, axis=-1)
    h = rmsnorm(x, g_pre) * (1 + scale) + shift
    z = h @ w_in
    uA, vA, gA, q, k, v, gB = jnp.split(z, SPLITS, axis=-1)
    uA = jax.nn.gelu(uA)
    vA = layernorm(jax.nn.gelu(vA), ln_g, ln_b)
    bsz, t = x.shape[0], x.shape[1]
    q = q.reshape(bsz, t, B_HEADS, B_HDIM)
    k = k.reshape(bsz, t, B_HEADS, B_HDIM)
    v = v.reshape(bsz, t, B_HEADS, B_HDIM)
    return gate, uA, vA, gA, q, k, v, gB


def mixer_output(x, gate, yA, gA, yB, gB, w_out, g_post):
    bsz, t = x.shape[0], x.shape[1]
    o = jnp.concatenate([yA * jax.nn.silu(gA),
                         yB.reshape(bsz, t, B_WIDTH) * jax.nn.silu(gB)], axis=-1) @ w_out
    return x + gate * rmsnorm(o, g_post)


def gmlp_prompt(u, v, w_s, b_s):
    bsz, s, _ = u.shape
    n = s // MLP_CHUNK
    vg = v.reshape(bsz, n, MLP_CHUNK, A_GROUPS, A_GDIM)
    mask = jnp.tril(jnp.ones((MLP_CHUNK, MLP_CHUNK), dtype=bool))
    ws = jnp.where(mask[None], w_s, jnp.zeros_like(w_s))
    mixed = jnp.einsum('gts,bnsgd->bntgd', ws, vg) + b_s.T[None, None, :, :, None]
    return u * mixed.reshape(bsz, s, A_WIDTH)


def gmlp_sample(u, v, w_s, b_s):
    bsz, t, _ = u.shape
    vg = v.reshape(bsz, t, A_GROUPS, A_GDIM)
    mask = jnp.tril(jnp.ones((t, t), dtype=bool))
    ws = w_s[:, :t, :t]
    ws = jnp.where(mask[None], ws, jnp.zeros_like(ws))
    mixed = jnp.einsum('gts,bsgd->btgd', ws, vg) + b_s[:, :t].T[None, :, :, None]
    return u * mixed.reshape(bsz, t, A_WIDTH)


def band_attention_prompt(q, k, v, rel_bias):
    bsz, s, nh, dh = q.shape
    n_chunks = s // CHUNK
    pad = ((0, 0), (KV_WIN, 0), (0, 0), (0, 0))
    kp = jnp.pad(k, pad)
    vp = jnp.pad(v, pad)
    kpos = jnp.arange(BAND)
    qpos = jnp.arange(CHUNK) + KV_WIN
    bias = rel_bias_lookup(rel_bias, qpos[:, None] - kpos[None, :])
    scale = dh ** -0.5

    def one_chunk(ci):
        start = ci * CHUNK
        qc = lax.dynamic_slice_in_dim(q, start, CHUNK, axis=1)
        kc = lax.dynamic_slice_in_dim(kp, start, BAND, axis=1)
        vc = lax.dynamic_slice_in_dim(vp, start, BAND, axis=1)
        valid = (kpos + start - KV_WIN) >= 0
        sc = jnp.einsum('bqhd,bkhd->bhqk', qc, kc,
                        preferred_element_type=jnp.float32) * scale + bias[None]
        sc = jnp.where(valid[None, None, None, :], sc, NEG)
        p = jax.nn.softmax(sc, axis=-1).astype(vc.dtype)
        return jnp.einsum('bhqk,bkhd->bqhd', p, vc)

    out = lax.map(one_chunk, jnp.arange(n_chunks))
    return out.transpose(1, 0, 2, 3, 4).reshape(bsz, s, nh, dh)


def band_attention_sample(q, k_new, v_new, k_cache, v_cache, rel_bias):
    w = k_cache.shape[1]
    t = q.shape[1]
    k = jnp.concatenate([k_cache.astype(k_new.dtype), k_new], axis=1)
    v = jnp.concatenate([v_cache.astype(v_new.dtype), v_new], axis=1)
    kpos = jnp.arange(w + t)
    qpos = w + jnp.arange(t)
    bias = rel_bias_lookup(rel_bias, qpos[:, None] - kpos[None, :])
    scale = q.shape[-1] ** -0.5
    sc = jnp.einsum('bqhd,bkhd->bhqk', q, k,
                    preferred_element_type=jnp.float32) * scale + bias[None]
    p = jax.nn.softmax(sc, axis=-1).astype(v.dtype)
    return jnp.einsum('bhqk,bkhd->bqhd', p, v)


def setup_inputs(seed: int = 0) -> dict:
    key = jax.random.key(seed)
    ks = jax.random.split(key, 20)
    f32 = jnp.float32
    win_rows = min(KV_WIN, PAST_LEN)
    nrm = lambda k_, shape: jax.random.normal(k_, shape, f32)
    return {
        "x_prompt": nrm(ks[0], (BATCH, SEQ, D_MODEL)),
        "x_sample": nrm(ks[1], (DEC_BATCH, DEC_SEQ, D_MODEL)),
        "cache_attn_k": nrm(ks[2], (DEPTH, DEC_BATCH, win_rows, B_HEADS, B_HDIM)),
        "cache_attn_v": nrm(ks[3], (DEPTH, DEC_BATCH, win_rows, B_HEADS, B_HDIM)),
        "c_prompt": nrm(ks[4], (BATCH, D_MODEL)),
        "c_sample": nrm(ks[5], (DEC_BATCH, D_MODEL)),
        "g_pre": 1.0 + 0.02 * nrm(ks[6], (DEPTH, D_MODEL)),
        "w_ada": 0.5 * D_MODEL ** -0.5 * nrm(ks[7], (DEPTH, D_MODEL, 3 * D_MODEL)),
        "b_ada": 0.02 * nrm(ks[8], (DEPTH, 3 * D_MODEL)),
        "w_in": D_MODEL ** -0.5 * nrm(ks[9], (DEPTH, D_MODEL, D_IN)),
        "ln_g": 1.0 + 0.02 * nrm(ks[10], (DEPTH, A_WIDTH)),
        "ln_b": 0.02 * nrm(ks[11], (DEPTH, A_WIDTH)),
        "w_s": MLP_CHUNK ** -0.5 * nrm(ks[12], (DEPTH, A_GROUPS, MLP_CHUNK, MLP_CHUNK)),
        "b_s": 1.0 + 0.02 * nrm(ks[13], (DEPTH, A_GROUPS, MLP_CHUNK)),
        "rel_bias": 0.1 * nrm(ks[14], (DEPTH, B_HEADS, N_REL)),
        "w_out": D_MIX ** -0.5 * nrm(ks[15], (DEPTH, D_MIX, D_MODEL)),
        "g_post": 1.0 + 0.02 * nrm(ks[16], (DEPTH, D_MODEL)),
    }


def reference(x_prompt, x_sample, cache_attn_k, cache_attn_v, c_prompt, c_sample,
              g_pre, w_ada, b_ada, w_in, ln_g, ln_b, w_s, b_s, rel_bias, w_out, g_post):
    yp = x_prompt
    ys = x_sample
    kp_rows, vp_rows, ks_rows, vs_rows, va_rows = [], [], [], [], []
    prompt_rows = min(KV_WIN, x_prompt.shape[1])
    for l in range(DEPTH):
        gate, uA, vA, gA, q, k, v, gB = mixer_inputs(yp, c_prompt, g_pre[l], w_ada[l], b_ada[l],
                                                     w_in[l], ln_g[l], ln_b[l])
        yA = gmlp_prompt(uA, vA, w_s[l], b_s[l])
        yB = band_attention_prompt(q, k, v, rel_bias[l])
        kp_rows.append(k[:, -prompt_rows:])
        vp_rows.append(v[:, -prompt_rows:])
        yp = mixer_output(yp, gate, yA, gA, yB, gB, w_out[l], g_post[l])

        gate, uA, vA, gA, q, k, v, gB = mixer_inputs(ys, c_sample, g_pre[l], w_ada[l], b_ada[l],
                                                     w_in[l], ln_g[l], ln_b[l])
        yA = gmlp_sample(uA, vA, w_s[l], b_s[l])
        yB = band_attention_sample(q, k, v, cache_attn_k[l], cache_attn_v[l], rel_bias[l])
        ks_rows.append(k)
        vs_rows.append(v)
        va_rows.append(vA)
        ys = mixer_output(ys, gate, yA, gA, yB, gB, w_out[l], g_post[l])

    new_k_prompt = jnp.stack(kp_rows)
    new_v_prompt = jnp.stack(vp_rows)
    new_k_sample = jnp.stack(ks_rows)
    new_v_sample = jnp.stack(vs_rows)
    new_gmlp_v_sample = jnp.stack(va_rows)
    return (yp, ys, new_k_prompt, new_v_prompt, new_k_sample, new_v_sample, new_gmlp_v_sample)
```

```python
import functools

import jax
import jax.numpy as jnp
from jax import lax
from jax.experimental import pallas as pl
from jax.experimental.pallas import tpu as pltpu

LANES = 128
SUBLANES = 8
BF16_ROWS = 16
VMEM_LIMIT_BYTES = 56 * 1024 * 1024

D_MODEL = 1024
A_WIDTH = 512
A_GROUPS = 8
A_GDIM = A_WIDTH // A_GROUPS
MLP_CHUNK = 128
B_WIDTH = 512
B_HEADS = 8
B_HDIM = B_WIDTH // B_HEADS
CHUNK = 64
KV_WIN = 512
REL_CLIP = 128
N_REL = 2 * REL_CLIP + 1
EPS = 1e-6
D_IN = 3 * A_WIDTH + 4 * B_WIDTH
NEG = -1e30

N_PAIRS = B_HEADS // 2
Q_BLOCK = 2 * CHUNK
K_BLOCK = KV_WIN + Q_BLOCK
SEQ_TILE = 512
ADA_ROWS = 16
ADA_COLS = 512

C_U, C_V, C_GA, C_Q, C_K, C_VV, C_GB = (i * 512 for i in range(7))


def _gelu(x):
    return jax.nn.gelu(x)


def _silu(x):
    return x * jax.nn.sigmoid(x)


def _even_lane_mask(shape):
    lane = lax.broadcasted_iota(jnp.int32, shape, len(shape) - 1)
    return (lane % LANES) < B_HDIM


def _ada_kernel(c_ref, w_ref, b_ref, o_ref):
    c = c_ref[...]
    o_ref[...] = jnp.dot(_silu(c), w_ref[...], preferred_element_type=jnp.float32) + b_ref[...]


def _ada_call(c_all, w_ada, b_ada):
    n = w_ada.shape[1]
    return pl.pallas_call(
        _ada_kernel,
        grid=(n // ADA_COLS,),
        in_specs=[
            pl.BlockSpec((ADA_ROWS, D_MODEL), lambda j: (0, 0)),
            pl.BlockSpec((D_MODEL, ADA_COLS), lambda j: (0, j)),
            pl.BlockSpec((1, ADA_COLS), lambda j: (0, j)),
        ],
        out_specs=pl.BlockSpec((ADA_ROWS, ADA_COLS), lambda j: (0, j)),
        out_shape=jax.ShapeDtypeStruct((ADA_ROWS, n), jnp.float32),
        name="ada",
    )(c_all, w_ada, b_ada.reshape(1, n))


def _relb_kernel(base_ref, o_ref):
    base = base_ref[0]
    x = jnp.broadcast_to(base, (Q_BLOCK, K_BLOCK))
    row = lax.broadcasted_iota(jnp.int32, (Q_BLOCK, K_BLOCK), 0)
    col = lax.broadcasted_iota(jnp.int32, (Q_BLOCK, K_BLOCK), 1)
    shift = 1
    while shift < Q_BLOCK:
        x = jnp.where((row & shift) != 0, jnp.roll(x, shift, axis=1), x)
        shift *= 2
    far = jnp.broadcast_to(base[:, 0:1], (Q_BLOCK, K_BLOCK))
    x = jnp.where(col < row, far, x)
    band_lo = jnp.where(row < CHUNK, 0, CHUNK)
    rel = col - band_lo
    o_ref[0] = jnp.where(jnp.logical_or(rel < 0, rel >= KV_WIN + CHUNK), NEG, x)


def _relb_call(base):
    return pl.pallas_call(
        _relb_kernel,
        grid=(B_HEADS,),
        in_specs=[pl.BlockSpec((1, 1, K_BLOCK), lambda h: (h, 0, 0))],
        out_specs=pl.BlockSpec((1, Q_BLOCK, K_BLOCK), lambda h: (h, 0, 0)),
        out_shape=jax.ShapeDtypeStruct((B_HEADS, Q_BLOCK, K_BLOCK), jnp.float32),
        name="relb",
    )(base.reshape(B_HEADS, 1, K_BLOCK))


def _tril_pairs(ws_ref, wt_ref, n):
    row = lax.broadcasted_iota(jnp.int32, (n, n), 0)
    col = lax.broadcasted_iota(jnp.int32, (n, n), 1)
    keep = col <= row
    for g in range(A_GROUPS):
        w = jnp.where(keep, ws_ref[g, :n, :n], 0.0)
        wt_ref[g // 2, (g % 2) * n:(g % 2 + 1) * n, :] = w.astype(jnp.bfloat16)


def _rms_rows(x):
    return x * lax.rsqrt(jnp.mean(x * x, axis=-1, keepdims=True) + EPS)


def _layernorm(x, g, b):
    mu = jnp.mean(x, axis=-1, keepdims=True)
    xc = x - mu
    var = jnp.mean(xc * xc, axis=-1, keepdims=True)
    return xc * lax.rsqrt(var + EPS) * g + b


def _pair_select(stacked, n):
    return jnp.where(_even_lane_mask((n, LANES)), stacked[:n], stacked[n:])


def _prompt_kernel(x_ref, mod_ref, gpre_ref, win_ref, lng_ref, lnb_ref, ws_ref, bfull_ref,
                   bias_ref, wout_ref, gpost_ref,
                   y_ref, klast_ref, vlast_ref,
                   h_s, usg_s, va_s, qe_s, qo_s, k_s, v_s, sgb_s, o_s, wt_s):
    t = pl.program_id(1)
    nt = pl.num_programs(1)
    T = SEQ_TILE

    @pl.when(jnp.logical_and(pl.program_id(0) == 0, t == 0))
    def _():
        _tril_pairs(ws_ref, wt_s, MLP_CHUNK)

    @pl.when(t == 0)
    def _():
        k_s[0:KV_WIN, :] = jnp.zeros((KV_WIN, B_WIDTH), jnp.bfloat16)
        v_s[0:KV_WIN, :] = jnp.zeros((KV_WIN, B_WIDTH), jnp.bfloat16)

    shift = mod_ref[0, 0:1, :]
    scale = mod_ref[0, 1:2, :]
    gate = mod_ref[0, 2:3, :]

    x = x_ref[0]
    h = _rms_rows(x) * (gpre_ref[...] * (1.0 + scale)) + shift
    h_s[...] = h.astype(jnp.bfloat16)

    def proj(c0):
        return jnp.dot(h_s[...], win_ref[:, c0:c0 + 512], preferred_element_type=jnp.float32)

    usg_s[...] = _gelu(proj(C_U))
    va_s[...] = _layernorm(_gelu(proj(C_V)), lng_ref[...], lnb_ref[...]).astype(jnp.bfloat16)
    usg_s[...] = usg_s[...] * _silu(proj(C_GA))
    q = (proj(C_Q) * (B_HDIM ** -0.5)).astype(jnp.bfloat16)
    even = _even_lane_mask((T, B_WIDTH))
    zero = jnp.zeros((), jnp.bfloat16)
    qe_s[...] = jnp.where(even, q, zero)
    qo_s[...] = jnp.where(even, zero, q)
    k = proj(C_K)
    v = proj(C_VV)
    k_s[KV_WIN:KV_WIN + T, :] = k.astype(jnp.bfloat16)
    v_s[KV_WIN:KV_WIN + T, :] = v.astype(jnp.bfloat16)

    @pl.when(t == nt - 1)
    def _():
        klast_ref[0] = k[T - KV_WIN:, :]
        vlast_ref[0] = v[T - KV_WIN:, :]

    sgb_s[...] = _silu(proj(C_GB))

    for c in range(T // MLP_CHUNK):
        rows = slice(c * MLP_CHUNK, (c + 1) * MLP_CHUNK)
        for p in range(N_PAIRS):
            lanes = slice(p * LANES, (p + 1) * LANES)
            mix = jnp.dot(wt_s[p], va_s[rows, lanes], preferred_element_type=jnp.float32)
            mixed = _pair_select(mix, MLP_CHUNK) + bfull_ref[:, lanes]
            o_s[rows, lanes] = (usg_s[rows, lanes] * mixed).astype(jnp.bfloat16)

    first_tile = t == 0

    def q_block(qb, carry):
        r0 = pl.multiple_of(qb * Q_BLOCK, Q_BLOCK)
        n_invalid = jnp.where(first_tile, KV_WIN - r0, 0)
        col = lax.broadcasted_iota(jnp.int32, (2 * Q_BLOCK, K_BLOCK), 1)
        invalid = col < n_invalid
        for p in range(N_PAIRS):
            lanes = slice(p * LANES, (p + 1) * LANES)
            qs = jnp.concatenate([qe_s[pl.ds(r0, Q_BLOCK), lanes],
                                  qo_s[pl.ds(r0, Q_BLOCK), lanes]], axis=0)
            k2 = k_s[pl.ds(r0, K_BLOCK), lanes]
            v2 = v_s[pl.ds(r0, K_BLOCK), lanes]
            s = lax.dot_general(qs, k2, (((1,), (1,)), ((), ())),
                                preferred_element_type=jnp.float32)
            s = jnp.where(invalid, NEG, s + bias_ref[p])
            m = jnp.max(s, axis=-1, keepdims=True)
            e = jnp.exp(s - m)
            l = jnp.sum(e, axis=-1, keepdims=True)
            o2 = jnp.dot(e.astype(jnp.bfloat16), v2, preferred_element_type=jnp.float32)
            o2 = o2 * (1.0 / l)
            yb = _pair_select(o2, Q_BLOCK) * sgb_s[pl.ds(r0, Q_BLOCK), lanes]
            o_s[pl.ds(r0, Q_BLOCK), B_WIDTH + p * LANES:B_WIDTH + (p + 1) * LANES] = (
                yb.astype(jnp.bfloat16))
        return carry

    lax.fori_loop(0, T // Q_BLOCK, q_block, 0)

    k_s[0:KV_WIN, :] = k_s[T:T + KV_WIN, :]
    v_s[0:KV_WIN, :] = v_s[T:T + KV_WIN, :]

    o = jnp.dot(o_s[...], wout_ref[...], preferred_element_type=jnp.float32)
    y_ref[0] = x_ref[0] + gate * (_rms_rows(o) * gpost_ref[...])


def _const_spec(shape):
    nd = len(shape)
    return pl.BlockSpec(shape, lambda b, t: (0,) * nd, pipeline_mode=pl.Buffered(1))


def _prompt_call(x, mod, g_pre, w_in, ln_g, ln_b, w_s, b_full, bias, w_out, g_post):
    bsz, seq, _ = x.shape
    T = SEQ_TILE
    assert seq % T == 0 and T >= KV_WIN
    nt = seq // T
    f32, bf16 = jnp.float32, jnp.bfloat16
    return pl.pallas_call(
        _prompt_kernel,
        grid=(bsz, nt),
        in_specs=[
            pl.BlockSpec((1, T, D_MODEL), lambda b, t: (b, t, 0)),
            pl.BlockSpec((1, 3, D_MODEL), lambda b, t: (b, 0, 0)),
            _const_spec((1, D_MODEL)),
            _const_spec((D_MODEL, D_IN)),
            _const_spec((1, A_WIDTH)),
            _const_spec((1, A_WIDTH)),
            _const_spec((A_GROUPS, MLP_CHUNK, MLP_CHUNK)),
            _const_spec((MLP_CHUNK, A_WIDTH)),
            _const_spec((N_PAIRS, 2 * Q_BLOCK, K_BLOCK)),
            _const_spec((D_MODEL, D_MODEL)),
            _const_spec((1, D_MODEL)),
        ],
        out_specs=[
            pl.BlockSpec((1, T, D_MODEL), lambda b, t: (b, t, 0)),
            pl.BlockSpec((1, KV_WIN, B_WIDTH), lambda b, t: (b, 0, 0)),
            pl.BlockSpec((1, KV_WIN, B_WIDTH), lambda b, t: (b, 0, 0)),
        ],
        out_shape=[
            jax.ShapeDtypeStruct((bsz, seq, D_MODEL), f32),
            jax.ShapeDtypeStruct((bsz, KV_WIN, B_WIDTH), f32),
            jax.ShapeDtypeStruct((bsz, KV_WIN, B_WIDTH), f32),
        ],
        scratch_shapes=[
            pltpu.VMEM((T, D_MODEL), bf16),
            pltpu.VMEM((T, A_WIDTH), f32),
            pltpu.VMEM((T, A_WIDTH), bf16),
            pltpu.VMEM((T, B_WIDTH), bf16),
            pltpu.VMEM((T, B_WIDTH), bf16),
            pltpu.VMEM((KV_WIN + T, B_WIDTH), bf16),
            pltpu.VMEM((KV_WIN + T, B_WIDTH), bf16),
            pltpu.VMEM((T, B_WIDTH), f32),
            pltpu.VMEM((T, D_MODEL), bf16),
            pltpu.VMEM((N_PAIRS, 2 * MLP_CHUNK, MLP_CHUNK), bf16),
        ],
        compiler_params=pltpu.CompilerParams(
            dimension_semantics=("arbitrary", "arbitrary"),
            vmem_limit_bytes=VMEM_LIMIT_BYTES),
        name="prompt",
    )(x, mod, g_pre, w_in, ln_g, ln_b, w_s, b_full, bias, w_out, g_post)


def _sample_kernel(x_ref, mod_ref, gpre_ref, win_ref, lng_ref, lnb_ref, ws_ref, bfull_ref,
                   bias_ref, wout_ref, gpost_ref, ck_ref, cv_ref,
                   y_ref, knew_ref, vnew_ref, vanew_ref,
                   usg_s, va_s, qe_s, qo_s, k_s, v_s, sgb_s, o_s, wt_s, *, n_streams, n_new):
    b = pl.program_id(0)
    rows_all = n_streams * n_new
    S = n_new

    @pl.when(b == 0)
    def _():
        _tril_pairs(ws_ref, wt_s, S)
        gpre = gpre_ref[...]
        hs = []
        for i in range(n_streams):
            xi = x_ref[i * S:(i + 1) * S, :]
            hs.append(_rms_rows(xi) * (gpre * (1.0 + mod_ref[i, 1:2, :])) + mod_ref[i, 0:1, :])
        h = jnp.concatenate(hs, axis=0).astype(jnp.bfloat16)

        def proj(c0):
            return jnp.dot(h, win_ref[:, c0:c0 + 512], preferred_element_type=jnp.float32)

        va = _layernorm(_gelu(proj(C_V)), lng_ref[...], lnb_ref[...])
        vanew_ref[...] = va
        va_s[...] = va.astype(jnp.bfloat16)
        usg_s[...] = _gelu(proj(C_U)) * _silu(proj(C_GA))
        q = (proj(C_Q) * (B_HDIM ** -0.5)).astype(jnp.bfloat16)
        even = _even_lane_mask((rows_all, B_WIDTH))
        zero = jnp.zeros((), jnp.bfloat16)
        qe_s[...] = jnp.where(even, q, zero)
        qo_s[...] = jnp.where(even, zero, q)
        k = proj(C_K)
        v = proj(C_VV)
        knew_ref[...] = k
        vnew_ref[...] = v
        k_s[...] = k.astype(jnp.bfloat16)
        v_s[...] = v.astype(jnp.bfloat16)
        sgb_s[...] = _silu(proj(C_GB))

    r0 = pl.multiple_of(b * S, S)
    rows = pl.ds(r0, S)
    for p in range(N_PAIRS):
        lanes = slice(p * LANES, (p + 1) * LANES)
        mix = jnp.dot(wt_s[p], va_s[rows, lanes], preferred_element_type=jnp.float32)
        mixed = _pair_select(mix, S) + bfull_ref[0:S, lanes]
        o_s[rows, lanes] = (usg_s[rows, lanes] * mixed).astype(jnp.bfloat16)

        qs = jnp.concatenate([qe_s[rows, lanes], qo_s[rows, lanes]], axis=0)
        kc = ck_ref[0, :, lanes].astype(jnp.bfloat16)
        vc = cv_ref[0, :, lanes].astype(jnp.bfloat16)
        kn = k_s[rows, lanes]
        vn = v_s[rows, lanes]
        dims = (((1,), (1,)), ((), ()))
        s1 = lax.dot_general(qs, kc, dims, preferred_element_type=jnp.float32)
        s2 = lax.dot_general(qs, kn, dims, preferred_element_type=jnp.float32)
        bias = jnp.concatenate([bias_ref[p, 0:S, :], bias_ref[p, Q_BLOCK:Q_BLOCK + S, :]], axis=0)
        s1 = s1 + bias[:, 0:KV_WIN]
        s2 = s2 + bias[:, KV_WIN:KV_WIN + S]
        m = jnp.maximum(jnp.max(s1, axis=-1, keepdims=True), jnp.max(s2, axis=-1, keepdims=True))
        e1 = jnp.exp(s1 - m)
        e2 = jnp.exp(s2 - m)
        l = jnp.sum(e1, axis=-1, keepdims=True) + jnp.sum(e2, axis=-1, keepdims=True)
        o2 = (jnp.dot(e1.astype(jnp.bfloat16), vc, preferred_element_type=jnp.float32)
              + jnp.dot(e2.astype(jnp.bfloat16), vn, preferred_element_type=jnp.float32))
        o2 = o2 * (1.0 / l)
        yb = _pair_select(o2, S) * sgb_s[rows, lanes]
        o_s[rows, B_WIDTH + p * LANES:B_WIDTH + (p + 1) * LANES] = yb.astype(jnp.bfloat16)

    @pl.when(b == n_streams - 1)
    def _():
        o = jnp.dot(o_s[...], wout_ref[...], preferred_element_type=jnp.float32)
        on = _rms_rows(o) * gpost_ref[...]
        for i in range(n_streams):
            sl = slice(i * S, (i + 1) * S)
            y_ref[sl, :] = x_ref[sl, :] + mod_ref[i, 2:3, :] * on[sl, :]


def _sample_call(x, mod, g_pre, w_in, ln_g, ln_b, w_s, b_full, bias, w_out, g_post, ck, cv):
    n_streams, n_new, _ = x.shape
    assert n_new % BF16_ROWS == 0 and n_new <= CHUNK
    rows = n_streams * n_new
    f32, bf16 = jnp.float32, jnp.bfloat16

    def const(shape):
        nd = len(shape)
        return pl.BlockSpec(shape, lambda b: (0,) * nd, pipeline_mode=pl.Buffered(1))

    kern = functools.partial(_sample_kernel, n_streams=n_streams, n_new=n_new)
    return pl.pallas_call(
        kern,
        grid=(n_streams,),
        in_specs=[
            const((rows, D_MODEL)),
            const((n_streams, 3, D_MODEL)),
            const((1, D_MODEL)),
            const((D_MODEL, D_IN)),
            const((1, A_WIDTH)),
            const((1, A_WIDTH)),
            const((A_GROUPS, MLP_CHUNK, MLP_CHUNK)),
            const((MLP_CHUNK, A_WIDTH)),
            const((N_PAIRS, 2 * Q_BLOCK, K_BLOCK)),
            const((D_MODEL, D_MODEL)),
            const((1, D_MODEL)),
            pl.BlockSpec((1, KV_WIN, B_WIDTH), lambda b: (b, 0, 0)),
            pl.BlockSpec((1, KV_WIN, B_WIDTH), lambda b: (b, 0, 0)),
        ],
        out_specs=[
            pl.BlockSpec((rows, D_MODEL), lambda b: (0, 0)),
            pl.BlockSpec((rows, B_WIDTH), lambda b: (0, 0)),
            pl.BlockSpec((rows, B_WIDTH), lambda b: (0, 0)),
            pl.BlockSpec((rows, A_WIDTH), lambda b: (0, 0)),
        ],
        out_shape=[
            jax.ShapeDtypeStruct((rows, D_MODEL), f32),
            jax.ShapeDtypeStruct((rows, B_WIDTH), f32),
            jax.ShapeDtypeStruct((rows, B_WIDTH), f32),
            jax.ShapeDtypeStruct((rows, A_WIDTH), f32),
        ],
        scratch_shapes=[
            pltpu.VMEM((rows, A_WIDTH), f32),
            pltpu.VMEM((rows, A_WIDTH), bf16),
            pltpu.VMEM((rows, B_WIDTH), bf16),
            pltpu.VMEM((rows, B_WIDTH), bf16),
            pltpu.VMEM((rows, B_WIDTH), bf16),
            pltpu.VMEM((rows, B_WIDTH), bf16),
            pltpu.VMEM((rows, B_WIDTH), f32),
            pltpu.VMEM((rows, D_MODEL), bf16),
            pltpu.VMEM((N_PAIRS, 2 * n_new, n_new), bf16),
        ],
        compiler_params=pltpu.CompilerParams(
            dimension_semantics=("arbitrary",),
            vmem_limit_bytes=VMEM_LIMIT_BYTES),
        name="sample",
    )(x.reshape(rows, D_MODEL), mod, g_pre, w_in, ln_g, ln_b, w_s, b_full, bias, w_out, g_post,
      ck, cv)


def _bias_base(rel_bias):
    n_far = KV_WIN - REL_CLIP
    far = jnp.broadcast_to(rel_bias[:, N_REL - 1:N_REL], (B_HEADS, n_far))
    near = rel_bias[:, N_REL - 1:0:-1]
    return jnp.concatenate([far, near], axis=1)


def kernel(x_prompt, x_sample, cache_attn_k, cache_attn_v, c_prompt, c_sample, g_pre, w_ada, b_ada,
           w_in, ln_g, ln_b, w_s, b_s, rel_bias, w_out, g_post):
    depth = g_pre.shape[0]
    bsz, seq, _ = x_prompt.shape
    n_streams, n_new, _ = x_sample.shape
    win = cache_attn_k.shape[2]
    assert win == KV_WIN and bsz + n_streams <= ADA_ROWS

    yp, ys = x_prompt, x_sample
    kp_rows, vp_rows, ks_rows, vs_rows, va_rows = [], [], [], [], []
    c_all = jnp.concatenate(
        [c_prompt, c_sample, jnp.zeros((ADA_ROWS - bsz - n_streams, D_MODEL), c_prompt.dtype)], axis=0)
    for l in range(depth):
        mod = _ada_call(c_all, w_ada[l], b_ada[l]).reshape(ADA_ROWS, 3, D_MODEL)
        bias = _relb_call(_bias_base(rel_bias[l])).reshape(N_PAIRS, 2 * Q_BLOCK, K_BLOCK)
        w_in_b = w_in[l].astype(jnp.bfloat16)
        w_out_b = w_out[l].astype(jnp.bfloat16)
        b_full = jnp.repeat(b_s[l].T, A_GDIM, axis=1)
        shared = (g_pre[l].reshape(1, D_MODEL), w_in_b, ln_g[l].reshape(1, A_WIDTH),
                  ln_b[l].reshape(1, A_WIDTH), w_s[l], b_full, bias, w_out_b,
                  g_post[l].reshape(1, D_MODEL))

        yp, k_last, v_last = _prompt_call(yp, mod[:bsz], *shared)
        kp_rows.append(k_last.reshape(bsz, KV_WIN, B_HEADS, B_HDIM))
        vp_rows.append(v_last.reshape(bsz, KV_WIN, B_HEADS, B_HDIM))

        ck = cache_attn_k[l].reshape(n_streams, KV_WIN, B_WIDTH)
        cv = cache_attn_v[l].reshape(n_streams, KV_WIN, B_WIDTH)
        ys2, k_new, v_new, va_new = _sample_call(ys, mod[bsz:bsz + n_streams], *shared, ck, cv)
        ys = ys2.reshape(n_streams, n_new, D_MODEL)
        ks_rows.append(k_new.reshape(n_streams, n_new, B_HEADS, B_HDIM))
        vs_rows.append(v_new.reshape(n_streams, n_new, B_HEADS, B_HDIM))
        va_rows.append(va_new.reshape(n_streams, n_new, A_WIDTH))

    return (yp, ys, jnp.stack(kp_rows), jnp.stack(vp_rows), jnp.stack(ks_rows), jnp.stack(vs_rows),
            jnp.stack(va_rows))
```

```python
import functools

import jax
import jax.numpy as jnp
from jax import lax
from jax.experimental import pallas as pl
from jax.experimental.pallas import tpu as pltpu

LANES = 128
SUBLANES = 8
BF16_ROWS = 16
VMEM_LIMIT_BYTES = 56 * 1024 * 1024

D_MODEL = 1024
A_WIDTH = 512
A_GROUPS = 8
A_GDIM = A_WIDTH // A_GROUPS
MLP_CHUNK = 128
B_WIDTH = 512
B_HEADS = 8
B_HDIM = B_WIDTH // B_HEADS
CHUNK = 64
KV_WIN = 512
REL_CLIP = 128
N_REL = 2 * REL_CLIP + 1
EPS = 1e-6
D_IN = 3 * A_WIDTH + 4 * B_WIDTH
NEG = -1e30
LOG2E = 1.4426950408889634
Q_SCALE = B_HDIM ** -0.5 * LOG2E

N_PAIRS = B_HEADS // 2
Q_BLOCK = 2 * CHUNK
K_BLOCK = KV_WIN + Q_BLOCK
SEQ_TILE = 512
ADA_ROWS = 16
ADA_COLS = 512

C_U, C_V, C_GA, C_Q, C_K, C_VV, C_GB = (i * 512 for i in range(7))

_CONTRACT_LANES = (((1,), (1,)), ((), ()))


def _gelu(x):
    return jax.nn.gelu(x)


def _silu(x):
    return x * jax.nn.sigmoid(x)


def _even_lane_mask(shape):
    lane = lax.broadcasted_iota(jnp.int32, shape, len(shape) - 1)
    return (lane % LANES) < B_HDIM


def _ada_kernel(c_ref, w_ref, b_ref, o_ref):
    c = c_ref[...]
    o_ref[...] = jnp.dot(_silu(c), w_ref[...], preferred_element_type=jnp.float32) + b_ref[...]


def _ada_call(c_all, w_ada, b_ada):
    n = w_ada.shape[1]
    return pl.pallas_call(
        _ada_kernel,
        grid=(n // ADA_COLS,),
        in_specs=[
            pl.BlockSpec((ADA_ROWS, D_MODEL), lambda j: (0, 0)),
            pl.BlockSpec((D_MODEL, ADA_COLS), lambda j: (0, j)),
            pl.BlockSpec((1, ADA_COLS), lambda j: (0, j)),
        ],
        out_specs=pl.BlockSpec((ADA_ROWS, ADA_COLS), lambda j: (0, j)),
        out_shape=jax.ShapeDtypeStruct((ADA_ROWS, n), jnp.float32),
        name="ada",
    )(c_all, w_ada, b_ada.reshape(1, n))


def _relb_kernel(base_ref, bt_ref, bs_ref, *, n_new):
    row = lax.broadcasted_iota(jnp.int32, (Q_BLOCK, K_BLOCK), 0)
    col = lax.broadcasted_iota(jnp.int32, (Q_BLOCK, K_BLOCK), 1)
    band_lo = jnp.where(row < CHUNK, 0, CHUNK)
    rel = col - band_lo
    outside = jnp.logical_or(rel < 0, rel >= KV_WIN + CHUNK)
    for par in range(2):
        base = base_ref[0, par:par + 1, :] * LOG2E
        x = jnp.broadcast_to(base, (Q_BLOCK, K_BLOCK))
        shift = 1
        while shift < Q_BLOCK:
            x = jnp.where((row & shift) != 0, jnp.roll(x, shift, axis=1), x)
            shift *= 2
        far = jnp.broadcast_to(base[:, 0:1], (Q_BLOCK, K_BLOCK))
        x = jnp.where(col < row, far, x)
        bs_ref[0, par] = x[0:n_new, :]
        bt_ref[0, :, par * Q_BLOCK:(par + 1) * Q_BLOCK] = jnp.where(outside, NEG, x).T


def _relb_call(base, n_new):
    return pl.pallas_call(
        functools.partial(_relb_kernel, n_new=n_new),
        grid=(N_PAIRS,),
        in_specs=[pl.BlockSpec((1, 2, K_BLOCK), lambda p: (p, 0, 0))],
        out_specs=[pl.BlockSpec((1, K_BLOCK, 2 * Q_BLOCK), lambda p: (p, 0, 0)),
                   pl.BlockSpec((1, 2, n_new, K_BLOCK), lambda p: (p, 0, 0, 0))],
        out_shape=[jax.ShapeDtypeStruct((N_PAIRS, K_BLOCK, 2 * Q_BLOCK), jnp.float32),
                   jax.ShapeDtypeStruct((N_PAIRS, 2, n_new, K_BLOCK), jnp.float32)],
        name="relb",
    )(base.reshape(N_PAIRS, 2, K_BLOCK))


def _tril_pairs(ws_ref, wt_ref, n):
    row = lax.broadcasted_iota(jnp.int32, (n, n), 0)
    col = lax.broadcasted_iota(jnp.int32, (n, n), 1)
    keep = col <= row
    for g in range(A_GROUPS):
        w = jnp.where(keep, ws_ref[g, :n, :n], 0.0)
        wt_ref[g // 2, (g % 2) * n:(g % 2 + 1) * n, :] = w.astype(jnp.bfloat16)


def _rms_rows(x):
    return x * lax.rsqrt(jnp.mean(x * x, axis=-1, keepdims=True) + EPS)


def _layernorm(x, g, b):
    mu = jnp.mean(x, axis=-1, keepdims=True)
    xc = x - mu
    var = jnp.mean(xc * xc, axis=-1, keepdims=True)
    return xc * lax.rsqrt(var + EPS) * g + b


def _pair_select(stacked, n):
    return jnp.where(_even_lane_mask((n, LANES)), stacked[:n], stacked[n:])


def _store_q(q, qe_ref, qo_ref):
    q = (q * Q_SCALE).astype(jnp.bfloat16)
    even = _even_lane_mask(q.shape)
    zero = jnp.zeros((), jnp.bfloat16)
    qe_ref[...] = jnp.where(even, q, zero)
    qo_ref[...] = jnp.where(even, zero, q)


def _prompt_kernel(x_ref, mod_ref, gpre_ref, win_ref, lng_ref, lnb_ref, ws_ref, bfull_ref,
                   biast_ref, wout_ref, gpost_ref,
                   y_ref, klast_ref, vlast_ref,
                   h_s, usg_s, va_s, qe_s, qo_s, k_s, vt_s, sgb_s, o_s, wt_s, st_s):
    t = pl.program_id(1)
    nt = pl.num_programs(1)
    T = SEQ_TILE

    @pl.when(jnp.logical_and(pl.program_id(0) == 0, t == 0))
    def _():
        _tril_pairs(ws_ref, wt_s, MLP_CHUNK)

    shift = mod_ref[0, 0:1, :]
    scale = mod_ref[0, 1:2, :]
    gate = mod_ref[0, 2:3, :]

    x = x_ref[0]
    h = _rms_rows(x) * (gpre_ref[...] * (1.0 + scale)) + shift
    h_s[...] = h.astype(jnp.bfloat16)

    def proj(c0):
        return jnp.dot(h_s[...], win_ref[:, c0:c0 + 512], preferred_element_type=jnp.float32)

    usg_s[...] = _gelu(proj(C_U))
    va_s[...] = _layernorm(_gelu(proj(C_V)), lng_ref[...], lnb_ref[...]).astype(jnp.bfloat16)
    usg_s[...] = usg_s[...] * _silu(proj(C_GA))
    _store_q(proj(C_Q), qe_s, qo_s)
    k = proj(C_K)
    v = proj(C_VV)
    k_s[KV_WIN:KV_WIN + T, :] = k.astype(jnp.bfloat16)
    vt_s[:, KV_WIN:KV_WIN + T] = v.T.astype(jnp.bfloat16)

    @pl.when(t == nt - 1)
    def _():
        klast_ref[0] = k[T - KV_WIN:, :]
        vlast_ref[0] = v[T - KV_WIN:, :]

    sgb_s[...] = _silu(proj(C_GB))

    for c in range(T // MLP_CHUNK):
        rows = slice(c * MLP_CHUNK, (c + 1) * MLP_CHUNK)
        for p in range(N_PAIRS):
            lanes = slice(p * LANES, (p + 1) * LANES)
            mix = jnp.dot(wt_s[p], va_s[rows, lanes], preferred_element_type=jnp.float32)
            mixed = _pair_select(mix, MLP_CHUNK) + bfull_ref[:, lanes]
            o_s[rows, lanes] = (usg_s[rows, lanes] * mixed).astype(jnp.bfloat16)

    def key_lo(qb, first_tile):
        return KV_WIN - qb * Q_BLOCK if first_tile else 0

    def scores(qb, p, slot, first_tile):
        r0, lo = qb * Q_BLOCK, key_lo(qb, first_tile)
        lanes = slice(p * LANES, (p + 1) * LANES)
        qs = jnp.concatenate([qe_s[r0:r0 + Q_BLOCK, lanes],
                              qo_s[r0:r0 + Q_BLOCK, lanes]], axis=0)
        half = (K_BLOCK - lo) // 2
        for a in (lo, lo + half):
            st_s[slot, a:a + half, :] = lax.dot_general(
                k_s[r0 + a:r0 + a + half, lanes], qs, _CONTRACT_LANES,
                preferred_element_type=jnp.float32) + biast_ref[p, a:a + half, :]

    def finish(qb, p, slot, first_tile):
        r0, lo = qb * Q_BLOCK, key_lo(qb, first_tile)
        st = st_s[slot, lo:K_BLOCK, :]
        m = jnp.max(st, axis=0, keepdims=True)
        e = jnp.exp2(st - m)
        inv_l = 1.0 / jnp.sum(e, axis=0, keepdims=True)
        pt = e.astype(jnp.bfloat16)
        ots = []
        for par in range(2):
            feats = slice(p * LANES + par * B_HDIM, p * LANES + (par + 1) * B_HDIM)
            qcols = slice(par * Q_BLOCK, (par + 1) * Q_BLOCK)
            ot = jnp.dot(vt_s[feats, r0 + lo:r0 + K_BLOCK], pt[:, qcols],
                         preferred_element_type=jnp.float32)
            ots.append(ot * inv_l[:, qcols])
        yb = jnp.concatenate(ots, axis=0).T * sgb_s[r0:r0 + Q_BLOCK, p * LANES:(p + 1) * LANES]
        o_s[r0:r0 + Q_BLOCK, B_WIDTH + p * LANES:B_WIDTH + (p + 1) * LANES] = (
            yb.astype(jnp.bfloat16))

    def attend(first_tile):
        items = [(qb, p) for qb in range(T // Q_BLOCK) for p in range(N_PAIRS)]
        scores(*items[0], 0, first_tile)
        for i, (qb, p) in enumerate(items):
            if i + 1 < len(items):
                scores(*items[i + 1], (i + 1) % 2, first_tile)
            finish(qb, p, i % 2, first_tile)

    @pl.when(t == 0)
    def _():
        attend(True)

    @pl.when(t > 0)
    def _():
        attend(False)

    k_s[0:KV_WIN, :] = k_s[T:T + KV_WIN, :]
    vt_s[:, 0:KV_WIN] = vt_s[:, T:T + KV_WIN]

    o = jnp.dot(o_s[...], wout_ref[...], preferred_element_type=jnp.float32)
    y_ref[0] = x_ref[0] + gate * (_rms_rows(o) * gpost_ref[...])


def _const_spec(shape):
    nd = len(shape)
    return pl.BlockSpec(shape, lambda b, t: (0,) * nd, pipeline_mode=pl.Buffered(1))


def _prompt_call(x, mod, g_pre, w_in, ln_g, ln_b, w_s, b_full, bias_t, w_out, g_post):
    bsz, seq, _ = x.shape
    T = SEQ_TILE
    assert seq % T == 0 and T >= KV_WIN
    nt = seq // T
    f32, bf16 = jnp.float32, jnp.bfloat16
    return pl.pallas_call(
        _prompt_kernel,
        grid=(bsz, nt),
        in_specs=[
            pl.BlockSpec((1, T, D_MODEL), lambda b, t: (b, t, 0)),
            pl.BlockSpec((1, 3, D_MODEL), lambda b, t: (b, 0, 0)),
            _const_spec((1, D_MODEL)),
            _const_spec((D_MODEL, D_IN)),
            _const_spec((1, A_WIDTH)),
            _const_spec((1, A_WIDTH)),
            _const_spec((A_GROUPS, MLP_CHUNK, MLP_CHUNK)),
            _const_spec((MLP_CHUNK, A_WIDTH)),
            _const_spec((N_PAIRS, K_BLOCK, 2 * Q_BLOCK)),
            _const_spec((D_MODEL, D_MODEL)),
            _const_spec((1, D_MODEL)),
        ],
        out_specs=[
            pl.BlockSpec((1, T, D_MODEL), lambda b, t: (b, t, 0)),
            pl.BlockSpec((1, KV_WIN, B_WIDTH), lambda b, t: (b, 0, 0)),
            pl.BlockSpec((1, KV_WIN, B_WIDTH), lambda b, t: (b, 0, 0)),
        ],
        out_shape=[
            jax.ShapeDtypeStruct((bsz, seq, D_MODEL), f32),
            jax.ShapeDtypeStruct((bsz, KV_WIN, B_WIDTH), f32),
            jax.ShapeDtypeStruct((bsz, KV_WIN, B_WIDTH), f32),
        ],
        scratch_shapes=[
            pltpu.VMEM((T, D_MODEL), bf16),
            pltpu.VMEM((T, A_WIDTH), f32),
            pltpu.VMEM((T, A_WIDTH), bf16),
            pltpu.VMEM((T, B_WIDTH), bf16),
            pltpu.VMEM((T, B_WIDTH), bf16),
            pltpu.VMEM((KV_WIN + T, B_WIDTH), bf16),
            pltpu.VMEM((B_WIDTH, KV_WIN + T), bf16),
            pltpu.VMEM((T, B_WIDTH), f32),
            pltpu.VMEM((T, D_MODEL), bf16),
            pltpu.VMEM((N_PAIRS, 2 * MLP_CHUNK, MLP_CHUNK), bf16),
            pltpu.VMEM((2, K_BLOCK, 2 * Q_BLOCK), f32),
        ],
        compiler_params=pltpu.CompilerParams(
            dimension_semantics=("arbitrary", "arbitrary"),
            vmem_limit_bytes=VMEM_LIMIT_BYTES),
        name="prompt",
    )(x, mod, g_pre, w_in, ln_g, ln_b, w_s, b_full, bias_t, w_out, g_post)


def _sample_kernel(x_ref, mod_ref, gpre_ref, win_ref, lng_ref, lnb_ref, ws_ref, bfull_ref,
                   bias_ref, wout_ref, gpost_ref, ck_ref, cv_ref,
                   y_ref, knew_ref, vnew_ref, vanew_ref,
                   usg_s, va_s, qe_s, qo_s, k_s, v_s, sgb_s, o_s, wt_s, *, n_streams, n_new):
    b = pl.program_id(0)
    S = n_new

    @pl.when(b == 0)
    def _():
        _tril_pairs(ws_ref, wt_s, S)
        gpre = gpre_ref[...]
        hs = []
        for i in range(n_streams):
            xi = x_ref[i * S:(i + 1) * S, :]
            hs.append(_rms_rows(xi) * (gpre * (1.0 + mod_ref[i, 1:2, :])) + mod_ref[i, 0:1, :])
        h = jnp.concatenate(hs, axis=0).astype(jnp.bfloat16)

        def proj(c0):
            return jnp.dot(h, win_ref[:, c0:c0 + 512], preferred_element_type=jnp.float32)

        va = _layernorm(_gelu(proj(C_V)), lng_ref[...], lnb_ref[...])
        vanew_ref[...] = va
        va_s[...] = va.astype(jnp.bfloat16)
        usg_s[...] = _gelu(proj(C_U)) * _silu(proj(C_GA))
        _store_q(proj(C_Q), qe_s, qo_s)
        k = proj(C_K)
        v = proj(C_VV)
        knew_ref[...] = k
        vnew_ref[...] = v
        k_s[...] = k.astype(jnp.bfloat16)
        v_s[...] = v.astype(jnp.bfloat16)
        sgb_s[...] = _silu(proj(C_GB))

    r0 = pl.multiple_of(b * S, S)
    rows = pl.ds(r0, S)
    for p in range(N_PAIRS):
        lanes = slice(p * LANES, (p + 1) * LANES)
        mix = jnp.dot(wt_s[p], va_s[rows, lanes], preferred_element_type=jnp.float32)
        mixed = _pair_select(mix, S) + bfull_ref[0:S, lanes]
        o_s[rows, lanes] = (usg_s[rows, lanes] * mixed).astype(jnp.bfloat16)

        qs = jnp.concatenate([qe_s[rows, lanes], qo_s[rows, lanes]], axis=0)
        kc = ck_ref[0, :, lanes].astype(jnp.bfloat16)
        vc = cv_ref[0, :, lanes].astype(jnp.bfloat16)
        kn = k_s[rows, lanes]
        vn = v_s[rows, lanes]
        s1 = lax.dot_general(qs, kc, _CONTRACT_LANES, preferred_element_type=jnp.float32)
        s2 = lax.dot_general(qs, kn, _CONTRACT_LANES, preferred_element_type=jnp.float32)
        bias = jnp.concatenate([bias_ref[p, 0], bias_ref[p, 1]], axis=0)
        s1 = s1 + bias[:, 0:KV_WIN]
        s2 = s2 + bias[:, KV_WIN:KV_WIN + S]
        m = jnp.maximum(jnp.max(s1, axis=-1, keepdims=True), jnp.max(s2, axis=-1, keepdims=True))
        e1 = jnp.exp2(s1 - m)
        e2 = jnp.exp2(s2 - m)
        l = jnp.sum(e1, axis=-1, keepdims=True) + jnp.sum(e2, axis=-1, keepdims=True)
        o2 = (jnp.dot(e1.astype(jnp.bfloat16), vc, preferred_element_type=jnp.float32)
              + jnp.dot(e2.astype(jnp.bfloat16), vn, preferred_element_type=jnp.float32))
        o2 = o2 * (1.0 / l)
        yb = _pair_select(o2, S) * sgb_s[rows, lanes]
        o_s[rows, B_WIDTH + p * LANES:B_WIDTH + (p + 1) * LANES] = yb.astype(jnp.bfloat16)

    @pl.when(b == n_streams - 1)
    def _():
        o = jnp.dot(o_s[...], wout_ref[...], preferred_element_type=jnp.float32)
        on = _rms_rows(o) * gpost_ref[...]
        for i in range(n_streams):
            sl = slice(i * S, (i + 1) * S)
            y_ref[sl, :] = x_ref[sl, :] + mod_ref[i, 2:3, :] * on[sl, :]


def _sample_call(x, mod, g_pre, w_in, ln_g, ln_b, w_s, b_full, bias_s, w_out, g_post, ck, cv):
    n_streams, n_new, _ = x.shape
    assert n_new % BF16_ROWS == 0 and n_new <= CHUNK
    rows = n_streams * n_new
    f32, bf16 = jnp.float32, jnp.bfloat16

    def const(shape):
        nd = len(shape)
        return pl.BlockSpec(shape, lambda b: (0,) * nd, pipeline_mode=pl.Buffered(1))

    kern = functools.partial(_sample_kernel, n_streams=n_streams, n_new=n_new)
    return pl.pallas_call(
        kern,
        grid=(n_streams,),
        in_specs=[
            const((rows, D_MODEL)),
            const((n_streams, 3, D_MODEL)),
            const((1, D_MODEL)),
            const((D_MODEL, D_IN)),
            const((1, A_WIDTH)),
            const((1, A_WIDTH)),
            const((A_GROUPS, MLP_CHUNK, MLP_CHUNK)),
            const((MLP_CHUNK, A_WIDTH)),
            const((N_PAIRS, 2, n_new, K_BLOCK)),
            const((D_MODEL, D_MODEL)),
            const((1, D_MODEL)),
            pl.BlockSpec((1, KV_WIN, B_WIDTH), lambda b: (b, 0, 0)),
            pl.BlockSpec((1, KV_WIN, B_WIDTH), lambda b: (b, 0, 0)),
        ],
        out_specs=[
            pl.BlockSpec((rows, D_MODEL), lambda b: (0, 0)),
            pl.BlockSpec((rows, B_WIDTH), lambda b: (0, 0)),
            pl.BlockSpec((rows, B_WIDTH), lambda b: (0, 0)),
            pl.BlockSpec((rows, A_WIDTH), lambda b: (0, 0)),
        ],
        out_shape=[
            jax.ShapeDtypeStruct((rows, D_MODEL), f32),
            jax.ShapeDtypeStruct((rows, B_WIDTH), f32),
            jax.ShapeDtypeStruct((rows, B_WIDTH), f32),
            jax.ShapeDtypeStruct((rows, A_WIDTH), f32),
        ],
        scratch_shapes=[
            pltpu.VMEM((rows, A_WIDTH), f32),
            pltpu.VMEM((rows, A_WIDTH), bf16),
            pltpu.VMEM((rows, B_WIDTH), bf16),
            pltpu.VMEM((rows, B_WIDTH), bf16),
            pltpu.VMEM((rows, B_WIDTH), bf16),
            pltpu.VMEM((rows, B_WIDTH), bf16),
            pltpu.VMEM((rows, B_WIDTH), f32),
            pltpu.VMEM((rows, D_MODEL), bf16),
            pltpu.VMEM((N_PAIRS, 2 * n_new, n_new), bf16),
        ],
        compiler_params=pltpu.CompilerParams(
            dimension_semantics=("arbitrary",),
            vmem_limit_bytes=VMEM_LIMIT_BYTES),
        name="sample",
    )(x.reshape(rows, D_MODEL), mod, g_pre, w_in, ln_g, ln_b, w_s, b_full, bias_s, w_out, g_post,
      ck, cv)


def _bias_base(rel_bias):
    n_far = KV_WIN - REL_CLIP
    far = jnp.broadcast_to(rel_bias[:, N_REL - 1:N_REL], (B_HEADS, n_far))
    near = rel_bias[:, N_REL - 1:0:-1]
    return jnp.concatenate([far, near], axis=1)


def kernel(x_prompt, x_sample, cache_attn_k, cache_attn_v, c_prompt, c_sample, g_pre, w_ada, b_ada,
           w_in, ln_g, ln_b, w_s, b_s, rel_bias, w_out, g_post):
    depth = g_pre.shape[0]
    bsz, seq, _ = x_prompt.shape
    n_streams, n_new, _ = x_sample.shape
    win = cache_attn_k.shape[2]
    assert win == KV_WIN and bsz + n_streams <= ADA_ROWS

    yp, ys = x_prompt, x_sample
    kp_rows, vp_rows, ks_rows, vs_rows, va_rows = [], [], [], [], []
    c_all = jnp.concatenate(
        [c_prompt, c_sample, jnp.zeros((ADA_ROWS - bsz - n_streams, D_MODEL), c_prompt.dtype)], axis=0)
    for l in range(depth):
        mod = _ada_call(c_all, w_ada[l], b_ada[l]).reshape(ADA_ROWS, 3, D_MODEL)
        bias_t, bias_s = _relb_call(_bias_base(rel_bias[l]), n_new)
        w_in_b = w_in[l].astype(jnp.bfloat16)
        w_out_b = w_out[l].astype(jnp.bfloat16)
        b_full = jnp.repeat(b_s[l].T, A_GDIM, axis=1)
        pre = (g_pre[l].reshape(1, D_MODEL), w_in_b, ln_g[l].reshape(1, A_WIDTH),
               ln_b[l].reshape(1, A_WIDTH), w_s[l], b_full)
        post = (w_out_b, g_post[l].reshape(1, D_MODEL))

        yp, k_last, v_last = _prompt_call(yp, mod[:bsz], *pre, bias_t, *post)
        kp_rows.append(k_last.reshape(bsz, KV_WIN, B_HEADS, B_HDIM))
        vp_rows.append(v_last.reshape(bsz, KV_WIN, B_HEADS, B_HDIM))

        ck = cache_attn_k[l].reshape(n_streams, KV_WIN, B_WIDTH)
        cv = cache_attn_v[l].reshape(n_streams, KV_WIN, B_WIDTH)
        ys2, k_new, v_new, va_new = _sample_call(ys, mod[bsz:bsz + n_streams], *pre, bias_s, *post,
                                                 ck, cv)
        ys = ys2.reshape(n_streams, n_new, D_MODEL)
        ks_rows.append(k_new.reshape(n_streams, n_new, B_HEADS, B_HDIM))
        vs_rows.append(v_new.reshape(n_streams, n_new, B_HEADS, B_HDIM))
        va_rows.append(va_new.reshape(n_streams, n_new, A_WIDTH))

    return (yp, ys, jnp.stack(kp_rows), jnp.stack(vp_rows), jnp.stack(ks_rows), jnp.stack(vs_rows),
            jnp.stack(va_rows))
```

```python
import functools

import jax
import jax.numpy as jnp
from jax import lax
from jax.experimental import pallas as pl
from jax.experimental.pallas import tpu as pltpu

LANES = 128
SUBLANES = 8
BF16_ROWS = 16
VMEM_LIMIT_BYTES = 56 * 1024 * 1024

D_MODEL = 1024
A_WIDTH = 512
A_GROUPS = 8
A_GDIM = A_WIDTH // A_GROUPS
MLP_CHUNK = 128
B_WIDTH = 512
B_HEADS = 8
B_HDIM = B_WIDTH // B_HEADS
CHUNK = 64
KV_WIN = 512
REL_CLIP = 128
N_REL = 2 * REL_CLIP + 1
EPS = 1e-6
D_IN = 3 * A_WIDTH + 4 * B_WIDTH
NEG = -1e30
LOG2E = 1.4426950408889634
Q_SCALE = B_HDIM ** -0.5 * LOG2E

N_PAIRS = B_HEADS // 2
Q_BLOCK = 2 * CHUNK
K_BLOCK = KV_WIN + Q_BLOCK
SEQ_TILE = 512
SCORE_LEAD = 3
SCORE_SLOTS = SCORE_LEAD + 1
ADA_ROWS = 16
ADA_COLS = 512

C_U, C_V, C_GA, C_Q, C_K, C_VV, C_GB = (i * 512 for i in range(7))

_CONTRACT_LANES = (((1,), (1,)), ((), ()))


def _sigmoid_exp2(neg_arg_log2):
    return 1.0 / (1.0 + jnp.exp2(neg_arg_log2))


def _gelu(x):
    c1 = -2.0 * (2.0 / jnp.pi) ** 0.5 * LOG2E
    return x * _sigmoid_exp2(x * (c1 + (c1 * 0.044715) * (x * x)))


def _silu(x):
    return x * _sigmoid_exp2(x * (-LOG2E))


def _even_lane_mask(shape):
    lane = lax.broadcasted_iota(jnp.int32, shape, len(shape) - 1)
    return (lane % LANES) < B_HDIM


def _ada_kernel(c_ref, w_ref, b_ref, o_ref):
    c = c_ref[...]
    o_ref[...] = jnp.dot(_silu(c), w_ref[...], preferred_element_type=jnp.float32) + b_ref[...]


def _ada_call(c_all, w_ada, b_ada):
    n = w_ada.shape[1]
    return pl.pallas_call(
        _ada_kernel,
        grid=(n // ADA_COLS,),
        in_specs=[
            pl.BlockSpec((ADA_ROWS, D_MODEL), lambda j: (0, 0)),
            pl.BlockSpec((D_MODEL, ADA_COLS), lambda j: (0, j)),
            pl.BlockSpec((1, ADA_COLS), lambda j: (0, j)),
        ],
        out_specs=pl.BlockSpec((ADA_ROWS, ADA_COLS), lambda j: (0, j)),
        out_shape=jax.ShapeDtypeStruct((ADA_ROWS, n), jnp.float32),
        name="ada",
    )(c_all, w_ada, b_ada.reshape(1, n))


def _relb_kernel(base_ref, bt_ref, bs_ref, *, n_new):
    row = lax.broadcasted_iota(jnp.int32, (Q_BLOCK, K_BLOCK), 0)
    col = lax.broadcasted_iota(jnp.int32, (Q_BLOCK, K_BLOCK), 1)
    band_lo = jnp.where(row < CHUNK, 0, CHUNK)
    rel = col - band_lo
    outside = jnp.logical_or(rel < 0, rel >= KV_WIN + CHUNK)
    for par in range(2):
        base = base_ref[0, par:par + 1, :] * LOG2E
        x = jnp.broadcast_to(base, (Q_BLOCK, K_BLOCK))
        shift = 1
        while shift < Q_BLOCK:
            x = jnp.where((row & shift) != 0, jnp.roll(x, shift, axis=1), x)
            shift *= 2
        far = jnp.broadcast_to(base[:, 0:1], (Q_BLOCK, K_BLOCK))
        x = jnp.where(col < row, far, x)
        bs_ref[0, par] = x[0:n_new, :]
        bt_ref[0, :, par * Q_BLOCK:(par + 1) * Q_BLOCK] = jnp.where(outside, NEG, x).T


def _relb_call(base, n_new):
    return pl.pallas_call(
        functools.partial(_relb_kernel, n_new=n_new),
        grid=(N_PAIRS,),
        in_specs=[pl.BlockSpec((1, 2, K_BLOCK), lambda p: (p, 0, 0))],
        out_specs=[pl.BlockSpec((1, K_BLOCK, 2 * Q_BLOCK), lambda p: (p, 0, 0)),
                   pl.BlockSpec((1, 2, n_new, K_BLOCK), lambda p: (p, 0, 0, 0))],
        out_shape=[jax.ShapeDtypeStruct((N_PAIRS, K_BLOCK, 2 * Q_BLOCK), jnp.float32),
                   jax.ShapeDtypeStruct((N_PAIRS, 2, n_new, K_BLOCK), jnp.float32)],
        name="relb",
    )(base.reshape(N_PAIRS, 2, K_BLOCK))


def _tril_pairs(ws_ref, wt_ref, n):
    row = lax.broadcasted_iota(jnp.int32, (n, n), 0)
    col = lax.broadcasted_iota(jnp.int32, (n, n), 1)
    keep = col <= row
    for g in range(A_GROUPS):
        w = jnp.where(keep, ws_ref[g, :n, :n], 0.0)
        wt_ref[g // 2, (g % 2) * n:(g % 2 + 1) * n, :] = w.astype(jnp.bfloat16)


def _rms_rows(x):
    return x * lax.rsqrt(jnp.mean(x * x, axis=-1, keepdims=True) + EPS)


def _layernorm(x, g, b):
    mu = jnp.mean(x, axis=-1, keepdims=True)
    xc = x - mu
    var = jnp.mean(xc * xc, axis=-1, keepdims=True)
    return xc * lax.rsqrt(var + EPS) * g + b


def _pair_select(stacked, n):
    return jnp.where(_even_lane_mask((n, LANES)), stacked[:n], stacked[n:])


def _store_q(q, qe_ref, qo_ref):
    q = (q * Q_SCALE).astype(jnp.bfloat16)
    even = _even_lane_mask(q.shape)
    zero = jnp.zeros((), jnp.bfloat16)
    qe_ref[...] = jnp.where(even, q, zero)
    qo_ref[...] = jnp.where(even, zero, q)


def _prompt_kernel(x_ref, mod_ref, gpre_ref, win_ref, lng_ref, lnb_ref, ws_ref, bfull_ref,
                   biast_ref, wout_ref, gpost_ref,
                   y_ref, klast_ref, vlast_ref,
                   h_s, usg_s, va_s, qe_s, qo_s, k_s, vt_s, sgb_s, o_s, wt_s, st_s):
    t = pl.program_id(1)
    nt = pl.num_programs(1)
    T = SEQ_TILE

    @pl.when(jnp.logical_and(pl.program_id(0) == 0, t == 0))
    def _():
        _tril_pairs(ws_ref, wt_s, MLP_CHUNK)

    shift = mod_ref[0, 0:1, :]
    scale = mod_ref[0, 1:2, :]
    gate = mod_ref[0, 2:3, :]

    x = x_ref[0]
    h = _rms_rows(x) * (gpre_ref[...] * (1.0 + scale)) + shift
    h_s[...] = h.astype(jnp.bfloat16)

    def proj(c0):
        return jnp.dot(h_s[...], win_ref[:, c0:c0 + 512], preferred_element_type=jnp.float32)

    usg_s[...] = _gelu(proj(C_U))
    k = proj(C_K)
    k_s[KV_WIN:KV_WIN + T, :] = k.astype(jnp.bfloat16)
    va_s[...] = _layernorm(_gelu(proj(C_V)), lng_ref[...], lnb_ref[...]).astype(jnp.bfloat16)
    v = proj(C_VV)
    vt_s[:, KV_WIN:KV_WIN + T] = v.T.astype(jnp.bfloat16)
    usg_s[...] = usg_s[...] * _silu(proj(C_GA))
    _store_q(proj(C_Q), qe_s, qo_s)
    sgb_s[...] = _silu(proj(C_GB))
    klast_ref[0] = k[T - KV_WIN:, :]
    vlast_ref[0] = v[T - KV_WIN:, :]

    for c in range(T // MLP_CHUNK):
        rows = slice(c * MLP_CHUNK, (c + 1) * MLP_CHUNK)
        for p in range(N_PAIRS):
            lanes = slice(p * LANES, (p + 1) * LANES)
            mix = jnp.dot(wt_s[p], va_s[rows, lanes], preferred_element_type=jnp.float32)
            mixed = _pair_select(mix, MLP_CHUNK) + bfull_ref[:, lanes]
            o_s[rows, lanes] = (usg_s[rows, lanes] * mixed).astype(jnp.bfloat16)

    def key_lo(qb, first_tile):
        return KV_WIN - qb * Q_BLOCK if first_tile else 0

    def scores(qb, p, slot, first_tile):
        r0, lo = qb * Q_BLOCK, key_lo(qb, first_tile)
        lanes = slice(p * LANES, (p + 1) * LANES)
        qs = jnp.concatenate([qe_s[r0:r0 + Q_BLOCK, lanes],
                              qo_s[r0:r0 + Q_BLOCK, lanes]], axis=0)
        half = (K_BLOCK - lo) // 2
        for a in (lo, lo + half):
            st_s[slot, a:a + half, :] = lax.dot_general(
                k_s[r0 + a:r0 + a + half, lanes], qs, _CONTRACT_LANES,
                preferred_element_type=jnp.float32) + biast_ref[p, a:a + half, :]

    def finish(qb, p, slot, first_tile):
        r0, lo = qb * Q_BLOCK, key_lo(qb, first_tile)
        st = st_s[slot, lo:K_BLOCK, :]
        m = jnp.max(st, axis=0, keepdims=True)
        pt = jnp.exp2(st - m).astype(jnp.bfloat16)
        ones = jnp.ones((BF16_ROWS, K_BLOCK - lo), jnp.bfloat16)
        ots = []
        for par in range(2):
            feats = slice(p * LANES + par * B_HDIM, p * LANES + (par + 1) * B_HDIM)
            qcols = slice(par * Q_BLOCK, (par + 1) * Q_BLOCK)
            lhs = jnp.concatenate([vt_s[feats, r0 + lo:r0 + K_BLOCK], ones], axis=0)
            ot = jnp.dot(lhs, pt[:, qcols], preferred_element_type=jnp.float32)
            ots.append(ot[0:B_HDIM] * (1.0 / ot[B_HDIM:B_HDIM + 1]))
        yb = jnp.concatenate(ots, axis=0).T * sgb_s[r0:r0 + Q_BLOCK, p * LANES:(p + 1) * LANES]
        o_s[r0:r0 + Q_BLOCK, B_WIDTH + p * LANES:B_WIDTH + (p + 1) * LANES] = (
            yb.astype(jnp.bfloat16))

    def attend(first_tile):
        items = [(qb, p) for qb in range(T // Q_BLOCK) for p in range(N_PAIRS)]
        for j in range(SCORE_LEAD):
            scores(*items[j], j % SCORE_SLOTS, first_tile)
        for i, (qb, p) in enumerate(items):
            if i + SCORE_LEAD < len(items):
                scores(*items[i + SCORE_LEAD], (i + SCORE_LEAD) % SCORE_SLOTS, first_tile)
            finish(qb, p, i % SCORE_SLOTS, first_tile)

    @pl.when(t == 0)
    def _():
        attend(True)

    @pl.when(t > 0)
    def _():
        attend(False)

    k_s[0:KV_WIN, :] = k_s[T:T + KV_WIN, :]
    vt_s[:, 0:KV_WIN] = vt_s[:, T:T + KV_WIN]

    o = jnp.dot(o_s[...], wout_ref[...], preferred_element_type=jnp.float32)
    y_ref[0] = x_ref[0] + gate * (_rms_rows(o) * gpost_ref[...])


def _const_spec(shape):
    nd = len(shape)
    return pl.BlockSpec(shape, lambda b, t: (0,) * nd, pipeline_mode=pl.Buffered(1))


def _prompt_call(x, mod, g_pre, w_in, ln_g, ln_b, w_s, b_full, bias_t, w_out, g_post):
    bsz, seq, _ = x.shape
    T = SEQ_TILE
    assert seq % T == 0 and T >= KV_WIN
    nt = seq // T
    f32, bf16 = jnp.float32, jnp.bfloat16
    return pl.pallas_call(
        _prompt_kernel,
        grid=(bsz, nt),
        in_specs=[
            pl.BlockSpec((1, T, D_MODEL), lambda b, t: (b, t, 0)),
            pl.BlockSpec((1, 3, D_MODEL), lambda b, t: (b, 0, 0)),
            _const_spec((1, D_MODEL)),
            _const_spec((D_MODEL, D_IN)),
            _const_spec((1, A_WIDTH)),
            _const_spec((1, A_WIDTH)),
            _const_spec((A_GROUPS, MLP_CHUNK, MLP_CHUNK)),
            _const_spec((MLP_CHUNK, A_WIDTH)),
            _const_spec((N_PAIRS, K_BLOCK, 2 * Q_BLOCK)),
            _const_spec((D_MODEL, D_MODEL)),
            _const_spec((1, D_MODEL)),
        ],
        out_specs=[
            pl.BlockSpec((1, T, D_MODEL), lambda b, t: (b, t, 0)),
            pl.BlockSpec((1, KV_WIN, B_WIDTH), lambda b, t: (b, 0, 0)),
            pl.BlockSpec((1, KV_WIN, B_WIDTH), lambda b, t: (b, 0, 0)),
        ],
        out_shape=[
            jax.ShapeDtypeStruct((bsz, seq, D_MODEL), f32),
            jax.ShapeDtypeStruct((bsz, KV_WIN, B_WIDTH), f32),
            jax.ShapeDtypeStruct((bsz, KV_WIN, B_WIDTH), f32),
        ],
        scratch_shapes=[
            pltpu.VMEM((T, D_MODEL), bf16),
            pltpu.VMEM((T, A_WIDTH), f32),
            pltpu.VMEM((T, A_WIDTH), bf16),
            pltpu.VMEM((T, B_WIDTH), bf16),
            pltpu.VMEM((T, B_WIDTH), bf16),
            pltpu.VMEM((KV_WIN + T, B_WIDTH), bf16),
            pltpu.VMEM((B_WIDTH, KV_WIN + T), bf16),
            pltpu.VMEM((T, B_WIDTH), f32),
            pltpu.VMEM((T, D_MODEL), bf16),
            pltpu.VMEM((N_PAIRS, 2 * MLP_CHUNK, MLP_CHUNK), bf16),
            pltpu.VMEM((SCORE_SLOTS, K_BLOCK, 2 * Q_BLOCK), f32),
        ],
        compiler_params=pltpu.CompilerParams(
            dimension_semantics=("arbitrary", "arbitrary"),
            vmem_limit_bytes=VMEM_LIMIT_BYTES),
        name="prompt",
    )(x, mod, g_pre, w_in, ln_g, ln_b, w_s, b_full, bias_t, w_out, g_post)


def _sample_kernel(x_ref, mod_ref, gpre_ref, win_ref, lng_ref, lnb_ref, ws_ref, bfull_ref,
                   bias_ref, wout_ref, gpost_ref, ck_ref, cv_ref,
                   y_ref, knew_ref, vnew_ref, vanew_ref,
                   usg_s, va_s, qe_s, qo_s, k_s, v_s, sgb_s, o_s, wt_s, *, n_streams, n_new):
    b = pl.program_id(0)
    S = n_new

    @pl.when(b == 0)
    def _():
        _tril_pairs(ws_ref, wt_s, S)
        gpre = gpre_ref[...]
        hs = []
        for i in range(n_streams):
            xi = x_ref[i * S:(i + 1) * S, :]
            hs.append(_rms_rows(xi) * (gpre * (1.0 + mod_ref[i, 1:2, :])) + mod_ref[i, 0:1, :])
        h = jnp.concatenate(hs, axis=0).astype(jnp.bfloat16)

        def proj(c0):
            return jnp.dot(h, win_ref[:, c0:c0 + 512], preferred_element_type=jnp.float32)

        va = _layernorm(_gelu(proj(C_V)), lng_ref[...], lnb_ref[...])
        vanew_ref[...] = va
        va_s[...] = va.astype(jnp.bfloat16)
        usg_s[...] = _gelu(proj(C_U)) * _silu(proj(C_GA))
        _store_q(proj(C_Q), qe_s, qo_s)
        k = proj(C_K)
        v = proj(C_VV)
        knew_ref[...] = k
        vnew_ref[...] = v
        k_s[...] = k.astype(jnp.bfloat16)
        v_s[...] = v.astype(jnp.bfloat16)
        sgb_s[...] = _silu(proj(C_GB))

    r0 = pl.multiple_of(b * S, S)
    rows = pl.ds(r0, S)
    for p in range(N_PAIRS):
        lanes = slice(p * LANES, (p + 1) * LANES)
        mix = jnp.dot(wt_s[p], va_s[rows, lanes], preferred_element_type=jnp.float32)
        mixed = _pair_select(mix, S) + bfull_ref[0:S, lanes]
        o_s[rows, lanes] = (usg_s[rows, lanes] * mixed).astype(jnp.bfloat16)

        qs = jnp.concatenate([qe_s[rows, lanes], qo_s[rows, lanes]], axis=0)
        kc = ck_ref[0, :, lanes].astype(jnp.bfloat16)
        vc = cv_ref[0, :, lanes].astype(jnp.bfloat16)
        kn = k_s[rows, lanes]
        vn = v_s[rows, lanes]
        s1 = lax.dot_general(qs, kc, _CONTRACT_LANES, preferred_element_type=jnp.float32)
        s2 = lax.dot_general(qs, kn, _CONTRACT_LANES, preferred_element_type=jnp.float32)
        bias = jnp.concatenate([bias_ref[p, 0], bias_ref[p, 1]], axis=0)
        s1 = s1 + bias[:, 0:KV_WIN]
        s2 = s2 + bias[:, KV_WIN:KV_WIN + S]
        m = jnp.maximum(jnp.max(s1, axis=-1, keepdims=True), jnp.max(s2, axis=-1, keepdims=True))
        e1 = jnp.exp2(s1 - m)
        e2 = jnp.exp2(s2 - m)
        l = jnp.sum(e1, axis=-1, keepdims=True) + jnp.sum(e2, axis=-1, keepdims=True)
        o2 = (jnp.dot(e1.astype(jnp.bfloat16), vc, preferred_element_type=jnp.float32)
              + jnp.dot(e2.astype(jnp.bfloat16), vn, preferred_element_type=jnp.float32))
        o2 = o2 * (1.0 / l)
        yb = _pair_select(o2, S) * sgb_s[rows, lanes]
        o_s[rows, B_WIDTH + p * LANES:B_WIDTH + (p + 1) * LANES] = yb.astype(jnp.bfloat16)

    @pl.when(b == n_streams - 1)
    def _():
        o = jnp.dot(o_s[...], wout_ref[...], preferred_element_type=jnp.float32)
        on = _rms_rows(o) * gpost_ref[...]
        for i in range(n_streams):
            sl = slice(i * S, (i + 1) * S)
            y_ref[sl, :] = x_ref[sl, :] + mod_ref[i, 2:3, :] * on[sl, :]


def _sample_call(x, mod, g_pre, w_in, ln_g, ln_b, w_s, b_full, bias_s, w_out, g_post, ck, cv):
    n_streams, n_new, _ = x.shape
    assert n_new % BF16_ROWS == 0 and n_new <= CHUNK
    rows = n_streams * n_new
    f32, bf16 = jnp.float32, jnp.bfloat16

    def const(shape):
        nd = len(shape)
        return pl.BlockSpec(shape, lambda b: (0,) * nd, pipeline_mode=pl.Buffered(1))

    kern = functools.partial(_sample_kernel, n_streams=n_streams, n_new=n_new)
    return pl.pallas_call(
        kern,
        grid=(n_streams,),
        in_specs=[
            const((rows, D_MODEL)),
            const((n_streams, 3, D_MODEL)),
            const((1, D_MODEL)),
            const((D_MODEL, D_IN)),
            const((1, A_WIDTH)),
            const((1, A_WIDTH)),
            const((A_GROUPS, MLP_CHUNK, MLP_CHUNK)),
            const((MLP_CHUNK, A_WIDTH)),
            const((N_PAIRS, 2, n_new, K_BLOCK)),
            const((D_MODEL, D_MODEL)),
            const((1, D_MODEL)),
            pl.BlockSpec((1, KV_WIN, B_WIDTH), lambda b: (b, 0, 0)),
            pl.BlockSpec((1, KV_WIN, B_WIDTH), lambda b: (b, 0, 0)),
        ],
        out_specs=[
            pl.BlockSpec((rows, D_MODEL), lambda b: (0, 0)),
            pl.BlockSpec((rows, B_WIDTH), lambda b: (0, 0)),
            pl.BlockSpec((rows, B_WIDTH), lambda b: (0, 0)),
            pl.BlockSpec((rows, A_WIDTH), lambda b: (0, 0)),
        ],
        out_shape=[
            jax.ShapeDtypeStruct((rows, D_MODEL), f32),
            jax.ShapeDtypeStruct((rows, B_WIDTH), f32),
            jax.ShapeDtypeStruct((rows, B_WIDTH), f32),
            jax.ShapeDtypeStruct((rows, A_WIDTH), f32),
        ],
        scratch_shapes=[
            pltpu.VMEM((rows, A_WIDTH), f32),
            pltpu.VMEM((rows, A_WIDTH), bf16),
            pltpu.VMEM((rows, B_WIDTH), bf16),
            pltpu.VMEM((rows, B_WIDTH), bf16),
            pltpu.VMEM((rows, B_WIDTH), bf16),
            pltpu.VMEM((rows, B_WIDTH), bf16),
            pltpu.VMEM((rows, B_WIDTH), f32),
            pltpu.VMEM((rows, D_MODEL), bf16),
            pltpu.VMEM((N_PAIRS, 2 * n_new, n_new), bf16),
        ],
        compiler_params=pltpu.CompilerParams(
            dimension_semantics=("arbitrary",),
            vmem_limit_bytes=VMEM_LIMIT_BYTES),
        name="sample",
    )(x.reshape(rows, D_MODEL), mod, g_pre, w_in, ln_g, ln_b, w_s, b_full, bias_s, w_out, g_post,
      ck, cv)


def _bias_base(rel_bias):
    n_far = KV_WIN - REL_CLIP
    far = jnp.broadcast_to(rel_bias[:, N_REL - 1:N_REL], (B_HEADS, n_far))
    near = rel_bias[:, N_REL - 1:0:-1]
    return jnp.concatenate([far, near], axis=1)


def kernel(x_prompt, x_sample, cache_attn_k, cache_attn_v, c_prompt, c_sample, g_pre, w_ada, b_ada,
           w_in, ln_g, ln_b, w_s, b_s, rel_bias, w_out, g_post):
    depth = g_pre.shape[0]
    bsz, seq, _ = x_prompt.shape
    n_streams, n_new, _ = x_sample.shape
    win = cache_attn_k.shape[2]
    assert win == KV_WIN and bsz + n_streams <= ADA_ROWS

    yp, ys = x_prompt, x_sample
    kp_rows, vp_rows, ks_rows, vs_rows, va_rows = [], [], [], [], []
    c_all = jnp.concatenate(
        [c_prompt, c_sample, jnp.zeros((ADA_ROWS - bsz - n_streams, D_MODEL), c_prompt.dtype)], axis=0)
    for l in range(depth):
        mod = _ada_call(c_all, w_ada[l], b_ada[l]).reshape(ADA_ROWS, 3, D_MODEL)
        bias_t, bias_s = _relb_call(_bias_base(rel_bias[l]), n_new)
        w_in_b = w_in[l].astype(jnp.bfloat16)
        w_out_b = w_out[l].astype(jnp.bfloat16)
        b_full = jnp.repeat(b_s[l].T, A_GDIM, axis=1)
        pre = (g_pre[l].reshape(1, D_MODEL), w_in_b, ln_g[l].reshape(1, A_WIDTH),
               ln_b[l].reshape(1, A_WIDTH), w_s[l], b_full)
        post = (w_out_b, g_post[l].reshape(1, D_MODEL))

        yp, k_last, v_last = _prompt_call(yp, mod[:bsz], *pre, bias_t, *post)
        kp_rows.append(k_last.reshape(bsz, KV_WIN, B_HEADS, B_HDIM))
        vp_rows.append(v_last.reshape(bsz, KV_WIN, B_HEADS, B_HDIM))

        ck = cache_attn_k[l].reshape(n_streams, KV_WIN, B_WIDTH)
        cv = cache_attn_v[l].reshape(n_streams, KV_WIN, B_WIDTH)
        ys2, k_new, v_new, va_new = _sample_call(ys, mod[bsz:bsz + n_streams], *pre, bias_s, *post,
                                                 ck, cv)
        ys = ys2.reshape(n_streams, n_new, D_MODEL)
        ks_rows.append(k_new.reshape(n_streams, n_new, B_HEADS, B_HDIM))
        vs_rows.append(v_new.reshape(n_streams, n_new, B_HEADS, B_HDIM))
        va_rows.append(va_new.reshape(n_streams, n_new, A_WIDTH))

    return (yp, ys, jnp.stack(kp_rows), jnp.stack(vp_rows), jnp.stack(ks_rows), jnp.stack(vs_rows),
            jnp.stack(va_rows))
```

```python
import functools

import jax
import jax.numpy as jnp
from jax import lax
from jax.experimental import pallas as pl
from jax.experimental.pallas import tpu as pltpu

LANES = 128
SUBLANES = 8
BF16_ROWS = 16
VMEM_LIMIT_BYTES = 56 * 1024 * 1024

D_MODEL = 1024
A_WIDTH = 512
A_GROUPS = 8
A_GDIM = A_WIDTH // A_GROUPS
MLP_CHUNK = 128
B_WIDTH = 512
B_HEADS = 8
B_HDIM = B_WIDTH // B_HEADS
CHUNK = 64
KV_WIN = 512
REL_CLIP = 128
N_REL = 2 * REL_CLIP + 1
EPS = 1e-6
D_IN = 3 * A_WIDTH + 4 * B_WIDTH
NEG = -1e30
LOG2E = 1.4426950408889634
Q_SCALE = B_HDIM ** -0.5 * LOG2E

N_PAIRS = B_HEADS // 2
Q_BLOCK = 2 * CHUNK
K_BLOCK = KV_WIN + Q_BLOCK
SEQ_TILE = 512
SCORE_LEAD = 3
SCORE_SLOTS = SCORE_LEAD + 1
ADA_ROWS = 16

C_U, C_V, C_GA, C_Q, C_K, C_VV, C_GB = (i * 512 for i in range(7))

_CONTRACT_LANES = (((1,), (1,)), ((), ()))


def _sigmoid_exp2(neg_arg_log2):
    return 1.0 / (1.0 + jnp.exp2(neg_arg_log2))


def _gelu(x):
    c1 = -2.0 * (2.0 / jnp.pi) ** 0.5 * LOG2E
    return x * _sigmoid_exp2(x * (c1 + (c1 * 0.044715) * (x * x)))


def _silu(x):
    return x * _sigmoid_exp2(x * (-LOG2E))


def _even_lane_mask(shape):
    lane = lax.broadcasted_iota(jnp.int32, shape, len(shape) - 1)
    return (lane % LANES) < B_HDIM


def _ada_block(c_ref, w_ref, b_ref, o_ref):
    c = c_ref[...]
    o_ref[...] = jnp.dot(_silu(c), w_ref[...], preferred_element_type=jnp.float32) + b_ref[...]


def _relb_block(base_ref, bt_ref, bs_ref, n_new):
    row = lax.broadcasted_iota(jnp.int32, (Q_BLOCK, K_BLOCK), 0)
    col = lax.broadcasted_iota(jnp.int32, (Q_BLOCK, K_BLOCK), 1)
    band_lo = jnp.where(row < CHUNK, 0, CHUNK)
    rel = col - band_lo
    outside = jnp.logical_or(rel < 0, rel >= KV_WIN + CHUNK)
    for par in range(2):
        base = base_ref[0, par:par + 1, :] * LOG2E
        x = jnp.broadcast_to(base, (Q_BLOCK, K_BLOCK))
        shift = 1
        while shift < Q_BLOCK:
            x = jnp.where((row & shift) != 0, jnp.roll(x, shift, axis=1), x)
            shift *= 2
        far = jnp.broadcast_to(base[:, 0:1], (Q_BLOCK, K_BLOCK))
        x = jnp.where(col < row, far, x)
        bs_ref[0, par] = x[0:n_new, :]
        bt_ref[0, :, par * Q_BLOCK:(par + 1) * Q_BLOCK] = jnp.where(outside, NEG, x).T


PREP_COLS = 512
N_PREP = D_IN // PREP_COLS
N_ADA = 3 * D_MODEL // PREP_COLS
WOUT_COLS = D_MODEL // N_PAIRS


def _prep_kernel(c_ref, wada_ref, bada_ref, base_ref, win_ref, wout_ref,
                 mod_ref, bt_ref, bs_ref, winb_ref, woutb_ref, *, n_new):
    j = pl.program_id(0)
    winb_ref[...] = win_ref[...].astype(jnp.bfloat16)

    @pl.when(j < N_ADA)
    def _():
        _ada_block(c_ref, wada_ref, bada_ref, mod_ref)

    @pl.when(j < N_PAIRS)
    def _():
        woutb_ref[...] = wout_ref[...].astype(jnp.bfloat16)
        _relb_block(base_ref, bt_ref, bs_ref, n_new)


def _prep_call(c_all, w_ada, b_ada, base, w_in, w_out, n_new):
    f32, bf16 = jnp.float32, jnp.bfloat16

    def upto(n):
        return lambda j: jnp.minimum(j, n - 1)

    ada_j, pair_j = upto(N_ADA), upto(N_PAIRS)
    return pl.pallas_call(
        functools.partial(_prep_kernel, n_new=n_new),
        grid=(N_PREP,),
        in_specs=[
            pl.BlockSpec((ADA_ROWS, D_MODEL), lambda j: (0, 0)),
            pl.BlockSpec((D_MODEL, PREP_COLS), lambda j: (0, ada_j(j))),
            pl.BlockSpec((1, PREP_COLS), lambda j: (0, ada_j(j))),
            pl.BlockSpec((1, 2, K_BLOCK), lambda j: (pair_j(j), 0, 0)),
            pl.BlockSpec((D_MODEL, PREP_COLS), lambda j: (0, j)),
            pl.BlockSpec((D_MODEL, WOUT_COLS), lambda j: (0, pair_j(j))),
        ],
        out_specs=[
            pl.BlockSpec((ADA_ROWS, PREP_COLS), lambda j: (0, ada_j(j))),
            pl.BlockSpec((1, K_BLOCK, 2 * Q_BLOCK), lambda j: (pair_j(j), 0, 0)),
            pl.BlockSpec((1, 2, n_new, K_BLOCK), lambda j: (pair_j(j), 0, 0, 0)),
            pl.BlockSpec((D_MODEL, PREP_COLS), lambda j: (0, j)),
            pl.BlockSpec((D_MODEL, WOUT_COLS), lambda j: (0, pair_j(j))),
        ],
        out_shape=[
            jax.ShapeDtypeStruct((ADA_ROWS, 3 * D_MODEL), f32),
            jax.ShapeDtypeStruct((N_PAIRS, K_BLOCK, 2 * Q_BLOCK), f32),
            jax.ShapeDtypeStruct((N_PAIRS, 2, n_new, K_BLOCK), f32),
            jax.ShapeDtypeStruct((D_MODEL, D_IN), bf16),
            jax.ShapeDtypeStruct((D_MODEL, D_MODEL), bf16),
        ],
        compiler_params=pltpu.CompilerParams(dimension_semantics=("arbitrary",)),
        name="prep",
    )(c_all, w_ada, b_ada.reshape(1, 3 * D_MODEL), base.reshape(N_PAIRS, 2, K_BLOCK), w_in, w_out)


def _tril_pairs(ws_ref, wt_ref, n):
    row = lax.broadcasted_iota(jnp.int32, (n, n), 0)
    col = lax.broadcasted_iota(jnp.int32, (n, n), 1)
    keep = col <= row
    for g in range(A_GROUPS):
        w = jnp.where(keep, ws_ref[g, :n, :n], 0.0)
        wt_ref[g // 2, (g % 2) * n:(g % 2 + 1) * n, :] = w.astype(jnp.bfloat16)


def _rms_rows(x):
    return x * lax.rsqrt(jnp.mean(x * x, axis=-1, keepdims=True) + EPS)


def _layernorm(x, g, b):
    mu = jnp.mean(x, axis=-1, keepdims=True)
    xc = x - mu
    var = jnp.mean(xc * xc, axis=-1, keepdims=True)
    return xc * lax.rsqrt(var + EPS) * g + b


def _pair_select(stacked, n):
    return jnp.where(_even_lane_mask((n, LANES)), stacked[:n], stacked[n:])


def _split_heads(x):
    return x.reshape(x.shape[0], B_HEADS, B_HDIM)


def _merge_heads(x):
    return x.reshape(x.shape[0], B_WIDTH)


def _store_q(q, qe_ref, qo_ref):
    q = (q * Q_SCALE).astype(jnp.bfloat16)
    even = _even_lane_mask(q.shape)
    zero = jnp.zeros((), jnp.bfloat16)
    qe_ref[...] = jnp.where(even, q, zero)
    qo_ref[...] = jnp.where(even, zero, q)


def _prompt_kernel(x_ref, mod_ref, gpre_ref, win_ref, lng_ref, lnb_ref, ws_ref, bfull_ref,
                   biast_ref, wout_ref, gpost_ref,
                   y_ref, klast_ref, vlast_ref,
                   h_s, usg_s, va_s, qe_s, qo_s, k_s, vt_s, sgb_s, o_s, wt_s, st_s, kf_s, vf_s):
    t = pl.program_id(1)
    nt = pl.num_programs(1)
    T = SEQ_TILE

    @pl.when(jnp.logical_and(pl.program_id(0) == 0, t == 0))
    def _():
        _tril_pairs(ws_ref, wt_s, MLP_CHUNK)

    shift = mod_ref[0, 0:1, :]
    scale = mod_ref[0, 1:2, :]
    gate = mod_ref[0, 2:3, :]

    row_halves = [slice(i * (T // 2), (i + 1) * (T // 2)) for i in range(2)]
    pre_scale = gpre_ref[...] * (1.0 + scale)
    for rows in row_halves:
        h_s[rows, :] = (_rms_rows(x_ref[0, rows, :]) * pre_scale + shift).astype(jnp.bfloat16)

    def proj(c0, rows=slice(None)):
        return jnp.dot(h_s[rows, :], win_ref[:, c0:c0 + 512], preferred_element_type=jnp.float32)

    for rows in row_halves:
        usg_s[rows, :] = _gelu(proj(C_U, rows))
    k = proj(C_K)
    k_s[KV_WIN:KV_WIN + T, :] = k.astype(jnp.bfloat16)
    va_s[...] = _layernorm(_gelu(proj(C_V)), lng_ref[...], lnb_ref[...]).astype(jnp.bfloat16)
    v = proj(C_VV)
    vt_s[:, KV_WIN:KV_WIN + T] = v.T.astype(jnp.bfloat16)
    usg_s[...] = usg_s[...] * _silu(proj(C_GA))
    _store_q(proj(C_Q), qe_s, qo_s)
    sgb_s[...] = _silu(proj(C_GB))
    kf_s[...] = k[T - KV_WIN:, :]
    vf_s[...] = v[T - KV_WIN:, :]

    for c in range(T // MLP_CHUNK):
        rows = slice(c * MLP_CHUNK, (c + 1) * MLP_CHUNK)
        for p in range(N_PAIRS):
            lanes = slice(p * LANES, (p + 1) * LANES)
            mix = jnp.dot(wt_s[p], va_s[rows, lanes], preferred_element_type=jnp.float32)
            mixed = _pair_select(mix, MLP_CHUNK) + bfull_ref[:, lanes]
            o_s[rows, lanes] = (usg_s[rows, lanes] * mixed).astype(jnp.bfloat16)

    def key_lo(qb, first_tile):
        return KV_WIN - qb * Q_BLOCK if first_tile else 0

    def scores(qb, p, slot, first_tile):
        r0, lo = qb * Q_BLOCK, key_lo(qb, first_tile)
        lanes = slice(p * LANES, (p + 1) * LANES)
        qs = jnp.concatenate([qe_s[r0:r0 + Q_BLOCK, lanes],
                              qo_s[r0:r0 + Q_BLOCK, lanes]], axis=0)
        half = (K_BLOCK - lo) // 2
        for a in (lo, lo + half):
            st_s[slot, a:a + half, :] = lax.dot_general(
                k_s[r0 + a:r0 + a + half, lanes], qs, _CONTRACT_LANES,
                preferred_element_type=jnp.float32) + biast_ref[p, a:a + half, :]

    def finish(qb, p, slot, first_tile):
        r0, lo = qb * Q_BLOCK, key_lo(qb, first_tile)
        st = st_s[slot, lo:K_BLOCK, :]
        m = jnp.max(st, axis=0, keepdims=True)
        pt = jnp.exp2(st - m).astype(jnp.bfloat16)
        ones = jnp.ones((BF16_ROWS, K_BLOCK - lo), jnp.bfloat16)
        ots = []
        for par in range(2):
            feats = slice(p * LANES + par * B_HDIM, p * LANES + (par + 1) * B_HDIM)
            qcols = slice(par * Q_BLOCK, (par + 1) * Q_BLOCK)
            lhs = jnp.concatenate([vt_s[feats, r0 + lo:r0 + K_BLOCK], ones], axis=0)
            ot = jnp.dot(lhs, pt[:, qcols], preferred_element_type=jnp.float32)
            ots.append(ot[0:B_HDIM] * (1.0 / ot[B_HDIM:B_HDIM + 1]))
        yb = jnp.concatenate(ots, axis=0).T * sgb_s[r0:r0 + Q_BLOCK, p * LANES:(p + 1) * LANES]
        o_s[r0:r0 + Q_BLOCK, B_WIDTH + p * LANES:B_WIDTH + (p + 1) * LANES] = (
            yb.astype(jnp.bfloat16))

    def attend(first_tile):
        items = [(qb, p) for qb in range(T // Q_BLOCK) for p in range(N_PAIRS)]
        for j in range(SCORE_LEAD):
            scores(*items[j], j % SCORE_SLOTS, first_tile)
        for i, (qb, p) in enumerate(items):
            if i + SCORE_LEAD < len(items):
                scores(*items[i + SCORE_LEAD], (i + SCORE_LEAD) % SCORE_SLOTS, first_tile)
            finish(qb, p, i % SCORE_SLOTS, first_tile)

    @pl.when(t == 0)
    def _():
        attend(True)

    @pl.when(t > 0)
    def _():
        attend(False)

    k_s[0:KV_WIN, :] = k_s[T:T + KV_WIN, :]
    vt_s[:, 0:KV_WIN] = vt_s[:, T:T + KV_WIN]

    post_scale = gate * gpost_ref[...]
    for rows in row_halves:
        o = jnp.dot(o_s[rows, :], wout_ref[...], preferred_element_type=jnp.float32)
        y_ref[0, rows, :] = x_ref[0, rows, :] + _rms_rows(o) * post_scale

    @pl.when(t == nt - 1)
    def _():
        klast_ref[0] = _split_heads(kf_s[...])
        vlast_ref[0] = _split_heads(vf_s[...])


def _const_spec(shape):
    nd = len(shape)
    return pl.BlockSpec(shape, lambda b, t: (0,) * nd, pipeline_mode=pl.Buffered(1))


def _prompt_call(x, mod, g_pre, w_in, ln_g, ln_b, w_s, b_full, bias_t, w_out, g_post):
    bsz, seq, _ = x.shape
    T = SEQ_TILE
    assert seq % T == 0 and T >= KV_WIN
    nt = seq // T
    f32, bf16 = jnp.float32, jnp.bfloat16
    return pl.pallas_call(
        _prompt_kernel,
        grid=(bsz, nt),
        in_specs=[
            pl.BlockSpec((1, T, D_MODEL), lambda b, t: (b, t, 0)),
            pl.BlockSpec((1, 3, D_MODEL), lambda b, t: (b, 0, 0)),
            _const_spec((1, D_MODEL)),
            _const_spec((D_MODEL, D_IN)),
            _const_spec((1, A_WIDTH)),
            _const_spec((1, A_WIDTH)),
            _const_spec((A_GROUPS, MLP_CHUNK, MLP_CHUNK)),
            _const_spec((MLP_CHUNK, A_WIDTH)),
            _const_spec((N_PAIRS, K_BLOCK, 2 * Q_BLOCK)),
            _const_spec((D_MODEL, D_MODEL)),
            _const_spec((1, D_MODEL)),
        ],
        out_specs=[
            pl.BlockSpec((1, T, D_MODEL), lambda b, t: (b, t, 0)),
            pl.BlockSpec((1, KV_WIN, B_HEADS, B_HDIM), lambda b, t: (b, 0, 0, 0)),
            pl.BlockSpec((1, KV_WIN, B_HEADS, B_HDIM), lambda b, t: (b, 0, 0, 0)),
        ],
        out_shape=[
            jax.ShapeDtypeStruct((bsz, seq, D_MODEL), f32),
            jax.ShapeDtypeStruct((bsz, KV_WIN, B_HEADS, B_HDIM), f32),
            jax.ShapeDtypeStruct((bsz, KV_WIN, B_HEADS, B_HDIM), f32),
        ],
        scratch_shapes=[
            pltpu.VMEM((T, D_MODEL), bf16),
            pltpu.VMEM((T, A_WIDTH), f32),
            pltpu.VMEM((T, A_WIDTH), bf16),
            pltpu.VMEM((T, B_WIDTH), bf16),
            pltpu.VMEM((T, B_WIDTH), bf16),
            pltpu.VMEM((KV_WIN + T, B_WIDTH), bf16),
            pltpu.VMEM((B_WIDTH, KV_WIN + T), bf16),
            pltpu.VMEM((T, B_WIDTH), f32),
            pltpu.VMEM((T, D_MODEL), bf16),
            pltpu.VMEM((N_PAIRS, 2 * MLP_CHUNK, MLP_CHUNK), bf16),
            pltpu.VMEM((SCORE_SLOTS, K_BLOCK, 2 * Q_BLOCK), f32),
            pltpu.VMEM((KV_WIN, B_WIDTH), f32),
            pltpu.VMEM((KV_WIN, B_WIDTH), f32),
        ],
        compiler_params=pltpu.CompilerParams(
            dimension_semantics=("arbitrary", "arbitrary"),
            vmem_limit_bytes=VMEM_LIMIT_BYTES),
        name="prompt",
    )(x, mod, g_pre, w_in, ln_g, ln_b, w_s, b_full, bias_t, w_out, g_post)


def _sample_kernel(x_ref, mod_ref, gpre_ref, win_ref, lng_ref, lnb_ref, ws_ref, bfull_ref,
                   bias_ref, wout_ref, gpost_ref, ck_ref, cv_ref,
                   y_ref, knew_ref, vnew_ref, vanew_ref,
                   usg_s, va_s, qe_s, qo_s, k_s, v_s, sgb_s, o_s, wt_s, kc_s, vc_s,
                   *, n_streams, n_new):
    b = pl.program_id(0)
    S = n_new

    @pl.when(b == 0)
    def _():
        _tril_pairs(ws_ref, wt_s, S)
        gpre = gpre_ref[...]
        hs = []
        for i in range(n_streams):
            xi = x_ref[i * S:(i + 1) * S, :]
            hs.append(_rms_rows(xi) * (gpre * (1.0 + mod_ref[i, 1:2, :])) + mod_ref[i, 0:1, :])
        h = jnp.concatenate(hs, axis=0).astype(jnp.bfloat16)

        def proj(c0):
            return jnp.dot(h, win_ref[:, c0:c0 + 512], preferred_element_type=jnp.float32)

        va = _layernorm(_gelu(proj(C_V)), lng_ref[...], lnb_ref[...])
        vanew_ref[...] = va
        va_s[...] = va.astype(jnp.bfloat16)
        usg_s[...] = _gelu(proj(C_U)) * _silu(proj(C_GA))
        _store_q(proj(C_Q), qe_s, qo_s)
        k = proj(C_K)
        v = proj(C_VV)
        knew_ref[...] = _split_heads(k)
        vnew_ref[...] = _split_heads(v)
        k_s[...] = k.astype(jnp.bfloat16)
        v_s[...] = v.astype(jnp.bfloat16)
        sgb_s[...] = _silu(proj(C_GB))

    kc_s[...] = _merge_heads(ck_ref[0]).astype(jnp.bfloat16)
    vc_s[...] = _merge_heads(cv_ref[0]).astype(jnp.bfloat16)

    r0 = pl.multiple_of(b * S, S)
    rows = pl.ds(r0, S)
    for p in range(N_PAIRS):
        lanes = slice(p * LANES, (p + 1) * LANES)
        mix = jnp.dot(wt_s[p], va_s[rows, lanes], preferred_element_type=jnp.float32)
        mixed = _pair_select(mix, S) + bfull_ref[0:S, lanes]
        o_s[rows, lanes] = (usg_s[rows, lanes] * mixed).astype(jnp.bfloat16)

        qs = jnp.concatenate([qe_s[rows, lanes], qo_s[rows, lanes]], axis=0)
        kc = kc_s[:, lanes]
        vc = vc_s[:, lanes]
        kn = k_s[rows, lanes]
        vn = v_s[rows, lanes]
        s1 = lax.dot_general(qs, kc, _CONTRACT_LANES, preferred_element_type=jnp.float32)
        s2 = lax.dot_general(qs, kn, _CONTRACT_LANES, preferred_element_type=jnp.float32)
        bias = jnp.concatenate([bias_ref[p, 0], bias_ref[p, 1]], axis=0)
        s1 = s1 + bias[:, 0:KV_WIN]
        s2 = s2 + bias[:, KV_WIN:KV_WIN + S]
        m = jnp.maximum(jnp.max(s1, axis=-1, keepdims=True), jnp.max(s2, axis=-1, keepdims=True))
        e1 = jnp.exp2(s1 - m)
        e2 = jnp.exp2(s2 - m)
        l = jnp.sum(e1, axis=-1, keepdims=True) + jnp.sum(e2, axis=-1, keepdims=True)
        o2 = (jnp.dot(e1.astype(jnp.bfloat16), vc, preferred_element_type=jnp.float32)
              + jnp.dot(e2.astype(jnp.bfloat16), vn, preferred_element_type=jnp.float32))
        o2 = o2 * (1.0 / l)
        yb = _pair_select(o2, S) * sgb_s[rows, lanes]
        o_s[rows, B_WIDTH + p * LANES:B_WIDTH + (p + 1) * LANES] = yb.astype(jnp.bfloat16)

    @pl.when(b == n_streams - 1)
    def _():
        o = jnp.dot(o_s[...], wout_ref[...], preferred_element_type=jnp.float32)
        on = _rms_rows(o) * gpost_ref[...]
        for i in range(n_streams):
            sl = slice(i * S, (i + 1) * S)
            y_ref[sl, :] = x_ref[sl, :] + mod_ref[i, 2:3, :] * on[sl, :]


def _sample_call(x, mod, g_pre, w_in, ln_g, ln_b, w_s, b_full, bias_s, w_out, g_post, ck, cv):
    n_streams, n_new, _ = x.shape
    assert n_new % BF16_ROWS == 0 and n_new <= CHUNK
    rows = n_streams * n_new
    f32, bf16 = jnp.float32, jnp.bfloat16

    def const(shape):
        nd = len(shape)
        return pl.BlockSpec(shape, lambda b: (0,) * nd, pipeline_mode=pl.Buffered(1))

    kern = functools.partial(_sample_kernel, n_streams=n_streams, n_new=n_new)
    return pl.pallas_call(
        kern,
        grid=(n_streams,),
        in_specs=[
            const((rows, D_MODEL)),
            const((n_streams, 3, D_MODEL)),
            const((1, D_MODEL)),
            const((D_MODEL, D_IN)),
            const((1, A_WIDTH)),
            const((1, A_WIDTH)),
            const((A_GROUPS, MLP_CHUNK, MLP_CHUNK)),
            const((MLP_CHUNK, A_WIDTH)),
            const((N_PAIRS, 2, n_new, K_BLOCK)),
            const((D_MODEL, D_MODEL)),
            const((1, D_MODEL)),
            pl.BlockSpec((1, KV_WIN, B_HEADS, B_HDIM), lambda b: (b, 0, 0, 0)),
            pl.BlockSpec((1, KV_WIN, B_HEADS, B_HDIM), lambda b: (b, 0, 0, 0)),
        ],
        out_specs=[
            pl.BlockSpec((rows, D_MODEL), lambda b: (0, 0)),
            pl.BlockSpec((rows, B_HEADS, B_HDIM), lambda b: (0, 0, 0)),
            pl.BlockSpec((rows, B_HEADS, B_HDIM), lambda b: (0, 0, 0)),
            pl.BlockSpec((rows, A_WIDTH), lambda b: (0, 0)),
        ],
        out_shape=[
            jax.ShapeDtypeStruct((rows, D_MODEL), f32),
            jax.ShapeDtypeStruct((rows, B_HEADS, B_HDIM), f32),
            jax.ShapeDtypeStruct((rows, B_HEADS, B_HDIM), f32),
            jax.ShapeDtypeStruct((rows, A_WIDTH), f32),
        ],
        scratch_shapes=[
            pltpu.VMEM((rows, A_WIDTH), f32),
            pltpu.VMEM((rows, A_WIDTH), bf16),
            pltpu.VMEM((rows, B_WIDTH), bf16),
            pltpu.VMEM((rows, B_WIDTH), bf16),
            pltpu.VMEM((rows, B_WIDTH), bf16),
            pltpu.VMEM((rows, B_WIDTH), bf16),
            pltpu.VMEM((rows, B_WIDTH), f32),
            pltpu.VMEM((rows, D_MODEL), bf16),
            pltpu.VMEM((N_PAIRS, 2 * n_new, n_new), bf16),
            pltpu.VMEM((KV_WIN, B_WIDTH), bf16),
            pltpu.VMEM((KV_WIN, B_WIDTH), bf16),
        ],
        compiler_params=pltpu.CompilerParams(
            dimension_semantics=("arbitrary",),
            vmem_limit_bytes=VMEM_LIMIT_BYTES),
        name="sample",
    )(x.reshape(rows, D_MODEL), mod, g_pre, w_in, ln_g, ln_b, w_s, b_full, bias_s, w_out, g_post,
      ck, cv)


def _bias_base(rel_bias):
    n_far = KV_WIN - REL_CLIP
    far = jnp.broadcast_to(rel_bias[:, N_REL - 1:N_REL], (B_HEADS, n_far))
    near = rel_bias[:, N_REL - 1:0:-1]
    return jnp.concatenate([far, near], axis=1)


def kernel(x_prompt, x_sample, cache_attn_k, cache_attn_v, c_prompt, c_sample, g_pre, w_ada, b_ada,
           w_in, ln_g, ln_b, w_s, b_s, rel_bias, w_out, g_post):
    depth = g_pre.shape[0]
    bsz, seq, _ = x_prompt.shape
    n_streams, n_new, _ = x_sample.shape
    win = cache_attn_k.shape[2]
    assert win == KV_WIN and bsz + n_streams <= ADA_ROWS

    yp, ys = x_prompt, x_sample
    kp_rows, vp_rows, ks_rows, vs_rows, va_rows = [], [], [], [], []
    c_all = jnp.concatenate(
        [c_prompt, c_sample, jnp.zeros((ADA_ROWS - bsz - n_streams, D_MODEL), c_prompt.dtype)], axis=0)
    for l in range(depth):
        mod, bias_t, bias_s, w_in_b, w_out_b = _prep_call(
            c_all, w_ada[l], b_ada[l], _bias_base(rel_bias[l]), w_in[l], w_out[l], n_new)
        mod = mod.reshape(ADA_ROWS, 3, D_MODEL)
        b_full = jnp.repeat(b_s[l].T, A_GDIM, axis=1)
        pre = (g_pre[l].reshape(1, D_MODEL), w_in_b, ln_g[l].reshape(1, A_WIDTH),
               ln_b[l].reshape(1, A_WIDTH), w_s[l], b_full)
        post = (w_out_b, g_post[l].reshape(1, D_MODEL))

        yp, k_last, v_last = _prompt_call(yp, mod[:bsz], *pre, bias_t, *post)
        kp_rows.append(k_last)
        vp_rows.append(v_last)

        ys2, k_new, v_new, va_new = _sample_call(ys, mod[bsz:bsz + n_streams], *pre, bias_s, *post,
                                                 cache_attn_k[l], cache_attn_v[l])
        ys = ys2.reshape(n_streams, n_new, D_MODEL)
        ks_rows.append(k_new.reshape(n_streams, n_new, B_HEADS, B_HDIM))
        vs_rows.append(v_new.reshape(n_streams, n_new, B_HEADS, B_HDIM))
        va_rows.append(va_new.reshape(n_streams, n_new, A_WIDTH))

    return (yp, ys, jnp.stack(kp_rows), jnp.stack(vp_rows), jnp.stack(ks_rows), jnp.stack(vs_rows),
            jnp.stack(va_rows))
```

```python
import functools

import jax
import jax.numpy as jnp
from jax import lax
from jax.experimental import pallas as pl
from jax.experimental.pallas import tpu as pltpu

LANES = 128
SUBLANES = 8
BF16_ROWS = 16
VMEM_LIMIT_BYTES = 56 * 1024 * 1024

D_MODEL = 1024
A_WIDTH = 512
A_GROUPS = 8
A_GDIM = A_WIDTH // A_GROUPS
MLP_CHUNK = 128
B_WIDTH = 512
B_HEADS = 8
B_HDIM = B_WIDTH // B_HEADS
CHUNK = 64
KV_WIN = 512
REL_CLIP = 128
N_REL = 2 * REL_CLIP + 1
EPS = 1e-6
D_IN = 3 * A_WIDTH + 4 * B_WIDTH
NEG = -1e30
LOG2E = 1.4426950408889634
Q_SCALE = B_HDIM ** -0.5 * LOG2E

N_PAIRS = B_HEADS // 2
Q_BLOCK = 2 * CHUNK
K_BLOCK = KV_WIN + Q_BLOCK
SEQ_TILE = 512
SCORE_LEAD = 3
SCORE_SLOTS = SCORE_LEAD + 1
ADA_ROWS = 16

C_U, C_V, C_GA, C_Q, C_K, C_VV, C_GB = (i * 512 for i in range(7))

_CONTRACT_LANES = (((1,), (1,)), ((), ()))

_TO_FEATURE_MAJOR = (0, 2, 3, 1)
_FROM_FEATURE_MAJOR = (0, 3, 1, 2)


def _sigmoid_exp2(neg_arg_log2):
    return 1.0 / (1.0 + jnp.exp2(neg_arg_log2))


def _gelu(x):
    c1 = -2.0 * (2.0 / jnp.pi) ** 0.5 * LOG2E
    return x * _sigmoid_exp2(x * (c1 + (c1 * 0.044715) * (x * x)))


def _silu(x):
    return x * _sigmoid_exp2(x * (-LOG2E))


def _even_lane_mask(shape):
    lane = lax.broadcasted_iota(jnp.int32, shape, len(shape) - 1)
    return (lane % LANES) < B_HDIM


def _ada_block(c_ref, w_ref, b_ref, o_ref):
    c = c_ref[...]
    o_ref[...] = jnp.dot(_silu(c), w_ref[...], preferred_element_type=jnp.float32) + b_ref[...]


def _relb_block(base_ref, bt_ref, bs_ref, n_new):
    row = lax.broadcasted_iota(jnp.int32, (Q_BLOCK, K_BLOCK), 0)
    col = lax.broadcasted_iota(jnp.int32, (Q_BLOCK, K_BLOCK), 1)
    band_lo = jnp.where(row < CHUNK, 0, CHUNK)
    rel = col - band_lo
    outside = jnp.logical_or(rel < 0, rel >= KV_WIN + CHUNK)
    for par in range(2):
        base = base_ref[0, par:par + 1, :] * LOG2E
        x = jnp.broadcast_to(base, (Q_BLOCK, K_BLOCK))
        shift = 1
        while shift < Q_BLOCK:
            x = jnp.where((row & shift) != 0, jnp.roll(x, shift, axis=1), x)
            shift *= 2
        far = jnp.broadcast_to(base[:, 0:1], (Q_BLOCK, K_BLOCK))
        x = jnp.where(col < row, far, x)
        bs_ref[0, par] = x[0:n_new, :]
        bt_ref[0, :, par * Q_BLOCK:(par + 1) * Q_BLOCK] = jnp.where(outside, NEG, x).T


PREP_COLS = 512
N_PREP = D_IN // PREP_COLS
N_ADA = 3 * D_MODEL // PREP_COLS
WOUT_COLS = D_MODEL // N_PAIRS


def _prep_kernel(c_ref, wada_ref, bada_ref, base_ref, win_ref, wout_ref,
                 mod_ref, bt_ref, bs_ref, winb_ref, woutb_ref, *, n_new):
    j = pl.program_id(0)
    winb_ref[...] = win_ref[...].astype(jnp.bfloat16)

    @pl.when(j < N_ADA)
    def _():
        _ada_block(c_ref, wada_ref, bada_ref, mod_ref)

    @pl.when(j < N_PAIRS)
    def _():
        woutb_ref[...] = wout_ref[...].astype(jnp.bfloat16)
        _relb_block(base_ref, bt_ref, bs_ref, n_new)


def _prep_call(c_all, w_ada, b_ada, base, w_in, w_out, n_new):
    f32, bf16 = jnp.float32, jnp.bfloat16

    def upto(n):
        return lambda j: jnp.minimum(j, n - 1)

    ada_j, pair_j = upto(N_ADA), upto(N_PAIRS)
    return pl.pallas_call(
        functools.partial(_prep_kernel, n_new=n_new),
        grid=(N_PREP,),
        in_specs=[
            pl.BlockSpec((ADA_ROWS, D_MODEL), lambda j: (0, 0)),
            pl.BlockSpec((D_MODEL, PREP_COLS), lambda j: (0, ada_j(j))),
            pl.BlockSpec((1, PREP_COLS), lambda j: (0, ada_j(j))),
            pl.BlockSpec((1, 2, K_BLOCK), lambda j: (pair_j(j), 0, 0)),
            pl.BlockSpec((D_MODEL, PREP_COLS), lambda j: (0, j)),
            pl.BlockSpec((D_MODEL, WOUT_COLS), lambda j: (0, pair_j(j))),
        ],
        out_specs=[
            pl.BlockSpec((ADA_ROWS, PREP_COLS), lambda j: (0, ada_j(j))),
            pl.BlockSpec((1, K_BLOCK, 2 * Q_BLOCK), lambda j: (pair_j(j), 0, 0)),
            pl.BlockSpec((1, 2, n_new, K_BLOCK), lambda j: (pair_j(j), 0, 0, 0)),
            pl.BlockSpec((D_MODEL, PREP_COLS), lambda j: (0, j)),
            pl.BlockSpec((D_MODEL, WOUT_COLS), lambda j: (0, pair_j(j))),
        ],
        out_shape=[
            jax.ShapeDtypeStruct((ADA_ROWS, 3 * D_MODEL), f32),
            jax.ShapeDtypeStruct((N_PAIRS, K_BLOCK, 2 * Q_BLOCK), f32),
            jax.ShapeDtypeStruct((N_PAIRS, 2, n_new, K_BLOCK), f32),
            jax.ShapeDtypeStruct((D_MODEL, D_IN), bf16),
            jax.ShapeDtypeStruct((D_MODEL, D_MODEL), bf16),
        ],
        compiler_params=pltpu.CompilerParams(dimension_semantics=("arbitrary",)),
        name="prep",
    )(c_all, w_ada, b_ada.reshape(1, 3 * D_MODEL), base.reshape(N_PAIRS, 2, K_BLOCK), w_in, w_out)


def _tril_pairs(ws_ref, wt_ref, n):
    row = lax.broadcasted_iota(jnp.int32, (n, n), 0)
    col = lax.broadcasted_iota(jnp.int32, (n, n), 1)
    keep = col <= row
    for g in range(A_GROUPS):
        w = jnp.where(keep, ws_ref[g, :n, :n], 0.0)
        wt_ref[g // 2, (g % 2) * n:(g % 2 + 1) * n, :] = w.astype(jnp.bfloat16)


def _rms_rows(x):
    return x * lax.rsqrt(jnp.mean(x * x, axis=-1, keepdims=True) + EPS)


def _layernorm(x, g, b):
    mu = jnp.mean(x, axis=-1, keepdims=True)
    xc = x - mu
    var = jnp.mean(xc * xc, axis=-1, keepdims=True)
    return xc * lax.rsqrt(var + EPS) * g + b


def _pair_select(stacked, n):
    return jnp.where(_even_lane_mask((n, LANES)), stacked[:n], stacked[n:])


def _split_heads(x):
    return x.reshape(x.shape[0], B_HEADS, B_HDIM)


def _to_feature_major(x):
    return x.T.reshape(B_HEADS, B_HDIM, x.shape[0])


def _store_q(q, qe_ref, qo_ref):
    q = (q * Q_SCALE).astype(jnp.bfloat16)
    even = _even_lane_mask(q.shape)
    zero = jnp.zeros((), jnp.bfloat16)
    qe_ref[...] = jnp.where(even, q, zero)
    qo_ref[...] = jnp.where(even, zero, q)


def _prompt_kernel(x_ref, mod_ref, gpre_ref, win_ref, lng_ref, lnb_ref, ws_ref, bfull_ref,
                   biast_ref, wout_ref, gpost_ref,
                   y_ref, klast_ref, vlast_ref,
                   h_s, usg_s, va_s, qe_s, qo_s, k_s, vt_s, sgb_s, o_s, wt_s, st_s, kf_s, vf_s):
    t = pl.program_id(1)
    nt = pl.num_programs(1)
    T = SEQ_TILE

    @pl.when(jnp.logical_and(pl.program_id(0) == 0, t == 0))
    def _():
        _tril_pairs(ws_ref, wt_s, MLP_CHUNK)

    shift = mod_ref[0, 0:1, :]
    scale = mod_ref[0, 1:2, :]
    gate = mod_ref[0, 2:3, :]

    row_halves = [slice(i * (T // 2), (i + 1) * (T // 2)) for i in range(2)]
    pre_scale = gpre_ref[...] * (1.0 + scale)
    for rows in row_halves:
        h_s[rows, :] = (_rms_rows(x_ref[0, rows, :]) * pre_scale + shift).astype(jnp.bfloat16)

    def proj(c0, rows=slice(None)):
        return jnp.dot(h_s[rows, :], win_ref[:, c0:c0 + 512], preferred_element_type=jnp.float32)

    for rows in row_halves:
        usg_s[rows, :] = _gelu(proj(C_U, rows))
    k = proj(C_K)
    k_s[KV_WIN:KV_WIN + T, :] = k.astype(jnp.bfloat16)
    va_s[...] = _layernorm(_gelu(proj(C_V)), lng_ref[...], lnb_ref[...]).astype(jnp.bfloat16)
    v = proj(C_VV)
    vt_s[:, KV_WIN:KV_WIN + T] = v.T.astype(jnp.bfloat16)
    usg_s[...] = usg_s[...] * _silu(proj(C_GA))
    _store_q(proj(C_Q), qe_s, qo_s)
    sgb_s[...] = _silu(proj(C_GB))
    kf_s[...] = k[T - KV_WIN:, :]
    vf_s[...] = v[T - KV_WIN:, :]

    for c in range(T // MLP_CHUNK):
        rows = slice(c * MLP_CHUNK, (c + 1) * MLP_CHUNK)
        for p in range(N_PAIRS):
            lanes = slice(p * LANES, (p + 1) * LANES)
            mix = jnp.dot(wt_s[p], va_s[rows, lanes], preferred_element_type=jnp.float32)
            mixed = _pair_select(mix, MLP_CHUNK) + bfull_ref[:, lanes]
            o_s[rows, lanes] = (usg_s[rows, lanes] * mixed).astype(jnp.bfloat16)

    def key_lo(qb, first_tile):
        return KV_WIN - qb * Q_BLOCK if first_tile else 0

    def scores(qb, p, slot, first_tile):
        r0, lo = qb * Q_BLOCK, key_lo(qb, first_tile)
        lanes = slice(p * LANES, (p + 1) * LANES)
        qs = jnp.concatenate([qe_s[r0:r0 + Q_BLOCK, lanes],
                              qo_s[r0:r0 + Q_BLOCK, lanes]], axis=0)
        half = (K_BLOCK - lo) // 2
        for a in (lo, lo + half):
            st_s[slot, a:a + half, :] = lax.dot_general(
                k_s[r0 + a:r0 + a + half, lanes], qs, _CONTRACT_LANES,
                preferred_element_type=jnp.float32) + biast_ref[p, a:a + half, :]

    def finish(qb, p, slot, first_tile):
        r0, lo = qb * Q_BLOCK, key_lo(qb, first_tile)
        st = st_s[slot, lo:K_BLOCK, :]
        m = jnp.max(st, axis=0, keepdims=True)
        pt = jnp.exp2(st - m).astype(jnp.bfloat16)
        ones = jnp.ones((BF16_ROWS, K_BLOCK - lo), jnp.bfloat16)
        ots = []
        for par in range(2):
            feats = slice(p * LANES + par * B_HDIM, p * LANES + (par + 1) * B_HDIM)
            qcols = slice(par * Q_BLOCK, (par + 1) * Q_BLOCK)
            lhs = jnp.concatenate([vt_s[feats, r0 + lo:r0 + K_BLOCK], ones], axis=0)
            ot = jnp.dot(lhs, pt[:, qcols], preferred_element_type=jnp.float32)
            ots.append(ot[0:B_HDIM] * (1.0 / ot[B_HDIM:B_HDIM + 1]))
        yb = jnp.concatenate(ots, axis=0).T * sgb_s[r0:r0 + Q_BLOCK, p * LANES:(p + 1) * LANES]
        o_s[r0:r0 + Q_BLOCK, B_WIDTH + p * LANES:B_WIDTH + (p + 1) * LANES] = (
            yb.astype(jnp.bfloat16))

    def attend(first_tile):
        items = [(qb, p) for qb in range(T // Q_BLOCK) for p in range(N_PAIRS)]
        for j in range(SCORE_LEAD):
            scores(*items[j], j % SCORE_SLOTS, first_tile)
        for i, (qb, p) in enumerate(items):
            if i + SCORE_LEAD < len(items):
                scores(*items[i + SCORE_LEAD], (i + SCORE_LEAD) % SCORE_SLOTS, first_tile)
            finish(qb, p, i % SCORE_SLOTS, first_tile)

    @pl.when(t == 0)
    def _():
        attend(True)

    @pl.when(t > 0)
    def _():
        attend(False)

    k_s[0:KV_WIN, :] = k_s[T:T + KV_WIN, :]
    vt_s[:, 0:KV_WIN] = vt_s[:, T:T + KV_WIN]

    post_scale = gate * gpost_ref[...]
    for rows in row_halves:
        o = jnp.dot(o_s[rows, :], wout_ref[...], preferred_element_type=jnp.float32)
        y_ref[0, rows, :] = x_ref[0, rows, :] + _rms_rows(o) * post_scale

    @pl.when(t == nt - 1)
    def _():
        klast_ref[0] = _to_feature_major(kf_s[...])
        vlast_ref[0] = _to_feature_major(vf_s[...])


def _const_spec(shape):
    nd = len(shape)
    return pl.BlockSpec(shape, lambda b, t: (0,) * nd, pipeline_mode=pl.Buffered(1))


def _prompt_call(x, mod, g_pre, w_in, ln_g, ln_b, w_s, b_full, bias_t, w_out, g_post):
    bsz, seq, _ = x.shape
    T = SEQ_TILE
    assert seq % T == 0 and T >= KV_WIN
    nt = seq // T
    f32, bf16 = jnp.float32, jnp.bfloat16
    return pl.pallas_call(
        _prompt_kernel,
        grid=(bsz, nt),
        in_specs=[
            pl.BlockSpec((1, T, D_MODEL), lambda b, t: (b, t, 0)),
            pl.BlockSpec((1, 3, D_MODEL), lambda b, t: (b, 0, 0)),
            _const_spec((1, D_MODEL)),
            _const_spec((D_MODEL, D_IN)),
            _const_spec((1, A_WIDTH)),
            _const_spec((1, A_WIDTH)),
            _const_spec((A_GROUPS, MLP_CHUNK, MLP_CHUNK)),
            _const_spec((MLP_CHUNK, A_WIDTH)),
            _const_spec((N_PAIRS, K_BLOCK, 2 * Q_BLOCK)),
            _const_spec((D_MODEL, D_MODEL)),
            _const_spec((1, D_MODEL)),
        ],
        out_specs=[
            pl.BlockSpec((1, T, D_MODEL), lambda b, t: (b, t, 0)),
            pl.BlockSpec((1, B_HEADS, B_HDIM, KV_WIN), lambda b, t: (b, 0, 0, 0)),
            pl.BlockSpec((1, B_HEADS, B_HDIM, KV_WIN), lambda b, t: (b, 0, 0, 0)),
        ],
        out_shape=[
            jax.ShapeDtypeStruct((bsz, seq, D_MODEL), f32),
            jax.ShapeDtypeStruct((bsz, B_HEADS, B_HDIM, KV_WIN), f32),
            jax.ShapeDtypeStruct((bsz, B_HEADS, B_HDIM, KV_WIN), f32),
        ],
        scratch_shapes=[
            pltpu.VMEM((T, D_MODEL), bf16),
            pltpu.VMEM((T, A_WIDTH), f32),
            pltpu.VMEM((T, A_WIDTH), bf16),
            pltpu.VMEM((T, B_WIDTH), bf16),
            pltpu.VMEM((T, B_WIDTH), bf16),
            pltpu.VMEM((KV_WIN + T, B_WIDTH), bf16),
            pltpu.VMEM((B_WIDTH, KV_WIN + T), bf16),
            pltpu.VMEM((T, B_WIDTH), f32),
            pltpu.VMEM((T, D_MODEL), bf16),
            pltpu.VMEM((N_PAIRS, 2 * MLP_CHUNK, MLP_CHUNK), bf16),
            pltpu.VMEM((SCORE_SLOTS, K_BLOCK, 2 * Q_BLOCK), f32),
            pltpu.VMEM((KV_WIN, B_WIDTH), f32),
            pltpu.VMEM((KV_WIN, B_WIDTH), f32),
        ],
        compiler_params=pltpu.CompilerParams(
            dimension_semantics=("arbitrary", "arbitrary"),
            vmem_limit_bytes=VMEM_LIMIT_BYTES),
        name="prompt",
    )(x, mod, g_pre, w_in, ln_g, ln_b, w_s, b_full, bias_t, w_out, g_post)


def _sample_kernel(x_ref, mod_ref, gpre_ref, win_ref, lng_ref, lnb_ref, ws_ref, bfull_ref,
                   bias_ref, wout_ref, gpost_ref, ck_ref, cv_ref,
                   y_ref, knew_ref, vnew_ref, vanew_ref,
                   usg_s, va_s, qe_s, qo_s, k_s, v_s, sgb_s, o_s, wt_s, *, n_streams, n_new):
    b = pl.program_id(0)
    S = n_new

    @pl.when(b == 0)
    def _():
        _tril_pairs(ws_ref, wt_s, S)
        gpre = gpre_ref[...]
        hs = []
        for i in range(n_streams):
            xi = x_ref[i * S:(i + 1) * S, :]
            hs.append(_rms_rows(xi) * (gpre * (1.0 + mod_ref[i, 1:2, :])) + mod_ref[i, 0:1, :])
        h = jnp.concatenate(hs, axis=0).astype(jnp.bfloat16)

        def proj(c0):
            return jnp.dot(h, win_ref[:, c0:c0 + 512], preferred_element_type=jnp.float32)

        va = _layernorm(_gelu(proj(C_V)), lng_ref[...], lnb_ref[...])
        vanew_ref[...] = va
        va_s[...] = va.astype(jnp.bfloat16)
        usg_s[...] = _gelu(proj(C_U)) * _silu(proj(C_GA))
        _store_q(proj(C_Q), qe_s, qo_s)
        k = proj(C_K)
        v = proj(C_VV)
        knew_ref[...] = _split_heads(k)
        vnew_ref[...] = _split_heads(v)
        k_s[...] = k.astype(jnp.bfloat16)
        v_s[...] = v.astype(jnp.bfloat16)
        sgb_s[...] = _silu(proj(C_GB))

    r0 = pl.multiple_of(b * S, S)
    rows = pl.ds(r0, S)
    for p in range(N_PAIRS):
        lanes = slice(p * LANES, (p + 1) * LANES)
        mix = jnp.dot(wt_s[p], va_s[rows, lanes], preferred_element_type=jnp.float32)
        mixed = _pair_select(mix, S) + bfull_ref[0:S, lanes]
        o_s[rows, lanes] = (usg_s[rows, lanes] * mixed).astype(jnp.bfloat16)

        qs = jnp.concatenate([qe_s[rows, lanes], qo_s[rows, lanes]], axis=0)
        kct = ck_ref[0, 2 * p:2 * p + 2].reshape(LANES, KV_WIN).astype(jnp.bfloat16)
        vct = cv_ref[0, 2 * p:2 * p + 2].reshape(LANES, KV_WIN).astype(jnp.bfloat16)
        kn = k_s[rows, lanes]
        vn = v_s[rows, lanes]
        s1 = jnp.dot(qs, kct, preferred_element_type=jnp.float32)
        s2 = lax.dot_general(qs, kn, _CONTRACT_LANES, preferred_element_type=jnp.float32)
        bias = jnp.concatenate([bias_ref[p, 0], bias_ref[p, 1]], axis=0)
        s1 = s1 + bias[:, 0:KV_WIN]
        s2 = s2 + bias[:, KV_WIN:KV_WIN + S]
        m = jnp.maximum(jnp.max(s1, axis=-1, keepdims=True), jnp.max(s2, axis=-1, keepdims=True))
        e1 = jnp.exp2(s1 - m)
        e2 = jnp.exp2(s2 - m)
        l = jnp.sum(e1, axis=-1, keepdims=True) + jnp.sum(e2, axis=-1, keepdims=True)
        o2 = (lax.dot_general(e1.astype(jnp.bfloat16), vct, _CONTRACT_LANES,
                              preferred_element_type=jnp.float32)
              + jnp.dot(e2.astype(jnp.bfloat16), vn, preferred_element_type=jnp.float32))
        o2 = o2 * (1.0 / l)
        yb = _pair_select(o2, S) * sgb_s[rows, lanes]
        o_s[rows, B_WIDTH + p * LANES:B_WIDTH + (p + 1) * LANES] = yb.astype(jnp.bfloat16)

    @pl.when(b == n_streams - 1)
    def _():
        o = jnp.dot(o_s[...], wout_ref[...], preferred_element_type=jnp.float32)
        on = _rms_rows(o) * gpost_ref[...]
        for i in range(n_streams):
            sl = slice(i * S, (i + 1) * S)
            y_ref[sl, :] = x_ref[sl, :] + mod_ref[i, 2:3, :] * on[sl, :]


def _sample_call(x, mod, g_pre, w_in, ln_g, ln_b, w_s, b_full, bias_s, w_out, g_post, ck, cv):
    n_streams, n_new, _ = x.shape
    assert n_new % BF16_ROWS == 0 and n_new <= CHUNK
    rows = n_streams * n_new
    f32, bf16 = jnp.float32, jnp.bfloat16

    def const(shape):
        nd = len(shape)
        return pl.BlockSpec(shape, lambda b: (0,) * nd, pipeline_mode=pl.Buffered(1))

    kern = functools.partial(_sample_kernel, n_streams=n_streams, n_new=n_new)
    return pl.pallas_call(
        kern,
        grid=(n_streams,),
        in_specs=[
            const((rows, D_MODEL)),
            const((n_streams, 3, D_MODEL)),
            const((1, D_MODEL)),
            const((D_MODEL, D_IN)),
            const((1, A_WIDTH)),
            const((1, A_WIDTH)),
            const((A_GROUPS, MLP_CHUNK, MLP_CHUNK)),
            const((MLP_CHUNK, A_WIDTH)),
            const((N_PAIRS, 2, n_new, K_BLOCK)),
            const((D_MODEL, D_MODEL)),
            const((1, D_MODEL)),
            pl.BlockSpec((1, B_HEADS, B_HDIM, KV_WIN), lambda b: (b, 0, 0, 0)),
            pl.BlockSpec((1, B_HEADS, B_HDIM, KV_WIN), lambda b: (b, 0, 0, 0)),
        ],
        out_specs=[
            pl.BlockSpec((rows, D_MODEL), lambda b: (0, 0)),
            pl.BlockSpec((rows, B_HEADS, B_HDIM), lambda b: (0, 0, 0)),
            pl.BlockSpec((rows, B_HEADS, B_HDIM), lambda b: (0, 0, 0)),
            pl.BlockSpec((rows, A_WIDTH), lambda b: (0, 0)),
        ],
        out_shape=[
            jax.ShapeDtypeStruct((rows, D_MODEL), f32),
            jax.ShapeDtypeStruct((rows, B_HEADS, B_HDIM), f32),
            jax.ShapeDtypeStruct((rows, B_HEADS, B_HDIM), f32),
            jax.ShapeDtypeStruct((rows, A_WIDTH), f32),
        ],
        scratch_shapes=[
            pltpu.VMEM((rows, A_WIDTH), f32),
            pltpu.VMEM((rows, A_WIDTH), bf16),
            pltpu.VMEM((rows, B_WIDTH), bf16),
            pltpu.VMEM((rows, B_WIDTH), bf16),
            pltpu.VMEM((rows, B_WIDTH), bf16),
            pltpu.VMEM((rows, B_WIDTH), bf16),
            pltpu.VMEM((rows, B_WIDTH), f32),
            pltpu.VMEM((rows, D_MODEL), bf16),
            pltpu.VMEM((N_PAIRS, 2 * n_new, n_new), bf16),
        ],
        compiler_params=pltpu.CompilerParams(
            dimension_semantics=("arbitrary",),
            vmem_limit_bytes=VMEM_LIMIT_BYTES),
        name="sample",
    )(x.reshape(rows, D_MODEL), mod, g_pre, w_in, ln_g, ln_b, w_s, b_full, bias_s, w_out, g_post,
      ck, cv)


def _bias_base(rel_bias):
    n_far = KV_WIN - REL_CLIP
    far = jnp.broadcast_to(rel_bias[:, N_REL - 1:N_REL], (B_HEADS, n_far))
    near = rel_bias[:, N_REL - 1:0:-1]
    return jnp.concatenate([far, near], axis=1)


def kernel(x_prompt, x_sample, cache_attn_k, cache_attn_v, c_prompt, c_sample, g_pre, w_ada, b_ada,
           w_in, ln_g, ln_b, w_s, b_s, rel_bias, w_out, g_post):
    depth = g_pre.shape[0]
    bsz, seq, _ = x_prompt.shape
    n_streams, n_new, _ = x_sample.shape
    win = cache_attn_k.shape[2]
    assert win == KV_WIN and bsz + n_streams <= ADA_ROWS

    yp, ys = x_prompt, x_sample
    kp_rows, vp_rows, ks_rows, vs_rows, va_rows = [], [], [], [], []
    c_all = jnp.concatenate(
        [c_prompt, c_sample, jnp.zeros((ADA_ROWS - bsz - n_streams, D_MODEL), c_prompt.dtype)], axis=0)
    for l in range(depth):
        mod, bias_t, bias_s, w_in_b, w_out_b = _prep_call(
            c_all, w_ada[l], b_ada[l], _bias_base(rel_bias[l]), w_in[l], w_out[l], n_new)
        mod = mod.reshape(ADA_ROWS, 3, D_MODEL)
        b_full = jnp.repeat(b_s[l].T, A_GDIM, axis=1)
        pre = (g_pre[l].reshape(1, D_MODEL), w_in_b, ln_g[l].reshape(1, A_WIDTH),
               ln_b[l].reshape(1, A_WIDTH), w_s[l], b_full)
        post = (w_out_b, g_post[l].reshape(1, D_MODEL))

        yp, k_last, v_last = _prompt_call(yp, mod[:bsz], *pre, bias_t, *post)
        kp_rows.append(jnp.transpose(k_last, _FROM_FEATURE_MAJOR))
        vp_rows.append(jnp.transpose(v_last, _FROM_FEATURE_MAJOR))

        ys2, k_new, v_new, va_new = _sample_call(ys, mod[bsz:bsz + n_streams], *pre, bias_s, *post,
                                                 jnp.transpose(cache_attn_k[l], _TO_FEATURE_MAJOR),
                                                 jnp.transpose(cache_attn_v[l], _TO_FEATURE_MAJOR))
        ys = ys2.reshape(n_streams, n_new, D_MODEL)
        ks_rows.append(k_new.reshape(n_streams, n_new, B_HEADS, B_HDIM))
        vs_rows.append(v_new.reshape(n_streams, n_new, B_HEADS, B_HDIM))
        va_rows.append(va_new.reshape(n_streams, n_new, A_WIDTH))

    return (yp, ys, jnp.stack(kp_rows), jnp.stack(vp_rows), jnp.stack(ks_rows), jnp.stack(vs_rows),
            jnp.stack(va_rows))
```

```python
import functools

import jax
import jax.numpy as jnp
from jax import lax
from jax.experimental import pallas as pl
from jax.experimental.pallas import tpu as pltpu

LANES = 128
SUBLANES = 8
BF16_ROWS = 16
VMEM_LIMIT_BYTES = 60 * 1024 * 1024

D_MODEL = 1024
A_WIDTH = 512
A_GROUPS = 8
A_GDIM = A_WIDTH // A_GROUPS
MLP_CHUNK = 128
B_WIDTH = 512
B_HEADS = 8
B_HDIM = B_WIDTH // B_HEADS
CHUNK = 64
KV_WIN = 512
REL_CLIP = 128
N_REL = 2 * REL_CLIP + 1
EPS = 1e-6
D_IN = 3 * A_WIDTH + 4 * B_WIDTH
NEG = -1e30
LOG2E = 1.4426950408889634
Q_SCALE = B_HDIM ** -0.5 * LOG2E

N_PAIRS = B_HEADS // 2
Q_BLOCK = 2 * CHUNK
K_BLOCK = KV_WIN + Q_BLOCK
SEQ_TILE = 1024
PROJ_ROWS = 512
POST_ROWS = 256
SCORE_LEAD = 3
SCORE_SLOTS = SCORE_LEAD + 1
ADA_ROWS = 16

C_U, C_V, C_GA, C_Q, C_K, C_VV, C_GB = (i * 512 for i in range(7))

_CONTRACT_LANES = (((1,), (1,)), ((), ()))

_TO_FEATURE_MAJOR = (0, 2, 3, 1)
_FROM_FEATURE_MAJOR = (0, 3, 1, 2)


def _sigmoid_exp2(neg_arg_log2):
    return 1.0 / (1.0 + jnp.exp2(neg_arg_log2))


def _gelu(x):
    c1 = -2.0 * (2.0 / jnp.pi) ** 0.5 * LOG2E
    return x * _sigmoid_exp2(x * (c1 + (c1 * 0.044715) * (x * x)))


def _silu(x):
    return x * _sigmoid_exp2(x * (-LOG2E))


def _even_lane_mask(shape):
    lane = lax.broadcasted_iota(jnp.int32, shape, len(shape) - 1)
    return (lane % LANES) < B_HDIM


def _ada_block(c_ref, w_ref, b_ref, o_ref):
    c = c_ref[...]
    o_ref[...] = jnp.dot(_silu(c), w_ref[...], preferred_element_type=jnp.float32) + b_ref[...]


def _relb_block(base_ref, bt_ref, bs_ref, n_new):
    row = lax.broadcasted_iota(jnp.int32, (Q_BLOCK, K_BLOCK), 0)
    col = lax.broadcasted_iota(jnp.int32, (Q_BLOCK, K_BLOCK), 1)
    band_lo = jnp.where(row < CHUNK, 0, CHUNK)
    rel = col - band_lo
    outside = jnp.logical_or(rel < 0, rel >= KV_WIN + CHUNK)
    for par in range(2):
        base = base_ref[0, par:par + 1, :] * LOG2E
        x = jnp.broadcast_to(base, (Q_BLOCK, K_BLOCK))
        shift = 1
        while shift < Q_BLOCK:
            x = jnp.where((row & shift) != 0, jnp.roll(x, shift, axis=1), x)
            shift *= 2
        far = jnp.broadcast_to(base[:, 0:1], (Q_BLOCK, K_BLOCK))
        x = jnp.where(col < row, far, x)
        bs_ref[0, par] = x[0:n_new, :]
        bt_ref[0, :, par * Q_BLOCK:(par + 1) * Q_BLOCK] = jnp.where(outside, NEG, x).T


PREP_COLS = 512
N_PREP = D_IN // PREP_COLS
N_ADA = 3 * D_MODEL // PREP_COLS
WOUT_COLS = D_MODEL // N_PAIRS


def _prep_kernel(c_ref, wada_ref, bada_ref, base_ref, win_ref, wout_ref,
                 mod_ref, bt_ref, bs_ref, winb_ref, woutb_ref, *, n_new):
    j = pl.program_id(0)
    winb_ref[...] = win_ref[...].astype(jnp.bfloat16)

    @pl.when(j < N_ADA)
    def _():
        _ada_block(c_ref, wada_ref, bada_ref, mod_ref)

    @pl.when(j < N_PAIRS)
    def _():
        woutb_ref[...] = wout_ref[...].astype(jnp.bfloat16)
        _relb_block(base_ref, bt_ref, bs_ref, n_new)


def _prep_call(c_all, w_ada, b_ada, base, w_in, w_out, n_new):
    f32, bf16 = jnp.float32, jnp.bfloat16

    def upto(n):
        return lambda j: jnp.minimum(j, n - 1)

    ada_j, pair_j = upto(N_ADA), upto(N_PAIRS)
    return pl.pallas_call(
        functools.partial(_prep_kernel, n_new=n_new),
        grid=(N_PREP,),
        in_specs=[
            pl.BlockSpec((ADA_ROWS, D_MODEL), lambda j: (0, 0)),
            pl.BlockSpec((D_MODEL, PREP_COLS), lambda j: (0, ada_j(j))),
            pl.BlockSpec((1, PREP_COLS), lambda j: (0, ada_j(j))),
            pl.BlockSpec((1, 2, K_BLOCK), lambda j: (pair_j(j), 0, 0)),
            pl.BlockSpec((D_MODEL, PREP_COLS), lambda j: (0, j)),
            pl.BlockSpec((D_MODEL, WOUT_COLS), lambda j: (0, pair_j(j))),
        ],
        out_specs=[
            pl.BlockSpec((ADA_ROWS, PREP_COLS), lambda j: (0, ada_j(j))),
            pl.BlockSpec((1, K_BLOCK, 2 * Q_BLOCK), lambda j: (pair_j(j), 0, 0)),
            pl.BlockSpec((1, 2, n_new, K_BLOCK), lambda j: (pair_j(j), 0, 0, 0)),
            pl.BlockSpec((D_MODEL, PREP_COLS), lambda j: (0, j)),
            pl.BlockSpec((D_MODEL, WOUT_COLS), lambda j: (0, pair_j(j))),
        ],
        out_shape=[
            jax.ShapeDtypeStruct((ADA_ROWS, 3 * D_MODEL), f32),
            jax.ShapeDtypeStruct((N_PAIRS, K_BLOCK, 2 * Q_BLOCK), f32),
            jax.ShapeDtypeStruct((N_PAIRS, 2, n_new, K_BLOCK), f32),
            jax.ShapeDtypeStruct((D_MODEL, D_IN), bf16),
            jax.ShapeDtypeStruct((D_MODEL, D_MODEL), bf16),
        ],
        compiler_params=pltpu.CompilerParams(dimension_semantics=("arbitrary",)),
        name="prep",
    )(c_all, w_ada, b_ada.reshape(1, 3 * D_MODEL), base.reshape(N_PAIRS, 2, K_BLOCK), w_in, w_out)


def _tril_pairs(ws_ref, wt_ref, n):
    row = lax.broadcasted_iota(jnp.int32, (n, n), 0)
    col = lax.broadcasted_iota(jnp.int32, (n, n), 1)
    keep = col <= row
    for g in range(A_GROUPS):
        w = jnp.where(keep, ws_ref[g, :n, :n], 0.0)
        wt_ref[g // 2, (g % 2) * n:(g % 2 + 1) * n, :] = w.astype(jnp.bfloat16)


def _rms_rows(x):
    return x * lax.rsqrt(jnp.mean(x * x, axis=-1, keepdims=True) + EPS)


def _layernorm(x, g, b):
    mu = jnp.mean(x, axis=-1, keepdims=True)
    xc = x - mu
    var = jnp.mean(xc * xc, axis=-1, keepdims=True)
    return xc * lax.rsqrt(var + EPS) * g + b


def _pair_select(stacked, n):
    return jnp.where(_even_lane_mask((n, LANES)), stacked[:n], stacked[n:])


def _split_heads(x):
    return x.reshape(x.shape[0], B_HEADS, B_HDIM)


def _to_feature_major(x):
    return x.T.reshape(B_HEADS, B_HDIM, x.shape[0])


def _store_q(q, qe_ref, qo_ref):
    q = (q * Q_SCALE).astype(jnp.bfloat16)
    even = _even_lane_mask(q.shape)
    zero = jnp.zeros((), jnp.bfloat16)
    qe_ref[...] = jnp.where(even, q, zero)
    qo_ref[...] = jnp.where(even, zero, q)


def _prompt_kernel(x_ref, mod_ref, gpre_ref, win_ref, lng_ref, lnb_ref, ws_ref, bfull_ref,
                   biast_ref, wout_ref, gpost_ref,
                   y_ref, klast_ref, vlast_ref,
                   h_s, usg_s, va_s, qe_s, qo_s, k_s, vt_s, sgb_s, o_s, wt_s, st_s, kf_s, vf_s):
    t = pl.program_id(1)
    nt = pl.num_programs(1)
    T = SEQ_TILE

    @pl.when(jnp.logical_and(pl.program_id(0) == 0, t == 0))
    def _():
        _tril_pairs(ws_ref, wt_s, MLP_CHUNK)

    shift = mod_ref[0, 0:1, :]
    scale = mod_ref[0, 1:2, :]
    gate = mod_ref[0, 2:3, :]

    pre_scale = gpre_ref[...] * (1.0 + scale)

    def proj(c0, rows):
        return jnp.dot(h_s[rows, :], win_ref[:, c0:c0 + 512], preferred_element_type=jnp.float32)

    for lo in range(0, T, PROJ_ROWS):
        rb = slice(lo, lo + PROJ_ROWS)
        halves = [slice(lo, lo + PROJ_ROWS // 2), slice(lo + PROJ_ROWS // 2, lo + PROJ_ROWS)]
        for rows in halves:
            h_s[rows, :] = (_rms_rows(x_ref[0, rows, :]) * pre_scale + shift).astype(jnp.bfloat16)
        for rows in halves:
            usg_s[rows, :] = _gelu(proj(C_U, rows))
        k = proj(C_K, rb)
        k_s[KV_WIN + lo:KV_WIN + lo + PROJ_ROWS, :] = k.astype(jnp.bfloat16)
        va_s[rb, :] = _layernorm(_gelu(proj(C_V, rb)), lng_ref[...], lnb_ref[...]).astype(jnp.bfloat16)
        v = proj(C_VV, rb)
        vt_s[:, KV_WIN + lo:KV_WIN + lo + PROJ_ROWS] = v.T.astype(jnp.bfloat16)
        usg_s[rb, :] = usg_s[rb, :] * _silu(proj(C_GA, rb))
        _store_q(proj(C_Q, rb), qe_s.at[rb, :], qo_s.at[rb, :])
        sgb_s[rb, :] = _silu(proj(C_GB, rb))
        if lo + PROJ_ROWS == T:
            kf_s[...] = k
            vf_s[...] = v

        for c0 in range(lo, lo + PROJ_ROWS, MLP_CHUNK):
            rows = slice(c0, c0 + MLP_CHUNK)
            for p in range(N_PAIRS):
                lanes = slice(p * LANES, (p + 1) * LANES)
                mix = jnp.dot(wt_s[p], va_s[rows, lanes], preferred_element_type=jnp.float32)
                mixed = _pair_select(mix, MLP_CHUNK) + bfull_ref[:, lanes]
                o_s[rows, lanes] = (usg_s[rows, lanes] * mixed).astype(jnp.bfloat16)

    def key_lo(qb, first_tile):
        return max(KV_WIN - qb * Q_BLOCK, 0) if first_tile else 0

    def scores(qb, p, slot, first_tile):
        r0, lo = qb * Q_BLOCK, key_lo(qb, first_tile)
        lanes = slice(p * LANES, (p + 1) * LANES)
        qs = jnp.concatenate([qe_s[r0:r0 + Q_BLOCK, lanes],
                              qo_s[r0:r0 + Q_BLOCK, lanes]], axis=0)
        half = (K_BLOCK - lo) // 2
        for a in (lo, lo + half):
            st_s[slot, a:a + half, :] = lax.dot_general(
                k_s[r0 + a:r0 + a + half, lanes], qs, _CONTRACT_LANES,
                preferred_element_type=jnp.float32) + biast_ref[p, a:a + half, :]

    def finish(qb, p, slot, first_tile):
        r0, lo = qb * Q_BLOCK, key_lo(qb, first_tile)
        st = st_s[slot, lo:K_BLOCK, :]
        m = jnp.max(st, axis=0, keepdims=True)
        pt = jnp.exp2(st - m).astype(jnp.bfloat16)
        ones = jnp.ones((BF16_ROWS, K_BLOCK - lo), jnp.bfloat16)
        ots = []
        for par in range(2):
            feats = slice(p * LANES + par * B_HDIM, p * LANES + (par + 1) * B_HDIM)
            qcols = slice(par * Q_BLOCK, (par + 1) * Q_BLOCK)
            lhs = jnp.concatenate([vt_s[feats, r0 + lo:r0 + K_BLOCK], ones], axis=0)
            ot = jnp.dot(lhs, pt[:, qcols], preferred_element_type=jnp.float32)
            ots.append(ot[0:B_HDIM] * (1.0 / ot[B_HDIM:B_HDIM + 1]))
        yb = jnp.concatenate(ots, axis=0).T * sgb_s[r0:r0 + Q_BLOCK, p * LANES:(p + 1) * LANES]
        o_s[r0:r0 + Q_BLOCK, B_WIDTH + p * LANES:B_WIDTH + (p + 1) * LANES] = (
            yb.astype(jnp.bfloat16))

    def attend(first_tile):
        items = [(qb, p) for qb in range(T // Q_BLOCK) for p in range(N_PAIRS)]
        for j in range(SCORE_LEAD):
            scores(*items[j], j % SCORE_SLOTS, first_tile)
        for i, (qb, p) in enumerate(items):
            if i + SCORE_LEAD < len(items):
                scores(*items[i + SCORE_LEAD], (i + SCORE_LEAD) % SCORE_SLOTS, first_tile)
            finish(qb, p, i % SCORE_SLOTS, first_tile)

    @pl.when(t == 0)
    def _():
        attend(True)

    @pl.when(t > 0)
    def _():
        attend(False)

    k_s[0:KV_WIN, :] = k_s[T:T + KV_WIN, :]
    vt_s[:, 0:KV_WIN] = vt_s[:, T:T + KV_WIN]

    post_scale = gate * gpost_ref[...]
    for lo in range(0, T, POST_ROWS):
        rows = slice(lo, lo + POST_ROWS)
        o = jnp.dot(o_s[rows, :], wout_ref[...], preferred_element_type=jnp.float32)
        y_ref[0, rows, :] = x_ref[0, rows, :] + _rms_rows(o) * post_scale

    @pl.when(t == nt - 1)
    def _():
        klast_ref[0] = _to_feature_major(kf_s[...])
        vlast_ref[0] = _to_feature_major(vf_s[...])


def _const_spec(shape):
    nd = len(shape)
    return pl.BlockSpec(shape, lambda b, t: (0,) * nd, pipeline_mode=pl.Buffered(1))


def _prompt_call(x, mod, g_pre, w_in, ln_g, ln_b, w_s, b_full, bias_t, w_out, g_post):
    bsz, seq, _ = x.shape
    T = SEQ_TILE
    assert seq % T == 0 and T % PROJ_ROWS == 0 and PROJ_ROWS == KV_WIN and T % POST_ROWS == 0
    nt = seq // T
    f32, bf16 = jnp.float32, jnp.bfloat16
    return pl.pallas_call(
        _prompt_kernel,
        grid=(bsz, nt),
        in_specs=[
            pl.BlockSpec((1, T, D_MODEL), lambda b, t: (b, t, 0)),
            pl.BlockSpec((1, 3, D_MODEL), lambda b, t: (b, 0, 0)),
            _const_spec((1, D_MODEL)),
            _const_spec((D_MODEL, D_IN)),
            _const_spec((1, A_WIDTH)),
            _const_spec((1, A_WIDTH)),
            _const_spec((A_GROUPS, MLP_CHUNK, MLP_CHUNK)),
            _const_spec((MLP_CHUNK, A_WIDTH)),
            _const_spec((N_PAIRS, K_BLOCK, 2 * Q_BLOCK)),
            _const_spec((D_MODEL, D_MODEL)),
            _const_spec((1, D_MODEL)),
        ],
        out_specs=[
            pl.BlockSpec((1, T, D_MODEL), lambda b, t: (b, t, 0)),
            pl.BlockSpec((1, B_HEADS, B_HDIM, KV_WIN), lambda b, t: (b, 0, 0, 0)),
            pl.BlockSpec((1, B_HEADS, B_HDIM, KV_WIN), lambda b, t: (b, 0, 0, 0)),
        ],
        out_shape=[
            jax.ShapeDtypeStruct((bsz, seq, D_MODEL), f32),
            jax.ShapeDtypeStruct((bsz, B_HEADS, B_HDIM, KV_WIN), f32),
            jax.ShapeDtypeStruct((bsz, B_HEADS, B_HDIM, KV_WIN), f32),
        ],
        scratch_shapes=[
            pltpu.VMEM((T, D_MODEL), bf16),
            pltpu.VMEM((T, A_WIDTH), f32),
            pltpu.VMEM((T, A_WIDTH), bf16),
            pltpu.VMEM((T, B_WIDTH), bf16),
            pltpu.VMEM((T, B_WIDTH), bf16),
            pltpu.VMEM((KV_WIN + T, B_WIDTH), bf16),
            pltpu.VMEM((B_WIDTH, KV_WIN + T), bf16),
            pltpu.VMEM((T, B_WIDTH), f32),
            pltpu.VMEM((T, D_MODEL), bf16),
            pltpu.VMEM((N_PAIRS, 2 * MLP_CHUNK, MLP_CHUNK), bf16),
            pltpu.VMEM((SCORE_SLOTS, K_BLOCK, 2 * Q_BLOCK), f32),
            pltpu.VMEM((KV_WIN, B_WIDTH), f32),
            pltpu.VMEM((KV_WIN, B_WIDTH), f32),
        ],
        compiler_params=pltpu.CompilerParams(
            dimension_semantics=("arbitrary", "arbitrary"),
            vmem_limit_bytes=VMEM_LIMIT_BYTES),
        name="prompt",
    )(x, mod, g_pre, w_in, ln_g, ln_b, w_s, b_full, bias_t, w_out, g_post)


def _sample_kernel(x_ref, mod_ref, gpre_ref, win_ref, lng_ref, lnb_ref, ws_ref, bfull_ref,
                   bias_ref, wout_ref, gpost_ref, ck_ref, cv_ref,
                   y_ref, knew_ref, vnew_ref, vanew_ref,
                   usg_s, va_s, qe_s, qo_s, k_s, v_s, sgb_s, o_s, wt_s, *, n_streams, n_new):
    b = pl.program_id(0)
    S = n_new

    @pl.when(b == 0)
    def _():
        _tril_pairs(ws_ref, wt_s, S)
        gpre = gpre_ref[...]
        hs = []
        for i in range(n_streams):
            xi = x_ref[i * S:(i + 1) * S, :]
            hs.append(_rms_rows(xi) * (gpre * (1.0 + mod_ref[i, 1:2, :])) + mod_ref[i, 0:1, :])
        h = jnp.concatenate(hs, axis=0).astype(jnp.bfloat16)

        def proj(c0):
            return jnp.dot(h, win_ref[:, c0:c0 + 512], preferred_element_type=jnp.float32)

        va = _layernorm(_gelu(proj(C_V)), lng_ref[...], lnb_ref[...])
        vanew_ref[...] = va
        va_s[...] = va.astype(jnp.bfloat16)
        usg_s[...] = _gelu(proj(C_U)) * _silu(proj(C_GA))
        _store_q(proj(C_Q), qe_s, qo_s)
        k = proj(C_K)
        v = proj(C_VV)
        knew_ref[...] = _split_heads(k)
        vnew_ref[...] = _split_heads(v)
        k_s[...] = k.astype(jnp.bfloat16)
        v_s[...] = v.astype(jnp.bfloat16)
        sgb_s[...] = _silu(proj(C_GB))

    r0 = pl.multiple_of(b * S, S)
    rows = pl.ds(r0, S)
    pair_lanes = [slice(p * LANES, (p + 1) * LANES) for p in range(N_PAIRS)]
    for p, lanes in enumerate(pair_lanes):
        mix = jnp.dot(wt_s[p], va_s[rows, lanes], preferred_element_type=jnp.float32)
        mixed = _pair_select(mix, S) + bfull_ref[0:S, lanes]
        o_s[rows, lanes] = (usg_s[rows, lanes] * mixed).astype(jnp.bfloat16)

    scores = []
    for p, lanes in enumerate(pair_lanes):
        qs = jnp.concatenate([qe_s[rows, lanes], qo_s[rows, lanes]], axis=0)
        kct = ck_ref[0, 2 * p:2 * p + 2].reshape(LANES, KV_WIN).astype(jnp.bfloat16)
        bias = jnp.concatenate([bias_ref[p, 0], bias_ref[p, 1]], axis=0)
        s1 = jnp.dot(qs, kct, preferred_element_type=jnp.float32) + bias[:, 0:KV_WIN]
        s2 = lax.dot_general(qs, k_s[rows, lanes], _CONTRACT_LANES,
                             preferred_element_type=jnp.float32) + bias[:, KV_WIN:KV_WIN + S]
        scores.append((s1, s2))
    weights = []
    for s1, s2 in scores:
        m = jnp.maximum(jnp.max(s1, axis=-1, keepdims=True), jnp.max(s2, axis=-1, keepdims=True))
        e1 = jnp.exp2(s1 - m)
        e2 = jnp.exp2(s2 - m)
        l = jnp.sum(e1, axis=-1, keepdims=True) + jnp.sum(e2, axis=-1, keepdims=True)
        weights.append((e1.astype(jnp.bfloat16), e2.astype(jnp.bfloat16), 1.0 / l))
    for p, lanes in enumerate(pair_lanes):
        e1, e2, inv_l = weights[p]
        vct = cv_ref[0, 2 * p:2 * p + 2].reshape(LANES, KV_WIN).astype(jnp.bfloat16)
        o2 = (lax.dot_general(e1, vct, _CONTRACT_LANES, preferred_element_type=jnp.float32)
              + jnp.dot(e2, v_s[rows, lanes], preferred_element_type=jnp.float32))
        yb = _pair_select(o2 * inv_l, S) * sgb_s[rows, lanes]
        o_s[rows, B_WIDTH + p * LANES:B_WIDTH + (p + 1) * LANES] = yb.astype(jnp.bfloat16)

    @pl.when(b == n_streams - 1)
    def _():
        o = jnp.dot(o_s[...], wout_ref[...], preferred_element_type=jnp.float32)
        on = _rms_rows(o) * gpost_ref[...]
        for i in range(n_streams):
            sl = slice(i * S, (i + 1) * S)
            y_ref[sl, :] = x_ref[sl, :] + mod_ref[i, 2:3, :] * on[sl, :]


def _sample_call(x, mod, g_pre, w_in, ln_g, ln_b, w_s, b_full, bias_s, w_out, g_post, ck, cv):
    n_streams, n_new, _ = x.shape
    assert n_new % BF16_ROWS == 0 and n_new <= CHUNK
    rows = n_streams * n_new
    f32, bf16 = jnp.float32, jnp.bfloat16

    def const(shape):
        nd = len(shape)
        return pl.BlockSpec(shape, lambda b: (0,) * nd, pipeline_mode=pl.Buffered(1))

    kern = functools.partial(_sample_kernel, n_streams=n_streams, n_new=n_new)
    return pl.pallas_call(
        kern,
        grid=(n_streams,),
        in_specs=[
            const((rows, D_MODEL)),
            const((n_streams, 3, D_MODEL)),
            const((1, D_MODEL)),
            const((D_MODEL, D_IN)),
            const((1, A_WIDTH)),
            const((1, A_WIDTH)),
            const((A_GROUPS, MLP_CHUNK, MLP_CHUNK)),
            const((MLP_CHUNK, A_WIDTH)),
            const((N_PAIRS, 2, n_new, K_BLOCK)),
            const((D_MODEL, D_MODEL)),
            const((1, D_MODEL)),
            pl.BlockSpec((1, B_HEADS, B_HDIM, KV_WIN), lambda b: (b, 0, 0, 0)),
            pl.BlockSpec((1, B_HEADS, B_HDIM, KV_WIN), lambda b: (b, 0, 0, 0)),
        ],
        out_specs=[
            pl.BlockSpec((rows, D_MODEL), lambda b: (0, 0)),
            pl.BlockSpec((rows, B_HEADS, B_HDIM), lambda b: (0, 0, 0)),
            pl.BlockSpec((rows, B_HEADS, B_HDIM), lambda b: (0, 0, 0)),
            pl.BlockSpec((rows, A_WIDTH), lambda b: (0, 0)),
        ],
        out_shape=[
            jax.ShapeDtypeStruct((rows, D_MODEL), f32),
            jax.ShapeDtypeStruct((rows, B_HEADS, B_HDIM), f32),
            jax.ShapeDtypeStruct((rows, B_HEADS, B_HDIM), f32),
            jax.ShapeDtypeStruct((rows, A_WIDTH), f32),
        ],
        scratch_shapes=[
            pltpu.VMEM((rows, A_WIDTH), f32),
            pltpu.VMEM((rows, A_WIDTH), bf16),
            pltpu.VMEM((rows, B_WIDTH), bf16),
            pltpu.VMEM((rows, B_WIDTH), bf16),
            pltpu.VMEM((rows, B_WIDTH), bf16),
            pltpu.VMEM((rows, B_WIDTH), bf16),
            pltpu.VMEM((rows, B_WIDTH), f32),
            pltpu.VMEM((rows, D_MODEL), bf16),
            pltpu.VMEM((N_PAIRS, 2 * n_new, n_new), bf16),
        ],
        compiler_params=pltpu.CompilerParams(
            dimension_semantics=("arbitrary",),
            vmem_limit_bytes=VMEM_LIMIT_BYTES),
        name="sample",
    )(x.reshape(rows, D_MODEL), mod, g_pre, w_in, ln_g, ln_b, w_s, b_full, bias_s, w_out, g_post,
      ck, cv)


def _bias_base(rel_bias):
    n_far = KV_WIN - REL_CLIP
    far = jnp.broadcast_to(rel_bias[:, N_REL - 1:N_REL], (B_HEADS, n_far))
    near = rel_bias[:, N_REL - 1:0:-1]
    return jnp.concatenate([far, near], axis=1)


def kernel(x_prompt, x_sample, cache_attn_k, cache_attn_v, c_prompt, c_sample, g_pre, w_ada, b_ada,
           w_in, ln_g, ln_b, w_s, b_s, rel_bias, w_out, g_post):
    depth = g_pre.shape[0]
    bsz, seq, _ = x_prompt.shape
    n_streams, n_new, _ = x_sample.shape
    win = cache_attn_k.shape[2]
    assert win == KV_WIN and bsz + n_streams <= ADA_ROWS

    yp, ys = x_prompt, x_sample
    kp_rows, vp_rows, ks_rows, vs_rows, va_rows = [], [], [], [], []
    c_all = jnp.concatenate(
        [c_prompt, c_sample, jnp.zeros((ADA_ROWS - bsz - n_streams, D_MODEL), c_prompt.dtype)], axis=0)
    for l in range(depth):
        mod, bias_t, bias_s, w_in_b, w_out_b = _prep_call(
            c_all, w_ada[l], b_ada[l], _bias_base(rel_bias[l]), w_in[l], w_out[l], n_new)
        mod = mod.reshape(ADA_ROWS, 3, D_MODEL)
        b_full = jnp.repeat(b_s[l].T, A_GDIM, axis=1)
        pre = (g_pre[l].reshape(1, D_MODEL), w_in_b, ln_g[l].reshape(1, A_WIDTH),
               ln_b[l].reshape(1, A_WIDTH), w_s[l], b_full)
        post = (w_out_b, g_post[l].reshape(1, D_MODEL))

        yp, k_last, v_last = _prompt_call(yp, mod[:bsz], *pre, bias_t, *post)
        kp_rows.append(jnp.transpose(k_last, _FROM_FEATURE_MAJOR))
        vp_rows.append(jnp.transpose(v_last, _FROM_FEATURE_MAJOR))

        ys2, k_new, v_new, va_new = _sample_call(ys, mod[bsz:bsz + n_streams], *pre, bias_s, *post,
                                                 jnp.transpose(cache_attn_k[l], _TO_FEATURE_MAJOR),
                                                 jnp.transpose(cache_attn_v[l], _TO_FEATURE_MAJOR))
        ys = ys2.reshape(n_streams, n_new, D_MODEL)
        ks_rows.append(k_new.reshape(n_streams, n_new, B_HEADS, B_HDIM))
        vs_rows.append(v_new.reshape(n_streams, n_new, B_HEADS, B_HDIM))
        va_rows.append(va_new.reshape(n_streams, n_new, A_WIDTH))

    return (yp, ys, jnp.stack(kp_rows), jnp.stack(vp_rows), jnp.stack(ks_rows), jnp.stack(vs_rows),
            jnp.stack(va_rows))
```

```python
import functools

import jax
import jax.numpy as jnp
from jax import lax
from jax.experimental import pallas as pl
from jax.experimental.pallas import tpu as pltpu

LANES = 128
SUBLANES = 8
BF16_ROWS = 16
VMEM_LIMIT_BYTES = 60 * 1024 * 1024

D_MODEL = 1024
A_WIDTH = 512
A_GROUPS = 8
A_GDIM = A_WIDTH // A_GROUPS
MLP_CHUNK = 128
B_WIDTH = 512
B_HEADS = 8
B_HDIM = B_WIDTH // B_HEADS
CHUNK = 64
KV_WIN = 512
REL_CLIP = 128
N_REL = 2 * REL_CLIP + 1
EPS = 1e-6
D_IN = 3 * A_WIDTH + 4 * B_WIDTH
NEG = -1e30
LOG2E = 1.4426950408889634
Q_SCALE = B_HDIM ** -0.5 * LOG2E

N_PAIRS = B_HEADS // 2
Q_BLOCK = 2 * CHUNK
K_BLOCK = KV_WIN + Q_BLOCK
SEQ_TILE = 1024
PROJ_ROWS = 512
POST_ROWS = 256
SCORE_LEAD = 3
SCORE_SLOTS = SCORE_LEAD + 1
ADA_ROWS = 16

C_U, C_V, C_GA, C_Q, C_K, C_VV, C_GB = (i * 512 for i in range(7))

_CONTRACT_LANES = (((1,), (1,)), ((), ()))

_TO_FEATURE_MAJOR = (0, 2, 3, 1)
_FROM_FEATURE_MAJOR = (0, 3, 1, 2)


def _sigmoid_exp2(neg_arg_log2):
    return 1.0 / (1.0 + jnp.exp2(neg_arg_log2))


def _gelu(x):
    c1 = -2.0 * (2.0 / jnp.pi) ** 0.5 * LOG2E
    return x * _sigmoid_exp2(x * (c1 + (c1 * 0.044715) * (x * x)))


def _silu(x):
    return x * _sigmoid_exp2(x * (-LOG2E))


def _even_lane_mask(shape):
    lane = lax.broadcasted_iota(jnp.int32, shape, len(shape) - 1)
    return (lane % LANES) < B_HDIM


def _ada_block(c_ref, w_ref, b_ref, o_ref):
    c = c_ref[...]
    o_ref[...] = jnp.dot(_silu(c), w_ref[...], preferred_element_type=jnp.float32) + b_ref[...]


def _relb_block(base_ref, bt_ref, bs_ref, n_new):
    row = lax.broadcasted_iota(jnp.int32, (Q_BLOCK, K_BLOCK), 0)
    col = lax.broadcasted_iota(jnp.int32, (Q_BLOCK, K_BLOCK), 1)
    band_lo = jnp.where(row < CHUNK, 0, CHUNK)
    rel = col - band_lo
    outside = jnp.logical_or(rel < 0, rel >= KV_WIN + CHUNK)
    for par in range(2):
        base = base_ref[0, par:par + 1, :] * LOG2E
        x = jnp.broadcast_to(base, (Q_BLOCK, K_BLOCK))
        shift = 1
        while shift < Q_BLOCK:
            x = jnp.where((row & shift) != 0, jnp.roll(x, shift, axis=1), x)
            shift *= 2
        far = jnp.broadcast_to(base[:, 0:1], (Q_BLOCK, K_BLOCK))
        x = jnp.where(col < row, far, x)
        bs_ref[0, par] = x[0:n_new, :]
        bt_ref[0, :, par * Q_BLOCK:(par + 1) * Q_BLOCK] = jnp.where(outside, NEG, x).T


PREP_COLS = 512
N_PREP = D_IN // PREP_COLS
N_ADA = 3 * D_MODEL // PREP_COLS
WOUT_COLS = D_MODEL // N_PAIRS


def _prep_kernel(c_ref, wada_ref, bada_ref, base_ref, win_ref, wout_ref,
                 mod_ref, bt_ref, bs_ref, winb_ref, woutb_ref, *, n_new):
    j = pl.program_id(0)
    winb_ref[...] = win_ref[...].astype(jnp.bfloat16)

    @pl.when(j < N_ADA)
    def _():
        _ada_block(c_ref, wada_ref, bada_ref, mod_ref)

    @pl.when(j < N_PAIRS)
    def _():
        woutb_ref[...] = wout_ref[...].astype(jnp.bfloat16)
        _relb_block(base_ref, bt_ref, bs_ref, n_new)


def _prep_call(c_all, w_ada, b_ada, base, w_in, w_out, n_new):
    f32, bf16 = jnp.float32, jnp.bfloat16

    def upto(n):
        return lambda j: jnp.minimum(j, n - 1)

    ada_j, pair_j = upto(N_ADA), upto(N_PAIRS)
    return pl.pallas_call(
        functools.partial(_prep_kernel, n_new=n_new),
        grid=(N_PREP,),
        in_specs=[
            pl.BlockSpec((ADA_ROWS, D_MODEL), lambda j: (0, 0)),
            pl.BlockSpec((D_MODEL, PREP_COLS), lambda j: (0, ada_j(j))),
            pl.BlockSpec((1, PREP_COLS), lambda j: (0, ada_j(j))),
            pl.BlockSpec((1, 2, K_BLOCK), lambda j: (pair_j(j), 0, 0)),
            pl.BlockSpec((D_MODEL, PREP_COLS), lambda j: (0, j)),
            pl.BlockSpec((D_MODEL, WOUT_COLS), lambda j: (0, pair_j(j))),
        ],
        out_specs=[
            pl.BlockSpec((ADA_ROWS, PREP_COLS), lambda j: (0, ada_j(j))),
            pl.BlockSpec((1, K_BLOCK, 2 * Q_BLOCK), lambda j: (pair_j(j), 0, 0)),
            pl.BlockSpec((1, 2, n_new, K_BLOCK), lambda j: (pair_j(j), 0, 0, 0)),
            pl.BlockSpec((D_MODEL, PREP_COLS), lambda j: (0, j)),
            pl.BlockSpec((D_MODEL, WOUT_COLS), lambda j: (0, pair_j(j))),
        ],
        out_shape=[
            jax.ShapeDtypeStruct((ADA_ROWS, 3 * D_MODEL), f32),
            jax.ShapeDtypeStruct((N_PAIRS, K_BLOCK, 2 * Q_BLOCK), f32),
            jax.ShapeDtypeStruct((N_PAIRS, 2, n_new, K_BLOCK), f32),
            jax.ShapeDtypeStruct((D_MODEL, D_IN), bf16),
            jax.ShapeDtypeStruct((D_MODEL, D_MODEL), bf16),
        ],
        compiler_params=pltpu.CompilerParams(dimension_semantics=("arbitrary",)),
        name="prep",
    )(c_all, w_ada, b_ada.reshape(1, 3 * D_MODEL), base.reshape(N_PAIRS, 2, K_BLOCK), w_in, w_out)


def _tril_pairs(ws_ref, wt_ref, n):
    row = lax.broadcasted_iota(jnp.int32, (n, n), 0)
    col = lax.broadcasted_iota(jnp.int32, (n, n), 1)
    keep = col <= row
    for g in range(A_GROUPS):
        w = jnp.where(keep, ws_ref[g, :n, :n], 0.0)
        wt_ref[g // 2, (g % 2) * n:(g % 2 + 1) * n, :] = w.astype(jnp.bfloat16)


def _rms_rows(x):
    return x * lax.rsqrt(jnp.mean(x * x, axis=-1, keepdims=True) + EPS)


def _layernorm(x, g, b):
    mu = jnp.mean(x, axis=-1, keepdims=True)
    xc = x - mu
    var = jnp.mean(xc * xc, axis=-1, keepdims=True)
    return xc * lax.rsqrt(var + EPS) * g + b


def _pair_select(stacked, n):
    return jnp.where(_even_lane_mask((n, LANES)), stacked[:n], stacked[n:])


def _split_heads(x):
    return x.reshape(x.shape[0], B_HEADS, B_HDIM)


def _to_feature_major(x):
    return x.T.reshape(B_HEADS, B_HDIM, x.shape[0])


def _store_q(q, qe_ref, qo_ref):
    q = (q * Q_SCALE).astype(jnp.bfloat16)
    even = _even_lane_mask(q.shape)
    zero = jnp.zeros((), jnp.bfloat16)
    qe_ref[...] = jnp.where(even, q, zero)
    qo_ref[...] = jnp.where(even, zero, q)


def _prompt_kernel(x_ref, mod_ref, gpre_ref, win_ref, lng_ref, lnb_ref, ws_ref, bfull_ref,
                   biast_ref, wout_ref, gpost_ref,
                   y_ref, klast_ref, vlast_ref,
                   h_s, usg_s, va_s, qe_s, qo_s, k_s, vt_s, sgb_s, o_s, wt_s, st_s, kf_s, vf_s):
    t = pl.program_id(1)
    nt = pl.num_programs(1)
    T = SEQ_TILE

    @pl.when(jnp.logical_and(pl.program_id(0) == 0, t == 0))
    def _():
        _tril_pairs(ws_ref, wt_s, MLP_CHUNK)

    shift = mod_ref[0, 0:1, :]
    scale = mod_ref[0, 1:2, :]
    gate = mod_ref[0, 2:3, :]

    pre_scale = gpre_ref[...] * (1.0 + scale)

    def proj(c0, rows):
        return jnp.dot(h_s[rows, :], win_ref[:, c0:c0 + 512], preferred_element_type=jnp.float32)

    def proj_block(i, carry):
        lo = pl.multiple_of(i * PROJ_ROWS, PROJ_ROWS)
        rb = pl.ds(lo, PROJ_ROWS)
        halves = [pl.ds(lo, PROJ_ROWS // 2),
                  pl.ds(pl.multiple_of(lo + PROJ_ROWS // 2, PROJ_ROWS // 2), PROJ_ROWS // 2)]
        for rows in halves:
            h_s[rows, :] = (_rms_rows(x_ref[0, rows, :]) * pre_scale + shift).astype(jnp.bfloat16)
        for rows in halves:
            usg_s[rows, :] = _gelu(proj(C_U, rows))
        k = proj(C_K, rb)
        hist = pl.ds(pl.multiple_of(KV_WIN + lo, PROJ_ROWS), PROJ_ROWS)
        k_s[hist, :] = k.astype(jnp.bfloat16)
        va_s[rb, :] = _layernorm(_gelu(proj(C_V, rb)), lng_ref[...], lnb_ref[...]).astype(jnp.bfloat16)
        v = proj(C_VV, rb)
        vt_s[:, hist] = v.T.astype(jnp.bfloat16)
        usg_s[rb, :] = usg_s[rb, :] * _silu(proj(C_GA, rb))
        _store_q(proj(C_Q, rb), qe_s.at[rb, :], qo_s.at[rb, :])
        sgb_s[rb, :] = _silu(proj(C_GB, rb))
        kf_s[...] = k
        vf_s[...] = v

        for c in range(PROJ_ROWS // MLP_CHUNK):
            rows = pl.ds(pl.multiple_of(lo + c * MLP_CHUNK, MLP_CHUNK), MLP_CHUNK)
            for p in range(N_PAIRS):
                lanes = slice(p * LANES, (p + 1) * LANES)
                mix = jnp.dot(wt_s[p], va_s[rows, lanes], preferred_element_type=jnp.float32)
                mixed = _pair_select(mix, MLP_CHUNK) + bfull_ref[:, lanes]
                o_s[rows, lanes] = (usg_s[rows, lanes] * mixed).astype(jnp.bfloat16)
        return carry

    lax.fori_loop(0, T // PROJ_ROWS, proj_block, 0)

    def key_lo(qb, first_tile):
        return max(KV_WIN - qb * Q_BLOCK, 0) if first_tile else 0

    def scores(qb, p, slot, first_tile):
        r0, lo = qb * Q_BLOCK, key_lo(qb, first_tile)
        lanes = slice(p * LANES, (p + 1) * LANES)
        qs = jnp.concatenate([qe_s[r0:r0 + Q_BLOCK, lanes],
                              qo_s[r0:r0 + Q_BLOCK, lanes]], axis=0)
        half = (K_BLOCK - lo) // 2
        for a in (lo, lo + half):
            st_s[slot, a:a + half, :] = lax.dot_general(
                k_s[r0 + a:r0 + a + half, lanes], qs, _CONTRACT_LANES,
                preferred_element_type=jnp.float32) + biast_ref[p, a:a + half, :]

    def finish(qb, p, slot, first_tile):
        r0, lo = qb * Q_BLOCK, key_lo(qb, first_tile)
        st = st_s[slot, lo:K_BLOCK, :]
        m = jnp.max(st, axis=0, keepdims=True)
        pt = jnp.exp2(st - m).astype(jnp.bfloat16)
        ones = jnp.ones((BF16_ROWS, K_BLOCK - lo), jnp.bfloat16)
        ots = []
        for par in range(2):
            feats = slice(p * LANES + par * B_HDIM, p * LANES + (par + 1) * B_HDIM)
            qcols = slice(par * Q_BLOCK, (par + 1) * Q_BLOCK)
            lhs = jnp.concatenate([vt_s[feats, r0 + lo:r0 + K_BLOCK], ones], axis=0)
            ot = jnp.dot(lhs, pt[:, qcols], preferred_element_type=jnp.float32)
            ots.append(ot[0:B_HDIM] * (1.0 / ot[B_HDIM:B_HDIM + 1]))
        yb = jnp.concatenate(ots, axis=0).T * sgb_s[r0:r0 + Q_BLOCK, p * LANES:(p + 1) * LANES]
        o_s[r0:r0 + Q_BLOCK, B_WIDTH + p * LANES:B_WIDTH + (p + 1) * LANES] = (
            yb.astype(jnp.bfloat16))

    def attend(first_tile):
        items = [(qb, p) for qb in range(T // Q_BLOCK) for p in range(N_PAIRS)]
        for j in range(SCORE_LEAD):
            scores(*items[j], j % SCORE_SLOTS, first_tile)
        for i, (qb, p) in enumerate(items):
            if i + SCORE_LEAD < len(items):
                scores(*items[i + SCORE_LEAD], (i + SCORE_LEAD) % SCORE_SLOTS, first_tile)
            finish(qb, p, i % SCORE_SLOTS, first_tile)

    @pl.when(t == 0)
    def _():
        attend(True)

    @pl.when(t > 0)
    def _():
        attend(False)

    k_s[0:KV_WIN, :] = k_s[T:T + KV_WIN, :]
    vt_s[:, 0:KV_WIN] = vt_s[:, T:T + KV_WIN]

    post_scale = gate * gpost_ref[...]
    for lo in range(0, T, POST_ROWS):
        rows = slice(lo, lo + POST_ROWS)
        o = jnp.dot(o_s[rows, :], wout_ref[...], preferred_element_type=jnp.float32)
        y_ref[0, rows, :] = x_ref[0, rows, :] + _rms_rows(o) * post_scale

    @pl.when(t == nt - 1)
    def _():
        klast_ref[0] = _to_feature_major(kf_s[...])
        vlast_ref[0] = _to_feature_major(vf_s[...])


def _const_spec(shape):
    nd = len(shape)
    return pl.BlockSpec(shape, lambda b, t: (0,) * nd, pipeline_mode=pl.Buffered(1))


def _prompt_call(x, mod, g_pre, w_in, ln_g, ln_b, w_s, b_full, bias_t, w_out, g_post):
    bsz, seq, _ = x.shape
    T = SEQ_TILE
    assert seq % T == 0 and T % PROJ_ROWS == 0 and PROJ_ROWS == KV_WIN and T % POST_ROWS == 0
    nt = seq // T
    f32, bf16 = jnp.float32, jnp.bfloat16
    return pl.pallas_call(
        _prompt_kernel,
        grid=(bsz, nt),
        in_specs=[
            pl.BlockSpec((1, T, D_MODEL), lambda b, t: (b, t, 0)),
            pl.BlockSpec((1, 3, D_MODEL), lambda b, t: (b, 0, 0)),
            _const_spec((1, D_MODEL)),
            _const_spec((D_MODEL, D_IN)),
            _const_spec((1, A_WIDTH)),
            _const_spec((1, A_WIDTH)),
            _const_spec((A_GROUPS, MLP_CHUNK, MLP_CHUNK)),
            _const_spec((MLP_CHUNK, A_WIDTH)),
            _const_spec((N_PAIRS, K_BLOCK, 2 * Q_BLOCK)),
            _const_spec((D_MODEL, D_MODEL)),
            _const_spec((1, D_MODEL)),
        ],
        out_specs=[
            pl.BlockSpec((1, T, D_MODEL), lambda b, t: (b, t, 0)),
            pl.BlockSpec((1, B_HEADS, B_HDIM, KV_WIN), lambda b, t: (b, 0, 0, 0)),
            pl.BlockSpec((1, B_HEADS, B_HDIM, KV_WIN), lambda b, t: (b, 0, 0, 0)),
        ],
        out_shape=[
            jax.ShapeDtypeStruct((bsz, seq, D_MODEL), f32),
            jax.ShapeDtypeStruct((bsz, B_HEADS, B_HDIM, KV_WIN), f32),
            jax.ShapeDtypeStruct((bsz, B_HEADS, B_HDIM, KV_WIN), f32),
        ],
        scratch_shapes=[
            pltpu.VMEM((T, D_MODEL), bf16),
            pltpu.VMEM((T, A_WIDTH), f32),
            pltpu.VMEM((T, A_WIDTH), bf16),
            pltpu.VMEM((T, B_WIDTH), bf16),
            pltpu.VMEM((T, B_WIDTH), bf16),
            pltpu.VMEM((KV_WIN + T, B_WIDTH), bf16),
            pltpu.VMEM((B_WIDTH, KV_WIN + T), bf16),
            pltpu.VMEM((T, B_WIDTH), f32),
            pltpu.VMEM((T, D_MODEL), bf16),
            pltpu.VMEM((N_PAIRS, 2 * MLP_CHUNK, MLP_CHUNK), bf16),
            pltpu.VMEM((SCORE_SLOTS, K_BLOCK, 2 * Q_BLOCK), f32),
            pltpu.VMEM((KV_WIN, B_WIDTH), f32),
            pltpu.VMEM((KV_WIN, B_WIDTH), f32),
        ],
        compiler_params=pltpu.CompilerParams(
            dimension_semantics=("arbitrary", "arbitrary"),
            vmem_limit_bytes=VMEM_LIMIT_BYTES),
        name="prompt",
    )(x, mod, g_pre, w_in, ln_g, ln_b, w_s, b_full, bias_t, w_out, g_post)


def _sample_kernel(x_ref, mod_ref, gpre_ref, win_ref, lng_ref, lnb_ref, ws_ref, bfull_ref,
                   bias_ref, wout_ref, gpost_ref, ck_ref, cv_ref,
                   y_ref, knew_ref, vnew_ref, vanew_ref,
                   usg_s, va_s, qe_s, qo_s, k_s, v_s, sgb_s, o_s, wt_s, *, n_streams, n_new):
    b = pl.program_id(0)
    S = n_new

    @pl.when(b == 0)
    def _():
        _tril_pairs(ws_ref, wt_s, S)
        gpre = gpre_ref[...]
        hs = []
        for i in range(n_streams):
            xi = x_ref[i * S:(i + 1) * S, :]
            hs.append(_rms_rows(xi) * (gpre * (1.0 + mod_ref[i, 1:2, :])) + mod_ref[i, 0:1, :])
        h = jnp.concatenate(hs, axis=0).astype(jnp.bfloat16)

        def proj(c0):
            return jnp.dot(h, win_ref[:, c0:c0 + 512], preferred_element_type=jnp.float32)

        va = _layernorm(_gelu(proj(C_V)), lng_ref[...], lnb_ref[...])
        vanew_ref[...] = va
        va_s[...] = va.astype(jnp.bfloat16)
        usg_s[...] = _gelu(proj(C_U)) * _silu(proj(C_GA))
        _store_q(proj(C_Q), qe_s, qo_s)
        k = proj(C_K)
        v = proj(C_VV)
        knew_ref[...] = _split_heads(k)
        vnew_ref[...] = _split_heads(v)
        k_s[...] = k.astype(jnp.bfloat16)
        v_s[...] = v.astype(jnp.bfloat16)
        sgb_s[...] = _silu(proj(C_GB))

    r0 = pl.multiple_of(b * S, S)
    rows = pl.ds(r0, S)
    pair_lanes = [slice(p * LANES, (p + 1) * LANES) for p in range(N_PAIRS)]
    for p, lanes in enumerate(pair_lanes):
        mix = jnp.dot(wt_s[p], va_s[rows, lanes], preferred_element_type=jnp.float32)
        mixed = _pair_select(mix, S) + bfull_ref[0:S, lanes]
        o_s[rows, lanes] = (usg_s[rows, lanes] * mixed).astype(jnp.bfloat16)

    scores = []
    for p, lanes in enumerate(pair_lanes):
        qs = jnp.concatenate([qe_s[rows, lanes], qo_s[rows, lanes]], axis=0)
        kct = ck_ref[0, 2 * p:2 * p + 2].reshape(LANES, KV_WIN).astype(jnp.bfloat16)
        bias = jnp.concatenate([bias_ref[p, 0], bias_ref[p, 1]], axis=0)
        s1 = jnp.dot(qs, kct, preferred_element_type=jnp.float32) + bias[:, 0:KV_WIN]
        s2 = lax.dot_general(qs, k_s[rows, lanes], _CONTRACT_LANES,
                             preferred_element_type=jnp.float32) + bias[:, KV_WIN:KV_WIN + S]
        scores.append((s1, s2))
    weights = []
    for s1, s2 in scores:
        m = jnp.maximum(jnp.max(s1, axis=-1, keepdims=True), jnp.max(s2, axis=-1, keepdims=True))
        e1 = jnp.exp2(s1 - m)
        e2 = jnp.exp2(s2 - m)
        l = jnp.sum(e1, axis=-1, keepdims=True) + jnp.sum(e2, axis=-1, keepdims=True)
        weights.append((e1.astype(jnp.bfloat16), e2.astype(jnp.bfloat16), 1.0 / l))
    for p, lanes in enumerate(pair_lanes):
        e1, e2, inv_l = weights[p]
        vct = cv_ref[0, 2 * p:2 * p + 2].reshape(LANES, KV_WIN).astype(jnp.bfloat16)
        o2 = (lax.dot_general(e1, vct, _CONTRACT_LANES, preferred_element_type=jnp.float32)
              + jnp.dot(e2, v_s[rows, lanes], preferred_element_type=jnp.float32))
        yb = _pair_select(o2 * inv_l, S) * sgb_s[rows, lanes]
        o_s[rows, B_WIDTH + p * LANES:B_WIDTH + (p + 1) * LANES] = yb.astype(jnp.bfloat16)

    @pl.when(b == n_streams - 1)
    def _():
        o = jnp.dot(o_s[...], wout_ref[...], preferred_element_type=jnp.float32)
        on = _rms_rows(o) * gpost_ref[...]
        for i in range(n_streams):
            sl = slice(i * S, (i + 1) * S)
            y_ref[sl, :] = x_ref[sl, :] + mod_ref[i, 2:3, :] * on[sl, :]


def _sample_call(x, mod, g_pre, w_in, ln_g, ln_b, w_s, b_full, bias_s, w_out, g_post, ck, cv):
    n_streams, n_new, _ = x.shape
    assert n_new % BF16_ROWS == 0 and n_new <= CHUNK
    rows = n_streams * n_new
    f32, bf16 = jnp.float32, jnp.bfloat16

    def const(shape):
        nd = len(shape)
        return pl.BlockSpec(shape, lambda b: (0,) * nd, pipeline_mode=pl.Buffered(1))

    kern = functools.partial(_sample_kernel, n_streams=n_streams, n_new=n_new)
    return pl.pallas_call(
        kern,
        grid=(n_streams,),
        in_specs=[
            const((rows, D_MODEL)),
            const((n_streams, 3, D_MODEL)),
            const((1, D_MODEL)),
            const((D_MODEL, D_IN)),
            const((1, A_WIDTH)),
            const((1, A_WIDTH)),
            const((A_GROUPS, MLP_CHUNK, MLP_CHUNK)),
            const((MLP_CHUNK, A_WIDTH)),
            const((N_PAIRS, 2, n_new, K_BLOCK)),
            const((D_MODEL, D_MODEL)),
            const((1, D_MODEL)),
            pl.BlockSpec((1, B_HEADS, B_HDIM, KV_WIN), lambda b: (b, 0, 0, 0)),
            pl.BlockSpec((1, B_HEADS, B_HDIM, KV_WIN), lambda b: (b, 0, 0, 0)),
        ],
        out_specs=[
            pl.BlockSpec((rows, D_MODEL), lambda b: (0, 0)),
            pl.BlockSpec((rows, B_HEADS, B_HDIM), lambda b: (0, 0, 0)),
            pl.BlockSpec((rows, B_HEADS, B_HDIM), lambda b: (0, 0, 0)),
            pl.BlockSpec((rows, A_WIDTH), lambda b: (0, 0)),
        ],
        out_shape=[
            jax.ShapeDtypeStruct((rows, D_MODEL), f32),
            jax.ShapeDtypeStruct((rows, B_HEADS, B_HDIM), f32),
            jax.ShapeDtypeStruct((rows, B_HEADS, B_HDIM), f32),
            jax.ShapeDtypeStruct((rows, A_WIDTH), f32),
        ],
        scratch_shapes=[
            pltpu.VMEM((rows, A_WIDTH), f32),
            pltpu.VMEM((rows, A_WIDTH), bf16),
            pltpu.VMEM((rows, B_WIDTH), bf16),
            pltpu.VMEM((rows, B_WIDTH), bf16),
            pltpu.VMEM((rows, B_WIDTH), bf16),
            pltpu.VMEM((rows, B_WIDTH), bf16),
            pltpu.VMEM((rows, B_WIDTH), f32),
            pltpu.VMEM((rows, D_MODEL), bf16),
            pltpu.VMEM((N_PAIRS, 2 * n_new, n_new), bf16),
        ],
        compiler_params=pltpu.CompilerParams(
            dimension_semantics=("arbitrary",),
            vmem_limit_bytes=VMEM_LIMIT_BYTES),
        name="sample",
    )(x.reshape(rows, D_MODEL), mod, g_pre, w_in, ln_g, ln_b, w_s, b_full, bias_s, w_out, g_post,
      ck, cv)


def _bias_base(rel_bias):
    n_far = KV_WIN - REL_CLIP
    far = jnp.broadcast_to(rel_bias[:, N_REL - 1:N_REL], (B_HEADS, n_far))
    near = rel_bias[:, N_REL - 1:0:-1]
    return jnp.concatenate([far, near], axis=1)


def kernel(x_prompt, x_sample, cache_attn_k, cache_attn_v, c_prompt, c_sample, g_pre, w_ada, b_ada,
           w_in, ln_g, ln_b, w_s, b_s, rel_bias, w_out, g_post):
    depth = g_pre.shape[0]
    bsz, seq, _ = x_prompt.shape
    n_streams, n_new, _ = x_sample.shape
    win = cache_attn_k.shape[2]
    assert win == KV_WIN and bsz + n_streams <= ADA_ROWS

    yp, ys = x_prompt, x_sample
    kp_rows, vp_rows, ks_rows, vs_rows, va_rows = [], [], [], [], []
    c_all = jnp.concatenate(
        [c_prompt, c_sample, jnp.zeros((ADA_ROWS - bsz - n_streams, D_MODEL), c_prompt.dtype)], axis=0)
    for l in range(depth):
        mod, bias_t, bias_s, w_in_b, w_out_b = _prep_call(
            c_all, w_ada[l], b_ada[l], _bias_base(rel_bias[l]), w_in[l], w_out[l], n_new)
        mod = mod.reshape(ADA_ROWS, 3, D_MODEL)
        b_full = jnp.repeat(b_s[l].T, A_GDIM, axis=1)
        pre = (g_pre[l].reshape(1, D_MODEL), w_in_b, ln_g[l].reshape(1, A_WIDTH),
               ln_b[l].reshape(1, A_WIDTH), w_s[l], b_full)
        post = (w_out_b, g_post[l].reshape(1, D_MODEL))

        yp, k_last, v_last = _prompt_call(yp, mod[:bsz], *pre, bias_t, *post)
        kp_rows.append(jnp.transpose(k_last, _FROM_FEATURE_MAJOR))
        vp_rows.append(jnp.transpose(v_last, _FROM_FEATURE_MAJOR))

        ys2, k_new, v_new, va_new = _sample_call(ys, mod[bsz:bsz + n_streams], *pre, bias_s, *post,
                                                 jnp.transpose(cache_attn_k[l], _TO_FEATURE_MAJOR),
                                                 jnp.transpose(cache_attn_v[l], _TO_FEATURE_MAJOR))
        ys = ys2.reshape(n_streams, n_new, D_MODEL)
        ks_rows.append(k_new.reshape(n_streams, n_new, B_HEADS, B_HDIM))
        vs_rows.append(v_new.reshape(n_streams, n_new, B_HEADS, B_HDIM))
        va_rows.append(va_new.reshape(n_streams, n_new, A_WIDTH))

    return (yp, ys, jnp.stack(kp_rows), jnp.stack(vp_rows), jnp.stack(ks_rows), jnp.stack(vs_rows),
            jnp.stack(va_rows))
```

```python
import functools

import jax
import jax.numpy as jnp
from jax import lax
from jax.experimental import pallas as pl
from jax.experimental.pallas import tpu as pltpu

LANES = 128
SUBLANES = 8
BF16_ROWS = 16
VMEM_LIMIT_BYTES = 60 * 1024 * 1024

D_MODEL = 1024
A_WIDTH = 512
A_GROUPS = 8
A_GDIM = A_WIDTH // A_GROUPS
MLP_CHUNK = 128
B_WIDTH = 512
B_HEADS = 8
B_HDIM = B_WIDTH // B_HEADS
CHUNK = 64
KV_WIN = 512
REL_CLIP = 128
N_REL = 2 * REL_CLIP + 1
EPS = 1e-6
D_IN = 3 * A_WIDTH + 4 * B_WIDTH
NEG = -1e30
LOG2E = 1.4426950408889634
Q_SCALE = B_HDIM ** -0.5 * LOG2E

N_PAIRS = B_HEADS // 2
Q_BLOCK = 2 * CHUNK
K_BLOCK = KV_WIN + Q_BLOCK
SEQ_TILE = 1024
PROJ_ROWS = 512
POST_ROWS = 256
SCORE_LEAD = 3
SCORE_SLOTS = SCORE_LEAD + 1
ADA_ROWS = 16

C_U, C_V, C_GA, C_Q, C_K, C_VV, C_GB = (i * 512 for i in range(7))

_CONTRACT_LANES = (((1,), (1,)), ((), ()))

_TO_FEATURE_MAJOR = (0, 2, 3, 1)
_FROM_FEATURE_MAJOR = (0, 3, 1, 2)


def _sigmoid_exp2(neg_arg_log2):
    return 1.0 / (1.0 + jnp.exp2(neg_arg_log2))


def _gelu(x):
    c1 = -2.0 * (2.0 / jnp.pi) ** 0.5 * LOG2E
    return x * _sigmoid_exp2(x * (c1 + (c1 * 0.044715) * (x * x)))


def _silu(x):
    return x * _sigmoid_exp2(x * (-LOG2E))


def _even_lane_mask(shape):
    lane = lax.broadcasted_iota(jnp.int32, shape, len(shape) - 1)
    return (lane % LANES) < B_HDIM


def _ada_block(c_ref, w_ref, b_ref, o_ref):
    c = c_ref[...]
    o_ref[...] = jnp.dot(_silu(c), w_ref[...], preferred_element_type=jnp.float32) + b_ref[...]


def _relb_block(base_ref, bt_ref, bs_ref, n_new):
    row = lax.broadcasted_iota(jnp.int32, (Q_BLOCK, K_BLOCK), 0)
    col = lax.broadcasted_iota(jnp.int32, (Q_BLOCK, K_BLOCK), 1)
    band_lo = jnp.where(row < CHUNK, 0, CHUNK)
    rel = col - band_lo
    outside = jnp.logical_or(rel < 0, rel >= KV_WIN + CHUNK)
    for par in range(2):
        base = base_ref[0, par:par + 1, :] * LOG2E
        x = jnp.broadcast_to(base, (Q_BLOCK, K_BLOCK))
        shift = 1
        while shift < Q_BLOCK:
            x = jnp.where((row & shift) != 0, jnp.roll(x, shift, axis=1), x)
            shift *= 2
        far = jnp.broadcast_to(base[:, 0:1], (Q_BLOCK, K_BLOCK))
        x = jnp.where(col < row, far, x)
        bs_ref[0, par] = x[0:n_new, :]
        bt_ref[0, :, par * Q_BLOCK:(par + 1) * Q_BLOCK] = jnp.where(outside, NEG, x).T


PREP_COLS = 512
N_PREP = D_IN // PREP_COLS
N_ADA = 3 * D_MODEL // PREP_COLS
WOUT_COLS = D_MODEL // N_PAIRS


def _prep_kernel(c_ref, wada_ref, bada_ref, base_ref, win_ref, wout_ref,
                 mod_ref, bt_ref, bs_ref, winb_ref, woutb_ref, *, n_new):
    j = pl.program_id(0)
    winb_ref[...] = win_ref[...].astype(jnp.bfloat16)

    @pl.when(j < N_ADA)
    def _():
        _ada_block(c_ref, wada_ref, bada_ref, mod_ref)

    @pl.when(j < N_PAIRS)
    def _():
        woutb_ref[...] = wout_ref[...].astype(jnp.bfloat16)
        _relb_block(base_ref, bt_ref, bs_ref, n_new)


def _prep_call(c_all, w_ada, b_ada, base, w_in, w_out, n_new):
    f32, bf16 = jnp.float32, jnp.bfloat16

    def upto(n):
        return lambda j: jnp.minimum(j, n - 1)

    ada_j, pair_j = upto(N_ADA), upto(N_PAIRS)
    return pl.pallas_call(
        functools.partial(_prep_kernel, n_new=n_new),
        grid=(N_PREP,),
        in_specs=[
            pl.BlockSpec((ADA_ROWS, D_MODEL), lambda j: (0, 0)),
            pl.BlockSpec((D_MODEL, PREP_COLS), lambda j: (0, ada_j(j))),
            pl.BlockSpec((1, PREP_COLS), lambda j: (0, ada_j(j))),
            pl.BlockSpec((1, 2, K_BLOCK), lambda j: (pair_j(j), 0, 0)),
            pl.BlockSpec((D_MODEL, PREP_COLS), lambda j: (0, j)),
            pl.BlockSpec((D_MODEL, WOUT_COLS), lambda j: (0, pair_j(j))),
        ],
        out_specs=[
            pl.BlockSpec((ADA_ROWS, PREP_COLS), lambda j: (0, ada_j(j))),
            pl.BlockSpec((1, K_BLOCK, 2 * Q_BLOCK), lambda j: (pair_j(j), 0, 0)),
            pl.BlockSpec((1, 2, n_new, K_BLOCK), lambda j: (pair_j(j), 0, 0, 0)),
            pl.BlockSpec((D_MODEL, PREP_COLS), lambda j: (0, j)),
            pl.BlockSpec((D_MODEL, WOUT_COLS), lambda j: (0, pair_j(j))),
        ],
        out_shape=[
            jax.ShapeDtypeStruct((ADA_ROWS, 3 * D_MODEL), f32),
            jax.ShapeDtypeStruct((N_PAIRS, K_BLOCK, 2 * Q_BLOCK), f32),
            jax.ShapeDtypeStruct((N_PAIRS, 2, n_new, K_BLOCK), f32),
            jax.ShapeDtypeStruct((D_MODEL, D_IN), bf16),
            jax.ShapeDtypeStruct((D_MODEL, D_MODEL), bf16),
        ],
        compiler_params=pltpu.CompilerParams(dimension_semantics=("arbitrary",)),
        name="prep",
    )(c_all, w_ada, b_ada.reshape(1, 3 * D_MODEL), base.reshape(N_PAIRS, 2, K_BLOCK), w_in, w_out)


def _tril_pairs(ws_ref, wt_ref, n):
    row = lax.broadcasted_iota(jnp.int32, (n, n), 0)
    col = lax.broadcasted_iota(jnp.int32, (n, n), 1)
    keep = col <= row
    for g in range(A_GROUPS):
        w = jnp.where(keep, ws_ref[g, :n, :n], 0.0)
        wt_ref[g // 2, (g % 2) * n:(g % 2 + 1) * n, :] = w.astype(jnp.bfloat16)


def _rms_rows(x):
    return x * lax.rsqrt(jnp.mean(x * x, axis=-1, keepdims=True) + EPS)


def _layernorm(x, g, b):
    mu = jnp.mean(x, axis=-1, keepdims=True)
    xc = x - mu
    var = jnp.mean(xc * xc, axis=-1, keepdims=True)
    return xc * lax.rsqrt(var + EPS) * g + b


def _pair_select(stacked, n):
    return jnp.where(_even_lane_mask((n, LANES)), stacked[:n], stacked[n:])


def _split_heads(x):
    return x.reshape(x.shape[0], B_HEADS, B_HDIM)


def _to_feature_major(x):
    return x.T.reshape(B_HEADS, B_HDIM, x.shape[0])


def _store_q(q, qe_ref, qo_ref):
    q = (q * Q_SCALE).astype(jnp.bfloat16)
    even = _even_lane_mask(q.shape)
    zero = jnp.zeros((), jnp.bfloat16)
    qe_ref[...] = jnp.where(even, q, zero)
    qo_ref[...] = jnp.where(even, zero, q)


def _prompt_kernel(x_ref, mod_ref, gpre_ref, win_ref, lng_ref, lnb_ref, ws_ref, bfull_ref,
                   biast_ref, wout_ref, gpost_ref,
                   y_ref, klast_ref, vlast_ref,
                   h_s, usg_s, va_s, qe_s, qo_s, k_s, vt_s, sgb_s, o_s, wt_s, st_s, kf_s, vf_s):
    t = pl.program_id(1)
    nt = pl.num_programs(1)
    T = SEQ_TILE

    @pl.when(jnp.logical_and(pl.program_id(0) == 0, t == 0))
    def _():
        _tril_pairs(ws_ref, wt_s, MLP_CHUNK)

    shift = mod_ref[0, 0:1, :]
    scale = mod_ref[0, 1:2, :]
    gate = mod_ref[0, 2:3, :]

    pre_scale = gpre_ref[...] * (1.0 + scale)

    def proj(c0, rows):
        return jnp.dot(h_s[rows, :], win_ref[:, c0:c0 + 512], preferred_element_type=jnp.float32)

    for lo in range(0, T, PROJ_ROWS):
        rb = slice(lo, lo + PROJ_ROWS)
        halves = [slice(lo, lo + PROJ_ROWS // 2), slice(lo + PROJ_ROWS // 2, lo + PROJ_ROWS)]
        for rows in halves:
            h_s[rows, :] = (_rms_rows(x_ref[0, rows, :]) * pre_scale + shift).astype(jnp.bfloat16)
        for rows in halves:
            usg_s[rows, :] = _gelu(proj(C_U, rows))
        v = proj(C_VV, rb)
        vt_s[:, KV_WIN + lo:KV_WIN + lo + PROJ_ROWS] = v.T.astype(jnp.bfloat16)
        va_s[rb, :] = _layernorm(_gelu(proj(C_V, rb)), lng_ref[...], lnb_ref[...]).astype(jnp.bfloat16)
        _store_q(proj(C_Q, rb), qe_s.at[rb, :], qo_s.at[rb, :])
        usg_s[rb, :] = usg_s[rb, :] * _silu(proj(C_GA, rb))

        for c0 in range(lo, lo + PROJ_ROWS, 2 * MLP_CHUNK):
            chunks = [slice(c0, c0 + MLP_CHUNK), slice(c0 + MLP_CHUNK, c0 + 2 * MLP_CHUNK)]
            for p in range(N_PAIRS):
                lanes = slice(p * LANES, (p + 1) * LANES)
                slabs = jnp.concatenate([va_s[rows, lanes] for rows in chunks], axis=1)
                mix = jnp.dot(wt_s[p], slabs, preferred_element_type=jnp.float32)
                for i, rows in enumerate(chunks):
                    mixed = _pair_select(mix[:, i * LANES:(i + 1) * LANES], MLP_CHUNK)
                    o_s[rows, lanes] = (usg_s[rows, lanes] * (mixed + bfull_ref[:, lanes])
                                        ).astype(jnp.bfloat16)

        sgb_s[rb, :] = _silu(proj(C_GB, rb))
        k = proj(C_K, rb)
        k_s[KV_WIN + lo:KV_WIN + lo + PROJ_ROWS, :] = k.astype(jnp.bfloat16)
        if lo + PROJ_ROWS == T:
            kf_s[...] = k
            vf_s[...] = v

    def key_lo(qb, first_tile):
        return max(KV_WIN - qb * Q_BLOCK, 0) if first_tile else 0

    def scores(qb, p, slot, first_tile):
        r0, lo = qb * Q_BLOCK, key_lo(qb, first_tile)
        lanes = slice(p * LANES, (p + 1) * LANES)
        qs = jnp.concatenate([qe_s[r0:r0 + Q_BLOCK, lanes],
                              qo_s[r0:r0 + Q_BLOCK, lanes]], axis=0)
        half = (K_BLOCK - lo) // 2
        for a in (lo, lo + half):
            st_s[slot, a:a + half, :] = lax.dot_general(
                k_s[r0 + a:r0 + a + half, lanes], qs, _CONTRACT_LANES,
                preferred_element_type=jnp.float32) + biast_ref[p, a:a + half, :]

    def finish(qb, p, slot, first_tile):
        r0, lo = qb * Q_BLOCK, key_lo(qb, first_tile)
        st = st_s[slot, lo:K_BLOCK, :]
        m = jnp.max(st, axis=0, keepdims=True)
        pt = jnp.exp2(st - m).astype(jnp.bfloat16)
        ones = jnp.ones((BF16_ROWS, K_BLOCK - lo), jnp.bfloat16)
        ots = []
        for par in range(2):
            feats = slice(p * LANES + par * B_HDIM, p * LANES + (par + 1) * B_HDIM)
            qcols = slice(par * Q_BLOCK, (par + 1) * Q_BLOCK)
            lhs = jnp.concatenate([vt_s[feats, r0 + lo:r0 + K_BLOCK], ones], axis=0)
            ot = jnp.dot(lhs, pt[:, qcols], preferred_element_type=jnp.float32)
            ots.append(ot[0:B_HDIM] * (1.0 / ot[B_HDIM:B_HDIM + 1]))
        yb = jnp.concatenate(ots, axis=0).T * sgb_s[r0:r0 + Q_BLOCK, p * LANES:(p + 1) * LANES]
        o_s[r0:r0 + Q_BLOCK, B_WIDTH + p * LANES:B_WIDTH + (p + 1) * LANES] = (
            yb.astype(jnp.bfloat16))

    def attend(first_tile):
        items = [(qb, p) for qb in range(T // Q_BLOCK) for p in range(N_PAIRS)]
        for j in range(SCORE_LEAD):
            scores(*items[j], j % SCORE_SLOTS, first_tile)
        for i, (qb, p) in enumerate(items):
            if i + SCORE_LEAD < len(items):
                scores(*items[i + SCORE_LEAD], (i + SCORE_LEAD) % SCORE_SLOTS, first_tile)
            finish(qb, p, i % SCORE_SLOTS, first_tile)

    @pl.when(t == 0)
    def _():
        attend(True)

    @pl.when(t > 0)
    def _():
        attend(False)

    post_scale = gate * gpost_ref[...]
    for lo in range(0, T, POST_ROWS):
        rows = slice(lo, lo + POST_ROWS)
        o = jnp.dot(o_s[rows, :], wout_ref[...], preferred_element_type=jnp.float32)
        y_ref[0, rows, :] = x_ref[0, rows, :] + _rms_rows(o) * post_scale

    k_s[0:KV_WIN, :] = k_s[T:T + KV_WIN, :]
    vt_s[:, 0:KV_WIN] = vt_s[:, T:T + KV_WIN]

    @pl.when(t == nt - 1)
    def _():
        klast_ref[0] = _to_feature_major(kf_s[...])
        vlast_ref[0] = _to_feature_major(vf_s[...])


def _const_spec(shape):
    nd = len(shape)
    return pl.BlockSpec(shape, lambda b, t: (0,) * nd, pipeline_mode=pl.Buffered(1))


def _prompt_call(x, mod, g_pre, w_in, ln_g, ln_b, w_s, b_full, bias_t, w_out, g_post):
    bsz, seq, _ = x.shape
    T = SEQ_TILE
    assert seq % T == 0 and T % PROJ_ROWS == 0 and PROJ_ROWS == KV_WIN and T % POST_ROWS == 0
    nt = seq // T
    f32, bf16 = jnp.float32, jnp.bfloat16
    return pl.pallas_call(
        _prompt_kernel,
        grid=(bsz, nt),
        in_specs=[
            pl.BlockSpec((1, T, D_MODEL), lambda b, t: (b, t, 0)),
            pl.BlockSpec((1, 3, D_MODEL), lambda b, t: (b, 0, 0)),
            _const_spec((1, D_MODEL)),
            _const_spec((D_MODEL, D_IN)),
            _const_spec((1, A_WIDTH)),
            _const_spec((1, A_WIDTH)),
            _const_spec((A_GROUPS, MLP_CHUNK, MLP_CHUNK)),
            _const_spec((MLP_CHUNK, A_WIDTH)),
            _const_spec((N_PAIRS, K_BLOCK, 2 * Q_BLOCK)),
            _const_spec((D_MODEL, D_MODEL)),
            _const_spec((1, D_MODEL)),
        ],
        out_specs=[
            pl.BlockSpec((1, T, D_MODEL), lambda b, t: (b, t, 0)),
            pl.BlockSpec((1, B_HEADS, B_HDIM, KV_WIN), lambda b, t: (b, 0, 0, 0)),
            pl.BlockSpec((1, B_HEADS, B_HDIM, KV_WIN), lambda b, t: (b, 0, 0, 0)),
        ],
        out_shape=[
            jax.ShapeDtypeStruct((bsz, seq, D_MODEL), f32),
            jax.ShapeDtypeStruct((bsz, B_HEADS, B_HDIM, KV_WIN), f32),
            jax.ShapeDtypeStruct((bsz, B_HEADS, B_HDIM, KV_WIN), f32),
        ],
        scratch_shapes=[
            pltpu.VMEM((T, D_MODEL), bf16),
            pltpu.VMEM((T, A_WIDTH), f32),
            pltpu.VMEM((T, A_WIDTH), bf16),
            pltpu.VMEM((T, B_WIDTH), bf16),
            pltpu.VMEM((T, B_WIDTH), bf16),
            pltpu.VMEM((KV_WIN + T, B_WIDTH), bf16),
            pltpu.VMEM((B_WIDTH, KV_WIN + T), bf16),
            pltpu.VMEM((T, B_WIDTH), f32),
            pltpu.VMEM((T, D_MODEL), bf16),
            pltpu.VMEM((N_PAIRS, 2 * MLP_CHUNK, MLP_CHUNK), bf16),
            pltpu.VMEM((SCORE_SLOTS, K_BLOCK, 2 * Q_BLOCK), f32),
            pltpu.VMEM((KV_WIN, B_WIDTH), f32),
            pltpu.VMEM((KV_WIN, B_WIDTH), f32),
        ],
        compiler_params=pltpu.CompilerParams(
            dimension_semantics=("arbitrary", "arbitrary"),
            vmem_limit_bytes=VMEM_LIMIT_BYTES),
        name="prompt",
    )(x, mod, g_pre, w_in, ln_g, ln_b, w_s, b_full, bias_t, w_out, g_post)


def _sample_kernel(x_ref, mod_ref, gpre_ref, win_ref, lng_ref, lnb_ref, ws_ref, bfull_ref,
                   bias_ref, wout_ref, gpost_ref, ck_ref, cv_ref,
                   y_ref, knew_ref, vnew_ref, vanew_ref,
                   usg_s, va_s, qe_s, qo_s, k_s, v_s, sgb_s, o_s, wt_s, *, n_streams, n_new):
    b = pl.program_id(0)
    S = n_new

    @pl.when(b == 0)
    def _():
        _tril_pairs(ws_ref, wt_s, S)
        gpre = gpre_ref[...]
        hs = []
        for i in range(n_streams):
            xi = x_ref[i * S:(i + 1) * S, :]
            hs.append(_rms_rows(xi) * (gpre * (1.0 + mod_ref[i, 1:2, :])) + mod_ref[i, 0:1, :])
        h = jnp.concatenate(hs, axis=0).astype(jnp.bfloat16)

        def proj(c0):
            return jnp.dot(h, win_ref[:, c0:c0 + 512], preferred_element_type=jnp.float32)

        va = _layernorm(_gelu(proj(C_V)), lng_ref[...], lnb_ref[...])
        vanew_ref[...] = va
        va_s[...] = va.astype(jnp.bfloat16)
        usg_s[...] = _gelu(proj(C_U)) * _silu(proj(C_GA))
        _store_q(proj(C_Q), qe_s, qo_s)
        k = proj(C_K)
        v = proj(C_VV)
        knew_ref[...] = _split_heads(k)
        vnew_ref[...] = _split_heads(v)
        k_s[...] = k.astype(jnp.bfloat16)
        v_s[...] = v.astype(jnp.bfloat16)
        sgb_s[...] = _silu(proj(C_GB))

    r0 = pl.multiple_of(b * S, S)
    rows = pl.ds(r0, S)
    pair_lanes = [slice(p * LANES, (p + 1) * LANES) for p in range(N_PAIRS)]
    for p, lanes in enumerate(pair_lanes):
        mix = jnp.dot(wt_s[p], va_s[rows, lanes], preferred_element_type=jnp.float32)
        mixed = _pair_select(mix, S) + bfull_ref[0:S, lanes]
        o_s[rows, lanes] = (usg_s[rows, lanes] * mixed).astype(jnp.bfloat16)

    scores = []
    for p, lanes in enumerate(pair_lanes):
        qs = jnp.concatenate([qe_s[rows, lanes], qo_s[rows, lanes]], axis=0)
        kct = ck_ref[0, 2 * p:2 * p + 2].reshape(LANES, KV_WIN).astype(jnp.bfloat16)
        bias = jnp.concatenate([bias_ref[p, 0], bias_ref[p, 1]], axis=0)
        s1 = jnp.dot(qs, kct, preferred_element_type=jnp.float32) + bias[:, 0:KV_WIN]
        s2 = lax.dot_general(qs, k_s[rows, lanes], _CONTRACT_LANES,
                             preferred_element_type=jnp.float32) + bias[:, KV_WIN:KV_WIN + S]
        scores.append((s1, s2))
    weights = []
    for s1, s2 in scores:
        m = jnp.maximum(jnp.max(s1, axis=-1, keepdims=True), jnp.max(s2, axis=-1, keepdims=True))
        e1 = jnp.exp2(s1 - m)
        e2 = jnp.exp2(s2 - m)
        l = jnp.sum(e1, axis=-1, keepdims=True) + jnp.sum(e2, axis=-1, keepdims=True)
        weights.append((e1.astype(jnp.bfloat16), e2.astype(jnp.bfloat16), 1.0 / l))
    for p, lanes in enumerate(pair_lanes):
        e1, e2, inv_l = weights[p]
        vct = cv_ref[0, 2 * p:2 * p + 2].reshape(LANES, KV_WIN).astype(jnp.bfloat16)
        o2 = (lax.dot_general(e1, vct, _CONTRACT_LANES, preferred_element_type=jnp.float32)
              + jnp.dot(e2, v_s[rows, lanes], preferred_element_type=jnp.float32))
        yb = _pair_select(o2 * inv_l, S) * sgb_s[rows, lanes]
        o_s[rows, B_WIDTH + p * LANES:B_WIDTH + (p + 1) * LANES] = yb.astype(jnp.bfloat16)

    @pl.when(b == n_streams - 1)
    def _():
        o = jnp.dot(o_s[...], wout_ref[...], preferred_element_type=jnp.float32)
        on = _rms_rows(o) * gpost_ref[...]
        for i in range(n_streams):
            sl = slice(i * S, (i + 1) * S)
            y_ref[sl, :] = x_ref[sl, :] + mod_ref[i, 2:3, :] * on[sl, :]


def _sample_call(x, mod, g_pre, w_in, ln_g, ln_b, w_s, b_full, bias_s, w_out, g_post, ck, cv):
    n_streams, n_new, _ = x.shape
    assert n_new % BF16_ROWS == 0 and n_new <= CHUNK
    rows = n_streams * n_new
    f32, bf16 = jnp.float32, jnp.bfloat16

    def const(shape):
        nd = len(shape)
        return pl.BlockSpec(shape, lambda b: (0,) * nd, pipeline_mode=pl.Buffered(1))

    kern = functools.partial(_sample_kernel, n_streams=n_streams, n_new=n_new)
    return pl.pallas_call(
        kern,
        grid=(n_streams,),
        in_specs=[
            const((rows, D_MODEL)),
            const((n_streams, 3, D_MODEL)),
            const((1, D_MODEL)),
            const((D_MODEL, D_IN)),
            const((1, A_WIDTH)),
            const((1, A_WIDTH)),
            const((A_GROUPS, MLP_CHUNK, MLP_CHUNK)),
            const((MLP_CHUNK, A_WIDTH)),
            const((N_PAIRS, 2, n_new, K_BLOCK)),
            const((D_MODEL, D_MODEL)),
            const((1, D_MODEL)),
            pl.BlockSpec((1, B_HEADS, B_HDIM, KV_WIN), lambda b: (b, 0, 0, 0)),
            pl.BlockSpec((1, B_HEADS, B_HDIM, KV_WIN), lambda b: (b, 0, 0, 0)),
        ],
        out_specs=[
            pl.BlockSpec((rows, D_MODEL), lambda b: (0, 0)),
            pl.BlockSpec((rows, B_HEADS, B_HDIM), lambda b: (0, 0, 0)),
            pl.BlockSpec((rows, B_HEADS, B_HDIM), lambda b: (0, 0, 0)),
            pl.BlockSpec((rows, A_WIDTH), lambda b: (0, 0)),
        ],
        out_shape=[
            jax.ShapeDtypeStruct((rows, D_MODEL), f32),
            jax.ShapeDtypeStruct((rows, B_HEADS, B_HDIM), f32),
            jax.ShapeDtypeStruct((rows, B_HEADS, B_HDIM), f32),
            jax.ShapeDtypeStruct((rows, A_WIDTH), f32),
        ],
        scratch_shapes=[
            pltpu.VMEM((rows, A_WIDTH), f32),
            pltpu.VMEM((rows, A_WIDTH), bf16),
            pltpu.VMEM((rows, B_WIDTH), bf16),
            pltpu.VMEM((rows, B_WIDTH), bf16),
            pltpu.VMEM((rows, B_WIDTH), bf16),
            pltpu.VMEM((rows, B_WIDTH), bf16),
            pltpu.VMEM((rows, B_WIDTH), f32),
            pltpu.VMEM((rows, D_MODEL), bf16),
            pltpu.VMEM((N_PAIRS, 2 * n_new, n_new), bf16),
        ],
        compiler_params=pltpu.CompilerParams(
            dimension_semantics=("arbitrary",),
            vmem_limit_bytes=VMEM_LIMIT_BYTES),
        name="sample",
    )(x.reshape(rows, D_MODEL), mod, g_pre, w_in, ln_g, ln_b, w_s, b_full, bias_s, w_out, g_post,
      ck, cv)


def _bias_base(rel_bias):
    n_far = KV_WIN - REL_CLIP
    far = jnp.broadcast_to(rel_bias[:, N_REL - 1:N_REL], (B_HEADS, n_far))
    near = rel_bias[:, N_REL - 1:0:-1]
    return jnp.concatenate([far, near], axis=1)


def kernel(x_prompt, x_sample, cache_attn_k, cache_attn_v, c_prompt, c_sample, g_pre, w_ada, b_ada,
           w_in, ln_g, ln_b, w_s, b_s, rel_bias, w_out, g_post):
    depth = g_pre.shape[0]
    bsz, seq, _ = x_prompt.shape
    n_streams, n_new, _ = x_sample.shape
    win = cache_attn_k.shape[2]
    assert win == KV_WIN and bsz + n_streams <= ADA_ROWS

    yp, ys = x_prompt, x_sample
    kp_rows, vp_rows, ks_rows, vs_rows, va_rows = [], [], [], [], []
    c_all = jnp.concatenate(
        [c_prompt, c_sample, jnp.zeros((ADA_ROWS - bsz - n_streams, D_MODEL), c_prompt.dtype)], axis=0)
    for l in range(depth):
        mod, bias_t, bias_s, w_in_b, w_out_b = _prep_call(
            c_all, w_ada[l], b_ada[l], _bias_base(rel_bias[l]), w_in[l], w_out[l], n_new)
        mod = mod.reshape(ADA_ROWS, 3, D_MODEL)
        b_full = jnp.repeat(b_s[l].T, A_GDIM, axis=1)
        pre = (g_pre[l].reshape(1, D_MODEL), w_in_b, ln_g[l].reshape(1, A_WIDTH),
               ln_b[l].reshape(1, A_WIDTH), w_s[l], b_full)
        post = (w_out_b, g_post[l].reshape(1, D_MODEL))

        yp, k_last, v_last = _prompt_call(yp, mod[:bsz], *pre, bias_t, *post)
        kp_rows.append(jnp.transpose(k_last, _FROM_FEATURE_MAJOR))
        vp_rows.append(jnp.transpose(v_last, _FROM_FEATURE_MAJOR))

        ys2, k_new, v_new, va_new = _sample_call(ys, mod[bsz:bsz + n_streams], *pre, bias_s, *post,
                                                 jnp.transpose(cache_attn_k[l], _TO_FEATURE_MAJOR),
                                                 jnp.transpose(cache_attn_v[l], _TO_FEATURE_MAJOR))
        ys = ys2.reshape(n_streams, n_new, D_MODEL)
        ks_rows.append(k_new.reshape(n_streams, n_new, B_HEADS, B_HDIM))
        vs_rows.append(v_new.reshape(n_streams, n_new, B_HEADS, B_HDIM))
        va_rows.append(va_new.reshape(n_streams, n_new, A_WIDTH))

    return (yp, ys, jnp.stack(kp_rows), jnp.stack(vp_rows), jnp.stack(ks_rows), jnp.stack(vs_rows),
            jnp.stack(va_rows))
```

```python
import functools

import jax
import jax.numpy as jnp
from jax import lax
from jax.experimental import pallas as pl
from jax.experimental.pallas import tpu as pltpu

LANES = 128
SUBLANES = 8
BF16_ROWS = 16
VMEM_LIMIT_BYTES = 60 * 1024 * 1024

D_MODEL = 1024
A_WIDTH = 512
A_GROUPS = 8
A_GDIM = A_WIDTH // A_GROUPS
MLP_CHUNK = 128
B_WIDTH = 512
B_HEADS = 8
B_HDIM = B_WIDTH // B_HEADS
CHUNK = 64
KV_WIN = 512
REL_CLIP = 128
N_REL = 2 * REL_CLIP + 1
EPS = 1e-6
D_IN = 3 * A_WIDTH + 4 * B_WIDTH
NEG = -1e30
LOG2E = 1.4426950408889634
Q_SCALE = B_HDIM ** -0.5 * LOG2E

N_PAIRS = B_HEADS // 2
Q_BLOCK = 2 * CHUNK
K_BLOCK = KV_WIN + Q_BLOCK
SEQ_TILE = 1024
PROJ_ROWS = 256
POST_ROWS = 256
SCORE_LEAD = 3
SCORE_SLOTS = SCORE_LEAD + 1
ADA_ROWS = 16

C_U, C_V, C_GA, C_Q, C_K, C_VV, C_GB = (i * 512 for i in range(7))

_CONTRACT_LANES = (((1,), (1,)), ((), ()))

_TO_FEATURE_MAJOR = (0, 2, 3, 1)
_FROM_FEATURE_MAJOR = (0, 3, 1, 2)


def _sigmoid_exp2(neg_arg_log2):
    return 1.0 / (1.0 + jnp.exp2(neg_arg_log2))


def _gelu(x):
    c1 = -2.0 * (2.0 / jnp.pi) ** 0.5 * LOG2E
    return x * _sigmoid_exp2(x * (c1 + (c1 * 0.044715) * (x * x)))


def _silu(x):
    return x * _sigmoid_exp2(x * (-LOG2E))


def _even_lane_mask(shape):
    lane = lax.broadcasted_iota(jnp.int32, shape, len(shape) - 1)
    return (lane % LANES) < B_HDIM


def _ada_block(c_ref, w_ref, b_ref, o_ref):
    c = c_ref[...]
    o_ref[...] = jnp.dot(_silu(c), w_ref[...], preferred_element_type=jnp.float32) + b_ref[...]


def _relb_block(base_ref, bt_ref, bs_ref, n_new):
    row = lax.broadcasted_iota(jnp.int32, (Q_BLOCK, K_BLOCK), 0)
    col = lax.broadcasted_iota(jnp.int32, (Q_BLOCK, K_BLOCK), 1)
    band_lo = jnp.where(row < CHUNK, 0, CHUNK)
    rel = col - band_lo
    outside = jnp.logical_or(rel < 0, rel >= KV_WIN + CHUNK)
    for par in range(2):
        base = base_ref[0, par:par + 1, :] * LOG2E
        x = jnp.broadcast_to(base, (Q_BLOCK, K_BLOCK))
        shift = 1
        while shift < Q_BLOCK:
            x = jnp.where((row & shift) != 0, jnp.roll(x, shift, axis=1), x)
            shift *= 2
        far = jnp.broadcast_to(base[:, 0:1], (Q_BLOCK, K_BLOCK))
        x = jnp.where(col < row, far, x)
        bs_ref[0, par] = x[0:n_new, :]
        bt_ref[0, :, par * Q_BLOCK:(par + 1) * Q_BLOCK] = jnp.where(outside, NEG, x).T


PREP_COLS = 512
N_PREP = D_IN // PREP_COLS
N_ADA = 3 * D_MODEL // PREP_COLS
WOUT_COLS = D_MODEL // N_PAIRS


def _prep_kernel(c_ref, wada_ref, bada_ref, base_ref, win_ref, wout_ref,
                 mod_ref, bt_ref, bs_ref, winb_ref, woutb_ref, *, n_new):
    j = pl.program_id(0)
    winb_ref[...] = win_ref[...].astype(jnp.bfloat16)

    @pl.when(j < N_ADA)
    def _():
        _ada_block(c_ref, wada_ref, bada_ref, mod_ref)

    @pl.when(j < N_PAIRS)
    def _():
        woutb_ref[...] = wout_ref[...].astype(jnp.bfloat16)
        _relb_block(base_ref, bt_ref, bs_ref, n_new)


def _prep_call(c_all, w_ada, b_ada, base, w_in, w_out, n_new):
    f32, bf16 = jnp.float32, jnp.bfloat16

    def upto(n):
        return lambda j: jnp.minimum(j, n - 1)

    ada_j, pair_j = upto(N_ADA), upto(N_PAIRS)
    return pl.pallas_call(
        functools.partial(_prep_kernel, n_new=n_new),
        grid=(N_PREP,),
        in_specs=[
            pl.BlockSpec((ADA_ROWS, D_MODEL), lambda j: (0, 0)),
            pl.BlockSpec((D_MODEL, PREP_COLS), lambda j: (0, ada_j(j))),
            pl.BlockSpec((1, PREP_COLS), lambda j: (0, ada_j(j))),
            pl.BlockSpec((1, 2, K_BLOCK), lambda j: (pair_j(j), 0, 0)),
            pl.BlockSpec((D_MODEL, PREP_COLS), lambda j: (0, j)),
            pl.BlockSpec((D_MODEL, WOUT_COLS), lambda j: (0, pair_j(j))),
        ],
        out_specs=[
            pl.BlockSpec((ADA_ROWS, PREP_COLS), lambda j: (0, ada_j(j))),
            pl.BlockSpec((1, K_BLOCK, 2 * Q_BLOCK), lambda j: (pair_j(j), 0, 0)),
            pl.BlockSpec((1, 2, n_new, K_BLOCK), lambda j: (pair_j(j), 0, 0, 0)),
            pl.BlockSpec((D_MODEL, PREP_COLS), lambda j: (0, j)),
            pl.BlockSpec((D_MODEL, WOUT_COLS), lambda j: (0, pair_j(j))),
        ],
        out_shape=[
            jax.ShapeDtypeStruct((ADA_ROWS, 3 * D_MODEL), f32),
            jax.ShapeDtypeStruct((N_PAIRS, K_BLOCK, 2 * Q_BLOCK), f32),
            jax.ShapeDtypeStruct((N_PAIRS, 2, n_new, K_BLOCK), f32),
            jax.ShapeDtypeStruct((D_MODEL, D_IN), bf16),
            jax.ShapeDtypeStruct((D_MODEL, D_MODEL), bf16),
        ],
        compiler_params=pltpu.CompilerParams(dimension_semantics=("arbitrary",)),
        name="prep",
    )(c_all, w_ada, b_ada.reshape(1, 3 * D_MODEL), base.reshape(N_PAIRS, 2, K_BLOCK), w_in, w_out)


def _tril_pairs(ws_ref, wt_ref, n):
    row = lax.broadcasted_iota(jnp.int32, (n, n), 0)
    col = lax.broadcasted_iota(jnp.int32, (n, n), 1)
    keep = col <= row
    for g in range(A_GROUPS):
        w = jnp.where(keep, ws_ref[g, :n, :n], 0.0)
        wt_ref[g // 2, (g % 2) * n:(g % 2 + 1) * n, :] = w.astype(jnp.bfloat16)


def _rms_rows(x):
    return x * lax.rsqrt(jnp.mean(x * x, axis=-1, keepdims=True) + EPS)


def _layernorm(x, g, b):
    mu = jnp.mean(x, axis=-1, keepdims=True)
    xc = x - mu
    var = jnp.mean(xc * xc, axis=-1, keepdims=True)
    return xc * lax.rsqrt(var + EPS) * g + b


def _pair_select(stacked, n):
    return jnp.where(_even_lane_mask((n, LANES)), stacked[:n], stacked[n:])


def _split_heads(x):
    return x.reshape(x.shape[0], B_HEADS, B_HDIM)


def _to_feature_major(x):
    return x.T.reshape(B_HEADS, B_HDIM, x.shape[0])


def _store_q(q, qe_ref, qo_ref):
    q = (q * Q_SCALE).astype(jnp.bfloat16)
    even = _even_lane_mask(q.shape)
    zero = jnp.zeros((), jnp.bfloat16)
    qe_ref[...] = jnp.where(even, q, zero)
    qo_ref[...] = jnp.where(even, zero, q)


def _prompt_kernel(x_ref, mod_ref, gpre_ref, win_ref, lng_ref, lnb_ref, ws_ref, bfull_ref,
                   biast_ref, wout_ref, gpost_ref,
                   y_ref, klast_ref, vlast_ref,
                   h_s, usg_s, va_s, qe_s, qo_s, k_s, vt_s, sgb_s, o_s, wt_s, st_s, kf_s, vf_s):
    t = pl.program_id(1)
    nt = pl.num_programs(1)
    T = SEQ_TILE

    @pl.when(jnp.logical_and(pl.program_id(0) == 0, t == 0))
    def _():
        _tril_pairs(ws_ref, wt_s, MLP_CHUNK)

    shift = mod_ref[0, 0:1, :]
    scale = mod_ref[0, 1:2, :]
    gate = mod_ref[0, 2:3, :]

    pre_scale = gpre_ref[...] * (1.0 + scale)

    def proj(c0, rows):
        return jnp.dot(h_s[rows, :], win_ref[:, c0:c0 + 512], preferred_element_type=jnp.float32)

    for lo in range(0, T, PROJ_ROWS):
        rb = slice(lo, lo + PROJ_ROWS)
        halves = [slice(lo, lo + PROJ_ROWS // 2), slice(lo + PROJ_ROWS // 2, lo + PROJ_ROWS)]
        for rows in halves:
            h_s[rows, :] = (_rms_rows(x_ref[0, rows, :]) * pre_scale + shift).astype(jnp.bfloat16)
        for rows in halves:
            usg_s[rows, :] = _gelu(proj(C_U, rows))
        v = proj(C_VV, rb)
        vt_s[:, KV_WIN + lo:KV_WIN + lo + PROJ_ROWS] = v.T.astype(jnp.bfloat16)
        va_s[rb, :] = _layernorm(_gelu(proj(C_V, rb)), lng_ref[...], lnb_ref[...]).astype(jnp.bfloat16)
        _store_q(proj(C_Q, rb), qe_s.at[rb, :], qo_s.at[rb, :])
        usg_s[rb, :] = usg_s[rb, :] * _silu(proj(C_GA, rb))

        for c0 in range(lo, lo + PROJ_ROWS, 2 * MLP_CHUNK):
            chunks = [slice(c0, c0 + MLP_CHUNK), slice(c0 + MLP_CHUNK, c0 + 2 * MLP_CHUNK)]
            for p in range(N_PAIRS):
                lanes = slice(p * LANES, (p + 1) * LANES)
                slabs = jnp.concatenate([va_s[rows, lanes] for rows in chunks], axis=1)
                mix = jnp.dot(wt_s[p], slabs, preferred_element_type=jnp.float32)
                for i, rows in enumerate(chunks):
                    mixed = _pair_select(mix[:, i * LANES:(i + 1) * LANES], MLP_CHUNK)
                    o_s[rows, lanes] = (usg_s[rows, lanes] * (mixed + bfull_ref[:, lanes])
                                        ).astype(jnp.bfloat16)

        sgb_s[rb, :] = _silu(proj(C_GB, rb))
        k = proj(C_K, rb)
        k_s[KV_WIN + lo:KV_WIN + lo + PROJ_ROWS, :] = k.astype(jnp.bfloat16)
        if lo >= T - KV_WIN:
            keep = slice(lo - (T - KV_WIN), lo - (T - KV_WIN) + PROJ_ROWS)
            kf_s[keep, :] = k
            vf_s[keep, :] = v

    def key_lo(qb, first_tile):
        return max(KV_WIN - qb * Q_BLOCK, 0) if first_tile else 0

    def scores(qb, p, slot, first_tile):
        r0, lo = qb * Q_BLOCK, key_lo(qb, first_tile)
        lanes = slice(p * LANES, (p + 1) * LANES)
        qs = jnp.concatenate([qe_s[r0:r0 + Q_BLOCK, lanes],
                              qo_s[r0:r0 + Q_BLOCK, lanes]], axis=0)
        half = (K_BLOCK - lo) // 2
        for a in (lo, lo + half):
            st_s[slot, a:a + half, :] = lax.dot_general(
                k_s[r0 + a:r0 + a + half, lanes], qs, _CONTRACT_LANES,
                preferred_element_type=jnp.float32) + biast_ref[p, a:a + half, :]

    def finish(qb, p, slot, first_tile):
        r0, lo = qb * Q_BLOCK, key_lo(qb, first_tile)
        st = st_s[slot, lo:K_BLOCK, :]
        m = jnp.max(st, axis=0, keepdims=True)
        pt = jnp.exp2(st - m).astype(jnp.bfloat16)
        ones = jnp.ones((BF16_ROWS, K_BLOCK - lo), jnp.bfloat16)
        ots = []
        for par in range(2):
            feats = slice(p * LANES + par * B_HDIM, p * LANES + (par + 1) * B_HDIM)
            qcols = slice(par * Q_BLOCK, (par + 1) * Q_BLOCK)
            lhs = jnp.concatenate([vt_s[feats, r0 + lo:r0 + K_BLOCK], ones], axis=0)
            ot = jnp.dot(lhs, pt[:, qcols], preferred_element_type=jnp.float32)
            ots.append(ot[0:B_HDIM] * (1.0 / ot[B_HDIM:B_HDIM + 1]))
        yb = jnp.concatenate(ots, axis=0).T * sgb_s[r0:r0 + Q_BLOCK, p * LANES:(p + 1) * LANES]
        o_s[r0:r0 + Q_BLOCK, B_WIDTH + p * LANES:B_WIDTH + (p + 1) * LANES] = (
            yb.astype(jnp.bfloat16))

    post_scale = gate * gpost_ref[...]

    def out_block(lo):
        rows = slice(lo, lo + POST_ROWS)
        o = jnp.dot(o_s[rows, :], wout_ref[...], preferred_element_type=jnp.float32)
        y_ref[0, rows, :] = x_ref[0, rows, :] + _rms_rows(o) * post_scale

    def attend(first_tile):
        items = [(qb, p) for qb in range(T // Q_BLOCK) for p in range(N_PAIRS)]
        for j in range(SCORE_LEAD):
            scores(*items[j], j % SCORE_SLOTS, first_tile)
        for i, (qb, p) in enumerate(items):
            if i + SCORE_LEAD < len(items):
                scores(*items[i + SCORE_LEAD], (i + SCORE_LEAD) % SCORE_SLOTS, first_tile)
            finish(qb, p, i % SCORE_SLOTS, first_tile)
            rows_done = (qb + 1) * Q_BLOCK
            if p == N_PAIRS - 1 and rows_done % POST_ROWS == 0:
                out_block(rows_done - POST_ROWS)

    @pl.when(t == 0)
    def _():
        attend(True)

    @pl.when(t > 0)
    def _():
        attend(False)

    k_s[0:KV_WIN, :] = k_s[T:T + KV_WIN, :]
    vt_s[:, 0:KV_WIN] = vt_s[:, T:T + KV_WIN]

    @pl.when(t == nt - 1)
    def _():
        klast_ref[0] = _to_feature_major(kf_s[...])
        vlast_ref[0] = _to_feature_major(vf_s[...])


def _const_spec(shape):
    nd = len(shape)
    return pl.BlockSpec(shape, lambda b, t: (0,) * nd, pipeline_mode=pl.Buffered(1))


def _prompt_call(x, mod, g_pre, w_in, ln_g, ln_b, w_s, b_full, bias_t, w_out, g_post):
    bsz, seq, _ = x.shape
    T = SEQ_TILE
    assert seq % T == 0 and T % PROJ_ROWS == 0 and KV_WIN % PROJ_ROWS == 0 and T % POST_ROWS == 0
    assert PROJ_ROWS % (2 * MLP_CHUNK) == 0
    nt = seq // T
    f32, bf16 = jnp.float32, jnp.bfloat16
    return pl.pallas_call(
        _prompt_kernel,
        grid=(bsz, nt),
        in_specs=[
            pl.BlockSpec((1, T, D_MODEL), lambda b, t: (b, t, 0)),
            pl.BlockSpec((1, 3, D_MODEL), lambda b, t: (b, 0, 0)),
            _const_spec((1, D_MODEL)),
            _const_spec((D_MODEL, D_IN)),
            _const_spec((1, A_WIDTH)),
            _const_spec((1, A_WIDTH)),
            _const_spec((A_GROUPS, MLP_CHUNK, MLP_CHUNK)),
            _const_spec((MLP_CHUNK, A_WIDTH)),
            _const_spec((N_PAIRS, K_BLOCK, 2 * Q_BLOCK)),
            _const_spec((D_MODEL, D_MODEL)),
            _const_spec((1, D_MODEL)),
        ],
        out_specs=[
            pl.BlockSpec((1, T, D_MODEL), lambda b, t: (b, t, 0)),
            pl.BlockSpec((1, B_HEADS, B_HDIM, KV_WIN), lambda b, t: (b, 0, 0, 0)),
            pl.BlockSpec((1, B_HEADS, B_HDIM, KV_WIN), lambda b, t: (b, 0, 0, 0)),
        ],
        out_shape=[
            jax.ShapeDtypeStruct((bsz, seq, D_MODEL), f32),
            jax.ShapeDtypeStruct((bsz, B_HEADS, B_HDIM, KV_WIN), f32),
            jax.ShapeDtypeStruct((bsz, B_HEADS, B_HDIM, KV_WIN), f32),
        ],
        scratch_shapes=[
            pltpu.VMEM((T, D_MODEL), bf16),
            pltpu.VMEM((T, A_WIDTH), f32),
            pltpu.VMEM((T, A_WIDTH), bf16),
            pltpu.VMEM((T, B_WIDTH), bf16),
            pltpu.VMEM((T, B_WIDTH), bf16),
            pltpu.VMEM((KV_WIN + T, B_WIDTH), bf16),
            pltpu.VMEM((B_WIDTH, KV_WIN + T), bf16),
            pltpu.VMEM((T, B_WIDTH), f32),
            pltpu.VMEM((T, D_MODEL), bf16),
            pltpu.VMEM((N_PAIRS, 2 * MLP_CHUNK, MLP_CHUNK), bf16),
            pltpu.VMEM((SCORE_SLOTS, K_BLOCK, 2 * Q_BLOCK), f32),
            pltpu.VMEM((KV_WIN, B_WIDTH), f32),
            pltpu.VMEM((KV_WIN, B_WIDTH), f32),
        ],
        compiler_params=pltpu.CompilerParams(
            dimension_semantics=("arbitrary", "arbitrary"),
            vmem_limit_bytes=VMEM_LIMIT_BYTES),
        name="prompt",
    )(x, mod, g_pre, w_in, ln_g, ln_b, w_s, b_full, bias_t, w_out, g_post)


def _sample_kernel(x_ref, mod_ref, gpre_ref, win_ref, lng_ref, lnb_ref, ws_ref, bfull_ref,
                   bias_ref, wout_ref, gpost_ref, ck_ref, cv_ref,
                   y_ref, knew_ref, vnew_ref, vanew_ref,
                   usg_s, va_s, qe_s, qo_s, k_s, v_s, sgb_s, o_s, wt_s, *, n_streams, n_new):
    b = pl.program_id(0)
    S = n_new

    @pl.when(b == 0)
    def _():
        _tril_pairs(ws_ref, wt_s, S)
        gpre = gpre_ref[...]
        hs = []
        for i in range(n_streams):
            xi = x_ref[i * S:(i + 1) * S, :]
            hs.append(_rms_rows(xi) * (gpre * (1.0 + mod_ref[i, 1:2, :])) + mod_ref[i, 0:1, :])
        h = jnp.concatenate(hs, axis=0).astype(jnp.bfloat16)

        def proj(c0):
            return jnp.dot(h, win_ref[:, c0:c0 + 512], preferred_element_type=jnp.float32)

        va = _layernorm(_gelu(proj(C_V)), lng_ref[...], lnb_ref[...])
        vanew_ref[...] = va
        va_s[...] = va.astype(jnp.bfloat16)
        usg_s[...] = _gelu(proj(C_U)) * _silu(proj(C_GA))
        _store_q(proj(C_Q), qe_s, qo_s)
        k = proj(C_K)
        v = proj(C_VV)
        knew_ref[...] = _split_heads(k)
        vnew_ref[...] = _split_heads(v)
        k_s[...] = k.astype(jnp.bfloat16)
        v_s[...] = v.astype(jnp.bfloat16)
        sgb_s[...] = _silu(proj(C_GB))

    r0 = pl.multiple_of(b * S, S)
    rows = pl.ds(r0, S)
    pair_lanes = [slice(p * LANES, (p + 1) * LANES) for p in range(N_PAIRS)]
    for p, lanes in enumerate(pair_lanes):
        mix = jnp.dot(wt_s[p], va_s[rows, lanes], preferred_element_type=jnp.float32)
        mixed = _pair_select(mix, S) + bfull_ref[0:S, lanes]
        o_s[rows, lanes] = (usg_s[rows, lanes] * mixed).astype(jnp.bfloat16)

    scores = []
    for p, lanes in enumerate(pair_lanes):
        qs = jnp.concatenate([qe_s[rows, lanes], qo_s[rows, lanes]], axis=0)
        kct = ck_ref[0, 2 * p:2 * p + 2].reshape(LANES, KV_WIN).astype(jnp.bfloat16)
        bias = jnp.concatenate([bias_ref[p, 0], bias_ref[p, 1]], axis=0)
        s1 = jnp.dot(qs, kct, preferred_element_type=jnp.float32) + bias[:, 0:KV_WIN]
        s2 = lax.dot_general(qs, k_s[rows, lanes], _CONTRACT_LANES,
                             preferred_element_type=jnp.float32) + bias[:, KV_WIN:KV_WIN + S]
        scores.append((s1, s2))
    weights = []
    for s1, s2 in scores:
        m = jnp.maximum(jnp.max(s1, axis=-1, keepdims=True), jnp.max(s2, axis=-1, keepdims=True))
        e1 = jnp.exp2(s1 - m)
        e2 = jnp.exp2(s2 - m)
        l = jnp.sum(e1, axis=-1, keepdims=True) + jnp.sum(e2, axis=-1, keepdims=True)
        weights.append((e1.astype(jnp.bfloat16), e2.astype(jnp.bfloat16), 1.0 / l))
    for p, lanes in enumerate(pair_lanes):
        e1, e2, inv_l = weights[p]
        vct = cv_ref[0, 2 * p:2 * p + 2].reshape(LANES, KV_WIN).astype(jnp.bfloat16)
        o2 = (lax.dot_general(e1, vct, _CONTRACT_LANES, preferred_element_type=jnp.float32)
              + jnp.dot(e2, v_s[rows, lanes], preferred_element_type=jnp.float32))
        yb = _pair_select(o2 * inv_l, S) * sgb_s[rows, lanes]
        o_s[rows, B_WIDTH + p * LANES:B_WIDTH + (p + 1) * LANES] = yb.astype(jnp.bfloat16)

    @pl.when(b == n_streams - 1)
    def _():
        o = jnp.dot(o_s[...], wout_ref[...], preferred_element_type=jnp.float32)
        on = _rms_rows(o) * gpost_ref[...]
        for i in range(n_streams):
            sl = slice(i * S, (i + 1) * S)
            y_ref[sl, :] = x_ref[sl, :] + mod_ref[i, 2:3, :] * on[sl, :]


def _sample_call(x, mod, g_pre, w_in, ln_g, ln_b, w_s, b_full, bias_s, w_out, g_post, ck, cv):
    n_streams, n_new, _ = x.shape
    assert n_new % BF16_ROWS == 0 and n_new <= CHUNK
    rows = n_streams * n_new
    f32, bf16 = jnp.float32, jnp.bfloat16

    def const(shape):
        nd = len(shape)
        return pl.BlockSpec(shape, lambda b: (0,) * nd, pipeline_mode=pl.Buffered(1))

    kern = functools.partial(_sample_kernel, n_streams=n_streams, n_new=n_new)
    return pl.pallas_call(
        kern,
        grid=(n_streams,),
        in_specs=[
            const((rows, D_MODEL)),
            const((n_streams, 3, D_MODEL)),
            const((1, D_MODEL)),
            const((D_MODEL, D_IN)),
            const((1, A_WIDTH)),
            const((1, A_WIDTH)),
            const((A_GROUPS, MLP_CHUNK, MLP_CHUNK)),
            const((MLP_CHUNK, A_WIDTH)),
            const((N_PAIRS, 2, n_new, K_BLOCK)),
            const((D_MODEL, D_MODEL)),
            const((1, D_MODEL)),
            pl.BlockSpec((1, B_HEADS, B_HDIM, KV_WIN), lambda b: (b, 0, 0, 0)),
            pl.BlockSpec((1, B_HEADS, B_HDIM, KV_WIN), lambda b: (b, 0, 0, 0)),
        ],
        out_specs=[
            pl.BlockSpec((rows, D_MODEL), lambda b: (0, 0)),
            pl.BlockSpec((rows, B_HEADS, B_HDIM), lambda b: (0, 0, 0)),
            pl.BlockSpec((rows, B_HEADS, B_HDIM), lambda b: (0, 0, 0)),
            pl.BlockSpec((rows, A_WIDTH), lambda b: (0, 0)),
        ],
        out_shape=[
            jax.ShapeDtypeStruct((rows, D_MODEL), f32),
            jax.ShapeDtypeStruct((rows, B_HEADS, B_HDIM), f32),
            jax.ShapeDtypeStruct((rows, B_HEADS, B_HDIM), f32),
            jax.ShapeDtypeStruct((rows, A_WIDTH), f32),
        ],
        scratch_shapes=[
            pltpu.VMEM((rows, A_WIDTH), f32),
            pltpu.VMEM((rows, A_WIDTH), bf16),
            pltpu.VMEM((rows, B_WIDTH), bf16),
            pltpu.VMEM((rows, B_WIDTH), bf16),
            pltpu.VMEM((rows, B_WIDTH), bf16),
            pltpu.VMEM((rows, B_WIDTH), bf16),
            pltpu.VMEM((rows, B_WIDTH), f32),
            pltpu.VMEM((rows, D_MODEL), bf16),
            pltpu.VMEM((N_PAIRS, 2 * n_new, n_new), bf16),
        ],
        compiler_params=pltpu.CompilerParams(
            dimension_semantics=("arbitrary",),
            vmem_limit_bytes=VMEM_LIMIT_BYTES),
        name="sample",
    )(x.reshape(rows, D_MODEL), mod, g_pre, w_in, ln_g, ln_b, w_s, b_full, bias_s, w_out, g_post,
      ck, cv)


def _bias_base(rel_bias):
    n_far = KV_WIN - REL_CLIP
    far = jnp.broadcast_to(rel_bias[:, N_REL - 1:N_REL], (B_HEADS, n_far))
    near = rel_bias[:, N_REL - 1:0:-1]
    return jnp.concatenate([far, near], axis=1)


def kernel(x_prompt, x_sample, cache_attn_k, cache_attn_v, c_prompt, c_sample, g_pre, w_ada, b_ada,
           w_in, ln_g, ln_b, w_s, b_s, rel_bias, w_out, g_post):
    depth = g_pre.shape[0]
    bsz, seq, _ = x_prompt.shape
    n_streams, n_new, _ = x_sample.shape
    win = cache_attn_k.shape[2]
    assert win == KV_WIN and bsz + n_streams <= ADA_ROWS

    yp, ys = x_prompt, x_sample
    kp_rows, vp_rows, ks_rows, vs_rows, va_rows = [], [], [], [], []
    c_all = jnp.concatenate(
        [c_prompt, c_sample, jnp.zeros((ADA_ROWS - bsz - n_streams, D_MODEL), c_prompt.dtype)], axis=0)
    for l in range(depth):
        mod, bias_t, bias_s, w_in_b, w_out_b = _prep_call(
            c_all, w_ada[l], b_ada[l], _bias_base(rel_bias[l]), w_in[l], w_out[l], n_new)
        mod = mod.reshape(ADA_ROWS, 3, D_MODEL)
        b_full = jnp.repeat(b_s[l].T, A_GDIM, axis=1)
        pre = (g_pre[l].reshape(1, D_MODEL), w_in_b, ln_g[l].reshape(1, A_WIDTH),
               ln_b[l].reshape(1, A_WIDTH), w_s[l], b_full)
        post = (w_out_b, g_post[l].reshape(1, D_MODEL))

        yp, k_last, v_last = _prompt_call(yp, mod[:bsz], *pre, bias_t, *post)
        kp_rows.append(jnp.transpose(k_last, _FROM_FEATURE_MAJOR))
        vp_rows.append(jnp.transpose(v_last, _FROM_FEATURE_MAJOR))

        ys2, k_new, v_new, va_new = _sample_call(ys, mod[bsz:bsz + n_streams], *pre, bias_s, *post,
                                                 jnp.transpose(cache_attn_k[l], _TO_FEATURE_MAJOR),
                                                 jnp.transpose(cache_attn_v[l], _TO_FEATURE_MAJOR))
        ys = ys2.reshape(n_streams, n_new, D_MODEL)
        ks_rows.append(k_new.reshape(n_streams, n_new, B_HEADS, B_HDIM))
        vs_rows.append(v_new.reshape(n_streams, n_new, B_HEADS, B_HDIM))
        va_rows.append(va_new.reshape(n_streams, n_new, A_WIDTH))

    return (yp, ys, jnp.stack(kp_rows), jnp.stack(vp_rows), jnp.stack(ks_rows), jnp.stack(vs_rows),
            jnp.stack(va_rows))
```

```python
import functools

import jax
import jax.numpy as jnp
from jax import lax
from jax.experimental import pallas as pl
from jax.experimental.pallas import tpu as pltpu

LANES = 128
SUBLANES = 8
BF16_ROWS = 16
VMEM_LIMIT_BYTES = 60 * 1024 * 1024

D_MODEL = 1024
A_WIDTH = 512
A_GROUPS = 8
A_GDIM = A_WIDTH // A_GROUPS
MLP_CHUNK = 128
B_WIDTH = 512
B_HEADS = 8
B_HDIM = B_WIDTH // B_HEADS
CHUNK = 64
KV_WIN = 512
REL_CLIP = 128
N_REL = 2 * REL_CLIP + 1
EPS = 1e-6
D_IN = 3 * A_WIDTH + 4 * B_WIDTH
NEG = -1e30
LOG2E = 1.4426950408889634
Q_SCALE = B_HDIM ** -0.5 * LOG2E

N_PAIRS = B_HEADS // 2
Q_BLOCK = 2 * CHUNK
K_BLOCK = KV_WIN + Q_BLOCK
SEQ_TILE = 1024
PROJ_ROWS = 256
POST_ROWS = 256
SCORE_LEAD = 3
SCORE_SLOTS = SCORE_LEAD + 1
ADA_ROWS = 16

C_U, C_V, C_GA, C_Q, C_K, C_VV, C_GB = (i * 512 for i in range(7))

_CONTRACT_LANES = (((1,), (1,)), ((), ()))

_TO_FEATURE_MAJOR = (0, 2, 3, 1)
_FROM_FEATURE_MAJOR = (0, 3, 1, 2)


def _sigmoid_exp2(neg_arg_log2):
    return 1.0 / (1.0 + jnp.exp2(neg_arg_log2))


def _gelu(x):
    c1 = -2.0 * (2.0 / jnp.pi) ** 0.5 * LOG2E
    return x * _sigmoid_exp2(x * (c1 + (c1 * 0.044715) * (x * x)))


def _silu(x):
    return x * _sigmoid_exp2(x * (-LOG2E))


def _even_lane_mask(shape):
    lane = lax.broadcasted_iota(jnp.int32, shape, len(shape) - 1)
    return (lane % LANES) < B_HDIM


def _ada_block(c_ref, w_ref, b_ref, o_ref):
    c = c_ref[...]
    o_ref[...] = jnp.dot(_silu(c), w_ref[...], preferred_element_type=jnp.float32) + b_ref[...]


def _relb_block(base_ref, bt_ref, bs_ref, n_new):
    row = lax.broadcasted_iota(jnp.int32, (Q_BLOCK, K_BLOCK), 0)
    col = lax.broadcasted_iota(jnp.int32, (Q_BLOCK, K_BLOCK), 1)
    band_lo = jnp.where(row < CHUNK, 0, CHUNK)
    rel = col - band_lo
    outside = jnp.logical_or(rel < 0, rel >= KV_WIN + CHUNK)
    for par in range(2):
        base = base_ref[0, par:par + 1, :] * LOG2E
        x = jnp.broadcast_to(base, (Q_BLOCK, K_BLOCK))
        shift = 1
        while shift < Q_BLOCK:
            x = jnp.where((row & shift) != 0, jnp.roll(x, shift, axis=1), x)
            shift *= 2
        far = jnp.broadcast_to(base[:, 0:1], (Q_BLOCK, K_BLOCK))
        x = jnp.where(col < row, far, x)
        bs_ref[0, par] = x[0:n_new, :]
        bt_ref[0, :, par * Q_BLOCK:(par + 1) * Q_BLOCK] = jnp.where(outside, NEG, x).T


PREP_COLS = 512
N_PREP = D_IN // PREP_COLS
N_ADA = 3 * D_MODEL // PREP_COLS
WOUT_COLS = D_MODEL // N_PAIRS


def _prep_kernel(c_ref, wada_ref, bada_ref, base_ref, win_ref, wout_ref,
                 mod_ref, bt_ref, bs_ref, winb_ref, woutb_ref, *, n_new):
    j = pl.program_id(0)
    winb_ref[...] = win_ref[...].astype(jnp.bfloat16)

    @pl.when(j < N_ADA)
    def _():
        _ada_block(c_ref, wada_ref, bada_ref, mod_ref)

    @pl.when(j < N_PAIRS)
    def _():
        woutb_ref[...] = wout_ref[...].astype(jnp.bfloat16)
        _relb_block(base_ref, bt_ref, bs_ref, n_new)


def _prep_call(c_all, w_ada, b_ada, base, w_in, w_out, n_new):
    f32, bf16 = jnp.float32, jnp.bfloat16

    def upto(n):
        return lambda j: jnp.minimum(j, n - 1)

    ada_j, pair_j = upto(N_ADA), upto(N_PAIRS)
    return pl.pallas_call(
        functools.partial(_prep_kernel, n_new=n_new),
        grid=(N_PREP,),
        in_specs=[
            pl.BlockSpec((ADA_ROWS, D_MODEL), lambda j: (0, 0)),
            pl.BlockSpec((D_MODEL, PREP_COLS), lambda j: (0, ada_j(j))),
            pl.BlockSpec((1, PREP_COLS), lambda j: (0, ada_j(j))),
            pl.BlockSpec((1, 2, K_BLOCK), lambda j: (pair_j(j), 0, 0)),
            pl.BlockSpec((D_MODEL, PREP_COLS), lambda j: (0, j)),
            pl.BlockSpec((D_MODEL, WOUT_COLS), lambda j: (0, pair_j(j))),
        ],
        out_specs=[
            pl.BlockSpec((ADA_ROWS, PREP_COLS), lambda j: (0, ada_j(j))),
            pl.BlockSpec((1, K_BLOCK, 2 * Q_BLOCK), lambda j: (pair_j(j), 0, 0)),
            pl.BlockSpec((1, 2, n_new, K_BLOCK), lambda j: (pair_j(j), 0, 0, 0)),
            pl.BlockSpec((D_MODEL, PREP_COLS), lambda j: (0, j)),
            pl.BlockSpec((D_MODEL, WOUT_COLS), lambda j: (0, pair_j(j))),
        ],
        out_shape=[
            jax.ShapeDtypeStruct((ADA_ROWS, 3 * D_MODEL), f32),
            jax.ShapeDtypeStruct((N_PAIRS, K_BLOCK, 2 * Q_BLOCK), f32),
            jax.ShapeDtypeStruct((N_PAIRS, 2, n_new, K_BLOCK), f32),
            jax.ShapeDtypeStruct((D_MODEL, D_IN), bf16),
            jax.ShapeDtypeStruct((D_MODEL, D_MODEL), bf16),
        ],
        compiler_params=pltpu.CompilerParams(dimension_semantics=("arbitrary",)),
        name="prep",
    )(c_all, w_ada, b_ada.reshape(1, 3 * D_MODEL), base.reshape(N_PAIRS, 2, K_BLOCK), w_in, w_out)


def _tril_pairs(ws_ref, wt_ref, n):
    row = lax.broadcasted_iota(jnp.int32, (n, n), 0)
    col = lax.broadcasted_iota(jnp.int32, (n, n), 1)
    keep = col <= row
    for g in range(A_GROUPS):
        w = jnp.where(keep, ws_ref[g, :n, :n], 0.0)
        wt_ref[g // 2, (g % 2) * n:(g % 2 + 1) * n, :] = w.astype(jnp.bfloat16)


def _rms_rows(x):
    return x * lax.rsqrt(jnp.mean(x * x, axis=-1, keepdims=True) + EPS)


def _layernorm(x, g, b):
    mu = jnp.mean(x, axis=-1, keepdims=True)
    xc = x - mu
    var = jnp.mean(xc * xc, axis=-1, keepdims=True)
    return xc * lax.rsqrt(var + EPS) * g + b


def _pair_select(stacked, n):
    return jnp.where(_even_lane_mask((n, LANES)), stacked[:n], stacked[n:])


def _split_heads(x):
    return x.reshape(x.shape[0], B_HEADS, B_HDIM)


def _to_feature_major(x):
    return x.T.reshape(B_HEADS, B_HDIM, x.shape[0])


def _store_q(q, qe_ref, qo_ref):
    q = (q * Q_SCALE).astype(jnp.bfloat16)
    even = _even_lane_mask(q.shape)
    zero = jnp.zeros((), jnp.bfloat16)
    qe_ref[...] = jnp.where(even, q, zero)
    qo_ref[...] = jnp.where(even, zero, q)


def _prompt_kernel(x_ref, mod_ref, gpre_ref, win_ref, lng_ref, lnb_ref, ws_ref, bfull_ref,
                   biast_ref, wout_ref, gpost_ref,
                   y_ref, klast_ref, vlast_ref,
                   h_s, usg_s, va_s, qe_s, qo_s, k_s, vt_s, sgb_s, o_s, wt_s, st_s, kf_s, vf_s):
    t = pl.program_id(1)
    nt = pl.num_programs(1)
    T = SEQ_TILE

    @pl.when(jnp.logical_and(pl.program_id(0) == 0, t == 0))
    def _():
        _tril_pairs(ws_ref, wt_s, MLP_CHUNK)

    shift = mod_ref[0, 0:1, :]
    scale = mod_ref[0, 1:2, :]
    gate = mod_ref[0, 2:3, :]

    pre_scale = gpre_ref[...] * (1.0 + scale)

    def proj(c0, rows):
        return jnp.dot(h_s[rows, :], win_ref[:, c0:c0 + 512], preferred_element_type=jnp.float32)

    def proj_pieces(lo):
        rb = slice(lo, lo + PROJ_ROWS)
        halves = [slice(lo, lo + PROJ_ROWS // 2), slice(lo + PROJ_ROWS // 2, lo + PROJ_ROWS)]
        hist = slice(KV_WIN + lo, KV_WIN + lo + PROJ_ROWS)
        keep = slice(lo - (T - KV_WIN), lo - (T - KV_WIN) + PROJ_ROWS) if lo >= T - KV_WIN else None

        def pre_norm():
            for rows in halves:
                h_s[rows, :] = (_rms_rows(x_ref[0, rows, :]) * pre_scale + shift).astype(jnp.bfloat16)

        def u_half(rows):
            usg_s[rows, :] = _gelu(proj(C_U, rows))

        def v_block():
            v = proj(C_VV, rb)
            vt_s[:, hist] = v.T.astype(jnp.bfloat16)
            if keep is not None:
                vf_s[keep, :] = v

        def va_block():
            va_s[rb, :] = _layernorm(_gelu(proj(C_V, rb)), lng_ref[...], lnb_ref[...]
                                     ).astype(jnp.bfloat16)

        def q_block():
            _store_q(proj(C_Q, rb), qe_s.at[rb, :], qo_s.at[rb, :])

        def ga_block():
            usg_s[rb, :] = usg_s[rb, :] * _silu(proj(C_GA, rb))

        def gmlp():
            for c0 in range(lo, lo + PROJ_ROWS, 2 * MLP_CHUNK):
                chunks = [slice(c0, c0 + MLP_CHUNK), slice(c0 + MLP_CHUNK, c0 + 2 * MLP_CHUNK)]
                for p in range(N_PAIRS):
                    lanes = slice(p * LANES, (p + 1) * LANES)
                    slabs = jnp.concatenate([va_s[rows, lanes] for rows in chunks], axis=1)
                    mix = jnp.dot(wt_s[p], slabs, preferred_element_type=jnp.float32)
                    for i, rows in enumerate(chunks):
                        mixed = _pair_select(mix[:, i * LANES:(i + 1) * LANES], MLP_CHUNK)
                        o_s[rows, lanes] = (usg_s[rows, lanes] * (mixed + bfull_ref[:, lanes])
                                            ).astype(jnp.bfloat16)

        def gb_block():
            sgb_s[rb, :] = _silu(proj(C_GB, rb))

        def k_block():
            k = proj(C_K, rb)
            k_s[hist, :] = k.astype(jnp.bfloat16)
            if keep is not None:
                kf_s[keep, :] = k

        return [pre_norm, functools.partial(u_half, halves[0]), functools.partial(u_half, halves[1]),
                v_block, va_block, q_block, ga_block, gmlp, gb_block, k_block]

    def key_lo(qb, first_tile):
        return max(KV_WIN - qb * Q_BLOCK, 0) if first_tile else 0

    def scores(qb, p, slot, first_tile):
        r0, lo = qb * Q_BLOCK, key_lo(qb, first_tile)
        lanes = slice(p * LANES, (p + 1) * LANES)
        qs = jnp.concatenate([qe_s[r0:r0 + Q_BLOCK, lanes],
                              qo_s[r0:r0 + Q_BLOCK, lanes]], axis=0)
        half = (K_BLOCK - lo) // 2
        for a in (lo, lo + half):
            st_s[slot, a:a + half, :] = lax.dot_general(
                k_s[r0 + a:r0 + a + half, lanes], qs, _CONTRACT_LANES,
                preferred_element_type=jnp.float32) + biast_ref[p, a:a + half, :]

    def finish(qb, p, slot, first_tile):
        r0, lo = qb * Q_BLOCK, key_lo(qb, first_tile)
        st = st_s[slot, lo:K_BLOCK, :]
        m = jnp.max(st, axis=0, keepdims=True)
        pt = jnp.exp2(st - m).astype(jnp.bfloat16)
        ones = jnp.ones((BF16_ROWS, K_BLOCK - lo), jnp.bfloat16)
        ots = []
        for par in range(2):
            feats = slice(p * LANES + par * B_HDIM, p * LANES + (par + 1) * B_HDIM)
            qcols = slice(par * Q_BLOCK, (par + 1) * Q_BLOCK)
            lhs = jnp.concatenate([vt_s[feats, r0 + lo:r0 + K_BLOCK], ones], axis=0)
            ot = jnp.dot(lhs, pt[:, qcols], preferred_element_type=jnp.float32)
            ots.append(ot[0:B_HDIM] * (1.0 / ot[B_HDIM:B_HDIM + 1]))
        yb = jnp.concatenate(ots, axis=0).T * sgb_s[r0:r0 + Q_BLOCK, p * LANES:(p + 1) * LANES]
        o_s[r0:r0 + Q_BLOCK, B_WIDTH + p * LANES:B_WIDTH + (p + 1) * LANES] = (
            yb.astype(jnp.bfloat16))

    post_scale = gate * gpost_ref[...]

    def out_block(lo, n):
        rows = slice(lo, lo + n)
        o = jnp.dot(o_s[rows, :], wout_ref[...], preferred_element_type=jnp.float32)
        y_ref[0, rows, :] = x_ref[0, rows, :] + _rms_rows(o) * post_scale

    def attend(first_tile):
        items = [(qb, p) for qb in range(T // Q_BLOCK) for p in range(N_PAIRS)]
        per_block = (PROJ_ROWS // Q_BLOCK) * N_PAIRS
        n_blocks = T // PROJ_ROWS
        for piece in proj_pieces(0):
            piece()
        pending = []
        for j in range(SCORE_LEAD):
            scores(*items[j], j % SCORE_SLOTS, first_tile)
        for i, (qb, p) in enumerate(items):
            blk, pos = divmod(i, per_block)
            if pos == 0 and blk + 1 < n_blocks:
                pending = proj_pieces((blk + 1) * PROJ_ROWS)
            if i + SCORE_LEAD < len(items):
                assert pos < per_block - SCORE_LEAD or not pending
                scores(*items[i + SCORE_LEAD], (i + SCORE_LEAD) % SCORE_SLOTS, first_tile)
            finish(qb, p, i % SCORE_SLOTS, first_tile)
            slots_left = per_block - SCORE_LEAD - pos
            if pending and slots_left > 0:
                n_now = -(-len(pending) // slots_left)
                for piece in pending[:n_now]:
                    piece()
                pending = pending[n_now:]
            rows_done = (qb + 1) * Q_BLOCK
            if p == N_PAIRS - 1 and rows_done % POST_ROWS == 0:
                out_block(rows_done - POST_ROWS, POST_ROWS)

    @pl.when(t == 0)
    def _():
        attend(True)

    @pl.when(t > 0)
    def _():
        attend(False)

    k_s[0:KV_WIN, :] = k_s[T:T + KV_WIN, :]
    vt_s[:, 0:KV_WIN] = vt_s[:, T:T + KV_WIN]

    @pl.when(t == nt - 1)
    def _():
        klast_ref[0] = _to_feature_major(kf_s[...])
        vlast_ref[0] = _to_feature_major(vf_s[...])


def _const_spec(shape):
    nd = len(shape)
    return pl.BlockSpec(shape, lambda b, t: (0,) * nd, pipeline_mode=pl.Buffered(1))


def _prompt_call(x, mod, g_pre, w_in, ln_g, ln_b, w_s, b_full, bias_t, w_out, g_post):
    bsz, seq, _ = x.shape
    T = SEQ_TILE
    assert seq % T == 0 and T % PROJ_ROWS == 0 and KV_WIN % PROJ_ROWS == 0 and T % POST_ROWS == 0
    assert PROJ_ROWS % (2 * MLP_CHUNK) == 0
    nt = seq // T
    f32, bf16 = jnp.float32, jnp.bfloat16
    return pl.pallas_call(
        _prompt_kernel,
        grid=(bsz, nt),
        in_specs=[
            pl.BlockSpec((1, T, D_MODEL), lambda b, t: (b, t, 0)),
            pl.BlockSpec((1, 3, D_MODEL), lambda b, t: (b, 0, 0)),
            _const_spec((1, D_MODEL)),
            _const_spec((D_MODEL, D_IN)),
            _const_spec((1, A_WIDTH)),
            _const_spec((1, A_WIDTH)),
            _const_spec((A_GROUPS, MLP_CHUNK, MLP_CHUNK)),
            _const_spec((MLP_CHUNK, A_WIDTH)),
            _const_spec((N_PAIRS, K_BLOCK, 2 * Q_BLOCK)),
            _const_spec((D_MODEL, D_MODEL)),
            _const_spec((1, D_MODEL)),
        ],
        out_specs=[
            pl.BlockSpec((1, T, D_MODEL), lambda b, t: (b, t, 0)),
            pl.BlockSpec((1, B_HEADS, B_HDIM, KV_WIN), lambda b, t: (b, 0, 0, 0)),
            pl.BlockSpec((1, B_HEADS, B_HDIM, KV_WIN), lambda b, t: (b, 0, 0, 0)),
        ],
        out_shape=[
            jax.ShapeDtypeStruct((bsz, seq, D_MODEL), f32),
            jax.ShapeDtypeStruct((bsz, B_HEADS, B_HDIM, KV_WIN), f32),
            jax.ShapeDtypeStruct((bsz, B_HEADS, B_HDIM, KV_WIN), f32),
        ],
        scratch_shapes=[
            pltpu.VMEM((T, D_MODEL), bf16),
            pltpu.VMEM((T, A_WIDTH), f32),
            pltpu.VMEM((T, A_WIDTH), bf16),
            pltpu.VMEM((T, B_WIDTH), bf16),
            pltpu.VMEM((T, B_WIDTH), bf16),
            pltpu.VMEM((KV_WIN + T, B_WIDTH), bf16),
            pltpu.VMEM((B_WIDTH, KV_WIN + T), bf16),
            pltpu.VMEM((T, B_WIDTH), f32),
            pltpu.VMEM((T, D_MODEL), bf16),
            pltpu.VMEM((N_PAIRS, 2 * MLP_CHUNK, MLP_CHUNK), bf16),
            pltpu.VMEM((SCORE_SLOTS, K_BLOCK, 2 * Q_BLOCK), f32),
            pltpu.VMEM((KV_WIN, B_WIDTH), f32),
            pltpu.VMEM((KV_WIN, B_WIDTH), f32),
        ],
        compiler_params=pltpu.CompilerParams(
            dimension_semantics=("arbitrary", "arbitrary"),
            vmem_limit_bytes=VMEM_LIMIT_BYTES),
        name="prompt",
    )(x, mod, g_pre, w_in, ln_g, ln_b, w_s, b_full, bias_t, w_out, g_post)


def _sample_kernel(x_ref, mod_ref, gpre_ref, win_ref, lng_ref, lnb_ref, ws_ref, bfull_ref,
                   bias_ref, wout_ref, gpost_ref, ck_ref, cv_ref,
                   y_ref, knew_ref, vnew_ref, vanew_ref,
                   usg_s, va_s, qe_s, qo_s, k_s, v_s, sgb_s, o_s, wt_s, *, n_streams, n_new):
    b = pl.program_id(0)
    S = n_new

    @pl.when(b == 0)
    def _():
        _tril_pairs(ws_ref, wt_s, S)
        gpre = gpre_ref[...]
        hs = []
        for i in range(n_streams):
            xi = x_ref[i * S:(i + 1) * S, :]
            hs.append(_rms_rows(xi) * (gpre * (1.0 + mod_ref[i, 1:2, :])) + mod_ref[i, 0:1, :])
        h = jnp.concatenate(hs, axis=0).astype(jnp.bfloat16)

        def proj(c0):
            return jnp.dot(h, win_ref[:, c0:c0 + 512], preferred_element_type=jnp.float32)

        va = _layernorm(_gelu(proj(C_V)), lng_ref[...], lnb_ref[...])
        vanew_ref[...] = va
        va_s[...] = va.astype(jnp.bfloat16)
        usg_s[...] = _gelu(proj(C_U)) * _silu(proj(C_GA))
        _store_q(proj(C_Q), qe_s, qo_s)
        k = proj(C_K)
        v = proj(C_VV)
        knew_ref[...] = _split_heads(k)
        vnew_ref[...] = _split_heads(v)
        k_s[...] = k.astype(jnp.bfloat16)
        v_s[...] = v.astype(jnp.bfloat16)
        sgb_s[...] = _silu(proj(C_GB))

    r0 = pl.multiple_of(b * S, S)
    rows = pl.ds(r0, S)
    pair_lanes = [slice(p * LANES, (p + 1) * LANES) for p in range(N_PAIRS)]
    for p, lanes in enumerate(pair_lanes):
        mix = jnp.dot(wt_s[p], va_s[rows, lanes], preferred_element_type=jnp.float32)
        mixed = _pair_select(mix, S) + bfull_ref[0:S, lanes]
        o_s[rows, lanes] = (usg_s[rows, lanes] * mixed).astype(jnp.bfloat16)

    scores = []
    for p, lanes in enumerate(pair_lanes):
        qs = jnp.concatenate([qe_s[rows, lanes], qo_s[rows, lanes]], axis=0)
        kct = ck_ref[0, 2 * p:2 * p + 2].reshape(LANES, KV_WIN).astype(jnp.bfloat16)
        bias = jnp.concatenate([bias_ref[p, 0], bias_ref[p, 1]], axis=0)
        s1 = jnp.dot(qs, kct, preferred_element_type=jnp.float32) + bias[:, 0:KV_WIN]
        s2 = lax.dot_general(qs, k_s[rows, lanes], _CONTRACT_LANES,
                             preferred_element_type=jnp.float32) + bias[:, KV_WIN:KV_WIN + S]
        scores.append((s1, s2))
    weights = []
    for s1, s2 in scores:
        m = jnp.maximum(jnp.max(s1, axis=-1, keepdims=True), jnp.max(s2, axis=-1, keepdims=True))
        e1 = jnp.exp2(s1 - m)
        e2 = jnp.exp2(s2 - m)
        l = jnp.sum(e1, axis=-1, keepdims=True) + jnp.sum(e2, axis=-1, keepdims=True)
        weights.append((e1.astype(jnp.bfloat16), e2.astype(jnp.bfloat16), 1.0 / l))
    for p, lanes in enumerate(pair_lanes):
        e1, e2, inv_l = weights[p]
        vct = cv_ref[0, 2 * p:2 * p + 2].reshape(LANES, KV_WIN).astype(jnp.bfloat16)
        o2 = (lax.dot_general(e1, vct, _CONTRACT_LANES, preferred_element_type=jnp.float32)
              + jnp.dot(e2, v_s[rows, lanes], preferred_element_type=jnp.float32))
        yb = _pair_select(o2 * inv_l, S) * sgb_s[rows, lanes]
        o_s[rows, B_WIDTH + p * LANES:B_WIDTH + (p + 1) * LANES] = yb.astype(jnp.bfloat16)

    @pl.when(b == n_streams - 1)
    def _():
        o = jnp.dot(o_s[...], wout_ref[...], preferred_element_type=jnp.float32)
        on = _rms_rows(o) * gpost_ref[...]
        for i in range(n_streams):
            sl = slice(i * S, (i + 1) * S)
            y_ref[sl, :] = x_ref[sl, :] + mod_ref[i, 2:3, :] * on[sl, :]


def _sample_call(x, mod, g_pre, w_in, ln_g, ln_b, w_s, b_full, bias_s, w_out, g_post, ck, cv):
    n_streams, n_new, _ = x.shape
    assert n_new % BF16_ROWS == 0 and n_new <= CHUNK
    rows = n_streams * n_new
    f32, bf16 = jnp.float32, jnp.bfloat16

    def const(shape):
        nd = len(shape)
        return pl.BlockSpec(shape, lambda b: (0,) * nd, pipeline_mode=pl.Buffered(1))

    kern = functools.partial(_sample_kernel, n_streams=n_streams, n_new=n_new)
    return pl.pallas_call(
        kern,
        grid=(n_streams,),
        in_specs=[
            const((rows, D_MODEL)),
            const((n_streams, 3, D_MODEL)),
            const((1, D_MODEL)),
            const((D_MODEL, D_IN)),
            const((1, A_WIDTH)),
            const((1, A_WIDTH)),
            const((A_GROUPS, MLP_CHUNK, MLP_CHUNK)),
            const((MLP_CHUNK, A_WIDTH)),
            const((N_PAIRS, 2, n_new, K_BLOCK)),
            const((D_MODEL, D_MODEL)),
            const((1, D_MODEL)),
            pl.BlockSpec((1, B_HEADS, B_HDIM, KV_WIN), lambda b: (b, 0, 0, 0)),
            pl.BlockSpec((1, B_HEADS, B_HDIM, KV_WIN), lambda b: (b, 0, 0, 0)),
        ],
        out_specs=[
            pl.BlockSpec((rows, D_MODEL), lambda b: (0, 0)),
            pl.BlockSpec((rows, B_HEADS, B_HDIM), lambda b: (0, 0, 0)),
            pl.BlockSpec((rows, B_HEADS, B_HDIM), lambda b: (0, 0, 0)),
            pl.BlockSpec((rows, A_WIDTH), lambda b: (0, 0)),
        ],
        out_shape=[
            jax.ShapeDtypeStruct((rows, D_MODEL), f32),
            jax.ShapeDtypeStruct((rows, B_HEADS, B_HDIM), f32),
            jax.ShapeDtypeStruct((rows, B_HEADS, B_HDIM), f32),
            jax.ShapeDtypeStruct((rows, A_WIDTH), f32),
        ],
        scratch_shapes=[
            pltpu.VMEM((rows, A_WIDTH), f32),
            pltpu.VMEM((rows, A_WIDTH), bf16),
            pltpu.VMEM((rows, B_WIDTH), bf16),
            pltpu.VMEM((rows, B_WIDTH), bf16),
            pltpu.VMEM((rows, B_WIDTH), bf16),
            pltpu.VMEM((rows, B_WIDTH), bf16),
            pltpu.VMEM((rows, B_WIDTH), f32),
            pltpu.VMEM((rows, D_MODEL), bf16),
            pltpu.VMEM((N_PAIRS, 2 * n_new, n_new), bf16),
        ],
        compiler_params=pltpu.CompilerParams(
            dimension_semantics=("arbitrary",),
            vmem_limit_bytes=VMEM_LIMIT_BYTES),
        name="sample",
    )(x.reshape(rows, D_MODEL), mod, g_pre, w_in, ln_g, ln_b, w_s, b_full, bias_s, w_out, g_post,
      ck, cv)


def _bias_base(rel_bias):
    n_far = KV_WIN - REL_CLIP
    far = jnp.broadcast_to(rel_bias[:, N_REL - 1:N_REL], (B_HEADS, n_far))
    near = rel_bias[:, N_REL - 1:0:-1]
    return jnp.concatenate([far, near], axis=1)


def kernel(x_prompt, x_sample, cache_attn_k, cache_attn_v, c_prompt, c_sample, g_pre, w_ada, b_ada,
           w_in, ln_g, ln_b, w_s, b_s, rel_bias, w_out, g_post):
    depth = g_pre.shape[0]
    bsz, seq, _ = x_prompt.shape
    n_streams, n_new, _ = x_sample.shape
    win = cache_attn_k.shape[2]
    assert win == KV_WIN and bsz + n_streams <= ADA_ROWS

    yp, ys = x_prompt, x_sample
    kp_rows, vp_rows, ks_rows, vs_rows, va_rows = [], [], [], [], []
    c_all = jnp.concatenate(
        [c_prompt, c_sample, jnp.zeros((ADA_ROWS - bsz - n_streams, D_MODEL), c_prompt.dtype)], axis=0)
    for l in range(depth):
        mod, bias_t, bias_s, w_in_b, w_out_b = _prep_call(
            c_all, w_ada[l], b_ada[l], _bias_base(rel_bias[l]), w_in[l], w_out[l], n_new)
        mod = mod.reshape(ADA_ROWS, 3, D_MODEL)
        b_full = jnp.repeat(b_s[l].T, A_GDIM, axis=1)
        pre = (g_pre[l].reshape(1, D_MODEL), w_in_b, ln_g[l].reshape(1, A_WIDTH),
               ln_b[l].reshape(1, A_WIDTH), w_s[l], b_full)
        post = (w_out_b, g_post[l].reshape(1, D_MODEL))

        yp, k_last, v_last = _prompt_call(yp, mod[:bsz], *pre, bias_t, *post)
        kp_rows.append(jnp.transpose(k_last, _FROM_FEATURE_MAJOR))
        vp_rows.append(jnp.transpose(v_last, _FROM_FEATURE_MAJOR))

        ys2, k_new, v_new, va_new = _sample_call(ys, mod[bsz:bsz + n_streams], *pre, bias_s, *post,
                                                 jnp.transpose(cache_attn_k[l], _TO_FEATURE_MAJOR),
                                                 jnp.transpose(cache_attn_v[l], _TO_FEATURE_MAJOR))
        ys = ys2.reshape(n_streams, n_new, D_MODEL)
        ks_rows.append(k_new.reshape(n_streams, n_new, B_HEADS, B_HDIM))
        vs_rows.append(v_new.reshape(n_streams, n_new, B_HEADS, B_HDIM))
        va_rows.append(va_new.reshape(n_streams, n_new, A_WIDTH))

    return (yp, ys, jnp.stack(kp_rows), jnp.stack(vp_rows), jnp.stack(ks_rows), jnp.stack(vs_rows),
            jnp.stack(va_rows))
```

```python
import functools

import jax
import jax.numpy as jnp
from jax import lax
from jax.experimental import pallas as pl
from jax.experimental.pallas import tpu as pltpu

LANES = 128
SUBLANES = 8
BF16_ROWS = 16
VMEM_LIMIT_BYTES = 60 * 1024 * 1024

D_MODEL = 1024
A_WIDTH = 512
A_GROUPS = 8
A_GDIM = A_WIDTH // A_GROUPS
MLP_CHUNK = 128
B_WIDTH = 512
B_HEADS = 8
B_HDIM = B_WIDTH // B_HEADS
CHUNK = 64
KV_WIN = 512
REL_CLIP = 128
N_REL = 2 * REL_CLIP + 1
EPS = 1e-6
D_IN = 3 * A_WIDTH + 4 * B_WIDTH
NEG = -1e30
LOG2E = 1.4426950408889634
Q_SCALE = B_HDIM ** -0.5 * LOG2E

N_PAIRS = B_HEADS // 2
Q_BLOCK = 2 * CHUNK
K_BLOCK = KV_WIN + Q_BLOCK
SEQ_TILE = 1024
PROJ_ROWS = 256
POST_ROWS = 256
SCORE_LEAD = 3
SCORE_SLOTS = SCORE_LEAD + 1
ADA_ROWS = 16

C_U, C_V, C_GA, C_Q, C_K, C_VV, C_GB = (i * 512 for i in range(7))

_CONTRACT_LANES = (((1,), (1,)), ((), ()))

_TO_FEATURE_MAJOR = (0, 2, 3, 1)
_FROM_FEATURE_MAJOR = (0, 3, 1, 2)


def _sigmoid_exp2(neg_arg_log2):
    return 1.0 / (1.0 + jnp.exp2(neg_arg_log2))


def _gelu(x):
    c1 = -2.0 * (2.0 / jnp.pi) ** 0.5 * LOG2E
    return x * _sigmoid_exp2(x * (c1 + (c1 * 0.044715) * (x * x)))


def _silu(x):
    return x * _sigmoid_exp2(x * (-LOG2E))


def _even_lane_mask(shape):
    lane = lax.broadcasted_iota(jnp.int32, shape, len(shape) - 1)
    return (lane % LANES) < B_HDIM


def _ada_block(c_ref, w_ref, b_ref, o_ref):
    c = c_ref[...]
    o_ref[...] = jnp.dot(_silu(c), w_ref[...], preferred_element_type=jnp.float32) + b_ref[...]


def _relb_block(base_ref, bt_ref, bs_ref, n_new):
    row = lax.broadcasted_iota(jnp.int32, (Q_BLOCK, K_BLOCK), 0)
    col = lax.broadcasted_iota(jnp.int32, (Q_BLOCK, K_BLOCK), 1)
    band_lo = jnp.where(row < CHUNK, 0, CHUNK)
    rel = col - band_lo
    outside = jnp.logical_or(rel < 0, rel >= KV_WIN + CHUNK)
    for par in range(2):
        base = base_ref[0, par:par + 1, :] * LOG2E
        x = jnp.broadcast_to(base, (Q_BLOCK, K_BLOCK))
        shift = 1
        while shift < Q_BLOCK:
            x = jnp.where((row & shift) != 0, jnp.roll(x, shift, axis=1), x)
            shift *= 2
        far = jnp.broadcast_to(base[:, 0:1], (Q_BLOCK, K_BLOCK))
        x = jnp.where(col < row, far, x)
        bs_ref[0, par] = x[0:n_new, :]
        bt_ref[0, :, par * Q_BLOCK:(par + 1) * Q_BLOCK] = jnp.where(outside, NEG, x).T


PREP_COLS = 512
N_PREP = D_IN // PREP_COLS
N_ADA = 3 * D_MODEL // PREP_COLS
WOUT_COLS = D_MODEL // N_PAIRS


def _prep_kernel(c_ref, wada_ref, bada_ref, base_ref, win_ref, wout_ref,
                 mod_ref, bt_ref, bs_ref, winb_ref, woutb_ref, *, n_new):
    j = pl.program_id(0)
    winb_ref[...] = win_ref[...].astype(jnp.bfloat16)

    @pl.when(j < N_ADA)
    def _():
        _ada_block(c_ref, wada_ref, bada_ref, mod_ref)

    @pl.when(j < N_PAIRS)
    def _():
        woutb_ref[...] = wout_ref[...].astype(jnp.bfloat16)
        _relb_block(base_ref, bt_ref, bs_ref, n_new)


def _prep_call(c_all, w_ada, b_ada, base, w_in, w_out, n_new):
    f32, bf16 = jnp.float32, jnp.bfloat16

    def upto(n):
        return lambda j: jnp.minimum(j, n - 1)

    ada_j, pair_j = upto(N_ADA), upto(N_PAIRS)
    return pl.pallas_call(
        functools.partial(_prep_kernel, n_new=n_new),
        grid=(N_PREP,),
        in_specs=[
            pl.BlockSpec((ADA_ROWS, D_MODEL), lambda j: (0, 0)),
            pl.BlockSpec((D_MODEL, PREP_COLS), lambda j: (0, ada_j(j))),
            pl.BlockSpec((1, PREP_COLS), lambda j: (0, ada_j(j))),
            pl.BlockSpec((1, 2, K_BLOCK), lambda j: (pair_j(j), 0, 0)),
            pl.BlockSpec((D_MODEL, PREP_COLS), lambda j: (0, j)),
            pl.BlockSpec((D_MODEL, WOUT_COLS), lambda j: (0, pair_j(j))),
        ],
        out_specs=[
            pl.BlockSpec((ADA_ROWS, PREP_COLS), lambda j: (0, ada_j(j))),
            pl.BlockSpec((1, K_BLOCK, 2 * Q_BLOCK), lambda j: (pair_j(j), 0, 0)),
            pl.BlockSpec((1, 2, n_new, K_BLOCK), lambda j: (pair_j(j), 0, 0, 0)),
            pl.BlockSpec((D_MODEL, PREP_COLS), lambda j: (0, j)),
            pl.BlockSpec((D_MODEL, WOUT_COLS), lambda j: (0, pair_j(j))),
        ],
        out_shape=[
            jax.ShapeDtypeStruct((ADA_ROWS, 3 * D_MODEL), f32),
            jax.ShapeDtypeStruct((N_PAIRS, K_BLOCK, 2 * Q_BLOCK), f32),
            jax.ShapeDtypeStruct((N_PAIRS, 2, n_new, K_BLOCK), f32),
            jax.ShapeDtypeStruct((D_MODEL, D_IN), bf16),
            jax.ShapeDtypeStruct((D_MODEL, D_MODEL), bf16),
        ],
        compiler_params=pltpu.CompilerParams(dimension_semantics=("arbitrary",)),
        name="prep",
    )(c_all, w_ada, b_ada.reshape(1, 3 * D_MODEL), base.reshape(N_PAIRS, 2, K_BLOCK), w_in, w_out)


def _tril_pairs(ws_ref, wt_ref, n):
    row = lax.broadcasted_iota(jnp.int32, (n, n), 0)
    col = lax.broadcasted_iota(jnp.int32, (n, n), 1)
    keep = col <= row
    for g in range(A_GROUPS):
        w = jnp.where(keep, ws_ref[g, :n, :n], 0.0)
        wt_ref[g // 2, (g % 2) * n:(g % 2 + 1) * n, :] = w.astype(jnp.bfloat16)


def _rms_rows(x):
    return x * lax.rsqrt(jnp.mean(x * x, axis=-1, keepdims=True) + EPS)


def _layernorm(x, g, b):
    mu = jnp.mean(x, axis=-1, keepdims=True)
    xc = x - mu
    var = jnp.mean(xc * xc, axis=-1, keepdims=True)
    return xc * lax.rsqrt(var + EPS) * g + b


def _pair_select(stacked, n):
    return jnp.where(_even_lane_mask((n, LANES)), stacked[:n], stacked[n:])


def _split_heads(x):
    return x.reshape(x.shape[0], B_HEADS, B_HDIM)


def _to_feature_major(x):
    return x.T.reshape(B_HEADS, B_HDIM, x.shape[0])


def _store_q(q, qe_ref, qo_ref):
    q = (q * Q_SCALE).astype(jnp.bfloat16)
    even = _even_lane_mask(q.shape)
    zero = jnp.zeros((), jnp.bfloat16)
    qe_ref[...] = jnp.where(even, q, zero)
    qo_ref[...] = jnp.where(even, zero, q)


def _prompt_kernel(x_ref, mod_ref, gpre_ref, win_ref, lng_ref, lnb_ref, ws_ref, bfull_ref,
                   biast_ref, wout_ref, gpost_ref,
                   y_ref, klast_ref, vlast_ref,
                   h_s, usg_s, va_s, qe_s, qo_s, k_s, vt_s, sgb_s, o_s, wt_s, st_s, kf_s, vf_s):
    t = pl.program_id(1)
    nt = pl.num_programs(1)
    T = SEQ_TILE

    @pl.when(jnp.logical_and(pl.program_id(0) == 0, t == 0))
    def _():
        _tril_pairs(ws_ref, wt_s, MLP_CHUNK)

    shift = mod_ref[0, 0:1, :]
    scale = mod_ref[0, 1:2, :]
    gate = mod_ref[0, 2:3, :]

    pre_scale = gpre_ref[...] * (1.0 + scale)

    def proj(c0, rows):
        return jnp.dot(h_s[rows, :], win_ref[:, c0:c0 + 512], preferred_element_type=jnp.float32)

    def proj_pieces(lo):
        rb = slice(lo, lo + PROJ_ROWS)
        halves = [slice(lo, lo + PROJ_ROWS // 2), slice(lo + PROJ_ROWS // 2, lo + PROJ_ROWS)]
        hist = slice(KV_WIN + lo, KV_WIN + lo + PROJ_ROWS)
        keep = slice(lo - (T - KV_WIN), lo - (T - KV_WIN) + PROJ_ROWS) if lo >= T - KV_WIN else None

        def pre_norm():
            for rows in halves:
                h_s[rows, :] = (_rms_rows(x_ref[0, rows, :]) * pre_scale + shift).astype(jnp.bfloat16)

        def u_half(rows):
            usg_s[rows, :] = _gelu(proj(C_U, rows))

        def v_block():
            v = proj(C_VV, rb)
            vt_s[:, hist] = v.T.astype(jnp.bfloat16)
            if keep is not None:
                vf_s[keep, :] = v

        def va_block():
            va_s[rb, :] = _layernorm(_gelu(proj(C_V, rb)), lng_ref[...], lnb_ref[...]
                                     ).astype(jnp.bfloat16)

        def q_block():
            _store_q(proj(C_Q, rb), qe_s.at[rb, :], qo_s.at[rb, :])

        def ga_block():
            usg_s[rb, :] = usg_s[rb, :] * _silu(proj(C_GA, rb))

        def gmlp(p):
            lanes = slice(p * LANES, (p + 1) * LANES)
            for c0 in range(lo, lo + PROJ_ROWS, 2 * MLP_CHUNK):
                chunks = [slice(c0, c0 + MLP_CHUNK), slice(c0 + MLP_CHUNK, c0 + 2 * MLP_CHUNK)]
                slabs = jnp.concatenate([va_s[rows, lanes] for rows in chunks], axis=1)
                mix = jnp.dot(wt_s[p], slabs, preferred_element_type=jnp.float32)
                for i, rows in enumerate(chunks):
                    mixed = _pair_select(mix[:, i * LANES:(i + 1) * LANES], MLP_CHUNK)
                    o_s[rows, lanes] = (usg_s[rows, lanes] * (mixed + bfull_ref[:, lanes])
                                        ).astype(jnp.bfloat16)

        def gb_block():
            sgb_s[rb, :] = _silu(proj(C_GB, rb))

        def k_block():
            k = proj(C_K, rb)
            k_s[hist, :] = k.astype(jnp.bfloat16)
            if keep is not None:
                kf_s[keep, :] = k

        g = [functools.partial(gmlp, p) for p in range(N_PAIRS)]
        return [pre_norm, functools.partial(u_half, halves[0]), functools.partial(u_half, halves[1]),
                v_block, va_block, ga_block, g[0], q_block, g[1], gb_block, g[2], k_block, g[3]]

    def key_lo(qb, first_tile):
        return max(KV_WIN - qb * Q_BLOCK, 0) if first_tile else 0

    def scores(qb, p, slot, first_tile):
        r0, lo = qb * Q_BLOCK, key_lo(qb, first_tile)
        lanes = slice(p * LANES, (p + 1) * LANES)
        qs = jnp.concatenate([qe_s[r0:r0 + Q_BLOCK, lanes],
                              qo_s[r0:r0 + Q_BLOCK, lanes]], axis=0)
        half = (K_BLOCK - lo) // 2
        for a in (lo, lo + half):
            st_s[slot, a:a + half, :] = lax.dot_general(
                k_s[r0 + a:r0 + a + half, lanes], qs, _CONTRACT_LANES,
                preferred_element_type=jnp.float32) + biast_ref[p, a:a + half, :]

    def finish(qb, p, slot, first_tile):
        r0, lo = qb * Q_BLOCK, key_lo(qb, first_tile)
        st = st_s[slot, lo:K_BLOCK, :]
        m = jnp.max(st, axis=0, keepdims=True)
        pt = jnp.exp2(st - m).astype(jnp.bfloat16)
        ones = jnp.ones((BF16_ROWS, K_BLOCK - lo), jnp.bfloat16)
        ots = []
        for par in range(2):
            feats = slice(p * LANES + par * B_HDIM, p * LANES + (par + 1) * B_HDIM)
            qcols = slice(par * Q_BLOCK, (par + 1) * Q_BLOCK)
            lhs = jnp.concatenate([vt_s[feats, r0 + lo:r0 + K_BLOCK], ones], axis=0)
            ot = jnp.dot(lhs, pt[:, qcols], preferred_element_type=jnp.float32)
            ots.append(ot[0:B_HDIM] * (1.0 / ot[B_HDIM:B_HDIM + 1]))
        yb = jnp.concatenate(ots, axis=0).T * sgb_s[r0:r0 + Q_BLOCK, p * LANES:(p + 1) * LANES]
        o_s[r0:r0 + Q_BLOCK, B_WIDTH + p * LANES:B_WIDTH + (p + 1) * LANES] = (
            yb.astype(jnp.bfloat16))

    post_scale = gate * gpost_ref[...]

    def out_block(lo, n):
        rows = slice(lo, lo + n)
        o = jnp.dot(o_s[rows, :], wout_ref[...], preferred_element_type=jnp.float32)
        y_ref[0, rows, :] = x_ref[0, rows, :] + _rms_rows(o) * post_scale

    def attend(first_tile):
        items = [(qb, p) for qb in range(T // Q_BLOCK) for p in range(N_PAIRS)]
        per_block = (PROJ_ROWS // Q_BLOCK) * N_PAIRS
        n_blocks = T // PROJ_ROWS
        for piece in proj_pieces(0):
            piece()
        pending = []
        for j in range(SCORE_LEAD):
            scores(*items[j], j % SCORE_SLOTS, first_tile)
        for i, (qb, p) in enumerate(items):
            blk, pos = divmod(i, per_block)
            if pos == 0 and blk + 1 < n_blocks:
                pending = proj_pieces((blk + 1) * PROJ_ROWS)
            if i + SCORE_LEAD < len(items):
                assert pos < per_block - SCORE_LEAD or not pending
                scores(*items[i + SCORE_LEAD], (i + SCORE_LEAD) % SCORE_SLOTS, first_tile)
            finish(qb, p, i % SCORE_SLOTS, first_tile)
            slots_left = per_block - SCORE_LEAD - pos
            if pending and slots_left > 0:
                n_now = -(-len(pending) // slots_left)
                for piece in pending[:n_now]:
                    piece()
                pending = pending[n_now:]
            rows_done = (qb + 1) * Q_BLOCK
            if p == N_PAIRS - 1 and rows_done % POST_ROWS == 0:
                out_block(rows_done - POST_ROWS, POST_ROWS)

    @pl.when(t == 0)
    def _():
        attend(True)

    @pl.when(t > 0)
    def _():
        attend(False)

    k_s[0:KV_WIN, :] = k_s[T:T + KV_WIN, :]
    vt_s[:, 0:KV_WIN] = vt_s[:, T:T + KV_WIN]

    @pl.when(t == nt - 1)
    def _():
        klast_ref[0] = _to_feature_major(kf_s[...])
        vlast_ref[0] = _to_feature_major(vf_s[...])


def _const_spec(shape):
    nd = len(shape)
    return pl.BlockSpec(shape, lambda b, t: (0,) * nd, pipeline_mode=pl.Buffered(1))


def _prompt_call(x, mod, g_pre, w_in, ln_g, ln_b, w_s, b_full, bias_t, w_out, g_post):
    bsz, seq, _ = x.shape
    T = SEQ_TILE
    assert seq % T == 0 and T % PROJ_ROWS == 0 and KV_WIN % PROJ_ROWS == 0 and T % POST_ROWS == 0
    assert PROJ_ROWS % (2 * MLP_CHUNK) == 0
    nt = seq // T
    f32, bf16 = jnp.float32, jnp.bfloat16
    return pl.pallas_call(
        _prompt_kernel,
        grid=(bsz, nt),
        in_specs=[
            pl.BlockSpec((1, T, D_MODEL), lambda b, t: (b, t, 0)),
            pl.BlockSpec((1, 3, D_MODEL), lambda b, t: (b, 0, 0)),
            _const_spec((1, D_MODEL)),
            _const_spec((D_MODEL, D_IN)),
            _const_spec((1, A_WIDTH)),
            _const_spec((1, A_WIDTH)),
            _const_spec((A_GROUPS, MLP_CHUNK, MLP_CHUNK)),
            _const_spec((MLP_CHUNK, A_WIDTH)),
            _const_spec((N_PAIRS, K_BLOCK, 2 * Q_BLOCK)),
            _const_spec((D_MODEL, D_MODEL)),
            _const_spec((1, D_MODEL)),
        ],
        out_specs=[
            pl.BlockSpec((1, T, D_MODEL), lambda b, t: (b, t, 0)),
            pl.BlockSpec((1, B_HEADS, B_HDIM, KV_WIN), lambda b, t: (b, 0, 0, 0)),
            pl.BlockSpec((1, B_HEADS, B_HDIM, KV_WIN), lambda b, t: (b, 0, 0, 0)),
        ],
        out_shape=[
            jax.ShapeDtypeStruct((bsz, seq, D_MODEL), f32),
            jax.ShapeDtypeStruct((bsz, B_HEADS, B_HDIM, KV_WIN), f32),
            jax.ShapeDtypeStruct((bsz, B_HEADS, B_HDIM, KV_WIN), f32),
        ],
        scratch_shapes=[
            pltpu.VMEM((T, D_MODEL), bf16),
            pltpu.VMEM((T, A_WIDTH), f32),
            pltpu.VMEM((T, A_WIDTH), bf16),
            pltpu.VMEM((T, B_WIDTH), bf16),
            pltpu.VMEM((T, B_WIDTH), bf16),
            pltpu.VMEM((KV_WIN + T, B_WIDTH), bf16),
            pltpu.VMEM((B_WIDTH, KV_WIN + T), bf16),
            pltpu.VMEM((T, B_WIDTH), f32),
            pltpu.VMEM((T, D_MODEL), bf16),
            pltpu.VMEM((N_PAIRS, 2 * MLP_CHUNK, MLP_CHUNK), bf16),
            pltpu.VMEM((SCORE_SLOTS, K_BLOCK, 2 * Q_BLOCK), f32),
            pltpu.VMEM((KV_WIN, B_WIDTH), f32),
            pltpu.VMEM((KV_WIN, B_WIDTH), f32),
        ],
        compiler_params=pltpu.CompilerParams(
            dimension_semantics=("arbitrary", "arbitrary"),
            vmem_limit_bytes=VMEM_LIMIT_BYTES),
        name="prompt",
    )(x, mod, g_pre, w_in, ln_g, ln_b, w_s, b_full, bias_t, w_out, g_post)


def _sample_kernel(x_ref, mod_ref, gpre_ref, win_ref, lng_ref, lnb_ref, ws_ref, bfull_ref,
                   bias_ref, wout_ref, gpost_ref, ck_ref, cv_ref,
                   y_ref, knew_ref, vnew_ref, vanew_ref,
                   usg_s, va_s, qe_s, qo_s, k_s, v_s, sgb_s, o_s, wt_s, *, n_streams, n_new):
    b = pl.program_id(0)
    S = n_new

    @pl.when(b == 0)
    def _():
        _tril_pairs(ws_ref, wt_s, S)
        gpre = gpre_ref[...]
        hs = []
        for i in range(n_streams):
            xi = x_ref[i * S:(i + 1) * S, :]
            hs.append(_rms_rows(xi) * (gpre * (1.0 + mod_ref[i, 1:2, :])) + mod_ref[i, 0:1, :])
        h = jnp.concatenate(hs, axis=0).astype(jnp.bfloat16)

        def proj(c0):
            return jnp.dot(h, win_ref[:, c0:c0 + 512], preferred_element_type=jnp.float32)

        va = _layernorm(_gelu(proj(C_V)), lng_ref[...], lnb_ref[...])
        vanew_ref[...] = va
        va_s[...] = va.astype(jnp.bfloat16)
        usg_s[...] = _gelu(proj(C_U)) * _silu(proj(C_GA))
        _store_q(proj(C_Q), qe_s, qo_s)
        k = proj(C_K)
        v = proj(C_VV)
        knew_ref[...] = _split_heads(k)
        vnew_ref[...] = _split_heads(v)
        k_s[...] = k.astype(jnp.bfloat16)
        v_s[...] = v.astype(jnp.bfloat16)
        sgb_s[...] = _silu(proj(C_GB))

    r0 = pl.multiple_of(b * S, S)
    rows = pl.ds(r0, S)
    pair_lanes = [slice(p * LANES, (p + 1) * LANES) for p in range(N_PAIRS)]
    for p, lanes in enumerate(pair_lanes):
        mix = jnp.dot(wt_s[p], va_s[rows, lanes], preferred_element_type=jnp.float32)
        mixed = _pair_select(mix, S) + bfull_ref[0:S, lanes]
        o_s[rows, lanes] = (usg_s[rows, lanes] * mixed).astype(jnp.bfloat16)

    scores = []
    for p, lanes in enumerate(pair_lanes):
        qs = jnp.concatenate([qe_s[rows, lanes], qo_s[rows, lanes]], axis=0)
        kct = ck_ref[0, 2 * p:2 * p + 2].reshape(LANES, KV_WIN).astype(jnp.bfloat16)
        bias = jnp.concatenate([bias_ref[p, 0], bias_ref[p, 1]], axis=0)
        s1 = jnp.dot(qs, kct, preferred_element_type=jnp.float32) + bias[:, 0:KV_WIN]
        s2 = lax.dot_general(qs, k_s[rows, lanes], _CONTRACT_LANES,
                             preferred_element_type=jnp.float32) + bias[:, KV_WIN:KV_WIN + S]
        scores.append((s1, s2))
    weights = []
    for s1, s2 in scores:
        m = jnp.maximum(jnp.max(s1, axis=-1, keepdims=True), jnp.max(s2, axis=-1, keepdims=True))
        e1 = jnp.exp2(s1 - m)
        e2 = jnp.exp2(s2 - m)
        l = jnp.sum(e1, axis=-1, keepdims=True) + jnp.sum(e2, axis=-1, keepdims=True)
        weights.append((e1.astype(jnp.bfloat16), e2.astype(jnp.bfloat16), 1.0 / l))
    for p, lanes in enumerate(pair_lanes):
        e1, e2, inv_l = weights[p]
        vct = cv_ref[0, 2 * p:2 * p + 2].reshape(LANES, KV_WIN).astype(jnp.bfloat16)
        o2 = (lax.dot_general(e1, vct, _CONTRACT_LANES, preferred_element_type=jnp.float32)
              + jnp.dot(e2, v_s[rows, lanes], preferred_element_type=jnp.float32))
        yb = _pair_select(o2 * inv_l, S) * sgb_s[rows, lanes]
        o_s[rows, B_WIDTH + p * LANES:B_WIDTH + (p + 1) * LANES] = yb.astype(jnp.bfloat16)

    @pl.when(b == n_streams - 1)
    def _():
        o = jnp.dot(o_s[...], wout_ref[...], preferred_element_type=jnp.float32)
        on = _rms_rows(o) * gpost_ref[...]
        for i in range(n_streams):
            sl = slice(i * S, (i + 1) * S)
            y_ref[sl, :] = x_ref[sl, :] + mod_ref[i, 2:3, :] * on[sl, :]


def _sample_call(x, mod, g_pre, w_in, ln_g, ln_b, w_s, b_full, bias_s, w_out, g_post, ck, cv):
    n_streams, n_new, _ = x.shape
    assert n_new % BF16_ROWS == 0 and n_new <= CHUNK
    rows = n_streams * n_new
    f32, bf16 = jnp.float32, jnp.bfloat16

    def const(shape):
        nd = len(shape)
        return pl.BlockSpec(shape, lambda b: (0,) * nd, pipeline_mode=pl.Buffered(1))

    kern = functools.partial(_sample_kernel, n_streams=n_streams, n_new=n_new)
    return pl.pallas_call(
        kern,
        grid=(n_streams,),
        in_specs=[
            const((rows, D_MODEL)),
            const((n_streams, 3, D_MODEL)),
            const((1, D_MODEL)),
            const((D_MODEL, D_IN)),
            const((1, A_WIDTH)),
            const((1, A_WIDTH)),
            const((A_GROUPS, MLP_CHUNK, MLP_CHUNK)),
            const((MLP_CHUNK, A_WIDTH)),
            const((N_PAIRS, 2, n_new, K_BLOCK)),
            const((D_MODEL, D_MODEL)),
            const((1, D_MODEL)),
            pl.BlockSpec((1, B_HEADS, B_HDIM, KV_WIN), lambda b: (b, 0, 0, 0)),
            pl.BlockSpec((1, B_HEADS, B_HDIM, KV_WIN), lambda b: (b, 0, 0, 0)),
        ],
        out_specs=[
            pl.BlockSpec((rows, D_MODEL), lambda b: (0, 0)),
            pl.BlockSpec((rows, B_HEADS, B_HDIM), lambda b: (0, 0, 0)),
            pl.BlockSpec((rows, B_HEADS, B_HDIM), lambda b: (0, 0, 0)),
            pl.BlockSpec((rows, A_WIDTH), lambda b: (0, 0)),
        ],
        out_shape=[
            jax.ShapeDtypeStruct((rows, D_MODEL), f32),
            jax.ShapeDtypeStruct((rows, B_HEADS, B_HDIM), f32),
            jax.ShapeDtypeStruct((rows, B_HEADS, B_HDIM), f32),
            jax.ShapeDtypeStruct((rows, A_WIDTH), f32),
        ],
        scratch_shapes=[
            pltpu.VMEM((rows, A_WIDTH), f32),
            pltpu.VMEM((rows, A_WIDTH), bf16),
            pltpu.VMEM((rows, B_WIDTH), bf16),
            pltpu.VMEM((rows, B_WIDTH), bf16),
            pltpu.VMEM((rows, B_WIDTH), bf16),
            pltpu.VMEM((rows, B_WIDTH), bf16),
            pltpu.VMEM((rows, B_WIDTH), f32),
            pltpu.VMEM((rows, D_MODEL), bf16),
            pltpu.VMEM((N_PAIRS, 2 * n_new, n_new), bf16),
        ],
        compiler_params=pltpu.CompilerParams(
            dimension_semantics=("arbitrary",),
            vmem_limit_bytes=VMEM_LIMIT_BYTES),
        name="sample",
    )(x.reshape(rows, D_MODEL), mod, g_pre, w_in, ln_g, ln_b, w_s, b_full, bias_s, w_out, g_post,
      ck, cv)


def _bias_base(rel_bias):
    n_far = KV_WIN - REL_CLIP
    far = jnp.broadcast_to(rel_bias[:, N_REL - 1:N_REL], (B_HEADS, n_far))
    near = rel_bias[:, N_REL - 1:0:-1]
    return jnp.concatenate([far, near], axis=1)


def kernel(x_prompt, x_sample, cache_attn_k, cache_attn_v, c_prompt, c_sample, g_pre, w_ada, b_ada,
           w_in, ln_g, ln_b, w_s, b_s, rel_bias, w_out, g_post):
    depth = g_pre.shape[0]
    bsz, seq, _ = x_prompt.shape
    n_streams, n_new, _ = x_sample.shape
    win = cache_attn_k.shape[2]
    assert win == KV_WIN and bsz + n_streams <= ADA_ROWS

    yp, ys = x_prompt, x_sample
    kp_rows, vp_rows, ks_rows, vs_rows, va_rows = [], [], [], [], []
    c_all = jnp.concatenate(
        [c_prompt, c_sample, jnp.zeros((ADA_ROWS - bsz - n_streams, D_MODEL), c_prompt.dtype)], axis=0)
    for l in range(depth):
        mod, bias_t, bias_s, w_in_b, w_out_b = _prep_call(
            c_all, w_ada[l], b_ada[l], _bias_base(rel_bias[l]), w_in[l], w_out[l], n_new)
        mod = mod.reshape(ADA_ROWS, 3, D_MODEL)
        b_full = jnp.repeat(b_s[l].T, A_GDIM, axis=1)
        pre = (g_pre[l].reshape(1, D_MODEL), w_in_b, ln_g[l].reshape(1, A_WIDTH),
               ln_b[l].reshape(1, A_WIDTH), w_s[l], b_full)
        post = (w_out_b, g_post[l].reshape(1, D_MODEL))

        yp, k_last, v_last = _prompt_call(yp, mod[:bsz], *pre, bias_t, *post)
        kp_rows.append(jnp.transpose(k_last, _FROM_FEATURE_MAJOR))
        vp_rows.append(jnp.transpose(v_last, _FROM_FEATURE_MAJOR))

        ys2, k_new, v_new, va_new = _sample_call(ys, mod[bsz:bsz + n_streams], *pre, bias_s, *post,
                                                 jnp.transpose(cache_attn_k[l], _TO_FEATURE_MAJOR),
                                                 jnp.transpose(cache_attn_v[l], _TO_FEATURE_MAJOR))
        ys = ys2.reshape(n_streams, n_new, D_MODEL)
        ks_rows.append(k_new.reshape(n_streams, n_new, B_HEADS, B_HDIM))
        vs_rows.append(v_new.reshape(n_streams, n_new, B_HEADS, B_HDIM))
        va_rows.append(va_new.reshape(n_streams, n_new, A_WIDTH))

    return (yp, ys, jnp.stack(kp_rows), jnp.stack(vp_rows), jnp.stack(ks_rows), jnp.stack(vs_rows),
            jnp.stack(va_rows))
```

```python
import functools

import jax
import jax.numpy as jnp
from jax import lax
from jax.experimental import pallas as pl
from jax.experimental.pallas import tpu as pltpu

LANES = 128
SUBLANES = 8
BF16_ROWS = 16
VMEM_LIMIT_BYTES = 60 * 1024 * 1024

D_MODEL = 1024
A_WIDTH = 512
A_GROUPS = 8
A_GDIM = A_WIDTH // A_GROUPS
MLP_CHUNK = 128
B_WIDTH = 512
B_HEADS = 8
B_HDIM = B_WIDTH // B_HEADS
CHUNK = 64
KV_WIN = 512
REL_CLIP = 128
N_REL = 2 * REL_CLIP + 1
EPS = 1e-6
D_IN = 3 * A_WIDTH + 4 * B_WIDTH
NEG = -1e30
LOG2E = 1.4426950408889634
Q_SCALE = B_HDIM ** -0.5 * LOG2E

N_PAIRS = B_HEADS // 2
Q_BLOCK = 2 * CHUNK
K_BLOCK = KV_WIN + Q_BLOCK
SEQ_TILE = 1024
PROJ_ROWS = 256
POST_ROWS = 256
SCORE_LEAD = 3
SCORE_SLOTS = SCORE_LEAD + 1
ADA_ROWS = 16

C_U, C_V, C_GA, C_Q, C_K, C_VV, C_GB = (i * 512 for i in range(7))

_CONTRACT_LANES = (((1,), (1,)), ((), ()))

_TO_FEATURE_MAJOR = (0, 2, 3, 1)
_FROM_FEATURE_MAJOR = (0, 3, 1, 2)


def _sigmoid_exp2(neg_arg_log2):
    return 1.0 / (1.0 + jnp.exp2(neg_arg_log2))


def _gelu(x):
    c1 = -2.0 * (2.0 / jnp.pi) ** 0.5 * LOG2E
    return x * _sigmoid_exp2(x * (c1 + (c1 * 0.044715) * (x * x)))


def _silu(x):
    return x * _sigmoid_exp2(x * (-LOG2E))


def _even_lane_mask(shape):
    lane = lax.broadcasted_iota(jnp.int32, shape, len(shape) - 1)
    return (lane % LANES) < B_HDIM


def _ada_block(c_ref, w_ref, b_ref, o_ref):
    c = c_ref[...]
    o_ref[...] = jnp.dot(_silu(c), w_ref[...], preferred_element_type=jnp.float32) + b_ref[...]


def _relb_block(base_ref, bt_ref, bs_ref, n_new):
    row = lax.broadcasted_iota(jnp.int32, (Q_BLOCK, K_BLOCK), 0)
    col = lax.broadcasted_iota(jnp.int32, (Q_BLOCK, K_BLOCK), 1)
    band_lo = jnp.where(row < CHUNK, 0, CHUNK)
    rel = col - band_lo
    outside = jnp.logical_or(rel < 0, rel >= KV_WIN + CHUNK)
    for par in range(2):
        base = base_ref[0, par:par + 1, :] * LOG2E
        x = jnp.broadcast_to(base, (Q_BLOCK, K_BLOCK))
        shift = 1
        while shift < Q_BLOCK:
            x = jnp.where((row & shift) != 0, jnp.roll(x, shift, axis=1), x)
            shift *= 2
        far = jnp.broadcast_to(base[:, 0:1], (Q_BLOCK, K_BLOCK))
        x = jnp.where(col < row, far, x)
        bs_ref[0, par] = x[0:n_new, :]
        bt_ref[0, :, par * Q_BLOCK:(par + 1) * Q_BLOCK] = jnp.where(outside, NEG, x).T


PREP_COLS = 512
N_PREP = D_IN // PREP_COLS
N_ADA = 3 * D_MODEL // PREP_COLS
WOUT_COLS = D_MODEL // N_PAIRS


def _prep_kernel(c_ref, wada_ref, bada_ref, base_ref, win_ref, wout_ref,
                 mod_ref, bt_ref, bs_ref, winb_ref, woutb_ref, *, n_new):
    j = pl.program_id(0)
    winb_ref[...] = win_ref[...].astype(jnp.bfloat16)

    @pl.when(j < N_ADA)
    def _():
        _ada_block(c_ref, wada_ref, bada_ref, mod_ref)

    @pl.when(j < N_PAIRS)
    def _():
        woutb_ref[...] = wout_ref[...].astype(jnp.bfloat16)
        _relb_block(base_ref, bt_ref, bs_ref, n_new)


def _prep_call(c_all, w_ada, b_ada, base, w_in, w_out, n_new):
    f32, bf16 = jnp.float32, jnp.bfloat16

    def upto(n):
        return lambda j: jnp.minimum(j, n - 1)

    ada_j, pair_j = upto(N_ADA), upto(N_PAIRS)
    return pl.pallas_call(
        functools.partial(_prep_kernel, n_new=n_new),
        grid=(N_PREP,),
        in_specs=[
            pl.BlockSpec((ADA_ROWS, D_MODEL), lambda j: (0, 0)),
            pl.BlockSpec((D_MODEL, PREP_COLS), lambda j: (0, ada_j(j))),
            pl.BlockSpec((1, PREP_COLS), lambda j: (0, ada_j(j))),
            pl.BlockSpec((1, 2, K_BLOCK), lambda j: (pair_j(j), 0, 0)),
            pl.BlockSpec((D_MODEL, PREP_COLS), lambda j: (0, j)),
            pl.BlockSpec((D_MODEL, WOUT_COLS), lambda j: (0, pair_j(j))),
        ],
        out_specs=[
            pl.BlockSpec((ADA_ROWS, PREP_COLS), lambda j: (0, ada_j(j))),
            pl.BlockSpec((1, K_BLOCK, 2 * Q_BLOCK), lambda j: (pair_j(j), 0, 0)),
            pl.BlockSpec((1, 2, n_new, K_BLOCK), lambda j: (pair_j(j), 0, 0, 0)),
            pl.BlockSpec((D_MODEL, PREP_COLS), lambda j: (0, j)),
            pl.BlockSpec((D_MODEL, WOUT_COLS), lambda j: (0, pair_j(j))),
        ],
        out_shape=[
            jax.ShapeDtypeStruct((ADA_ROWS, 3 * D_MODEL), f32),
            jax.ShapeDtypeStruct((N_PAIRS, K_BLOCK, 2 * Q_BLOCK), f32),
            jax.ShapeDtypeStruct((N_PAIRS, 2, n_new, K_BLOCK), f32),
            jax.ShapeDtypeStruct((D_MODEL, D_IN), bf16),
            jax.ShapeDtypeStruct((D_MODEL, D_MODEL), bf16),
        ],
        compiler_params=pltpu.CompilerParams(dimension_semantics=("arbitrary",)),
        name="prep",
    )(c_all, w_ada, b_ada.reshape(1, 3 * D_MODEL), base.reshape(N_PAIRS, 2, K_BLOCK), w_in, w_out)


def _tril_pairs(ws_ref, wt_ref, n):
    row = lax.broadcasted_iota(jnp.int32, (n, n), 0)
    col = lax.broadcasted_iota(jnp.int32, (n, n), 1)
    keep = col <= row
    for g in range(A_GROUPS):
        w = jnp.where(keep, ws_ref[g, :n, :n], 0.0)
        wt_ref[g // 2, (g % 2) * n:(g % 2 + 1) * n, :] = w.astype(jnp.bfloat16)


def _rms_rows(x):
    return x * lax.rsqrt(jnp.mean(x * x, axis=-1, keepdims=True) + EPS)


def _layernorm(x, g, b):
    mu = jnp.mean(x, axis=-1, keepdims=True)
    xc = x - mu
    var = jnp.mean(xc * xc, axis=-1, keepdims=True)
    return xc * lax.rsqrt(var + EPS) * g + b


def _pair_select(stacked, n):
    return jnp.where(_even_lane_mask((n, LANES)), stacked[:n], stacked[n:])


def _split_heads(x):
    return x.reshape(x.shape[0], B_HEADS, B_HDIM)


def _to_feature_major(x):
    return x.T.reshape(B_HEADS, B_HDIM, x.shape[0])


def _store_q(q, qe_ref, qo_ref):
    q = (q * Q_SCALE).astype(jnp.bfloat16)
    even = _even_lane_mask(q.shape)
    zero = jnp.zeros((), jnp.bfloat16)
    qe_ref[...] = jnp.where(even, q, zero)
    qo_ref[...] = jnp.where(even, zero, q)


def _prompt_kernel(x_ref, mod_ref, gpre_ref, win_ref, lng_ref, lnb_ref, ws_ref, bfull_ref,
                   biast_ref, wout_ref, gpost_ref,
                   y_ref, klast_ref, vlast_ref,
                   h_s, usg_s, va_s, qe_s, qo_s, k_s, vt_s, sgb_s, o_s, wt_s, st_s, kf_s, vf_s):
    t = pl.program_id(1)
    nt = pl.num_programs(1)
    T = SEQ_TILE

    @pl.when(jnp.logical_and(pl.program_id(0) == 0, t == 0))
    def _():
        _tril_pairs(ws_ref, wt_s, MLP_CHUNK)

    shift = mod_ref[0, 0:1, :]
    scale = mod_ref[0, 1:2, :]
    gate = mod_ref[0, 2:3, :]

    pre_scale = gpre_ref[...] * (1.0 + scale)

    def proj(c0, rows):
        return jnp.dot(h_s[rows, :], win_ref[:, c0:c0 + 512], preferred_element_type=jnp.float32)

    def proj_pieces(lo):
        rb = slice(lo, lo + PROJ_ROWS)
        halves = [slice(lo, lo + PROJ_ROWS // 2), slice(lo + PROJ_ROWS // 2, lo + PROJ_ROWS)]
        hist = slice(KV_WIN + lo, KV_WIN + lo + PROJ_ROWS)
        keep = slice(lo - (T - KV_WIN), lo - (T - KV_WIN) + PROJ_ROWS) if lo >= T - KV_WIN else None

        def pre_norm():
            for rows in halves:
                h_s[rows, :] = (_rms_rows(x_ref[0, rows, :]) * pre_scale + shift).astype(jnp.bfloat16)

        def u_half(rows):
            usg_s[rows, :] = _gelu(proj(C_U, rows))

        def v_block():
            v = proj(C_VV, rb)
            vt_s[:, hist] = v.T.astype(jnp.bfloat16)
            if keep is not None:
                vf_s[keep, :] = v

        def va_block():
            va_s[rb, :] = _layernorm(_gelu(proj(C_V, rb)), lng_ref[...], lnb_ref[...]
                                     ).astype(jnp.bfloat16)

        def q_block():
            _store_q(proj(C_Q, rb), qe_s.at[rb, :], qo_s.at[rb, :])

        def ga_block():
            usg_s[rb, :] = usg_s[rb, :] * _silu(proj(C_GA, rb))

        def gmlp(p):
            lanes = slice(p * LANES, (p + 1) * LANES)
            for c0 in range(lo, lo + PROJ_ROWS, 2 * MLP_CHUNK):
                chunks = [slice(c0, c0 + MLP_CHUNK), slice(c0 + MLP_CHUNK, c0 + 2 * MLP_CHUNK)]
                slabs = jnp.concatenate([va_s[rows, lanes] for rows in chunks], axis=1)
                mix = jnp.dot(wt_s[p], slabs, preferred_element_type=jnp.float32)
                for i, rows in enumerate(chunks):
                    mixed = _pair_select(mix[:, i * LANES:(i + 1) * LANES], MLP_CHUNK)
                    o_s[rows, lanes] = (usg_s[rows, lanes] * (mixed + bfull_ref[:, lanes])
                                        ).astype(jnp.bfloat16)

        def gb_block():
            sgb_s[rb, :] = _silu(proj(C_GB, rb))

        def k_block():
            k = proj(C_K, rb)
            k_s[hist, :] = k.astype(jnp.bfloat16)
            if keep is not None:
                kf_s[keep, :] = k

        g = [functools.partial(gmlp, p) for p in range(N_PAIRS)]
        return [pre_norm, functools.partial(u_half, halves[0]), functools.partial(u_half, halves[1]),
                v_block, va_block, ga_block, g[0], q_block, g[1], gb_block, g[2], k_block, g[3]]

    def key_lo(qb, first_tile):
        return max(KV_WIN - qb * Q_BLOCK, 0) if first_tile else 0

    def scores(qb, p, slot, first_tile):
        r0, lo = qb * Q_BLOCK, key_lo(qb, first_tile)
        lanes = slice(p * LANES, (p + 1) * LANES)
        qs = jnp.concatenate([qe_s[r0:r0 + Q_BLOCK, lanes],
                              qo_s[r0:r0 + Q_BLOCK, lanes]], axis=0)
        half = (K_BLOCK - lo) // 2
        for a in (lo, lo + half):
            st_s[slot, a:a + half, :] = lax.dot_general(
                k_s[r0 + a:r0 + a + half, lanes], qs, _CONTRACT_LANES,
                preferred_element_type=jnp.float32) + biast_ref[p, a:a + half, :]

    def finish(qb, p, slot, first_tile):
        r0, lo = qb * Q_BLOCK, key_lo(qb, first_tile)
        st = st_s[slot, lo:K_BLOCK, :]
        m = jnp.max(st, axis=0, keepdims=True)
        pt = jnp.exp2(st - m).astype(jnp.bfloat16)
        ones = jnp.ones((BF16_ROWS, K_BLOCK - lo), jnp.bfloat16)
        lhs = jnp.concatenate([vt_s[p * LANES:(p + 1) * LANES, r0 + lo:r0 + K_BLOCK], ones], axis=0)
        ot = jnp.dot(lhs, pt, preferred_element_type=jnp.float32)
        inv_l = 1.0 / ot[LANES:LANES + 1, :]
        ots = [ot[par * B_HDIM:(par + 1) * B_HDIM, par * Q_BLOCK:(par + 1) * Q_BLOCK]
               * inv_l[:, par * Q_BLOCK:(par + 1) * Q_BLOCK] for par in range(2)]
        yb = jnp.concatenate(ots, axis=0).T * sgb_s[r0:r0 + Q_BLOCK, p * LANES:(p + 1) * LANES]
        o_s[r0:r0 + Q_BLOCK, B_WIDTH + p * LANES:B_WIDTH + (p + 1) * LANES] = (
            yb.astype(jnp.bfloat16))

    post_scale = gate * gpost_ref[...]

    def out_block(lo, n):
        rows = slice(lo, lo + n)
        o = jnp.dot(o_s[rows, :], wout_ref[...], preferred_element_type=jnp.float32)
        y_ref[0, rows, :] = x_ref[0, rows, :] + _rms_rows(o) * post_scale

    def attend(first_tile):
        items = [(qb, p) for qb in range(T // Q_BLOCK) for p in range(N_PAIRS)]
        per_block = (PROJ_ROWS // Q_BLOCK) * N_PAIRS
        n_blocks = T // PROJ_ROWS
        for piece in proj_pieces(0):
            piece()
        pending = []
        for j in range(SCORE_LEAD):
            scores(*items[j], j % SCORE_SLOTS, first_tile)
        for i, (qb, p) in enumerate(items):
            blk, pos = divmod(i, per_block)
            if pos == 0 and blk + 1 < n_blocks:
                pending = proj_pieces((blk + 1) * PROJ_ROWS)
            if i + SCORE_LEAD < len(items):
                assert pos < per_block - SCORE_LEAD or not pending
                scores(*items[i + SCORE_LEAD], (i + SCORE_LEAD) % SCORE_SLOTS, first_tile)
            finish(qb, p, i % SCORE_SLOTS, first_tile)
            slots_left = per_block - SCORE_LEAD - pos
            if pending and slots_left > 0:
                n_now = -(-len(pending) // slots_left)
                for piece in pending[:n_now]:
                    piece()
                pending = pending[n_now:]
            rows_done = (qb + 1) * Q_BLOCK
            if p == N_PAIRS - 1 and rows_done % POST_ROWS == 0:
                out_block(rows_done - POST_ROWS, POST_ROWS)

    @pl.when(t == 0)
    def _():
        attend(True)

    @pl.when(t > 0)
    def _():
        attend(False)

    k_s[0:KV_WIN, :] = k_s[T:T + KV_WIN, :]
    vt_s[:, 0:KV_WIN] = vt_s[:, T:T + KV_WIN]

    @pl.when(t == nt - 1)
    def _():
        klast_ref[0] = _to_feature_major(kf_s[...])
        vlast_ref[0] = _to_feature_major(vf_s[...])


def _const_spec(shape):
    nd = len(shape)
    return pl.BlockSpec(shape, lambda b, t: (0,) * nd, pipeline_mode=pl.Buffered(1))


def _prompt_call(x, mod, g_pre, w_in, ln_g, ln_b, w_s, b_full, bias_t, w_out, g_post):
    bsz, seq, _ = x.shape
    T = SEQ_TILE
    assert seq % T == 0 and T % PROJ_ROWS == 0 and KV_WIN % PROJ_ROWS == 0 and T % POST_ROWS == 0
    assert PROJ_ROWS % (2 * MLP_CHUNK) == 0
    nt = seq // T
    f32, bf16 = jnp.float32, jnp.bfloat16
    return pl.pallas_call(
        _prompt_kernel,
        grid=(bsz, nt),
        in_specs=[
            pl.BlockSpec((1, T, D_MODEL), lambda b, t: (b, t, 0)),
            pl.BlockSpec((1, 3, D_MODEL), lambda b, t: (b, 0, 0)),
            _const_spec((1, D_MODEL)),
            _const_spec((D_MODEL, D_IN)),
            _const_spec((1, A_WIDTH)),
            _const_spec((1, A_WIDTH)),
            _const_spec((A_GROUPS, MLP_CHUNK, MLP_CHUNK)),
            _const_spec((MLP_CHUNK, A_WIDTH)),
            _const_spec((N_PAIRS, K_BLOCK, 2 * Q_BLOCK)),
            _const_spec((D_MODEL, D_MODEL)),
            _const_spec((1, D_MODEL)),
        ],
        out_specs=[
            pl.BlockSpec((1, T, D_MODEL), lambda b, t: (b, t, 0)),
            pl.BlockSpec((1, B_HEADS, B_HDIM, KV_WIN), lambda b, t: (b, 0, 0, 0)),
            pl.BlockSpec((1, B_HEADS, B_HDIM, KV_WIN), lambda b, t: (b, 0, 0, 0)),
        ],
        out_shape=[
            jax.ShapeDtypeStruct((bsz, seq, D_MODEL), f32),
            jax.ShapeDtypeStruct((bsz, B_HEADS, B_HDIM, KV_WIN), f32),
            jax.ShapeDtypeStruct((bsz, B_HEADS, B_HDIM, KV_WIN), f32),
        ],
        scratch_shapes=[
            pltpu.VMEM((T, D_MODEL), bf16),
            pltpu.VMEM((T, A_WIDTH), f32),
            pltpu.VMEM((T, A_WIDTH), bf16),
            pltpu.VMEM((T, B_WIDTH), bf16),
            pltpu.VMEM((T, B_WIDTH), bf16),
            pltpu.VMEM((KV_WIN + T, B_WIDTH), bf16),
            pltpu.VMEM((B_WIDTH, KV_WIN + T), bf16),
            pltpu.VMEM((T, B_WIDTH), f32),
            pltpu.VMEM((T, D_MODEL), bf16),
            pltpu.VMEM((N_PAIRS, 2 * MLP_CHUNK, MLP_CHUNK), bf16),
            pltpu.VMEM((SCORE_SLOTS, K_BLOCK, 2 * Q_BLOCK), f32),
            pltpu.VMEM((KV_WIN, B_WIDTH), f32),
            pltpu.VMEM((KV_WIN, B_WIDTH), f32),
        ],
        compiler_params=pltpu.CompilerParams(
            dimension_semantics=("arbitrary", "arbitrary"),
            vmem_limit_bytes=VMEM_LIMIT_BYTES),
        name="prompt",
    )(x, mod, g_pre, w_in, ln_g, ln_b, w_s, b_full, bias_t, w_out, g_post)


def _sample_kernel(x_ref, mod_ref, gpre_ref, win_ref, lng_ref, lnb_ref, ws_ref, bfull_ref,
                   bias_ref, wout_ref, gpost_ref, ck_ref, cv_ref,
                   y_ref, knew_ref, vnew_ref, vanew_ref,
                   usg_s, va_s, qe_s, qo_s, k_s, v_s, sgb_s, o_s, wt_s, *, n_streams, n_new):
    b = pl.program_id(0)
    S = n_new

    @pl.when(b == 0)
    def _():
        _tril_pairs(ws_ref, wt_s, S)
        gpre = gpre_ref[...]
        hs = []
        for i in range(n_streams):
            xi = x_ref[i * S:(i + 1) * S, :]
            hs.append(_rms_rows(xi) * (gpre * (1.0 + mod_ref[i, 1:2, :])) + mod_ref[i, 0:1, :])
        h = jnp.concatenate(hs, axis=0).astype(jnp.bfloat16)

        def proj(c0):
            return jnp.dot(h, win_ref[:, c0:c0 + 512], preferred_element_type=jnp.float32)

        va = _layernorm(_gelu(proj(C_V)), lng_ref[...], lnb_ref[...])
        vanew_ref[...] = va
        va_s[...] = va.astype(jnp.bfloat16)
        usg_s[...] = _gelu(proj(C_U)) * _silu(proj(C_GA))
        _store_q(proj(C_Q), qe_s, qo_s)
        k = proj(C_K)
        v = proj(C_VV)
        knew_ref[...] = _split_heads(k)
        vnew_ref[...] = _split_heads(v)
        k_s[...] = k.astype(jnp.bfloat16)
        v_s[...] = v.astype(jnp.bfloat16)
        sgb_s[...] = _silu(proj(C_GB))

    r0 = pl.multiple_of(b * S, S)
    rows = pl.ds(r0, S)
    pair_lanes = [slice(p * LANES, (p + 1) * LANES) for p in range(N_PAIRS)]
    for p, lanes in enumerate(pair_lanes):
        mix = jnp.dot(wt_s[p], va_s[rows, lanes], preferred_element_type=jnp.float32)
        mixed = _pair_select(mix, S) + bfull_ref[0:S, lanes]
        o_s[rows, lanes] = (usg_s[rows, lanes] * mixed).astype(jnp.bfloat16)

    scores = []
    for p, lanes in enumerate(pair_lanes):
        qs = jnp.concatenate([qe_s[rows, lanes], qo_s[rows, lanes]], axis=0)
        kct = ck_ref[0, 2 * p:2 * p + 2].reshape(LANES, KV_WIN).astype(jnp.bfloat16)
        bias = jnp.concatenate([bias_ref[p, 0], bias_ref[p, 1]], axis=0)
        s1 = jnp.dot(qs, kct, preferred_element_type=jnp.float32) + bias[:, 0:KV_WIN]
        s2 = lax.dot_general(qs, k_s[rows, lanes], _CONTRACT_LANES,
                             preferred_element_type=jnp.float32) + bias[:, KV_WIN:KV_WIN + S]
        scores.append((s1, s2))
    weights = []
    for s1, s2 in scores:
        m = jnp.maximum(jnp.max(s1, axis=-1, keepdims=True), jnp.max(s2, axis=-1, keepdims=True))
        e1 = jnp.exp2(s1 - m)
        e2 = jnp.exp2(s2 - m)
        l = jnp.sum(e1, axis=-1, keepdims=True) + jnp.sum(e2, axis=-1, keepdims=True)
        weights.append((e1.astype(jnp.bfloat16), e2.astype(jnp.bfloat16), 1.0 / l))
    for p, lanes in enumerate(pair_lanes):
        e1, e2, inv_l = weights[p]
        vct = cv_ref[0, 2 * p:2 * p + 2].reshape(LANES, KV_WIN).astype(jnp.bfloat16)
        o2 = (lax.dot_general(e1, vct, _CONTRACT_LANES, preferred_element_type=jnp.float32)
              + jnp.dot(e2, v_s[rows, lanes], preferred_element_type=jnp.float32))
        yb = _pair_select(o2 * inv_l, S) * sgb_s[rows, lanes]
        o_s[rows, B_WIDTH + p * LANES:B_WIDTH + (p + 1) * LANES] = yb.astype(jnp.bfloat16)

    @pl.when(b == n_streams - 1)
    def _():
        o = jnp.dot(o_s[...], wout_ref[...], preferred_element_type=jnp.float32)
        on = _rms_rows(o) * gpost_ref[...]
        for i in range(n_streams):
            sl = slice(i * S, (i + 1) * S)
            y_ref[sl, :] = x_ref[sl, :] + mod_ref[i, 2:3, :] * on[sl, :]


def _sample_call(x, mod, g_pre, w_in, ln_g, ln_b, w_s, b_full, bias_s, w_out, g_post, ck, cv):
    n_streams, n_new, _ = x.shape
    assert n_new % BF16_ROWS == 0 and n_new <= CHUNK
    rows = n_streams * n_new
    f32, bf16 = jnp.float32, jnp.bfloat16

    def const(shape):
        nd = len(shape)
        return pl.BlockSpec(shape, lambda b: (0,) * nd, pipeline_mode=pl.Buffered(1))

    kern = functools.partial(_sample_kernel, n_streams=n_streams, n_new=n_new)
    return pl.pallas_call(
        kern,
        grid=(n_streams,),
        in_specs=[
            const((rows, D_MODEL)),
            const((n_streams, 3, D_MODEL)),
            const((1, D_MODEL)),
            const((D_MODEL, D_IN)),
            const((1, A_WIDTH)),
            const((1, A_WIDTH)),
            const((A_GROUPS, MLP_CHUNK, MLP_CHUNK)),
            const((MLP_CHUNK, A_WIDTH)),
            const((N_PAIRS, 2, n_new, K_BLOCK)),
            const((D_MODEL, D_MODEL)),
            const((1, D_MODEL)),
            pl.BlockSpec((1, B_HEADS, B_HDIM, KV_WIN), lambda b: (b, 0, 0, 0)),
            pl.BlockSpec((1, B_HEADS, B_HDIM, KV_WIN), lambda b: (b, 0, 0, 0)),
        ],
        out_specs=[
            pl.BlockSpec((rows, D_MODEL), lambda b: (0, 0)),
            pl.BlockSpec((rows, B_HEADS, B_HDIM), lambda b: (0, 0, 0)),
            pl.BlockSpec((rows, B_HEADS, B_HDIM), lambda b: (0, 0, 0)),
            pl.BlockSpec((rows, A_WIDTH), lambda b: (0, 0)),
        ],
        out_shape=[
            jax.ShapeDtypeStruct((rows, D_MODEL), f32),
            jax.ShapeDtypeStruct((rows, B_HEADS, B_HDIM), f32),
            jax.ShapeDtypeStruct((rows, B_HEADS, B_HDIM), f32),
            jax.ShapeDtypeStruct((rows, A_WIDTH), f32),
        ],
        scratch_shapes=[
            pltpu.VMEM((rows, A_WIDTH), f32),
            pltpu.VMEM((rows, A_WIDTH), bf16),
            pltpu.VMEM((rows, B_WIDTH), bf16),
            pltpu.VMEM((rows, B_WIDTH), bf16),
            pltpu.VMEM((rows, B_WIDTH), bf16),
            pltpu.VMEM((rows, B_WIDTH), bf16),
            pltpu.VMEM((rows, B_WIDTH), f32),
            pltpu.VMEM((rows, D_MODEL), bf16),
            pltpu.VMEM((N_PAIRS, 2 * n_new, n_new), bf16),
        ],
        compiler_params=pltpu.CompilerParams(
            dimension_semantics=("arbitrary",),
            vmem_limit_bytes=VMEM_LIMIT_BYTES),
        name="sample",
    )(x.reshape(rows, D_MODEL), mod, g_pre, w_in, ln_g, ln_b, w_s, b_full, bias_s, w_out, g_post,
      ck, cv)


def _bias_base(rel_bias):
    n_far = KV_WIN - REL_CLIP
    far = jnp.broadcast_to(rel_bias[:, N_REL - 1:N_REL], (B_HEADS, n_far))
    near = rel_bias[:, N_REL - 1:0:-1]
    return jnp.concatenate([far, near], axis=1)


def kernel(x_prompt, x_sample, cache_attn_k, cache_attn_v, c_prompt, c_sample, g_pre, w_ada, b_ada,
           w_in, ln_g, ln_b, w_s, b_s, rel_bias, w_out, g_post):
    depth = g_pre.shape[0]
    bsz, seq, _ = x_prompt.shape
    n_streams, n_new, _ = x_sample.shape
    win = cache_attn_k.shape[2]
    assert win == KV_WIN and bsz + n_streams <= ADA_ROWS

    yp, ys = x_prompt, x_sample
    kp_rows, vp_rows, ks_rows, vs_rows, va_rows = [], [], [], [], []
    c_all = jnp.concatenate(
        [c_prompt, c_sample, jnp.zeros((ADA_ROWS - bsz - n_streams, D_MODEL), c_prompt.dtype)], axis=0)
    for l in range(depth):
        mod, bias_t, bias_s, w_in_b, w_out_b = _prep_call(
            c_all, w_ada[l], b_ada[l], _bias_base(rel_bias[l]), w_in[l], w_out[l], n_new)
        mod = mod.reshape(ADA_ROWS, 3, D_MODEL)
        b_full = jnp.repeat(b_s[l].T, A_GDIM, axis=1)
        pre = (g_pre[l].reshape(1, D_MODEL), w_in_b, ln_g[l].reshape(1, A_WIDTH),
               ln_b[l].reshape(1, A_WIDTH), w_s[l], b_full)
        post = (w_out_b, g_post[l].reshape(1, D_MODEL))

        yp, k_last, v_last = _prompt_call(yp, mod[:bsz], *pre, bias_t, *post)
        kp_rows.append(jnp.transpose(k_last, _FROM_FEATURE_MAJOR))
        vp_rows.append(jnp.transpose(v_last, _FROM_FEATURE_MAJOR))

        ys2, k_new, v_new, va_new = _sample_call(ys, mod[bsz:bsz + n_streams], *pre, bias_s, *post,
                                                 jnp.transpose(cache_attn_k[l], _TO_FEATURE_MAJOR),
                                                 jnp.transpose(cache_attn_v[l], _TO_FEATURE_MAJOR))
        ys = ys2.reshape(n_streams, n_new, D_MODEL)
        ks_rows.append(k_new.reshape(n_streams, n_new, B_HEADS, B_HDIM))
        vs_rows.append(v_new.reshape(n_streams, n_new, B_HEADS, B_HDIM))
        va_rows.append(va_new.reshape(n_streams, n_new, A_WIDTH))

    return (yp, ys, jnp.stack(kp_rows), jnp.stack(vp_rows), jnp.stack(ks_rows), jnp.stack(vs_rows),
            jnp.stack(va_rows))
```

```python
import functools

import jax
import jax.numpy as jnp
from jax import lax
from jax.experimental import pallas as pl
from jax.experimental.pallas import tpu as pltpu

LANES = 128
SUBLANES = 8
BF16_ROWS = 16
VMEM_LIMIT_BYTES = 60 * 1024 * 1024

D_MODEL = 1024
A_WIDTH = 512
A_GROUPS = 8
A_GDIM = A_WIDTH // A_GROUPS
MLP_CHUNK = 128
B_WIDTH = 512
B_HEADS = 8
B_HDIM = B_WIDTH // B_HEADS
CHUNK = 64
KV_WIN = 512
REL_CLIP = 128
N_REL = 2 * REL_CLIP + 1
EPS = 1e-6
D_IN = 3 * A_WIDTH + 4 * B_WIDTH
NEG = -1e30
LOG2E = 1.4426950408889634
Q_SCALE = B_HDIM ** -0.5 * LOG2E

N_PAIRS = B_HEADS // 2
Q_BLOCK = 2 * CHUNK
K_BLOCK = KV_WIN + Q_BLOCK
SEQ_TILE = 1024
PROJ_ROWS = 512
POST_ROWS = 256
SCORE_LEAD = 3
SCORE_SLOTS = SCORE_LEAD + 1
ADA_ROWS = 16

C_U, C_V, C_GA, C_Q, C_K, C_VV, C_GB = (i * 512 for i in range(7))

_CONTRACT_LANES = (((1,), (1,)), ((), ()))

_TO_FEATURE_MAJOR = (0, 2, 3, 1)
_FROM_FEATURE_MAJOR = (0, 3, 1, 2)


def _sigmoid_exp2(neg_arg_log2):
    return 1.0 / (1.0 + jnp.exp2(neg_arg_log2))


def _gelu(x):
    c1 = -2.0 * (2.0 / jnp.pi) ** 0.5 * LOG2E
    return x * _sigmoid_exp2(x * (c1 + (c1 * 0.044715) * (x * x)))


def _silu(x):
    return x * _sigmoid_exp2(x * (-LOG2E))


def _even_lane_mask(shape):
    lane = lax.broadcasted_iota(jnp.int32, shape, len(shape) - 1)
    return (lane % LANES) < B_HDIM


def _ada_block(c_ref, w_ref, b_ref, o_ref):
    c = c_ref[...]
    o_ref[...] = jnp.dot(_silu(c), w_ref[...], preferred_element_type=jnp.float32) + b_ref[...]


def _relb_block(base_ref, bt_ref, bs_ref, n_new):
    row = lax.broadcasted_iota(jnp.int32, (Q_BLOCK, K_BLOCK), 0)
    col = lax.broadcasted_iota(jnp.int32, (Q_BLOCK, K_BLOCK), 1)
    band_lo = jnp.where(row < CHUNK, 0, CHUNK)
    rel = col - band_lo
    outside = jnp.logical_or(rel < 0, rel >= KV_WIN + CHUNK)
    for par in range(2):
        base = base_ref[0, par:par + 1, :] * LOG2E
        x = jnp.broadcast_to(base, (Q_BLOCK, K_BLOCK))
        shift = 1
        while shift < Q_BLOCK:
            x = jnp.where((row & shift) != 0, jnp.roll(x, shift, axis=1), x)
            shift *= 2
        far = jnp.broadcast_to(base[:, 0:1], (Q_BLOCK, K_BLOCK))
        x = jnp.where(col < row, far, x)
        bs_ref[0, par] = x[0:n_new, :]
        bt_ref[0, :, par * Q_BLOCK:(par + 1) * Q_BLOCK] = jnp.where(outside, NEG, x).T


PREP_COLS = 512
N_PREP = D_IN // PREP_COLS
N_ADA = 3 * D_MODEL // PREP_COLS
WOUT_COLS = D_MODEL // N_PAIRS


def _prep_kernel(c_ref, wada_ref, bada_ref, base_ref, win_ref, wout_ref,
                 mod_ref, bt_ref, bs_ref, winb_ref, woutb_ref, *, n_new):
    j = pl.program_id(0)
    winb_ref[...] = win_ref[...].astype(jnp.bfloat16)

    @pl.when(j < N_ADA)
    def _():
        _ada_block(c_ref, wada_ref, bada_ref, mod_ref)

    @pl.when(j < N_PAIRS)
    def _():
        woutb_ref[...] = wout_ref[...].astype(jnp.bfloat16)
        _relb_block(base_ref, bt_ref, bs_ref, n_new)


def _prep_call(c_all, w_ada, b_ada, base, w_in, w_out, n_new):
    f32, bf16 = jnp.float32, jnp.bfloat16

    def upto(n):
        return lambda j: jnp.minimum(j, n - 1)

    ada_j, pair_j = upto(N_ADA), upto(N_PAIRS)
    return pl.pallas_call(
        functools.partial(_prep_kernel, n_new=n_new),
        grid=(N_PREP,),
        in_specs=[
            pl.BlockSpec((ADA_ROWS, D_MODEL), lambda j: (0, 0)),
            pl.BlockSpec((D_MODEL, PREP_COLS), lambda j: (0, ada_j(j))),
            pl.BlockSpec((1, PREP_COLS), lambda j: (0, ada_j(j))),
            pl.BlockSpec((1, 2, K_BLOCK), lambda j: (pair_j(j), 0, 0)),
            pl.BlockSpec((D_MODEL, PREP_COLS), lambda j: (0, j)),
            pl.BlockSpec((D_MODEL, WOUT_COLS), lambda j: (0, pair_j(j))),
        ],
        out_specs=[
            pl.BlockSpec((ADA_ROWS, PREP_COLS), lambda j: (0, ada_j(j))),
            pl.BlockSpec((1, K_BLOCK, 2 * Q_BLOCK), lambda j: (pair_j(j), 0, 0)),
            pl.BlockSpec((1, 2, n_new, K_BLOCK), lambda j: (pair_j(j), 0, 0, 0)),
            pl.BlockSpec((D_MODEL, PREP_COLS), lambda j: (0, j)),
            pl.BlockSpec((D_MODEL, WOUT_COLS), lambda j: (0, pair_j(j))),
        ],
        out_shape=[
            jax.ShapeDtypeStruct((ADA_ROWS, 3 * D_MODEL), f32),
            jax.ShapeDtypeStruct((N_PAIRS, K_BLOCK, 2 * Q_BLOCK), f32),
            jax.ShapeDtypeStruct((N_PAIRS, 2, n_new, K_BLOCK), f32),
            jax.ShapeDtypeStruct((D_MODEL, D_IN), bf16),
            jax.ShapeDtypeStruct((D_MODEL, D_MODEL), bf16),
        ],
        compiler_params=pltpu.CompilerParams(dimension_semantics=("arbitrary",)),
        name="prep",
    )(c_all, w_ada, b_ada.reshape(1, 3 * D_MODEL), base.reshape(N_PAIRS, 2, K_BLOCK), w_in, w_out)


def _tril_pairs(ws_ref, wt_ref, n):
    row = lax.broadcasted_iota(jnp.int32, (n, n), 0)
    col = lax.broadcasted_iota(jnp.int32, (n, n), 1)
    keep = col <= row
    for g in range(A_GROUPS):
        w = jnp.where(keep, ws_ref[g, :n, :n], 0.0)
        wt_ref[g // 2, (g % 2) * n:(g % 2 + 1) * n, :] = w.astype(jnp.bfloat16)


def _rms_rows(x):
    return x * lax.rsqrt(jnp.mean(x * x, axis=-1, keepdims=True) + EPS)


def _layernorm(x, g, b):
    mu = jnp.mean(x, axis=-1, keepdims=True)
    xc = x - mu
    var = jnp.mean(xc * xc, axis=-1, keepdims=True)
    return xc * lax.rsqrt(var + EPS) * g + b


def _pair_select(stacked, n):
    return jnp.where(_even_lane_mask((n, LANES)), stacked[:n], stacked[n:])


def _split_heads(x):
    return x.reshape(x.shape[0], B_HEADS, B_HDIM)


def _to_feature_major(x):
    return x.T.reshape(B_HEADS, B_HDIM, x.shape[0])


def _store_q(q, qe_ref, qo_ref):
    q = (q * Q_SCALE).astype(jnp.bfloat16)
    even = _even_lane_mask(q.shape)
    zero = jnp.zeros((), jnp.bfloat16)
    qe_ref[...] = jnp.where(even, q, zero)
    qo_ref[...] = jnp.where(even, zero, q)


def _prompt_kernel(x_ref, mod_ref, gpre_ref, win_ref, lng_ref, lnb_ref, ws_ref, bfull_ref,
                   biast_ref, wout_ref, gpost_ref,
                   y_ref, klast_ref, vlast_ref,
                   h_s, usg_s, va_s, qe_s, qo_s, k_s, vt_s, sgb_s, o_s, wt_s, st_s, kf_s, vf_s):
    t = pl.program_id(1)
    nt = pl.num_programs(1)
    T = SEQ_TILE

    @pl.when(jnp.logical_and(pl.program_id(0) == 0, t == 0))
    def _():
        _tril_pairs(ws_ref, wt_s, MLP_CHUNK)

    shift = mod_ref[0, 0:1, :]
    scale = mod_ref[0, 1:2, :]
    gate = mod_ref[0, 2:3, :]

    pre_scale = gpre_ref[...] * (1.0 + scale)

    def proj(c0, rows):
        return jnp.dot(h_s[rows, :], win_ref[:, c0:c0 + 512], preferred_element_type=jnp.float32)

    def proj_pieces(lo):
        rb = slice(lo, lo + PROJ_ROWS)
        halves = [slice(lo, lo + PROJ_ROWS // 2), slice(lo + PROJ_ROWS // 2, lo + PROJ_ROWS)]
        hist = slice(KV_WIN + lo, KV_WIN + lo + PROJ_ROWS)
        keep = slice(lo - (T - KV_WIN), lo - (T - KV_WIN) + PROJ_ROWS) if lo >= T - KV_WIN else None

        def pre_norm():
            for rows in halves:
                h_s[rows, :] = (_rms_rows(x_ref[0, rows, :]) * pre_scale + shift).astype(jnp.bfloat16)

        def u_half(rows):
            usg_s[rows, :] = _gelu(proj(C_U, rows))

        def v_block():
            v = proj(C_VV, rb)
            vt_s[:, hist] = v.T.astype(jnp.bfloat16)
            if keep is not None:
                vf_s[keep, :] = v

        def va_block():
            va_s[rb, :] = _layernorm(_gelu(proj(C_V, rb)), lng_ref[...], lnb_ref[...]
                                     ).astype(jnp.bfloat16)

        def q_block():
            _store_q(proj(C_Q, rb), qe_s.at[rb, :], qo_s.at[rb, :])

        def ga_block():
            usg_s[rb, :] = usg_s[rb, :] * _silu(proj(C_GA, rb))

        def gmlp(p):
            lanes = slice(p * LANES, (p + 1) * LANES)
            for c0 in range(lo, lo + PROJ_ROWS, 2 * MLP_CHUNK):
                chunks = [slice(c0, c0 + MLP_CHUNK), slice(c0 + MLP_CHUNK, c0 + 2 * MLP_CHUNK)]
                slabs = jnp.concatenate([va_s[rows, lanes] for rows in chunks], axis=1)
                mix = jnp.dot(wt_s[p], slabs, preferred_element_type=jnp.float32)
                for i, rows in enumerate(chunks):
                    mixed = _pair_select(mix[:, i * LANES:(i + 1) * LANES], MLP_CHUNK)
                    o_s[rows, lanes] = (usg_s[rows, lanes] * (mixed + bfull_ref[:, lanes])
                                        ).astype(jnp.bfloat16)

        def gb_block():
            sgb_s[rb, :] = _silu(proj(C_GB, rb))

        def k_block():
            k = proj(C_K, rb)
            k_s[hist, :] = k.astype(jnp.bfloat16)
            if keep is not None:
                kf_s[keep, :] = k

        g = [functools.partial(gmlp, p) for p in range(N_PAIRS)]
        return [pre_norm, functools.partial(u_half, halves[0]), functools.partial(u_half, halves[1]),
                v_block, va_block, ga_block, g[0], q_block, g[1], gb_block, g[2], k_block, g[3]]

    def key_lo(qb, first_tile):
        return max(KV_WIN - qb * Q_BLOCK, 0) if first_tile else 0

    def scores(qb, p, slot, first_tile):
        r0, lo = qb * Q_BLOCK, key_lo(qb, first_tile)
        lanes = slice(p * LANES, (p + 1) * LANES)
        qs = jnp.concatenate([qe_s[r0:r0 + Q_BLOCK, lanes],
                              qo_s[r0:r0 + Q_BLOCK, lanes]], axis=0)
        half = (K_BLOCK - lo) // 2
        for a in (lo, lo + half):
            st_s[slot, a:a + half, :] = lax.dot_general(
                k_s[r0 + a:r0 + a + half, lanes], qs, _CONTRACT_LANES,
                preferred_element_type=jnp.float32) + biast_ref[p, a:a + half, :]

    def finish(qb, p, slot, first_tile):
        r0, lo = qb * Q_BLOCK, key_lo(qb, first_tile)
        st = st_s[slot, lo:K_BLOCK, :]
        m = jnp.max(st, axis=0, keepdims=True)
        pt = jnp.exp2(st - m).astype(jnp.bfloat16)
        ones = jnp.ones((BF16_ROWS, K_BLOCK - lo), jnp.bfloat16)
        lhs = jnp.concatenate([vt_s[p * LANES:(p + 1) * LANES, r0 + lo:r0 + K_BLOCK], ones], axis=0)
        ot = jnp.dot(lhs, pt, preferred_element_type=jnp.float32)
        inv_l = 1.0 / ot[LANES:LANES + 1, :]
        ots = [ot[par * B_HDIM:(par + 1) * B_HDIM, par * Q_BLOCK:(par + 1) * Q_BLOCK]
               * inv_l[:, par * Q_BLOCK:(par + 1) * Q_BLOCK] for par in range(2)]
        yb = jnp.concatenate(ots, axis=0).T * sgb_s[r0:r0 + Q_BLOCK, p * LANES:(p + 1) * LANES]
        o_s[r0:r0 + Q_BLOCK, B_WIDTH + p * LANES:B_WIDTH + (p + 1) * LANES] = (
            yb.astype(jnp.bfloat16))

    post_scale = gate * gpost_ref[...]

    def out_block(lo, n):
        rows = slice(lo, lo + n)
        o = jnp.dot(o_s[rows, :], wout_ref[...], preferred_element_type=jnp.float32)
        y_ref[0, rows, :] = x_ref[0, rows, :] + _rms_rows(o) * post_scale

    def attend(first_tile):
        items = [(qb, p) for qb in range(T // Q_BLOCK) for p in range(N_PAIRS)]
        per_block = (PROJ_ROWS // Q_BLOCK) * N_PAIRS
        n_blocks = T // PROJ_ROWS
        for piece in proj_pieces(0):
            piece()
        pending = []
        for j in range(SCORE_LEAD):
            scores(*items[j], j % SCORE_SLOTS, first_tile)
        for i, (qb, p) in enumerate(items):
            blk, pos = divmod(i, per_block)
            if pos == 0 and blk + 1 < n_blocks:
                pending = proj_pieces((blk + 1) * PROJ_ROWS)
            if i + SCORE_LEAD < len(items):
                assert pos < per_block - SCORE_LEAD or not pending
                scores(*items[i + SCORE_LEAD], (i + SCORE_LEAD) % SCORE_SLOTS, first_tile)
            finish(qb, p, i % SCORE_SLOTS, first_tile)
            slots_left = per_block - SCORE_LEAD - pos
            if pending and slots_left > 0:
                n_now = -(-len(pending) // slots_left)
                for piece in pending[:n_now]:
                    piece()
                pending = pending[n_now:]
            rows_done = (qb + 1) * Q_BLOCK
            if p == N_PAIRS - 1 and rows_done % POST_ROWS == 0:
                out_block(rows_done - POST_ROWS, POST_ROWS)

    @pl.when(t == 0)
    def _():
        attend(True)

    @pl.when(t > 0)
    def _():
        attend(False)

    k_s[0:KV_WIN, :] = k_s[T:T + KV_WIN, :]
    vt_s[:, 0:KV_WIN] = vt_s[:, T:T + KV_WIN]

    @pl.when(t == nt - 1)
    def _():
        klast_ref[0] = _to_feature_major(kf_s[...])
        vlast_ref[0] = _to_feature_major(vf_s[...])


def _const_spec(shape):
    nd = len(shape)
    return pl.BlockSpec(shape, lambda b, t: (0,) * nd, pipeline_mode=pl.Buffered(1))


def _prompt_call(x, mod, g_pre, w_in, ln_g, ln_b, w_s, b_full, bias_t, w_out, g_post):
    bsz, seq, _ = x.shape
    T = SEQ_TILE
    assert seq % T == 0 and T % PROJ_ROWS == 0 and KV_WIN % PROJ_ROWS == 0 and T % POST_ROWS == 0
    assert PROJ_ROWS % (2 * MLP_CHUNK) == 0
    nt = seq // T
    f32, bf16 = jnp.float32, jnp.bfloat16
    return pl.pallas_call(
        _prompt_kernel,
        grid=(bsz, nt),
        in_specs=[
            pl.BlockSpec((1, T, D_MODEL), lambda b, t: (b, t, 0)),
            pl.BlockSpec((1, 3, D_MODEL), lambda b, t: (b, 0, 0)),
            _const_spec((1, D_MODEL)),
            _const_spec((D_MODEL, D_IN)),
            _const_spec((1, A_WIDTH)),
            _const_spec((1, A_WIDTH)),
            _const_spec((A_GROUPS, MLP_CHUNK, MLP_CHUNK)),
            _const_spec((MLP_CHUNK, A_WIDTH)),
            _const_spec((N_PAIRS, K_BLOCK, 2 * Q_BLOCK)),
            _const_spec((D_MODEL, D_MODEL)),
            _const_spec((1, D_MODEL)),
        ],
        out_specs=[
            pl.BlockSpec((1, T, D_MODEL), lambda b, t: (b, t, 0)),
            pl.BlockSpec((1, B_HEADS, B_HDIM, KV_WIN), lambda b, t: (b, 0, 0, 0)),
            pl.BlockSpec((1, B_HEADS, B_HDIM, KV_WIN), lambda b, t: (b, 0, 0, 0)),
        ],
        out_shape=[
            jax.ShapeDtypeStruct((bsz, seq, D_MODEL), f32),
            jax.ShapeDtypeStruct((bsz, B_HEADS, B_HDIM, KV_WIN), f32),
            jax.ShapeDtypeStruct((bsz, B_HEADS, B_HDIM, KV_WIN), f32),
        ],
        scratch_shapes=[
            pltpu.VMEM((T, D_MODEL), bf16),
            pltpu.VMEM((T, A_WIDTH), f32),
            pltpu.VMEM((T, A_WIDTH), bf16),
            pltpu.VMEM((T, B_WIDTH), bf16),
            pltpu.VMEM((T, B_WIDTH), bf16),
            pltpu.VMEM((KV_WIN + T, B_WIDTH), bf16),
            pltpu.VMEM((B_WIDTH, KV_WIN + T), bf16),
            pltpu.VMEM((T, B_WIDTH), f32),
            pltpu.VMEM((T, D_MODEL), bf16),
            pltpu.VMEM((N_PAIRS, 2 * MLP_CHUNK, MLP_CHUNK), bf16),
            pltpu.VMEM((SCORE_SLOTS, K_BLOCK, 2 * Q_BLOCK), f32),
            pltpu.VMEM((KV_WIN, B_WIDTH), f32),
            pltpu.VMEM((KV_WIN, B_WIDTH), f32),
        ],
        compiler_params=pltpu.CompilerParams(
            dimension_semantics=("arbitrary", "arbitrary"),
            vmem_limit_bytes=VMEM_LIMIT_BYTES),
        name="prompt",
    )(x, mod, g_pre, w_in, ln_g, ln_b, w_s, b_full, bias_t, w_out, g_post)


def _sample_kernel(x_ref, mod_ref, gpre_ref, win_ref, lng_ref, lnb_ref, ws_ref, bfull_ref,
                   bias_ref, wout_ref, gpost_ref, ck_ref, cv_ref,
                   y_ref, knew_ref, vnew_ref, vanew_ref,
                   usg_s, va_s, qe_s, qo_s, k_s, v_s, sgb_s, o_s, wt_s, *, n_streams, n_new):
    b = pl.program_id(0)
    S = n_new

    @pl.when(b == 0)
    def _():
        _tril_pairs(ws_ref, wt_s, S)
        gpre = gpre_ref[...]
        hs = []
        for i in range(n_streams):
            xi = x_ref[i * S:(i + 1) * S, :]
            hs.append(_rms_rows(xi) * (gpre * (1.0 + mod_ref[i, 1:2, :])) + mod_ref[i, 0:1, :])
        h = jnp.concatenate(hs, axis=0).astype(jnp.bfloat16)

        def proj(c0):
            return jnp.dot(h, win_ref[:, c0:c0 + 512], preferred_element_type=jnp.float32)

        va = _layernorm(_gelu(proj(C_V)), lng_ref[...], lnb_ref[...])
        vanew_ref[...] = va
        va_s[...] = va.astype(jnp.bfloat16)
        usg_s[...] = _gelu(proj(C_U)) * _silu(proj(C_GA))
        _store_q(proj(C_Q), qe_s, qo_s)
        k = proj(C_K)
        v = proj(C_VV)
        knew_ref[...] = _split_heads(k)
        vnew_ref[...] = _split_heads(v)
        k_s[...] = k.astype(jnp.bfloat16)
        v_s[...] = v.astype(jnp.bfloat16)
        sgb_s[...] = _silu(proj(C_GB))

    r0 = pl.multiple_of(b * S, S)
    rows = pl.ds(r0, S)
    pair_lanes = [slice(p * LANES, (p + 1) * LANES) for p in range(N_PAIRS)]
    for p, lanes in enumerate(pair_lanes):
        mix = jnp.dot(wt_s[p], va_s[rows, lanes], preferred_element_type=jnp.float32)
        mixed = _pair_select(mix, S) + bfull_ref[0:S, lanes]
        o_s[rows, lanes] = (usg_s[rows, lanes] * mixed).astype(jnp.bfloat16)

    scores = []
    for p, lanes in enumerate(pair_lanes):
        qs = jnp.concatenate([qe_s[rows, lanes], qo_s[rows, lanes]], axis=0)
        kct = ck_ref[0, 2 * p:2 * p + 2].reshape(LANES, KV_WIN).astype(jnp.bfloat16)
        bias = jnp.concatenate([bias_ref[p, 0], bias_ref[p, 1]], axis=0)
        s1 = jnp.dot(qs, kct, preferred_element_type=jnp.float32) + bias[:, 0:KV_WIN]
        s2 = lax.dot_general(qs, k_s[rows, lanes], _CONTRACT_LANES,
                             preferred_element_type=jnp.float32) + bias[:, KV_WIN:KV_WIN + S]
        scores.append((s1, s2))
    weights = []
    for s1, s2 in scores:
        m = jnp.maximum(jnp.max(s1, axis=-1, keepdims=True), jnp.max(s2, axis=-1, keepdims=True))
        e1 = jnp.exp2(s1 - m)
        e2 = jnp.exp2(s2 - m)
        l = jnp.sum(e1, axis=-1, keepdims=True) + jnp.sum(e2, axis=-1, keepdims=True)
        weights.append((e1.astype(jnp.bfloat16), e2.astype(jnp.bfloat16), 1.0 / l))
    for p, lanes in enumerate(pair_lanes):
        e1, e2, inv_l = weights[p]
        vct = cv_ref[0, 2 * p:2 * p + 2].reshape(LANES, KV_WIN).astype(jnp.bfloat16)
        o2 = (lax.dot_general(e1, vct, _CONTRACT_LANES, preferred_element_type=jnp.float32)
              + jnp.dot(e2, v_s[rows, lanes], preferred_element_type=jnp.float32))
        yb = _pair_select(o2 * inv_l, S) * sgb_s[rows, lanes]
        o_s[rows, B_WIDTH + p * LANES:B_WIDTH + (p + 1) * LANES] = yb.astype(jnp.bfloat16)

    @pl.when(b == n_streams - 1)
    def _():
        o = jnp.dot(o_s[...], wout_ref[...], preferred_element_type=jnp.float32)
        on = _rms_rows(o) * gpost_ref[...]
        for i in range(n_streams):
            sl = slice(i * S, (i + 1) * S)
            y_ref[sl, :] = x_ref[sl, :] + mod_ref[i, 2:3, :] * on[sl, :]


def _sample_call(x, mod, g_pre, w_in, ln_g, ln_b, w_s, b_full, bias_s, w_out, g_post, ck, cv):
    n_streams, n_new, _ = x.shape
    assert n_new % BF16_ROWS == 0 and n_new <= CHUNK
    rows = n_streams * n_new
    f32, bf16 = jnp.float32, jnp.bfloat16

    def const(shape):
        nd = len(shape)
        return pl.BlockSpec(shape, lambda b: (0,) * nd, pipeline_mode=pl.Buffered(1))

    kern = functools.partial(_sample_kernel, n_streams=n_streams, n_new=n_new)
    return pl.pallas_call(
        kern,
        grid=(n_streams,),
        in_specs=[
            const((rows, D_MODEL)),
            const((n_streams, 3, D_MODEL)),
            const((1, D_MODEL)),
            const((D_MODEL, D_IN)),
            const((1, A_WIDTH)),
            const((1, A_WIDTH)),
            const((A_GROUPS, MLP_CHUNK, MLP_CHUNK)),
            const((MLP_CHUNK, A_WIDTH)),
            const((N_PAIRS, 2, n_new, K_BLOCK)),
            const((D_MODEL, D_MODEL)),
            const((1, D_MODEL)),
            pl.BlockSpec((1, B_HEADS, B_HDIM, KV_WIN), lambda b: (b, 0, 0, 0)),
            pl.BlockSpec((1, B_HEADS, B_HDIM, KV_WIN), lambda b: (b, 0, 0, 0)),
        ],
        out_specs=[
            pl.BlockSpec((rows, D_MODEL), lambda b: (0, 0)),
            pl.BlockSpec((rows, B_HEADS, B_HDIM), lambda b: (0, 0, 0)),
            pl.BlockSpec((rows, B_HEADS, B_HDIM), lambda b: (0, 0, 0)),
            pl.BlockSpec((rows, A_WIDTH), lambda b: (0, 0)),
        ],
        out_shape=[
            jax.ShapeDtypeStruct((rows, D_MODEL), f32),
            jax.ShapeDtypeStruct((rows, B_HEADS, B_HDIM), f32),
            jax.ShapeDtypeStruct((rows, B_HEADS, B_HDIM), f32),
            jax.ShapeDtypeStruct((rows, A_WIDTH), f32),
        ],
        scratch_shapes=[
            pltpu.VMEM((rows, A_WIDTH), f32),
            pltpu.VMEM((rows, A_WIDTH), bf16),
            pltpu.VMEM((rows, B_WIDTH), bf16),
            pltpu.VMEM((rows, B_WIDTH), bf16),
            pltpu.VMEM((rows, B_WIDTH), bf16),
            pltpu.VMEM((rows, B_WIDTH), bf16),
            pltpu.VMEM((rows, B_WIDTH), f32),
            pltpu.VMEM((rows, D_MODEL), bf16),
            pltpu.VMEM((N_PAIRS, 2 * n_new, n_new), bf16),
        ],
        compiler_params=pltpu.CompilerParams(
            dimension_semantics=("arbitrary",),
            vmem_limit_bytes=VMEM_LIMIT_BYTES),
        name="sample",
    )(x.reshape(rows, D_MODEL), mod, g_pre, w_in, ln_g, ln_b, w_s, b_full, bias_s, w_out, g_post,
      ck, cv)


def _bias_base(rel_bias):
    n_far = KV_WIN - REL_CLIP
    far = jnp.broadcast_to(rel_bias[:, N_REL - 1:N_REL], (B_HEADS, n_far))
    near = rel_bias[:, N_REL - 1:0:-1]
    return jnp.concatenate([far, near], axis=1)


def kernel(x_prompt, x_sample, cache_attn_k, cache_attn_v, c_prompt, c_sample, g_pre, w_ada, b_ada,
           w_in, ln_g, ln_b, w_s, b_s, rel_bias, w_out, g_post):
    depth = g_pre.shape[0]
    bsz, seq, _ = x_prompt.shape
    n_streams, n_new, _ = x_sample.shape
    win = cache_attn_k.shape[2]
    assert win == KV_WIN and bsz + n_streams <= ADA_ROWS

    yp, ys = x_prompt, x_sample
    kp_rows, vp_rows, ks_rows, vs_rows, va_rows = [], [], [], [], []
    c_all = jnp.concatenate(
        [c_prompt, c_sample, jnp.zeros((ADA_ROWS - bsz - n_streams, D_MODEL), c_prompt.dtype)], axis=0)
    for l in range(depth):
        mod, bias_t, bias_s, w_in_b, w_out_b = _prep_call(
            c_all, w_ada[l], b_ada[l], _bias_base(rel_bias[l]), w_in[l], w_out[l], n_new)
        mod = mod.reshape(ADA_ROWS, 3, D_MODEL)
        b_full = jnp.repeat(b_s[l].T, A_GDIM, axis=1)
        pre = (g_pre[l].reshape(1, D_MODEL), w_in_b, ln_g[l].reshape(1, A_WIDTH),
               ln_b[l].reshape(1, A_WIDTH), w_s[l], b_full)
        post = (w_out_b, g_post[l].reshape(1, D_MODEL))

        yp, k_last, v_last = _prompt_call(yp, mod[:bsz], *pre, bias_t, *post)
        kp_rows.append(jnp.transpose(k_last, _FROM_FEATURE_MAJOR))
        vp_rows.append(jnp.transpose(v_last, _FROM_FEATURE_MAJOR))

        ys2, k_new, v_new, va_new = _sample_call(ys, mod[bsz:bsz + n_streams], *pre, bias_s, *post,
                                                 jnp.transpose(cache_attn_k[l], _TO_FEATURE_MAJOR),
                                                 jnp.transpose(cache_attn_v[l], _TO_FEATURE_MAJOR))
        ys = ys2.reshape(n_streams, n_new, D_MODEL)
        ks_rows.append(k_new.reshape(n_streams, n_new, B_HEADS, B_HDIM))
        vs_rows.append(v_new.reshape(n_streams, n_new, B_HEADS, B_HDIM))
        va_rows.append(va_new.reshape(n_streams, n_new, A_WIDTH))

    return (yp, ys, jnp.stack(kp_rows), jnp.stack(vp_rows), jnp.stack(ks_rows), jnp.stack(vs_rows),
            jnp.stack(va_rows))
```

```python
import functools

import jax
import jax.numpy as jnp
from jax import lax
from jax.experimental import pallas as pl
from jax.experimental.pallas import tpu as pltpu

LANES = 128
SUBLANES = 8
BF16_ROWS = 16
VMEM_LIMIT_BYTES = 60 * 1024 * 1024

D_MODEL = 1024
A_WIDTH = 512
A_GROUPS = 8
A_GDIM = A_WIDTH // A_GROUPS
MLP_CHUNK = 128
B_WIDTH = 512
B_HEADS = 8
B_HDIM = B_WIDTH // B_HEADS
CHUNK = 64
KV_WIN = 512
REL_CLIP = 128
N_REL = 2 * REL_CLIP + 1
EPS = 1e-6
D_IN = 3 * A_WIDTH + 4 * B_WIDTH
NEG = -1e30
LOG2E = 1.4426950408889634
Q_SCALE = B_HDIM ** -0.5 * LOG2E

N_PAIRS = B_HEADS // 2
Q_BLOCK = 2 * CHUNK
K_BLOCK = KV_WIN + Q_BLOCK
SEQ_TILE = 1024
PROJ_ROWS = 256
HALF_COLS = 256
POST_ROWS = 256
SCORE_LEAD = 3
SCORE_SLOTS = SCORE_LEAD + 1
ADA_ROWS = 16

C_U, C_V, C_GA, C_Q, C_K, C_VV, C_GB = (i * 512 for i in range(7))

_CONTRACT_LANES = (((1,), (1,)), ((), ()))

_TO_FEATURE_MAJOR = (0, 2, 3, 1)
_FROM_FEATURE_MAJOR = (0, 3, 1, 2)


def _sigmoid_exp2(neg_arg_log2):
    return 1.0 / (1.0 + jnp.exp2(neg_arg_log2))


def _gelu(x):
    c1 = -2.0 * (2.0 / jnp.pi) ** 0.5 * LOG2E
    return x * _sigmoid_exp2(x * (c1 + (c1 * 0.044715) * (x * x)))


def _silu(x):
    return x * _sigmoid_exp2(x * (-LOG2E))


def _even_lane_mask(shape):
    lane = lax.broadcasted_iota(jnp.int32, shape, len(shape) - 1)
    return (lane % LANES) < B_HDIM


def _ada_block(c_ref, w_ref, b_ref, o_ref):
    c = c_ref[...]
    o_ref[...] = jnp.dot(_silu(c), w_ref[...], preferred_element_type=jnp.float32) + b_ref[...]


def _relb_block(base_ref, bt_ref, bs_ref, n_new):
    row = lax.broadcasted_iota(jnp.int32, (Q_BLOCK, K_BLOCK), 0)
    col = lax.broadcasted_iota(jnp.int32, (Q_BLOCK, K_BLOCK), 1)
    band_lo = jnp.where(row < CHUNK, 0, CHUNK)
    rel = col - band_lo
    outside = jnp.logical_or(rel < 0, rel >= KV_WIN + CHUNK)
    for par in range(2):
        base = base_ref[0, par:par + 1, :] * LOG2E
        x = jnp.broadcast_to(base, (Q_BLOCK, K_BLOCK))
        shift = 1
        while shift < Q_BLOCK:
            x = jnp.where((row & shift) != 0, jnp.roll(x, shift, axis=1), x)
            shift *= 2
        far = jnp.broadcast_to(base[:, 0:1], (Q_BLOCK, K_BLOCK))
        x = jnp.where(col < row, far, x)
        bs_ref[0, par] = x[0:n_new, :]
        bt_ref[0, :, par * Q_BLOCK:(par + 1) * Q_BLOCK] = jnp.where(outside, NEG, x).T


PREP_COLS = 512
N_PREP = D_IN // PREP_COLS
N_ADA = 3 * D_MODEL // PREP_COLS
WOUT_COLS = D_MODEL // N_PAIRS


def _prep_kernel(c_ref, wada_ref, bada_ref, base_ref, win_ref, wout_ref,
                 mod_ref, bt_ref, bs_ref, winb_ref, woutb_ref, *, n_new):
    j = pl.program_id(0)
    winb_ref[...] = win_ref[...].astype(jnp.bfloat16)

    @pl.when(j < N_ADA)
    def _():
        _ada_block(c_ref, wada_ref, bada_ref, mod_ref)

    @pl.when(j < N_PAIRS)
    def _():
        woutb_ref[...] = wout_ref[...].astype(jnp.bfloat16)
        _relb_block(base_ref, bt_ref, bs_ref, n_new)


def _prep_call(c_all, w_ada, b_ada, base, w_in, w_out, n_new):
    f32, bf16 = jnp.float32, jnp.bfloat16

    def upto(n):
        return lambda j: jnp.minimum(j, n - 1)

    ada_j, pair_j = upto(N_ADA), upto(N_PAIRS)
    return pl.pallas_call(
        functools.partial(_prep_kernel, n_new=n_new),
        grid=(N_PREP,),
        in_specs=[
            pl.BlockSpec((ADA_ROWS, D_MODEL), lambda j: (0, 0)),
            pl.BlockSpec((D_MODEL, PREP_COLS), lambda j: (0, ada_j(j))),
            pl.BlockSpec((1, PREP_COLS), lambda j: (0, ada_j(j))),
            pl.BlockSpec((1, 2, K_BLOCK), lambda j: (pair_j(j), 0, 0)),
            pl.BlockSpec((D_MODEL, PREP_COLS), lambda j: (0, j)),
            pl.BlockSpec((D_MODEL, WOUT_COLS), lambda j: (0, pair_j(j))),
        ],
        out_specs=[
            pl.BlockSpec((ADA_ROWS, PREP_COLS), lambda j: (0, ada_j(j))),
            pl.BlockSpec((1, K_BLOCK, 2 * Q_BLOCK), lambda j: (pair_j(j), 0, 0)),
            pl.BlockSpec((1, 2, n_new, K_BLOCK), lambda j: (pair_j(j), 0, 0, 0)),
            pl.BlockSpec((D_MODEL, PREP_COLS), lambda j: (0, j)),
            pl.BlockSpec((D_MODEL, WOUT_COLS), lambda j: (0, pair_j(j))),
        ],
        out_shape=[
            jax.ShapeDtypeStruct((ADA_ROWS, 3 * D_MODEL), f32),
            jax.ShapeDtypeStruct((N_PAIRS, K_BLOCK, 2 * Q_BLOCK), f32),
            jax.ShapeDtypeStruct((N_PAIRS, 2, n_new, K_BLOCK), f32),
            jax.ShapeDtypeStruct((D_MODEL, D_IN), bf16),
            jax.ShapeDtypeStruct((D_MODEL, D_MODEL), bf16),
        ],
        compiler_params=pltpu.CompilerParams(dimension_semantics=("arbitrary",)),
        name="prep",
    )(c_all, w_ada, b_ada.reshape(1, 3 * D_MODEL), base.reshape(N_PAIRS, 2, K_BLOCK), w_in, w_out)


def _tril_pairs(ws_ref, wt_ref, n):
    row = lax.broadcasted_iota(jnp.int32, (n, n), 0)
    col = lax.broadcasted_iota(jnp.int32, (n, n), 1)
    keep = col <= row
    for g in range(A_GROUPS):
        w = jnp.where(keep, ws_ref[g, :n, :n], 0.0)
        wt_ref[g // 2, (g % 2) * n:(g % 2 + 1) * n, :] = w.astype(jnp.bfloat16)


def _rms_rows(x):
    return x * lax.rsqrt(jnp.mean(x * x, axis=-1, keepdims=True) + EPS)


def _layernorm(x, g, b):
    mu = jnp.mean(x, axis=-1, keepdims=True)
    xc = x - mu
    var = jnp.mean(xc * xc, axis=-1, keepdims=True)
    return xc * lax.rsqrt(var + EPS) * g + b


def _pair_select(stacked, n):
    return jnp.where(_even_lane_mask((n, LANES)), stacked[:n], stacked[n:])


def _split_heads(x):
    return x.reshape(x.shape[0], B_HEADS, B_HDIM)


def _to_feature_major(x):
    return x.T.reshape(B_HEADS, B_HDIM, x.shape[0])


def _store_q(q, qe_ref, qo_ref):
    q = (q * Q_SCALE).astype(jnp.bfloat16)
    even = _even_lane_mask(q.shape)
    zero = jnp.zeros((), jnp.bfloat16)
    qe_ref[...] = jnp.where(even, q, zero)
    qo_ref[...] = jnp.where(even, zero, q)


def _prompt_kernel(x_ref, mod_ref, gpre_ref, win_ref, lng_ref, lnb_ref, ws_ref, bfull_ref,
                   biast_ref, wout_ref, gpost_ref,
                   y_ref, klast_ref, vlast_ref,
                   h_s, usg_s, va_s, qe_s, qo_s, k_s, vt_s, sgb_s, o_s, wt_s, st_s, kf_s, vf_s):
    t = pl.program_id(1)
    nt = pl.num_programs(1)
    T = SEQ_TILE

    @pl.when(jnp.logical_and(pl.program_id(0) == 0, t == 0))
    def _():
        _tril_pairs(ws_ref, wt_s, MLP_CHUNK)

    shift = mod_ref[0, 0:1, :]
    scale = mod_ref[0, 1:2, :]
    gate = mod_ref[0, 2:3, :]

    pre_scale = gpre_ref[...] * (1.0 + scale)

    def proj(c0, rows, width=512):
        return jnp.dot(h_s[rows, :], win_ref[:, c0:c0 + width], preferred_element_type=jnp.float32)

    def proj_pieces(lo):
        rb = slice(lo, lo + PROJ_ROWS)
        halves = [slice(lo, lo + PROJ_ROWS // 2), slice(lo + PROJ_ROWS // 2, lo + PROJ_ROWS)]
        hist = slice(KV_WIN + lo, KV_WIN + lo + PROJ_ROWS)
        keep = slice(lo - (T - KV_WIN), lo - (T - KV_WIN) + PROJ_ROWS) if lo >= T - KV_WIN else None

        def pre_norm():
            for rows in halves:
                h_s[rows, :] = (_rms_rows(x_ref[0, rows, :]) * pre_scale + shift).astype(jnp.bfloat16)

        def cols(ci):
            return slice(ci * HALF_COLS, (ci + 1) * HALF_COLS)

        def u_half(ci):
            usg_s[rb, cols(ci)] = _gelu(proj(C_U + ci * HALF_COLS, rb, HALF_COLS))

        def v_block(ci):
            v = proj(C_VV + ci * HALF_COLS, rb, HALF_COLS)
            vt_s[cols(ci), hist] = v.T.astype(jnp.bfloat16)
            if keep is not None:
                vf_s[keep, cols(ci)] = v

        def va_block():
            va_s[rb, :] = _layernorm(_gelu(proj(C_V, rb)), lng_ref[...], lnb_ref[...]
                                     ).astype(jnp.bfloat16)

        def q_block(ci):
            _store_q(proj(C_Q + ci * HALF_COLS, rb, HALF_COLS),
                     qe_s.at[rb, cols(ci)], qo_s.at[rb, cols(ci)])

        def ga_block(ci):
            usg_s[rb, cols(ci)] = usg_s[rb, cols(ci)] * _silu(proj(C_GA + ci * HALF_COLS, rb, HALF_COLS))

        def gmlp(p):
            lanes = slice(p * LANES, (p + 1) * LANES)
            for c0 in range(lo, lo + PROJ_ROWS, 2 * MLP_CHUNK):
                chunks = [slice(c0, c0 + MLP_CHUNK), slice(c0 + MLP_CHUNK, c0 + 2 * MLP_CHUNK)]
                slabs = jnp.concatenate([va_s[rows, lanes] for rows in chunks], axis=1)
                mix = jnp.dot(wt_s[p], slabs, preferred_element_type=jnp.float32)
                for i, rows in enumerate(chunks):
                    mixed = _pair_select(mix[:, i * LANES:(i + 1) * LANES], MLP_CHUNK)
                    o_s[rows, lanes] = (usg_s[rows, lanes] * (mixed + bfull_ref[:, lanes])
                                        ).astype(jnp.bfloat16)

        def gb_block(ci):
            sgb_s[rb, cols(ci)] = _silu(proj(C_GB + ci * HALF_COLS, rb, HALF_COLS))

        def k_block(ci):
            k = proj(C_K + ci * HALF_COLS, rb, HALF_COLS)
            k_s[hist, cols(ci)] = k.astype(jnp.bfloat16)
            if keep is not None:
                kf_s[keep, cols(ci)] = k

        P = functools.partial
        return [pre_norm, P(u_half, 0), P(u_half, 1), P(v_block, 0), va_block, P(v_block, 1),
                P(ga_block, 0), P(gmlp, 0), P(ga_block, 1), P(gmlp, 1), P(q_block, 0), P(gmlp, 2),
                P(q_block, 1), P(gmlp, 3), P(gb_block, 0), P(gb_block, 1), P(k_block, 0),
                P(k_block, 1)]

    def key_lo(qb, first_tile):
        return max(KV_WIN - qb * Q_BLOCK, 0) if first_tile else 0

    def scores(qb, p, slot, first_tile):
        r0, lo = qb * Q_BLOCK, key_lo(qb, first_tile)
        lanes = slice(p * LANES, (p + 1) * LANES)
        qs = jnp.concatenate([qe_s[r0:r0 + Q_BLOCK, lanes],
                              qo_s[r0:r0 + Q_BLOCK, lanes]], axis=0)
        half = (K_BLOCK - lo) // 2
        for a in (lo, lo + half):
            st_s[slot, a:a + half, :] = lax.dot_general(
                k_s[r0 + a:r0 + a + half, lanes], qs, _CONTRACT_LANES,
                preferred_element_type=jnp.float32) + biast_ref[p, a:a + half, :]

    def finish(qb, p, slot, first_tile):
        r0, lo = qb * Q_BLOCK, key_lo(qb, first_tile)
        st = st_s[slot, lo:K_BLOCK, :]
        m = jnp.max(st, axis=0, keepdims=True)
        pt = jnp.exp2(st - m).astype(jnp.bfloat16)
        ones = jnp.ones((BF16_ROWS, K_BLOCK - lo), jnp.bfloat16)
        lhs = jnp.concatenate([vt_s[p * LANES:(p + 1) * LANES, r0 + lo:r0 + K_BLOCK], ones], axis=0)
        ot = jnp.dot(lhs, pt, preferred_element_type=jnp.float32)
        inv_l = 1.0 / ot[LANES:LANES + 1, :]
        ots = [ot[par * B_HDIM:(par + 1) * B_HDIM, par * Q_BLOCK:(par + 1) * Q_BLOCK]
               * inv_l[:, par * Q_BLOCK:(par + 1) * Q_BLOCK] for par in range(2)]
        yb = jnp.concatenate(ots, axis=0).T * sgb_s[r0:r0 + Q_BLOCK, p * LANES:(p + 1) * LANES]
        o_s[r0:r0 + Q_BLOCK, B_WIDTH + p * LANES:B_WIDTH + (p + 1) * LANES] = (
            yb.astype(jnp.bfloat16))

    post_scale = gate * gpost_ref[...]

    def out_block(lo, n):
        rows = slice(lo, lo + n)
        o = jnp.dot(o_s[rows, :], wout_ref[...], preferred_element_type=jnp.float32)
        y_ref[0, rows, :] = x_ref[0, rows, :] + _rms_rows(o) * post_scale

    def attend(first_tile):
        items = [(qb, p) for qb in range(T // Q_BLOCK) for p in range(N_PAIRS)]
        per_block = (PROJ_ROWS // Q_BLOCK) * N_PAIRS
        n_blocks = T // PROJ_ROWS
        for piece in proj_pieces(0):
            piece()
        pending = []
        for j in range(SCORE_LEAD):
            scores(*items[j], j % SCORE_SLOTS, first_tile)
        for i, (qb, p) in enumerate(items):
            blk, pos = divmod(i, per_block)
            if pos == 0 and blk + 1 < n_blocks:
                pending = proj_pieces((blk + 1) * PROJ_ROWS)
            if i + SCORE_LEAD < len(items):
                assert pos < per_block - SCORE_LEAD or not pending
                scores(*items[i + SCORE_LEAD], (i + SCORE_LEAD) % SCORE_SLOTS, first_tile)
            finish(qb, p, i % SCORE_SLOTS, first_tile)
            slots_left = per_block - SCORE_LEAD - pos
            if pending and slots_left > 0:
                n_now = -(-len(pending) // slots_left)
                for piece in pending[:n_now]:
                    piece()
                pending = pending[n_now:]
            rows_done = (qb + 1) * Q_BLOCK
            if p == N_PAIRS - 1 and rows_done % POST_ROWS == 0:
                out_block(rows_done - POST_ROWS, POST_ROWS)

    @pl.when(t == 0)
    def _():
        attend(True)

    @pl.when(t > 0)
    def _():
        attend(False)

    k_s[0:KV_WIN, :] = k_s[T:T + KV_WIN, :]
    vt_s[:, 0:KV_WIN] = vt_s[:, T:T + KV_WIN]

    @pl.when(t == nt - 1)
    def _():
        klast_ref[0] = _to_feature_major(kf_s[...])
        vlast_ref[0] = _to_feature_major(vf_s[...])


def _const_spec(shape):
    nd = len(shape)
    return pl.BlockSpec(shape, lambda b, t: (0,) * nd, pipeline_mode=pl.Buffered(1))


def _prompt_call(x, mod, g_pre, w_in, ln_g, ln_b, w_s, b_full, bias_t, w_out, g_post):
    bsz, seq, _ = x.shape
    T = SEQ_TILE
    assert seq % T == 0 and T % PROJ_ROWS == 0 and KV_WIN % PROJ_ROWS == 0 and T % POST_ROWS == 0
    assert PROJ_ROWS % (2 * MLP_CHUNK) == 0
    nt = seq // T
    f32, bf16 = jnp.float32, jnp.bfloat16
    return pl.pallas_call(
        _prompt_kernel,
        grid=(bsz, nt),
        in_specs=[
            pl.BlockSpec((1, T, D_MODEL), lambda b, t: (b, t, 0)),
            pl.BlockSpec((1, 3, D_MODEL), lambda b, t: (b, 0, 0)),
            _const_spec((1, D_MODEL)),
            _const_spec((D_MODEL, D_IN)),
            _const_spec((1, A_WIDTH)),
            _const_spec((1, A_WIDTH)),
            _const_spec((A_GROUPS, MLP_CHUNK, MLP_CHUNK)),
            _const_spec((MLP_CHUNK, A_WIDTH)),
            _const_spec((N_PAIRS, K_BLOCK, 2 * Q_BLOCK)),
            _const_spec((D_MODEL, D_MODEL)),
            _const_spec((1, D_MODEL)),
        ],
        out_specs=[
            pl.BlockSpec((1, T, D_MODEL), lambda b, t: (b, t, 0)),
            pl.BlockSpec((1, B_HEADS, B_HDIM, KV_WIN), lambda b, t: (b, 0, 0, 0)),
            pl.BlockSpec((1, B_HEADS, B_HDIM, KV_WIN), lambda b, t: (b, 0, 0, 0)),
        ],
        out_shape=[
            jax.ShapeDtypeStruct((bsz, seq, D_MODEL), f32),
            jax.ShapeDtypeStruct((bsz, B_HEADS, B_HDIM, KV_WIN), f32),
            jax.ShapeDtypeStruct((bsz, B_HEADS, B_HDIM, KV_WIN), f32),
        ],
        scratch_shapes=[
            pltpu.VMEM((T, D_MODEL), bf16),
            pltpu.VMEM((T, A_WIDTH), f32),
            pltpu.VMEM((T, A_WIDTH), bf16),
            pltpu.VMEM((T, B_WIDTH), bf16),
            pltpu.VMEM((T, B_WIDTH), bf16),
            pltpu.VMEM((KV_WIN + T, B_WIDTH), bf16),
            pltpu.VMEM((B_WIDTH, KV_WIN + T), bf16),
            pltpu.VMEM((T, B_WIDTH), f32),
            pltpu.VMEM((T, D_MODEL), bf16),
            pltpu.VMEM((N_PAIRS, 2 * MLP_CHUNK, MLP_CHUNK), bf16),
            pltpu.VMEM((SCORE_SLOTS, K_BLOCK, 2 * Q_BLOCK), f32),
            pltpu.VMEM((KV_WIN, B_WIDTH), f32),
            pltpu.VMEM((KV_WIN, B_WIDTH), f32),
        ],
        compiler_params=pltpu.CompilerParams(
            dimension_semantics=("arbitrary", "arbitrary"),
            vmem_limit_bytes=VMEM_LIMIT_BYTES),
        name="prompt",
    )(x, mod, g_pre, w_in, ln_g, ln_b, w_s, b_full, bias_t, w_out, g_post)


def _sample_kernel(x_ref, mod_ref, gpre_ref, win_ref, lng_ref, lnb_ref, ws_ref, bfull_ref,
                   bias_ref, wout_ref, gpost_ref, ck_ref, cv_ref,
                   y_ref, knew_ref, vnew_ref, vanew_ref,
                   usg_s, va_s, qe_s, qo_s, k_s, v_s, sgb_s, o_s, wt_s, *, n_streams, n_new):
    b = pl.program_id(0)
    S = n_new

    @pl.when(b == 0)
    def _():
        _tril_pairs(ws_ref, wt_s, S)
        gpre = gpre_ref[...]
        hs = []
        for i in range(n_streams):
            xi = x_ref[i * S:(i + 1) * S, :]
            hs.append(_rms_rows(xi) * (gpre * (1.0 + mod_ref[i, 1:2, :])) + mod_ref[i, 0:1, :])
        h = jnp.concatenate(hs, axis=0).astype(jnp.bfloat16)

        def proj(c0):
            return jnp.dot(h, win_ref[:, c0:c0 + 512], preferred_element_type=jnp.float32)

        va = _layernorm(_gelu(proj(C_V)), lng_ref[...], lnb_ref[...])
        vanew_ref[...] = va
        va_s[...] = va.astype(jnp.bfloat16)
        usg_s[...] = _gelu(proj(C_U)) * _silu(proj(C_GA))
        _store_q(proj(C_Q), qe_s, qo_s)
        k = proj(C_K)
        v = proj(C_VV)
        knew_ref[...] = _split_heads(k)
        vnew_ref[...] = _split_heads(v)
        k_s[...] = k.astype(jnp.bfloat16)
        v_s[...] = v.astype(jnp.bfloat16)
        sgb_s[...] = _silu(proj(C_GB))

    r0 = pl.multiple_of(b * S, S)
    rows = pl.ds(r0, S)
    pair_lanes = [slice(p * LANES, (p + 1) * LANES) for p in range(N_PAIRS)]
    for p, lanes in enumerate(pair_lanes):
        mix = jnp.dot(wt_s[p], va_s[rows, lanes], preferred_element_type=jnp.float32)
        mixed = _pair_select(mix, S) + bfull_ref[0:S, lanes]
        o_s[rows, lanes] = (usg_s[rows, lanes] * mixed).astype(jnp.bfloat16)

    scores = []
    for p, lanes in enumerate(pair_lanes):
        qs = jnp.concatenate([qe_s[rows, lanes], qo_s[rows, lanes]], axis=0)
        kct = ck_ref[0, 2 * p:2 * p + 2].reshape(LANES, KV_WIN).astype(jnp.bfloat16)
        bias = jnp.concatenate([bias_ref[p, 0], bias_ref[p, 1]], axis=0)
        s1 = jnp.dot(qs, kct, preferred_element_type=jnp.float32) + bias[:, 0:KV_WIN]
        s2 = lax.dot_general(qs, k_s[rows, lanes], _CONTRACT_LANES,
                             preferred_element_type=jnp.float32) + bias[:, KV_WIN:KV_WIN + S]
        scores.append((s1, s2))
    weights = []
    for s1, s2 in scores:
        m = jnp.maximum(jnp.max(s1, axis=-1, keepdims=True), jnp.max(s2, axis=-1, keepdims=True))
        e1 = jnp.exp2(s1 - m)
        e2 = jnp.exp2(s2 - m)
        l = jnp.sum(e1, axis=-1, keepdims=True) + jnp.sum(e2, axis=-1, keepdims=True)
        weights.append((e1.astype(jnp.bfloat16), e2.astype(jnp.bfloat16), 1.0 / l))
    for p, lanes in enumerate(pair_lanes):
        e1, e2, inv_l = weights[p]
        vct = cv_ref[0, 2 * p:2 * p + 2].reshape(LANES, KV_WIN).astype(jnp.bfloat16)
        o2 = (lax.dot_general(e1, vct, _CONTRACT_LANES, preferred_element_type=jnp.float32)
              + jnp.dot(e2, v_s[rows, lanes], preferred_element_type=jnp.float32))
        yb = _pair_select(o2 * inv_l, S) * sgb_s[rows, lanes]
        o_s[rows, B_WIDTH + p * LANES:B_WIDTH + (p + 1) * LANES] = yb.astype(jnp.bfloat16)

    @pl.when(b == n_streams - 1)
    def _():
        o = jnp.dot(o_s[...], wout_ref[...], preferred_element_type=jnp.float32)
        on = _rms_rows(o) * gpost_ref[...]
        for i in range(n_streams):
            sl = slice(i * S, (i + 1) * S)
            y_ref[sl, :] = x_ref[sl, :] + mod_ref[i, 2:3, :] * on[sl, :]


def _sample_call(x, mod, g_pre, w_in, ln_g, ln_b, w_s, b_full, bias_s, w_out, g_post, ck, cv):
    n_streams, n_new, _ = x.shape
    assert n_new % BF16_ROWS == 0 and n_new <= CHUNK
    rows = n_streams * n_new
    f32, bf16 = jnp.float32, jnp.bfloat16

    def const(shape):
        nd = len(shape)
        return pl.BlockSpec(shape, lambda b: (0,) * nd, pipeline_mode=pl.Buffered(1))

    kern = functools.partial(_sample_kernel, n_streams=n_streams, n_new=n_new)
    return pl.pallas_call(
        kern,
        grid=(n_streams,),
        in_specs=[
            const((rows, D_MODEL)),
            const((n_streams, 3, D_MODEL)),
            const((1, D_MODEL)),
            const((D_MODEL, D_IN)),
            const((1, A_WIDTH)),
            const((1, A_WIDTH)),
            const((A_GROUPS, MLP_CHUNK, MLP_CHUNK)),
            const((MLP_CHUNK, A_WIDTH)),
            const((N_PAIRS, 2, n_new, K_BLOCK)),
            const((D_MODEL, D_MODEL)),
            const((1, D_MODEL)),
            pl.BlockSpec((1, B_HEADS, B_HDIM, KV_WIN), lambda b: (b, 0, 0, 0)),
            pl.BlockSpec((1, B_HEADS, B_HDIM, KV_WIN), lambda b: (b, 0, 0, 0)),
        ],
        out_specs=[
            pl.BlockSpec((rows, D_MODEL), lambda b: (0, 0)),
            pl.BlockSpec((rows, B_HEADS, B_HDIM), lambda b: (0, 0, 0)),
            pl.BlockSpec((rows, B_HEADS, B_HDIM), lambda b: (0, 0, 0)),
            pl.BlockSpec((rows, A_WIDTH), lambda b: (0, 0)),
        ],
        out_shape=[
            jax.ShapeDtypeStruct((rows, D_MODEL), f32),
            jax.ShapeDtypeStruct((rows, B_HEADS, B_HDIM), f32),
            jax.ShapeDtypeStruct((rows, B_HEADS, B_HDIM), f32),
            jax.ShapeDtypeStruct((rows, A_WIDTH), f32),
        ],
        scratch_shapes=[
            pltpu.VMEM((rows, A_WIDTH), f32),
            pltpu.VMEM((rows, A_WIDTH), bf16),
            pltpu.VMEM((rows, B_WIDTH), bf16),
            pltpu.VMEM((rows, B_WIDTH), bf16),
            pltpu.VMEM((rows, B_WIDTH), bf16),
            pltpu.VMEM((rows, B_WIDTH), bf16),
            pltpu.VMEM((rows, B_WIDTH), f32),
            pltpu.VMEM((rows, D_MODEL), bf16),
            pltpu.VMEM((N_PAIRS, 2 * n_new, n_new), bf16),
        ],
        compiler_params=pltpu.CompilerParams(
            dimension_semantics=("arbitrary",),
            vmem_limit_bytes=VMEM_LIMIT_BYTES),
        name="sample",
    )(x.reshape(rows, D_MODEL), mod, g_pre, w_in, ln_g, ln_b, w_s, b_full, bias_s, w_out, g_post,
      ck, cv)


def _bias_base(rel_bias):
    n_far = KV_WIN - REL_CLIP
    far = jnp.broadcast_to(rel_bias[:, N_REL - 1:N_REL], (B_HEADS, n_far))
    near = rel_bias[:, N_REL - 1:0:-1]
    return jnp.concatenate([far, near], axis=1)


def kernel(x_prompt, x_sample, cache_attn_k, cache_attn_v, c_prompt, c_sample, g_pre, w_ada, b_ada,
           w_in, ln_g, ln_b, w_s, b_s, rel_bias, w_out, g_post):
    depth = g_pre.shape[0]
    bsz, seq, _ = x_prompt.shape
    n_streams, n_new, _ = x_sample.shape
    win = cache_attn_k.shape[2]
    assert win == KV_WIN and bsz + n_streams <= ADA_ROWS

    yp, ys = x_prompt, x_sample
    kp_rows, vp_rows, ks_rows, vs_rows, va_rows = [], [], [], [], []
    c_all = jnp.concatenate(
        [c_prompt, c_sample, jnp.zeros((ADA_ROWS - bsz - n_streams, D_MODEL), c_prompt.dtype)], axis=0)
    for l in range(depth):
        mod, bias_t, bias_s, w_in_b, w_out_b = _prep_call(
            c_all, w_ada[l], b_ada[l], _bias_base(rel_bias[l]), w_in[l], w_out[l], n_new)
        mod = mod.reshape(ADA_ROWS, 3, D_MODEL)
        b_full = jnp.repeat(b_s[l].T, A_GDIM, axis=1)
        pre = (g_pre[l].reshape(1, D_MODEL), w_in_b, ln_g[l].reshape(1, A_WIDTH),
               ln_b[l].reshape(1, A_WIDTH), w_s[l], b_full)
        post = (w_out_b, g_post[l].reshape(1, D_MODEL))

        yp, k_last, v_last = _prompt_call(yp, mod[:bsz], *pre, bias_t, *post)
        kp_rows.append(jnp.transpose(k_last, _FROM_FEATURE_MAJOR))
        vp_rows.append(jnp.transpose(v_last, _FROM_FEATURE_MAJOR))

        ys2, k_new, v_new, va_new = _sample_call(ys, mod[bsz:bsz + n_streams], *pre, bias_s, *post,
                                                 jnp.transpose(cache_attn_k[l], _TO_FEATURE_MAJOR),
                                                 jnp.transpose(cache_attn_v[l], _TO_FEATURE_MAJOR))
        ys = ys2.reshape(n_streams, n_new, D_MODEL)
        ks_rows.append(k_new.reshape(n_streams, n_new, B_HEADS, B_HDIM))
        vs_rows.append(v_new.reshape(n_streams, n_new, B_HEADS, B_HDIM))
        va_rows.append(va_new.reshape(n_streams, n_new, A_WIDTH))

    return (yp, ys, jnp.stack(kp_rows), jnp.stack(vp_rows), jnp.stack(ks_rows), jnp.stack(vs_rows),
            jnp.stack(va_rows))
```

```python
import functools

import jax
import jax.numpy as jnp
from jax import lax
from jax.experimental import pallas as pl
from jax.experimental.pallas import tpu as pltpu

LANES = 128
SUBLANES = 8
BF16_ROWS = 16
VMEM_LIMIT_BYTES = 60 * 1024 * 1024

D_MODEL = 1024
A_WIDTH = 512
A_GROUPS = 8
A_GDIM = A_WIDTH // A_GROUPS
MLP_CHUNK = 128
B_WIDTH = 512
B_HEADS = 8
B_HDIM = B_WIDTH // B_HEADS
CHUNK = 64
KV_WIN = 512
REL_CLIP = 128
N_REL = 2 * REL_CLIP + 1
EPS = 1e-6
D_IN = 3 * A_WIDTH + 4 * B_WIDTH
NEG = -1e30
LOG2E = 1.4426950408889634
Q_SCALE = B_HDIM ** -0.5 * LOG2E

N_PAIRS = B_HEADS // 2
Q_BLOCK = 2 * CHUNK
K_BLOCK = KV_WIN + Q_BLOCK
SEQ_TILE = 1024
PROJ_ROWS = 256
POST_ROWS = 512
SCORE_LEAD = 3
SCORE_SLOTS = SCORE_LEAD + 1
ADA_ROWS = 16

C_U, C_V, C_GA, C_Q, C_K, C_VV, C_GB = (i * 512 for i in range(7))

_CONTRACT_LANES = (((1,), (1,)), ((), ()))

_TO_FEATURE_MAJOR = (0, 2, 3, 1)
_FROM_FEATURE_MAJOR = (0, 3, 1, 2)


def _sigmoid_exp2(neg_arg_log2):
    return 1.0 / (1.0 + jnp.exp2(neg_arg_log2))


def _gelu(x):
    c1 = -2.0 * (2.0 / jnp.pi) ** 0.5 * LOG2E
    return x * _sigmoid_exp2(x * (c1 + (c1 * 0.044715) * (x * x)))


def _silu(x):
    return x * _sigmoid_exp2(x * (-LOG2E))


def _even_lane_mask(shape):
    lane = lax.broadcasted_iota(jnp.int32, shape, len(shape) - 1)
    return (lane % LANES) < B_HDIM


def _ada_block(c_ref, w_ref, b_ref, o_ref):
    c = c_ref[...]
    o_ref[...] = jnp.dot(_silu(c), w_ref[...], preferred_element_type=jnp.float32) + b_ref[...]


def _relb_block(base_ref, bt_ref, bs_ref, n_new):
    row = lax.broadcasted_iota(jnp.int32, (Q_BLOCK, K_BLOCK), 0)
    col = lax.broadcasted_iota(jnp.int32, (Q_BLOCK, K_BLOCK), 1)
    band_lo = jnp.where(row < CHUNK, 0, CHUNK)
    rel = col - band_lo
    outside = jnp.logical_or(rel < 0, rel >= KV_WIN + CHUNK)
    for par in range(2):
        base = base_ref[0, par:par + 1, :] * LOG2E
        x = jnp.broadcast_to(base, (Q_BLOCK, K_BLOCK))
        shift = 1
        while shift < Q_BLOCK:
            x = jnp.where((row & shift) != 0, jnp.roll(x, shift, axis=1), x)
            shift *= 2
        far = jnp.broadcast_to(base[:, 0:1], (Q_BLOCK, K_BLOCK))
        x = jnp.where(col < row, far, x)
        bs_ref[0, par] = x[0:n_new, :]
        bt_ref[0, :, par * Q_BLOCK:(par + 1) * Q_BLOCK] = jnp.where(outside, NEG, x).T


PREP_COLS = 512
N_PREP = D_IN // PREP_COLS
N_ADA = 3 * D_MODEL // PREP_COLS
WOUT_COLS = D_MODEL // N_PAIRS


def _prep_kernel(c_ref, wada_ref, bada_ref, base_ref, win_ref, wout_ref,
                 mod_ref, bt_ref, bs_ref, winb_ref, woutb_ref, *, n_new):
    j = pl.program_id(0)
    winb_ref[...] = win_ref[...].astype(jnp.bfloat16)

    @pl.when(j < N_ADA)
    def _():
        _ada_block(c_ref, wada_ref, bada_ref, mod_ref)

    @pl.when(j < N_PAIRS)
    def _():
        woutb_ref[...] = wout_ref[...].astype(jnp.bfloat16)
        _relb_block(base_ref, bt_ref, bs_ref, n_new)


def _prep_call(c_all, w_ada, b_ada, base, w_in, w_out, n_new):
    f32, bf16 = jnp.float32, jnp.bfloat16

    def upto(n):
        return lambda j: jnp.minimum(j, n - 1)

    ada_j, pair_j = upto(N_ADA), upto(N_PAIRS)
    return pl.pallas_call(
        functools.partial(_prep_kernel, n_new=n_new),
        grid=(N_PREP,),
        in_specs=[
            pl.BlockSpec((ADA_ROWS, D_MODEL), lambda j: (0, 0)),
            pl.BlockSpec((D_MODEL, PREP_COLS), lambda j: (0, ada_j(j))),
            pl.BlockSpec((1, PREP_COLS), lambda j: (0, ada_j(j))),
            pl.BlockSpec((1, 2, K_BLOCK), lambda j: (pair_j(j), 0, 0)),
            pl.BlockSpec((D_MODEL, PREP_COLS), lambda j: (0, j)),
            pl.BlockSpec((D_MODEL, WOUT_COLS), lambda j: (0, pair_j(j))),
        ],
        out_specs=[
            pl.BlockSpec((ADA_ROWS, PREP_COLS), lambda j: (0, ada_j(j))),
            pl.BlockSpec((1, K_BLOCK, 2 * Q_BLOCK), lambda j: (pair_j(j), 0, 0)),
            pl.BlockSpec((1, 2, n_new, K_BLOCK), lambda j: (pair_j(j), 0, 0, 0)),
            pl.BlockSpec((D_MODEL, PREP_COLS), lambda j: (0, j)),
            pl.BlockSpec((D_MODEL, WOUT_COLS), lambda j: (0, pair_j(j))),
        ],
        out_shape=[
            jax.ShapeDtypeStruct((ADA_ROWS, 3 * D_MODEL), f32),
            jax.ShapeDtypeStruct((N_PAIRS, K_BLOCK, 2 * Q_BLOCK), f32),
            jax.ShapeDtypeStruct((N_PAIRS, 2, n_new, K_BLOCK), f32),
            jax.ShapeDtypeStruct((D_MODEL, D_IN), bf16),
            jax.ShapeDtypeStruct((D_MODEL, D_MODEL), bf16),
        ],
        compiler_params=pltpu.CompilerParams(dimension_semantics=("arbitrary",)),
        name="prep",
    )(c_all, w_ada, b_ada.reshape(1, 3 * D_MODEL), base.reshape(N_PAIRS, 2, K_BLOCK), w_in, w_out)


def _tril_pairs(ws_ref, wt_ref, n):
    row = lax.broadcasted_iota(jnp.int32, (n, n), 0)
    col = lax.broadcasted_iota(jnp.int32, (n, n), 1)
    keep = col <= row
    for g in range(A_GROUPS):
        w = jnp.where(keep, ws_ref[g, :n, :n], 0.0)
        wt_ref[g // 2, (g % 2) * n:(g % 2 + 1) * n, :] = w.astype(jnp.bfloat16)


def _rms_rows(x):
    return x * lax.rsqrt(jnp.mean(x * x, axis=-1, keepdims=True) + EPS)


def _layernorm(x, g, b):
    mu = jnp.mean(x, axis=-1, keepdims=True)
    xc = x - mu
    var = jnp.mean(xc * xc, axis=-1, keepdims=True)
    return xc * lax.rsqrt(var + EPS) * g + b


def _pair_select(stacked, n):
    return jnp.where(_even_lane_mask((n, LANES)), stacked[:n], stacked[n:])


def _split_heads(x):
    return x.reshape(x.shape[0], B_HEADS, B_HDIM)


def _to_feature_major(x):
    return x.T.reshape(B_HEADS, B_HDIM, x.shape[0])


def _store_q(q, qe_ref, qo_ref):
    q = (q * Q_SCALE).astype(jnp.bfloat16)
    even = _even_lane_mask(q.shape)
    zero = jnp.zeros((), jnp.bfloat16)
    qe_ref[...] = jnp.where(even, q, zero)
    qo_ref[...] = jnp.where(even, zero, q)


def _prompt_kernel(x_ref, mod_ref, gpre_ref, win_ref, lng_ref, lnb_ref, ws_ref, bfull_ref,
                   biast_ref, wout_ref, gpost_ref,
                   y_ref, klast_ref, vlast_ref,
                   h_s, usg_s, va_s, qe_s, qo_s, k_s, vt_s, sgb_s, o_s, wt_s, st_s, kf_s, vf_s):
    t = pl.program_id(1)
    nt = pl.num_programs(1)
    T = SEQ_TILE

    @pl.when(jnp.logical_and(pl.program_id(0) == 0, t == 0))
    def _():
        _tril_pairs(ws_ref, wt_s, MLP_CHUNK)

    shift = mod_ref[0, 0:1, :]
    scale = mod_ref[0, 1:2, :]
    gate = mod_ref[0, 2:3, :]

    pre_scale = gpre_ref[...] * (1.0 + scale)

    def proj(c0, rows):
        return jnp.dot(h_s[rows, :], win_ref[:, c0:c0 + 512], preferred_element_type=jnp.float32)

    def proj_pieces(lo):
        rb = slice(lo, lo + PROJ_ROWS)
        halves = [slice(lo, lo + PROJ_ROWS // 2), slice(lo + PROJ_ROWS // 2, lo + PROJ_ROWS)]
        hist = slice(KV_WIN + lo, KV_WIN + lo + PROJ_ROWS)
        keep = slice(lo - (T - KV_WIN), lo - (T - KV_WIN) + PROJ_ROWS) if lo >= T - KV_WIN else None

        def pre_norm():
            for rows in halves:
                h_s[rows, :] = (_rms_rows(x_ref[0, rows, :]) * pre_scale + shift).astype(jnp.bfloat16)

        def u_half(rows):
            usg_s[rows, :] = _gelu(proj(C_U, rows))

        def v_block():
            v = proj(C_VV, rb)
            vt_s[:, hist] = v.T.astype(jnp.bfloat16)
            if keep is not None:
                vf_s[keep, :] = v

        def va_block():
            va_s[rb, :] = _layernorm(_gelu(proj(C_V, rb)), lng_ref[...], lnb_ref[...]
                                     ).astype(jnp.bfloat16)

        def q_block():
            _store_q(proj(C_Q, rb), qe_s.at[rb, :], qo_s.at[rb, :])

        def ga_block():
            usg_s[rb, :] = usg_s[rb, :] * _silu(proj(C_GA, rb))

        def gmlp(p):
            lanes = slice(p * LANES, (p + 1) * LANES)
            for c0 in range(lo, lo + PROJ_ROWS, 2 * MLP_CHUNK):
                chunks = [slice(c0, c0 + MLP_CHUNK), slice(c0 + MLP_CHUNK, c0 + 2 * MLP_CHUNK)]
                slabs = jnp.concatenate([va_s[rows, lanes] for rows in chunks], axis=1)
                mix = jnp.dot(wt_s[p], slabs, preferred_element_type=jnp.float32)
                for i, rows in enumerate(chunks):
                    mixed = _pair_select(mix[:, i * LANES:(i + 1) * LANES], MLP_CHUNK)
                    o_s[rows, lanes] = (usg_s[rows, lanes] * (mixed + bfull_ref[:, lanes])
                                        ).astype(jnp.bfloat16)

        def gb_block():
            sgb_s[rb, :] = _silu(proj(C_GB, rb))

        def k_block():
            k = proj(C_K, rb)
            k_s[hist, :] = k.astype(jnp.bfloat16)
            if keep is not None:
                kf_s[keep, :] = k

        g = [functools.partial(gmlp, p) for p in range(N_PAIRS)]
        return [pre_norm, functools.partial(u_half, halves[0]), functools.partial(u_half, halves[1]),
                v_block, va_block, ga_block, g[0], q_block, g[1], gb_block, g[2], k_block, g[3]]

    def key_lo(qb, first_tile):
        return max(KV_WIN - qb * Q_BLOCK, 0) if first_tile else 0

    def scores(qb, p, slot, first_tile):
        r0, lo = qb * Q_BLOCK, key_lo(qb, first_tile)
        lanes = slice(p * LANES, (p + 1) * LANES)
        qs = jnp.concatenate([qe_s[r0:r0 + Q_BLOCK, lanes],
                              qo_s[r0:r0 + Q_BLOCK, lanes]], axis=0)
        half = (K_BLOCK - lo) // 2
        for a in (lo, lo + half):
            st_s[slot, a:a + half, :] = lax.dot_general(
                k_s[r0 + a:r0 + a + half, lanes], qs, _CONTRACT_LANES,
                preferred_element_type=jnp.float32) + biast_ref[p, a:a + half, :]

    def finish(qb, p, slot, first_tile):
        r0, lo = qb * Q_BLOCK, key_lo(qb, first_tile)
        st = st_s[slot, lo:K_BLOCK, :]
        m = jnp.max(st, axis=0, keepdims=True)
        pt = jnp.exp2(st - m).astype(jnp.bfloat16)
        ones = jnp.ones((BF16_ROWS, K_BLOCK - lo), jnp.bfloat16)
        lhs = jnp.concatenate([vt_s[p * LANES:(p + 1) * LANES, r0 + lo:r0 + K_BLOCK], ones], axis=0)
        ot = jnp.dot(lhs, pt, preferred_element_type=jnp.float32)
        inv_l = 1.0 / ot[LANES:LANES + 1, :]
        ots = [ot[par * B_HDIM:(par + 1) * B_HDIM, par * Q_BLOCK:(par + 1) * Q_BLOCK]
               * inv_l[:, par * Q_BLOCK:(par + 1) * Q_BLOCK] for par in range(2)]
        yb = jnp.concatenate(ots, axis=0).T * sgb_s[r0:r0 + Q_BLOCK, p * LANES:(p + 1) * LANES]
        o_s[r0:r0 + Q_BLOCK, B_WIDTH + p * LANES:B_WIDTH + (p + 1) * LANES] = (
            yb.astype(jnp.bfloat16))

    post_scale = gate * gpost_ref[...]

    def out_block(lo, n):
        rows = slice(lo, lo + n)
        o = jnp.dot(o_s[rows, :], wout_ref[...], preferred_element_type=jnp.float32)
        y_ref[0, rows, :] = x_ref[0, rows, :] + _rms_rows(o) * post_scale

    def attend(first_tile):
        items = [(qb, p) for qb in range(T // Q_BLOCK) for p in range(N_PAIRS)]
        per_block = (PROJ_ROWS // Q_BLOCK) * N_PAIRS
        n_blocks = T // PROJ_ROWS
        for piece in proj_pieces(0):
            piece()
        pending = []
        for j in range(SCORE_LEAD):
            scores(*items[j], j % SCORE_SLOTS, first_tile)
        for i, (qb, p) in enumerate(items):
            blk, pos = divmod(i, per_block)
            if pos == 0 and blk + 1 < n_blocks:
                pending = proj_pieces((blk + 1) * PROJ_ROWS)
            if i + SCORE_LEAD < len(items):
                assert pos < per_block - SCORE_LEAD or not pending
                scores(*items[i + SCORE_LEAD], (i + SCORE_LEAD) % SCORE_SLOTS, first_tile)
            finish(qb, p, i % SCORE_SLOTS, first_tile)
            slots_left = per_block - SCORE_LEAD - pos
            if pending and slots_left > 0:
                n_now = -(-len(pending) // slots_left)
                for piece in pending[:n_now]:
                    piece()
                pending = pending[n_now:]
            rows_done = (qb + 1) * Q_BLOCK
            if p == N_PAIRS - 1 and rows_done % POST_ROWS == 0:
                out_block(rows_done - POST_ROWS, POST_ROWS)

    @pl.when(t == 0)
    def _():
        attend(True)

    @pl.when(t > 0)
    def _():
        attend(False)

    k_s[0:KV_WIN, :] = k_s[T:T + KV_WIN, :]
    vt_s[:, 0:KV_WIN] = vt_s[:, T:T + KV_WIN]

    @pl.when(t == nt - 1)
    def _():
        klast_ref[0] = _to_feature_major(kf_s[...])
        vlast_ref[0] = _to_feature_major(vf_s[...])


def _const_spec(shape):
    nd = len(shape)
    return pl.BlockSpec(shape, lambda b, t: (0,) * nd, pipeline_mode=pl.Buffered(1))


def _prompt_call(x, mod, g_pre, w_in, ln_g, ln_b, w_s, b_full, bias_t, w_out, g_post):
    bsz, seq, _ = x.shape
    T = SEQ_TILE
    assert seq % T == 0 and T % PROJ_ROWS == 0 and KV_WIN % PROJ_ROWS == 0 and T % POST_ROWS == 0
    assert PROJ_ROWS % (2 * MLP_CHUNK) == 0
    nt = seq // T
    f32, bf16 = jnp.float32, jnp.bfloat16
    return pl.pallas_call(
        _prompt_kernel,
        grid=(bsz, nt),
        in_specs=[
            pl.BlockSpec((1, T, D_MODEL), lambda b, t: (b, t, 0)),
            pl.BlockSpec((1, 3, D_MODEL), lambda b, t: (b, 0, 0)),
            _const_spec((1, D_MODEL)),
            _const_spec((D_MODEL, D_IN)),
            _const_spec((1, A_WIDTH)),
            _const_spec((1, A_WIDTH)),
            _const_spec((A_GROUPS, MLP_CHUNK, MLP_CHUNK)),
            _const_spec((MLP_CHUNK, A_WIDTH)),
            _const_spec((N_PAIRS, K_BLOCK, 2 * Q_BLOCK)),
            _const_spec((D_MODEL, D_MODEL)),
            _const_spec((1, D_MODEL)),
        ],
        out_specs=[
            pl.BlockSpec((1, T, D_MODEL), lambda b, t: (b, t, 0)),
            pl.BlockSpec((1, B_HEADS, B_HDIM, KV_WIN), lambda b, t: (b, 0, 0, 0)),
            pl.BlockSpec((1, B_HEADS, B_HDIM, KV_WIN), lambda b, t: (b, 0, 0, 0)),
        ],
        out_shape=[
            jax.ShapeDtypeStruct((bsz, seq, D_MODEL), f32),
            jax.ShapeDtypeStruct((bsz, B_HEADS, B_HDIM, KV_WIN), f32),
            jax.ShapeDtypeStruct((bsz, B_HEADS, B_HDIM, KV_WIN), f32),
        ],
        scratch_shapes=[
            pltpu.VMEM((T, D_MODEL), bf16),
            pltpu.VMEM((T, A_WIDTH), f32),
            pltpu.VMEM((T, A_WIDTH), bf16),
            pltpu.VMEM((T, B_WIDTH), bf16),
            pltpu.VMEM((T, B_WIDTH), bf16),
            pltpu.VMEM((KV_WIN + T, B_WIDTH), bf16),
            pltpu.VMEM((B_WIDTH, KV_WIN + T), bf16),
            pltpu.VMEM((T, B_WIDTH), f32),
            pltpu.VMEM((T, D_MODEL), bf16),
            pltpu.VMEM((N_PAIRS, 2 * MLP_CHUNK, MLP_CHUNK), bf16),
            pltpu.VMEM((SCORE_SLOTS, K_BLOCK, 2 * Q_BLOCK), f32),
            pltpu.VMEM((KV_WIN, B_WIDTH), f32),
            pltpu.VMEM((KV_WIN, B_WIDTH), f32),
        ],
        compiler_params=pltpu.CompilerParams(
            dimension_semantics=("arbitrary", "arbitrary"),
            vmem_limit_bytes=VMEM_LIMIT_BYTES),
        name="prompt",
    )(x, mod, g_pre, w_in, ln_g, ln_b, w_s, b_full, bias_t, w_out, g_post)


def _sample_kernel(x_ref, mod_ref, gpre_ref, win_ref, lng_ref, lnb_ref, ws_ref, bfull_ref,
                   bias_ref, wout_ref, gpost_ref, ck_ref, cv_ref,
                   y_ref, knew_ref, vnew_ref, vanew_ref,
                   usg_s, va_s, qe_s, qo_s, k_s, v_s, sgb_s, o_s, wt_s, *, n_streams, n_new):
    b = pl.program_id(0)
    S = n_new

    @pl.when(b == 0)
    def _():
        _tril_pairs(ws_ref, wt_s, S)
        gpre = gpre_ref[...]
        hs = []
        for i in range(n_streams):
            xi = x_ref[i * S:(i + 1) * S, :]
            hs.append(_rms_rows(xi) * (gpre * (1.0 + mod_ref[i, 1:2, :])) + mod_ref[i, 0:1, :])
        h = jnp.concatenate(hs, axis=0).astype(jnp.bfloat16)

        def proj(c0):
            return jnp.dot(h, win_ref[:, c0:c0 + 512], preferred_element_type=jnp.float32)

        va = _layernorm(_gelu(proj(C_V)), lng_ref[...], lnb_ref[...])
        vanew_ref[...] = va
        va_s[...] = va.astype(jnp.bfloat16)
        usg_s[...] = _gelu(proj(C_U)) * _silu(proj(C_GA))
        _store_q(proj(C_Q), qe_s, qo_s)
        k = proj(C_K)
        v = proj(C_VV)
        knew_ref[...] = _split_heads(k)
        vnew_ref[...] = _split_heads(v)
        k_s[...] = k.astype(jnp.bfloat16)
        v_s[...] = v.astype(jnp.bfloat16)
        sgb_s[...] = _silu(proj(C_GB))

    r0 = pl.multiple_of(b * S, S)
    rows = pl.ds(r0, S)
    pair_lanes = [slice(p * LANES, (p + 1) * LANES) for p in range(N_PAIRS)]
    for p, lanes in enumerate(pair_lanes):
        mix = jnp.dot(wt_s[p], va_s[rows, lanes], preferred_element_type=jnp.float32)
        mixed = _pair_select(mix, S) + bfull_ref[0:S, lanes]
        o_s[rows, lanes] = (usg_s[rows, lanes] * mixed).astype(jnp.bfloat16)

    scores = []
    for p, lanes in enumerate(pair_lanes):
        qs = jnp.concatenate([qe_s[rows, lanes], qo_s[rows, lanes]], axis=0)
        kct = ck_ref[0, 2 * p:2 * p + 2].reshape(LANES, KV_WIN).astype(jnp.bfloat16)
        bias = jnp.concatenate([bias_ref[p, 0], bias_ref[p, 1]], axis=0)
        s1 = jnp.dot(qs, kct, preferred_element_type=jnp.float32) + bias[:, 0:KV_WIN]
        s2 = lax.dot_general(qs, k_s[rows, lanes], _CONTRACT_LANES,
                             preferred_element_type=jnp.float32) + bias[:, KV_WIN:KV_WIN + S]
        scores.append((s1, s2))
    weights = []
    for s1, s2 in scores:
        m = jnp.maximum(jnp.max(s1, axis=-1, keepdims=True), jnp.max(s2, axis=-1, keepdims=True))
        e1 = jnp.exp2(s1 - m)
        e2 = jnp.exp2(s2 - m)
        l = jnp.sum(e1, axis=-1, keepdims=True) + jnp.sum(e2, axis=-1, keepdims=True)
        weights.append((e1.astype(jnp.bfloat16), e2.astype(jnp.bfloat16), 1.0 / l))
    for p, lanes in enumerate(pair_lanes):
        e1, e2, inv_l = weights[p]
        vct = cv_ref[0, 2 * p:2 * p + 2].reshape(LANES, KV_WIN).astype(jnp.bfloat16)
        o2 = (lax.dot_general(e1, vct, _CONTRACT_LANES, preferred_element_type=jnp.float32)
              + jnp.dot(e2, v_s[rows, lanes], preferred_element_type=jnp.float32))
        yb = _pair_select(o2 * inv_l, S) * sgb_s[rows, lanes]
        o_s[rows, B_WIDTH + p * LANES:B_WIDTH + (p + 1) * LANES] = yb.astype(jnp.bfloat16)

    @pl.when(b == n_streams - 1)
    def _():
        o = jnp.dot(o_s[...], wout_ref[...], preferred_element_type=jnp.float32)
        on = _rms_rows(o) * gpost_ref[...]
        for i in range(n_streams):
            sl = slice(i * S, (i + 1) * S)
            y_ref[sl, :] = x_ref[sl, :] + mod_ref[i, 2:3, :] * on[sl, :]


def _sample_call(x, mod, g_pre, w_in, ln_g, ln_b, w_s, b_full, bias_s, w_out, g_post, ck, cv):
    n_streams, n_new, _ = x.shape
    assert n_new % BF16_ROWS == 0 and n_new <= CHUNK
    rows = n_streams * n_new
    f32, bf16 = jnp.float32, jnp.bfloat16

    def const(shape):
        nd = len(shape)
        return pl.BlockSpec(shape, lambda b: (0,) * nd, pipeline_mode=pl.Buffered(1))

    kern = functools.partial(_sample_kernel, n_streams=n_streams, n_new=n_new)
    return pl.pallas_call(
        kern,
        grid=(n_streams,),
        in_specs=[
            const((rows, D_MODEL)),
            const((n_streams, 3, D_MODEL)),
            const((1, D_MODEL)),
            const((D_MODEL, D_IN)),
            const((1, A_WIDTH)),
            const((1, A_WIDTH)),
            const((A_GROUPS, MLP_CHUNK, MLP_CHUNK)),
            const((MLP_CHUNK, A_WIDTH)),
            const((N_PAIRS, 2, n_new, K_BLOCK)),
            const((D_MODEL, D_MODEL)),
            const((1, D_MODEL)),
            pl.BlockSpec((1, B_HEADS, B_HDIM, KV_WIN), lambda b: (b, 0, 0, 0)),
            pl.BlockSpec((1, B_HEADS, B_HDIM, KV_WIN), lambda b: (b, 0, 0, 0)),
        ],
        out_specs=[
            pl.BlockSpec((rows, D_MODEL), lambda b: (0, 0)),
            pl.BlockSpec((rows, B_HEADS, B_HDIM), lambda b: (0, 0, 0)),
            pl.BlockSpec((rows, B_HEADS, B_HDIM), lambda b: (0, 0, 0)),
            pl.BlockSpec((rows, A_WIDTH), lambda b: (0, 0)),
        ],
        out_shape=[
            jax.ShapeDtypeStruct((rows, D_MODEL), f32),
            jax.ShapeDtypeStruct((rows, B_HEADS, B_HDIM), f32),
            jax.ShapeDtypeStruct((rows, B_HEADS, B_HDIM), f32),
            jax.ShapeDtypeStruct((rows, A_WIDTH), f32),
        ],
        scratch_shapes=[
            pltpu.VMEM((rows, A_WIDTH), f32),
            pltpu.VMEM((rows, A_WIDTH), bf16),
            pltpu.VMEM((rows, B_WIDTH), bf16),
            pltpu.VMEM((rows, B_WIDTH), bf16),
            pltpu.VMEM((rows, B_WIDTH), bf16),
            pltpu.VMEM((rows, B_WIDTH), bf16),
            pltpu.VMEM((rows, B_WIDTH), f32),
            pltpu.VMEM((rows, D_MODEL), bf16),
            pltpu.VMEM((N_PAIRS, 2 * n_new, n_new), bf16),
        ],
        compiler_params=pltpu.CompilerParams(
            dimension_semantics=("arbitrary",),
            vmem_limit_bytes=VMEM_LIMIT_BYTES),
        name="sample",
    )(x.reshape(rows, D_MODEL), mod, g_pre, w_in, ln_g, ln_b, w_s, b_full, bias_s, w_out, g_post,
      ck, cv)


def _bias_base(rel_bias):
    n_far = KV_WIN - REL_CLIP
    far = jnp.broadcast_to(rel_bias[:, N_REL - 1:N_REL], (B_HEADS, n_far))
    near = rel_bias[:, N_REL - 1:0:-1]
    return jnp.concatenate([far, near], axis=1)


def kernel(x_prompt, x_sample, cache_attn_k, cache_attn_v, c_prompt, c_sample, g_pre, w_ada, b_ada,
           w_in, ln_g, ln_b, w_s, b_s, rel_bias, w_out, g_post):
    depth = g_pre.shape[0]
    bsz, seq, _ = x_prompt.shape
    n_streams, n_new, _ = x_sample.shape
    win = cache_attn_k.shape[2]
    assert win == KV_WIN and bsz + n_streams <= ADA_ROWS

    yp, ys = x_prompt, x_sample
    kp_rows, vp_rows, ks_rows, vs_rows, va_rows = [], [], [], [], []
    c_all = jnp.concatenate(
        [c_prompt, c_sample, jnp.zeros((ADA_ROWS - bsz - n_streams, D_MODEL), c_prompt.dtype)], axis=0)
    for l in range(depth):
        mod, bias_t, bias_s, w_in_b, w_out_b = _prep_call(
            c_all, w_ada[l], b_ada[l], _bias_base(rel_bias[l]), w_in[l], w_out[l], n_new)
        mod = mod.reshape(ADA_ROWS, 3, D_MODEL)
        b_full = jnp.repeat(b_s[l].T, A_GDIM, axis=1)
        pre = (g_pre[l].reshape(1, D_MODEL), w_in_b, ln_g[l].reshape(1, A_WIDTH),
               ln_b[l].reshape(1, A_WIDTH), w_s[l], b_full)
        post = (w_out_b, g_post[l].reshape(1, D_MODEL))

        yp, k_last, v_last = _prompt_call(yp, mod[:bsz], *pre, bias_t, *post)
        kp_rows.append(jnp.transpose(k_last, _FROM_FEATURE_MAJOR))
        vp_rows.append(jnp.transpose(v_last, _FROM_FEATURE_MAJOR))

        ys2, k_new, v_new, va_new = _sample_call(ys, mod[bsz:bsz + n_streams], *pre, bias_s, *post,
                                                 jnp.transpose(cache_attn_k[l], _TO_FEATURE_MAJOR),
                                                 jnp.transpose(cache_attn_v[l], _TO_FEATURE_MAJOR))
        ys = ys2.reshape(n_streams, n_new, D_MODEL)
        ks_rows.append(k_new.reshape(n_streams, n_new, B_HEADS, B_HDIM))
        vs_rows.append(v_new.reshape(n_streams, n_new, B_HEADS, B_HDIM))
        va_rows.append(va_new.reshape(n_streams, n_new, A_WIDTH))

    return (yp, ys, jnp.stack(kp_rows), jnp.stack(vp_rows), jnp.stack(ks_rows), jnp.stack(vs_rows),
            jnp.stack(va_rows))
```

```python
import functools

import jax
import jax.numpy as jnp
from jax import lax
from jax.experimental import pallas as pl
from jax.experimental.pallas import tpu as pltpu

LANES = 128
SUBLANES = 8
BF16_ROWS = 16
VMEM_LIMIT_BYTES = 60 * 1024 * 1024

D_MODEL = 1024
A_WIDTH = 512
A_GROUPS = 8
A_GDIM = A_WIDTH // A_GROUPS
MLP_CHUNK = 128
B_WIDTH = 512
B_HEADS = 8
B_HDIM = B_WIDTH // B_HEADS
CHUNK = 64
KV_WIN = 512
REL_CLIP = 128
N_REL = 2 * REL_CLIP + 1
EPS = 1e-6
D_IN = 3 * A_WIDTH + 4 * B_WIDTH
NEG = -1e30
LOG2E = 1.4426950408889634
Q_SCALE = B_HDIM ** -0.5 * LOG2E

N_PAIRS = B_HEADS // 2
Q_BLOCK = 2 * CHUNK
K_BLOCK = KV_WIN + Q_BLOCK
SEQ_TILE = 1024
PROJ_ROWS = 256
POST_ROWS = 256
SCORE_LEAD = 3
SCORE_SLOTS = SCORE_LEAD + 1

C_U, C_V, C_GA, C_Q, C_K, C_VV, C_GB = (i * 512 for i in range(7))

_CONTRACT_LANES = (((1,), (1,)), ((), ()))

_TO_FEATURE_MAJOR = (0, 2, 3, 1)
_FROM_FEATURE_MAJOR = (0, 3, 1, 2)


def _sigmoid_exp2(neg_arg_log2):
    return 1.0 / (1.0 + jnp.exp2(neg_arg_log2))


def _gelu(x):
    c1 = -2.0 * (2.0 / jnp.pi) ** 0.5 * LOG2E
    return x * _sigmoid_exp2(x * (c1 + (c1 * 0.044715) * (x * x)))


def _silu(x):
    return x * _sigmoid_exp2(x * (-LOG2E))


def _even_lane_mask(shape):
    lane = lax.broadcasted_iota(jnp.int32, shape, len(shape) - 1)
    return (lane % LANES) < B_HDIM


def _ada_block(c_ref, w_ref, b_ref, o_ref):
    c = c_ref[...]
    o_ref[...] = jnp.dot(_silu(c), w_ref[...], preferred_element_type=jnp.float32) + b_ref[...]


def _relb_block(base_ref, bt_ref, bs_ref, n_new):
    row = lax.broadcasted_iota(jnp.int32, (Q_BLOCK, K_BLOCK), 0)
    col = lax.broadcasted_iota(jnp.int32, (Q_BLOCK, K_BLOCK), 1)
    band_lo = jnp.where(row < CHUNK, 0, CHUNK)
    rel = col - band_lo
    outside = jnp.logical_or(rel < 0, rel >= KV_WIN + CHUNK)
    for par in range(2):
        base = base_ref[0, par:par + 1, :] * LOG2E
        x = jnp.broadcast_to(base, (Q_BLOCK, K_BLOCK))
        shift = 1
        while shift < Q_BLOCK:
            x = jnp.where((row & shift) != 0, jnp.roll(x, shift, axis=1), x)
            shift *= 2
        far = jnp.broadcast_to(base[:, 0:1], (Q_BLOCK, K_BLOCK))
        x = jnp.where(col < row, far, x)
        bs_ref[0, par] = x[0:n_new, :]
        bt_ref[0, :, par * Q_BLOCK:(par + 1) * Q_BLOCK] = jnp.where(outside, NEG, x).T


PREP_COLS = 512
N_PREP = D_IN // PREP_COLS
N_ADA = 3 * D_MODEL // PREP_COLS
WOUT_COLS = D_MODEL // N_PAIRS


def _prep_kernel(cp_ref, cs_ref, wada_ref, bada_ref, base_ref, win_ref, wout_ref, bs_ref,
                 modp_ref, mods_ref, bt_ref, bsam_ref, winb_ref, woutb_ref, bfull_ref, *, n_new):
    j = pl.program_id(0)
    winb_ref[...] = win_ref[...].astype(jnp.bfloat16)

    @pl.when(j == 0)
    def _():
        bst = bs_ref[...].T
        even = _even_lane_mask((MLP_CHUNK, LANES))
        for p in range(N_PAIRS):
            bfull_ref[:, p * LANES:(p + 1) * LANES] = jnp.where(
                even, bst[:, 2 * p:2 * p + 1], bst[:, 2 * p + 1:2 * p + 2])

    @pl.when(j < N_ADA)
    def _():
        _ada_block(cp_ref, wada_ref, bada_ref, modp_ref)
        _ada_block(cs_ref, wada_ref, bada_ref, mods_ref)

    @pl.when(j < N_PAIRS)
    def _():
        woutb_ref[...] = wout_ref[...].astype(jnp.bfloat16)
        _relb_block(base_ref, bt_ref, bsam_ref, n_new)


def _prep_call(c_prompt, c_sample, w_ada, b_ada, base, w_in, w_out, b_s, n_new):
    f32, bf16 = jnp.float32, jnp.bfloat16
    bsz, n_streams = c_prompt.shape[0], c_sample.shape[0]

    def upto(n):
        return lambda j: jnp.minimum(j, n - 1)

    ada_j, pair_j = upto(N_ADA), upto(N_PAIRS)
    return pl.pallas_call(
        functools.partial(_prep_kernel, n_new=n_new),
        grid=(N_PREP,),
        in_specs=[
            pl.BlockSpec((bsz, D_MODEL), lambda j: (0, 0)),
            pl.BlockSpec((n_streams, D_MODEL), lambda j: (0, 0)),
            pl.BlockSpec((D_MODEL, PREP_COLS), lambda j: (0, ada_j(j))),
            pl.BlockSpec((1, PREP_COLS), lambda j: (0, ada_j(j))),
            pl.BlockSpec((1, 2, K_BLOCK), lambda j: (pair_j(j), 0, 0)),
            pl.BlockSpec((D_MODEL, PREP_COLS), lambda j: (0, j)),
            pl.BlockSpec((D_MODEL, WOUT_COLS), lambda j: (0, pair_j(j))),
            pl.BlockSpec((A_GROUPS, MLP_CHUNK), lambda j: (0, 0)),
        ],
        out_specs=[
            pl.BlockSpec((bsz, PREP_COLS), lambda j: (0, ada_j(j))),
            pl.BlockSpec((n_streams, PREP_COLS), lambda j: (0, ada_j(j))),
            pl.BlockSpec((1, K_BLOCK, 2 * Q_BLOCK), lambda j: (pair_j(j), 0, 0)),
            pl.BlockSpec((1, 2, n_new, K_BLOCK), lambda j: (pair_j(j), 0, 0, 0)),
            pl.BlockSpec((D_MODEL, PREP_COLS), lambda j: (0, j)),
            pl.BlockSpec((D_MODEL, WOUT_COLS), lambda j: (0, pair_j(j))),
            pl.BlockSpec((MLP_CHUNK, A_WIDTH), lambda j: (0, 0)),
        ],
        out_shape=[
            jax.ShapeDtypeStruct((bsz, 3 * D_MODEL), f32),
            jax.ShapeDtypeStruct((n_streams, 3 * D_MODEL), f32),
            jax.ShapeDtypeStruct((N_PAIRS, K_BLOCK, 2 * Q_BLOCK), f32),
            jax.ShapeDtypeStruct((N_PAIRS, 2, n_new, K_BLOCK), f32),
            jax.ShapeDtypeStruct((D_MODEL, D_IN), bf16),
            jax.ShapeDtypeStruct((D_MODEL, D_MODEL), bf16),
            jax.ShapeDtypeStruct((MLP_CHUNK, A_WIDTH), f32),
        ],
        compiler_params=pltpu.CompilerParams(dimension_semantics=("arbitrary",)),
        name="prep",
    )(c_prompt, c_sample, w_ada, b_ada.reshape(1, 3 * D_MODEL), base.reshape(N_PAIRS, 2, K_BLOCK),
      w_in, w_out, b_s)


def _tril_pairs(ws_ref, wt_ref, n):
    row = lax.broadcasted_iota(jnp.int32, (n, n), 0)
    col = lax.broadcasted_iota(jnp.int32, (n, n), 1)
    keep = col <= row
    for g in range(A_GROUPS):
        w = jnp.where(keep, ws_ref[g, :n, :n], 0.0)
        wt_ref[g // 2, (g % 2) * n:(g % 2 + 1) * n, :] = w.astype(jnp.bfloat16)


def _mod_rows(mod_ref, row):
    return tuple(mod_ref[row, i * D_MODEL:(i + 1) * D_MODEL] for i in range(3))


def _rms_rows(x):
    return x * lax.rsqrt(jnp.mean(x * x, axis=-1, keepdims=True) + EPS)


def _layernorm(x, g, b):
    mu = jnp.mean(x, axis=-1, keepdims=True)
    xc = x - mu
    var = jnp.mean(xc * xc, axis=-1, keepdims=True)
    return xc * lax.rsqrt(var + EPS) * g + b


def _pair_select(stacked, n):
    return jnp.where(_even_lane_mask((n, LANES)), stacked[:n], stacked[n:])


def _split_heads(x):
    return x.reshape(x.shape[0], B_HEADS, B_HDIM)


def _to_feature_major(x):
    return x.T.reshape(B_HEADS, B_HDIM, x.shape[0])


def _store_q(q, qe_ref, qo_ref):
    q = (q * Q_SCALE).astype(jnp.bfloat16)
    even = _even_lane_mask(q.shape)
    zero = jnp.zeros((), jnp.bfloat16)
    qe_ref[...] = jnp.where(even, q, zero)
    qo_ref[...] = jnp.where(even, zero, q)


def _prompt_kernel(x_ref, mod_ref, gpre_ref, win_ref, lng_ref, lnb_ref, ws_ref, bfull_ref,
                   biast_ref, wout_ref, gpost_ref,
                   y_ref, klast_ref, vlast_ref,
                   h_s, usg_s, va_s, qe_s, qo_s, k_s, vt_s, sgb_s, o_s, wt_s, st_s, kf_s, vf_s):
    t = pl.program_id(1)
    nt = pl.num_programs(1)
    T = SEQ_TILE

    @pl.when(jnp.logical_and(pl.program_id(0) == 0, t == 0))
    def _():
        _tril_pairs(ws_ref, wt_s, MLP_CHUNK)

    shift, scale, gate = _mod_rows(mod_ref, pl.ds(pl.program_id(0), 1))

    pre_scale = gpre_ref[...] * (1.0 + scale)

    def proj(c0, rows):
        return jnp.dot(h_s[rows, :], win_ref[:, c0:c0 + 512], preferred_element_type=jnp.float32)

    def proj_pieces(lo):
        rb = slice(lo, lo + PROJ_ROWS)
        halves = [slice(lo, lo + PROJ_ROWS // 2), slice(lo + PROJ_ROWS // 2, lo + PROJ_ROWS)]
        hist = slice(KV_WIN + lo, KV_WIN + lo + PROJ_ROWS)
        keep = slice(lo - (T - KV_WIN), lo - (T - KV_WIN) + PROJ_ROWS) if lo >= T - KV_WIN else None

        def pre_norm():
            for rows in halves:
                h_s[rows, :] = (_rms_rows(x_ref[0, rows, :]) * pre_scale + shift).astype(jnp.bfloat16)

        def u_half(rows):
            usg_s[rows, :] = _gelu(proj(C_U, rows))

        def v_block():
            v = proj(C_VV, rb)
            vt_s[:, hist] = v.T.astype(jnp.bfloat16)
            if keep is not None:
                vf_s[keep, :] = v

        def va_block():
            va_s[rb, :] = _layernorm(_gelu(proj(C_V, rb)), lng_ref[...], lnb_ref[...]
                                     ).astype(jnp.bfloat16)

        def q_block():
            _store_q(proj(C_Q, rb), qe_s.at[rb, :], qo_s.at[rb, :])

        def ga_block():
            usg_s[rb, :] = usg_s[rb, :] * _silu(proj(C_GA, rb))

        def gmlp(p):
            lanes = slice(p * LANES, (p + 1) * LANES)
            for c0 in range(lo, lo + PROJ_ROWS, 2 * MLP_CHUNK):
                chunks = [slice(c0, c0 + MLP_CHUNK), slice(c0 + MLP_CHUNK, c0 + 2 * MLP_CHUNK)]
                slabs = jnp.concatenate([va_s[rows, lanes] for rows in chunks], axis=1)
                mix = jnp.dot(wt_s[p], slabs, preferred_element_type=jnp.float32)
                for i, rows in enumerate(chunks):
                    mixed = _pair_select(mix[:, i * LANES:(i + 1) * LANES], MLP_CHUNK)
                    o_s[rows, lanes] = (usg_s[rows, lanes] * (mixed + bfull_ref[:, lanes])
                                        ).astype(jnp.bfloat16)

        def gb_block():
            sgb_s[rb, :] = _silu(proj(C_GB, rb))

        def k_block():
            k = proj(C_K, rb)
            k_s[hist, :] = k.astype(jnp.bfloat16)
            if keep is not None:
                kf_s[keep, :] = k

        g = [functools.partial(gmlp, p) for p in range(N_PAIRS)]
        return [pre_norm, functools.partial(u_half, halves[0]), functools.partial(u_half, halves[1]),
                v_block, va_block, ga_block, g[0], q_block, g[1], gb_block, g[2], k_block, g[3]]

    def key_lo(qb, first_tile):
        return max(KV_WIN - qb * Q_BLOCK, 0) if first_tile else 0

    def scores(qb, p, slot, first_tile):
        r0, lo = qb * Q_BLOCK, key_lo(qb, first_tile)
        lanes = slice(p * LANES, (p + 1) * LANES)
        qs = jnp.concatenate([qe_s[r0:r0 + Q_BLOCK, lanes],
                              qo_s[r0:r0 + Q_BLOCK, lanes]], axis=0)
        half = (K_BLOCK - lo) // 2
        for a in (lo, lo + half):
            st_s[slot, a:a + half, :] = lax.dot_general(
                k_s[r0 + a:r0 + a + half, lanes], qs, _CONTRACT_LANES,
                preferred_element_type=jnp.float32) + biast_ref[p, a:a + half, :]

    def finish(qb, p, slot, first_tile):
        r0, lo = qb * Q_BLOCK, key_lo(qb, first_tile)
        st = st_s[slot, lo:K_BLOCK, :]
        m = jnp.max(st, axis=0, keepdims=True)
        pt = jnp.exp2(st - m).astype(jnp.bfloat16)
        ones = jnp.ones((BF16_ROWS, K_BLOCK - lo), jnp.bfloat16)
        lhs = jnp.concatenate([vt_s[p * LANES:(p + 1) * LANES, r0 + lo:r0 + K_BLOCK], ones], axis=0)
        ot = jnp.dot(lhs, pt, preferred_element_type=jnp.float32)
        inv_l = 1.0 / ot[LANES:LANES + 1, :]
        ots = [ot[par * B_HDIM:(par + 1) * B_HDIM, par * Q_BLOCK:(par + 1) * Q_BLOCK]
               * inv_l[:, par * Q_BLOCK:(par + 1) * Q_BLOCK] for par in range(2)]
        yb = jnp.concatenate(ots, axis=0).T * sgb_s[r0:r0 + Q_BLOCK, p * LANES:(p + 1) * LANES]
        o_s[r0:r0 + Q_BLOCK, B_WIDTH + p * LANES:B_WIDTH + (p + 1) * LANES] = (
            yb.astype(jnp.bfloat16))

    post_scale = gate * gpost_ref[...]

    def out_block(lo, n):
        rows = slice(lo, lo + n)
        o = jnp.dot(o_s[rows, :], wout_ref[...], preferred_element_type=jnp.float32)
        y_ref[0, rows, :] = x_ref[0, rows, :] + _rms_rows(o) * post_scale

    def attend(first_tile):
        items = [(qb, p) for qb in range(T // Q_BLOCK) for p in range(N_PAIRS)]
        per_block = (PROJ_ROWS // Q_BLOCK) * N_PAIRS
        n_blocks = T // PROJ_ROWS
        for piece in proj_pieces(0):
            piece()
        pending = []
        for j in range(SCORE_LEAD):
            scores(*items[j], j % SCORE_SLOTS, first_tile)
        for i, (qb, p) in enumerate(items):
            blk, pos = divmod(i, per_block)
            if pos == 0 and blk + 1 < n_blocks:
                pending = proj_pieces((blk + 1) * PROJ_ROWS)
            if i + SCORE_LEAD < len(items):
                assert pos < per_block - SCORE_LEAD or not pending
                scores(*items[i + SCORE_LEAD], (i + SCORE_LEAD) % SCORE_SLOTS, first_tile)
            finish(qb, p, i % SCORE_SLOTS, first_tile)
            slots_left = per_block - SCORE_LEAD - pos
            if pending and slots_left > 0:
                n_now = -(-len(pending) // slots_left)
                for piece in pending[:n_now]:
                    piece()
                pending = pending[n_now:]
            rows_done = (qb + 1) * Q_BLOCK
            if p == N_PAIRS - 1 and rows_done % POST_ROWS == 0:
                out_block(rows_done - POST_ROWS, POST_ROWS)

    @pl.when(t == 0)
    def _():
        attend(True)

    @pl.when(t > 0)
    def _():
        attend(False)

    k_s[0:KV_WIN, :] = k_s[T:T + KV_WIN, :]
    vt_s[:, 0:KV_WIN] = vt_s[:, T:T + KV_WIN]

    @pl.when(t == nt - 1)
    def _():
        klast_ref[0] = _to_feature_major(kf_s[...])
        vlast_ref[0] = _to_feature_major(vf_s[...])


def _const_spec(shape):
    nd = len(shape)
    return pl.BlockSpec(shape, lambda b, t: (0,) * nd, pipeline_mode=pl.Buffered(1))


def _prompt_call(x, mod, g_pre, w_in, ln_g, ln_b, w_s, b_full, bias_t, w_out, g_post):
    bsz, seq, _ = x.shape
    T = SEQ_TILE
    assert seq % T == 0 and T % PROJ_ROWS == 0 and KV_WIN % PROJ_ROWS == 0 and T % POST_ROWS == 0
    assert PROJ_ROWS % (2 * MLP_CHUNK) == 0
    nt = seq // T
    f32, bf16 = jnp.float32, jnp.bfloat16
    return pl.pallas_call(
        _prompt_kernel,
        grid=(bsz, nt),
        in_specs=[
            pl.BlockSpec((1, T, D_MODEL), lambda b, t: (b, t, 0)),
            _const_spec((bsz, 3 * D_MODEL)),
            _const_spec((1, D_MODEL)),
            _const_spec((D_MODEL, D_IN)),
            _const_spec((1, A_WIDTH)),
            _const_spec((1, A_WIDTH)),
            _const_spec((A_GROUPS, MLP_CHUNK, MLP_CHUNK)),
            _const_spec((MLP_CHUNK, A_WIDTH)),
            _const_spec((N_PAIRS, K_BLOCK, 2 * Q_BLOCK)),
            _const_spec((D_MODEL, D_MODEL)),
            _const_spec((1, D_MODEL)),
        ],
        out_specs=[
            pl.BlockSpec((1, T, D_MODEL), lambda b, t: (b, t, 0)),
            pl.BlockSpec((1, B_HEADS, B_HDIM, KV_WIN), lambda b, t: (b, 0, 0, 0)),
            pl.BlockSpec((1, B_HEADS, B_HDIM, KV_WIN), lambda b, t: (b, 0, 0, 0)),
        ],
        out_shape=[
            jax.ShapeDtypeStruct((bsz, seq, D_MODEL), f32),
            jax.ShapeDtypeStruct((bsz, B_HEADS, B_HDIM, KV_WIN), f32),
            jax.ShapeDtypeStruct((bsz, B_HEADS, B_HDIM, KV_WIN), f32),
        ],
        scratch_shapes=[
            pltpu.VMEM((T, D_MODEL), bf16),
            pltpu.VMEM((T, A_WIDTH), f32),
            pltpu.VMEM((T, A_WIDTH), bf16),
            pltpu.VMEM((T, B_WIDTH), bf16),
            pltpu.VMEM((T, B_WIDTH), bf16),
            pltpu.VMEM((KV_WIN + T, B_WIDTH), bf16),
            pltpu.VMEM((B_WIDTH, KV_WIN + T), bf16),
            pltpu.VMEM((T, B_WIDTH), f32),
            pltpu.VMEM((T, D_MODEL), bf16),
            pltpu.VMEM((N_PAIRS, 2 * MLP_CHUNK, MLP_CHUNK), bf16),
            pltpu.VMEM((SCORE_SLOTS, K_BLOCK, 2 * Q_BLOCK), f32),
            pltpu.VMEM((KV_WIN, B_WIDTH), f32),
            pltpu.VMEM((KV_WIN, B_WIDTH), f32),
        ],
        compiler_params=pltpu.CompilerParams(
            dimension_semantics=("arbitrary", "arbitrary"),
            vmem_limit_bytes=VMEM_LIMIT_BYTES),
        name="prompt",
    )(x, mod, g_pre, w_in, ln_g, ln_b, w_s, b_full, bias_t, w_out, g_post)


def _sample_kernel(x_ref, mod_ref, gpre_ref, win_ref, lng_ref, lnb_ref, ws_ref, bfull_ref,
                   bias_ref, wout_ref, gpost_ref, ck_ref, cv_ref,
                   y_ref, knew_ref, vnew_ref, vanew_ref,
                   usg_s, va_s, qe_s, qo_s, k_s, v_s, sgb_s, o_s, wt_s, *, n_streams, n_new):
    b = pl.program_id(0)
    S = n_new

    @pl.when(b == 0)
    def _():
        _tril_pairs(ws_ref, wt_s, S)
        gpre = gpre_ref[...]
        hs = []
        for i in range(n_streams):
            xi = x_ref[i * S:(i + 1) * S, :]
            shift, scale, _ = _mod_rows(mod_ref, slice(i, i + 1))
            hs.append(_rms_rows(xi) * (gpre * (1.0 + scale)) + shift)
        h = jnp.concatenate(hs, axis=0).astype(jnp.bfloat16)

        def proj(c0):
            return jnp.dot(h, win_ref[:, c0:c0 + 512], preferred_element_type=jnp.float32)

        va = _layernorm(_gelu(proj(C_V)), lng_ref[...], lnb_ref[...])
        vanew_ref[...] = va
        va_s[...] = va.astype(jnp.bfloat16)
        usg_s[...] = _gelu(proj(C_U)) * _silu(proj(C_GA))
        _store_q(proj(C_Q), qe_s, qo_s)
        k = proj(C_K)
        v = proj(C_VV)
        knew_ref[...] = _split_heads(k)
        vnew_ref[...] = _split_heads(v)
        k_s[...] = k.astype(jnp.bfloat16)
        v_s[...] = v.astype(jnp.bfloat16)
        sgb_s[...] = _silu(proj(C_GB))

    r0 = pl.multiple_of(b * S, S)
    rows = pl.ds(r0, S)
    pair_lanes = [slice(p * LANES, (p + 1) * LANES) for p in range(N_PAIRS)]
    for p, lanes in enumerate(pair_lanes):
        mix = jnp.dot(wt_s[p], va_s[rows, lanes], preferred_element_type=jnp.float32)
        mixed = _pair_select(mix, S) + bfull_ref[0:S, lanes]
        o_s[rows, lanes] = (usg_s[rows, lanes] * mixed).astype(jnp.bfloat16)

    scores = []
    for p, lanes in enumerate(pair_lanes):
        qs = jnp.concatenate([qe_s[rows, lanes], qo_s[rows, lanes]], axis=0)
        kct = ck_ref[0, 2 * p:2 * p + 2].reshape(LANES, KV_WIN).astype(jnp.bfloat16)
        bias = jnp.concatenate([bias_ref[p, 0], bias_ref[p, 1]], axis=0)
        s1 = jnp.dot(qs, kct, preferred_element_type=jnp.float32) + bias[:, 0:KV_WIN]
        s2 = lax.dot_general(qs, k_s[rows, lanes], _CONTRACT_LANES,
                             preferred_element_type=jnp.float32) + bias[:, KV_WIN:KV_WIN + S]
        scores.append((s1, s2))
    weights = []
    for s1, s2 in scores:
        m = jnp.maximum(jnp.max(s1, axis=-1, keepdims=True), jnp.max(s2, axis=-1, keepdims=True))
        e1 = jnp.exp2(s1 - m)
        e2 = jnp.exp2(s2 - m)
        l = jnp.sum(e1, axis=-1, keepdims=True) + jnp.sum(e2, axis=-1, keepdims=True)
        weights.append((e1.astype(jnp.bfloat16), e2.astype(jnp.bfloat16), 1.0 / l))
    for p, lanes in enumerate(pair_lanes):
        e1, e2, inv_l = weights[p]
        vct = cv_ref[0, 2 * p:2 * p + 2].reshape(LANES, KV_WIN).astype(jnp.bfloat16)
        o2 = (lax.dot_general(e1, vct, _CONTRACT_LANES, preferred_element_type=jnp.float32)
              + jnp.dot(e2, v_s[rows, lanes], preferred_element_type=jnp.float32))
        yb = _pair_select(o2 * inv_l, S) * sgb_s[rows, lanes]
        o_s[rows, B_WIDTH + p * LANES:B_WIDTH + (p + 1) * LANES] = yb.astype(jnp.bfloat16)

    @pl.when(b == n_streams - 1)
    def _():
        o = jnp.dot(o_s[...], wout_ref[...], preferred_element_type=jnp.float32)
        on = _rms_rows(o) * gpost_ref[...]
        for i in range(n_streams):
            sl = slice(i * S, (i + 1) * S)
            y_ref[sl, :] = x_ref[sl, :] + _mod_rows(mod_ref, slice(i, i + 1))[2] * on[sl, :]


def _sample_call(x, mod, g_pre, w_in, ln_g, ln_b, w_s, b_full, bias_s, w_out, g_post, ck, cv):
    n_streams, n_new, _ = x.shape
    assert n_new % BF16_ROWS == 0 and n_new <= CHUNK
    rows = n_streams * n_new
    f32, bf16 = jnp.float32, jnp.bfloat16

    def const(shape):
        nd = len(shape)
        return pl.BlockSpec(shape, lambda b: (0,) * nd, pipeline_mode=pl.Buffered(1))

    kern = functools.partial(_sample_kernel, n_streams=n_streams, n_new=n_new)
    return pl.pallas_call(
        kern,
        grid=(n_streams,),
        in_specs=[
            const((rows, D_MODEL)),
            const((n_streams, 3 * D_MODEL)),
            const((1, D_MODEL)),
            const((D_MODEL, D_IN)),
            const((1, A_WIDTH)),
            const((1, A_WIDTH)),
            const((A_GROUPS, MLP_CHUNK, MLP_CHUNK)),
            const((MLP_CHUNK, A_WIDTH)),
            const((N_PAIRS, 2, n_new, K_BLOCK)),
            const((D_MODEL, D_MODEL)),
            const((1, D_MODEL)),
            pl.BlockSpec((1, B_HEADS, B_HDIM, KV_WIN), lambda b: (b, 0, 0, 0)),
            pl.BlockSpec((1, B_HEADS, B_HDIM, KV_WIN), lambda b: (b, 0, 0, 0)),
        ],
        out_specs=[
            pl.BlockSpec((rows, D_MODEL), lambda b: (0, 0)),
            pl.BlockSpec((rows, B_HEADS, B_HDIM), lambda b: (0, 0, 0)),
            pl.BlockSpec((rows, B_HEADS, B_HDIM), lambda b: (0, 0, 0)),
            pl.BlockSpec((rows, A_WIDTH), lambda b: (0, 0)),
        ],
        out_shape=[
            jax.ShapeDtypeStruct((rows, D_MODEL), f32),
            jax.ShapeDtypeStruct((rows, B_HEADS, B_HDIM), f32),
            jax.ShapeDtypeStruct((rows, B_HEADS, B_HDIM), f32),
            jax.ShapeDtypeStruct((rows, A_WIDTH), f32),
        ],
        scratch_shapes=[
            pltpu.VMEM((rows, A_WIDTH), f32),
            pltpu.VMEM((rows, A_WIDTH), bf16),
            pltpu.VMEM((rows, B_WIDTH), bf16),
            pltpu.VMEM((rows, B_WIDTH), bf16),
            pltpu.VMEM((rows, B_WIDTH), bf16),
            pltpu.VMEM((rows, B_WIDTH), bf16),
            pltpu.VMEM((rows, B_WIDTH), f32),
            pltpu.VMEM((rows, D_MODEL), bf16),
            pltpu.VMEM((N_PAIRS, 2 * n_new, n_new), bf16),
        ],
        compiler_params=pltpu.CompilerParams(
            dimension_semantics=("arbitrary",),
            vmem_limit_bytes=VMEM_LIMIT_BYTES),
        name="sample",
    )(x.reshape(rows, D_MODEL), mod, g_pre, w_in, ln_g, ln_b, w_s, b_full, bias_s, w_out, g_post,
      ck, cv)


def _bias_base(rel_bias):
    n_far = KV_WIN - REL_CLIP
    far = jnp.broadcast_to(rel_bias[:, N_REL - 1:N_REL], (B_HEADS, n_far))
    near = rel_bias[:, N_REL - 1:0:-1]
    return jnp.concatenate([far, near], axis=1)


def kernel(x_prompt, x_sample, cache_attn_k, cache_attn_v, c_prompt, c_sample, g_pre, w_ada, b_ada,
           w_in, ln_g, ln_b, w_s, b_s, rel_bias, w_out, g_post):
    depth = g_pre.shape[0]
    bsz, seq, _ = x_prompt.shape
    n_streams, n_new, _ = x_sample.shape
    win = cache_attn_k.shape[2]
    assert win == KV_WIN

    yp, ys = x_prompt, x_sample
    kp_rows, vp_rows, ks_rows, vs_rows, va_rows = [], [], [], [], []
    for l in range(depth):
        mod_p, mod_s, bias_t, bias_s, w_in_b, w_out_b, b_full = _prep_call(
            c_prompt, c_sample, w_ada[l], b_ada[l], _bias_base(rel_bias[l]), w_in[l], w_out[l],
            b_s[l], n_new)
        pre = (g_pre[l].reshape(1, D_MODEL), w_in_b, ln_g[l].reshape(1, A_WIDTH),
               ln_b[l].reshape(1, A_WIDTH), w_s[l], b_full)
        post = (w_out_b, g_post[l].reshape(1, D_MODEL))

        yp, k_last, v_last = _prompt_call(yp, mod_p, *pre, bias_t, *post)
        kp_rows.append(jnp.transpose(k_last, _FROM_FEATURE_MAJOR))
        vp_rows.append(jnp.transpose(v_last, _FROM_FEATURE_MAJOR))

        ys2, k_new, v_new, va_new = _sample_call(ys, mod_s, *pre, bias_s, *post,
                                                 jnp.transpose(cache_attn_k[l], _TO_FEATURE_MAJOR),
                                                 jnp.transpose(cache_attn_v[l], _TO_FEATURE_MAJOR))
        ys = ys2.reshape(n_streams, n_new, D_MODEL)
        ks_rows.append(k_new.reshape(n_streams, n_new, B_HEADS, B_HDIM))
        vs_rows.append(v_new.reshape(n_streams, n_new, B_HEADS, B_HDIM))
        va_rows.append(va_new.reshape(n_streams, n_new, A_WIDTH))

    return (yp, ys, jnp.stack(kp_rows), jnp.stack(vp_rows), jnp.stack(ks_rows), jnp.stack(vs_rows),
            jnp.stack(va_rows))
```

```python
import functools

import jax
import jax.numpy as jnp
from jax import lax
from jax.experimental import pallas as pl
from jax.experimental.pallas import tpu as pltpu

LANES = 128
SUBLANES = 8
BF16_ROWS = 16
VMEM_LIMIT_BYTES = 60 * 1024 * 1024

D_MODEL = 1024
A_WIDTH = 512
A_GROUPS = 8
A_GDIM = A_WIDTH // A_GROUPS
MLP_CHUNK = 128
B_WIDTH = 512
B_HEADS = 8
B_HDIM = B_WIDTH // B_HEADS
CHUNK = 64
KV_WIN = 512
REL_CLIP = 128
N_REL = 2 * REL_CLIP + 1
EPS = 1e-6
D_IN = 3 * A_WIDTH + 4 * B_WIDTH
NEG = -1e30
LOG2E = 1.4426950408889634
Q_SCALE = B_HDIM ** -0.5 * LOG2E

N_PAIRS = B_HEADS // 2
Q_BLOCK = 2 * CHUNK
K_BLOCK = KV_WIN + Q_BLOCK
SEQ_TILE = 1024
PROJ_ROWS = 256
POST_ROWS = 256
SCORE_LEAD = 2
SCORE_SLOTS = SCORE_LEAD + 1

C_U, C_V, C_GA, C_Q, C_K, C_VV, C_GB = (i * 512 for i in range(7))

_CONTRACT_LANES = (((1,), (1,)), ((), ()))

_TO_FEATURE_MAJOR = (0, 2, 3, 1)
_FROM_FEATURE_MAJOR = (0, 3, 1, 2)


def _sigmoid_exp2(neg_arg_log2):
    return 1.0 / (1.0 + jnp.exp2(neg_arg_log2))


def _gelu(x):
    c1 = -2.0 * (2.0 / jnp.pi) ** 0.5 * LOG2E
    return x * _sigmoid_exp2(x * (c1 + (c1 * 0.044715) * (x * x)))


def _silu(x):
    return x * _sigmoid_exp2(x * (-LOG2E))


def _even_lane_mask(shape):
    lane = lax.broadcasted_iota(jnp.int32, shape, len(shape) - 1)
    return (lane % LANES) < B_HDIM


def _ada_block(c_ref, w_ref, b_ref, o_ref):
    c = c_ref[...]
    o_ref[...] = jnp.dot(_silu(c), w_ref[...], preferred_element_type=jnp.float32) + b_ref[...]


def _relb_block(base_ref, bt_ref, bs_ref, n_new):
    row = lax.broadcasted_iota(jnp.int32, (Q_BLOCK, K_BLOCK), 0)
    col = lax.broadcasted_iota(jnp.int32, (Q_BLOCK, K_BLOCK), 1)
    band_lo = jnp.where(row < CHUNK, 0, CHUNK)
    rel = col - band_lo
    outside = jnp.logical_or(rel < 0, rel >= KV_WIN + CHUNK)
    for par in range(2):
        base = base_ref[0, par:par + 1, :] * LOG2E
        x = jnp.broadcast_to(base, (Q_BLOCK, K_BLOCK))
        shift = 1
        while shift < Q_BLOCK:
            x = jnp.where((row & shift) != 0, jnp.roll(x, shift, axis=1), x)
            shift *= 2
        far = jnp.broadcast_to(base[:, 0:1], (Q_BLOCK, K_BLOCK))
        x = jnp.where(col < row, far, x)
        bs_ref[0, par] = x[0:n_new, :]
        bt_ref[0, :, par * Q_BLOCK:(par + 1) * Q_BLOCK] = jnp.where(outside, NEG, x).T


PREP_COLS = 512
N_PREP = D_IN // PREP_COLS
N_ADA = 3 * D_MODEL // PREP_COLS
WOUT_COLS = D_MODEL // N_PAIRS


def _prep_kernel(cp_ref, cs_ref, wada_ref, bada_ref, base_ref, win_ref, wout_ref, bs_ref,
                 modp_ref, mods_ref, bt_ref, bsam_ref, winb_ref, woutb_ref, bfull_ref, *, n_new):
    j = pl.program_id(0)
    winb_ref[...] = win_ref[...].astype(jnp.bfloat16)

    @pl.when(j == 0)
    def _():
        bst = bs_ref[...].T
        even = _even_lane_mask((MLP_CHUNK, LANES))
        for p in range(N_PAIRS):
            bfull_ref[:, p * LANES:(p + 1) * LANES] = jnp.where(
                even, bst[:, 2 * p:2 * p + 1], bst[:, 2 * p + 1:2 * p + 2])

    @pl.when(j < N_ADA)
    def _():
        _ada_block(cp_ref, wada_ref, bada_ref, modp_ref)
        _ada_block(cs_ref, wada_ref, bada_ref, mods_ref)

    @pl.when(j < N_PAIRS)
    def _():
        woutb_ref[...] = wout_ref[...].astype(jnp.bfloat16)
        _relb_block(base_ref, bt_ref, bsam_ref, n_new)


def _prep_call(c_prompt, c_sample, w_ada, b_ada, base, w_in, w_out, b_s, n_new):
    f32, bf16 = jnp.float32, jnp.bfloat16
    bsz, n_streams = c_prompt.shape[0], c_sample.shape[0]

    def upto(n):
        return lambda j: jnp.minimum(j, n - 1)

    ada_j, pair_j = upto(N_ADA), upto(N_PAIRS)
    return pl.pallas_call(
        functools.partial(_prep_kernel, n_new=n_new),
        grid=(N_PREP,),
        in_specs=[
            pl.BlockSpec((bsz, D_MODEL), lambda j: (0, 0)),
            pl.BlockSpec((n_streams, D_MODEL), lambda j: (0, 0)),
            pl.BlockSpec((D_MODEL, PREP_COLS), lambda j: (0, ada_j(j))),
            pl.BlockSpec((1, PREP_COLS), lambda j: (0, ada_j(j))),
            pl.BlockSpec((1, 2, K_BLOCK), lambda j: (pair_j(j), 0, 0)),
            pl.BlockSpec((D_MODEL, PREP_COLS), lambda j: (0, j)),
            pl.BlockSpec((D_MODEL, WOUT_COLS), lambda j: (0, pair_j(j))),
            pl.BlockSpec((A_GROUPS, MLP_CHUNK), lambda j: (0, 0)),
        ],
        out_specs=[
            pl.BlockSpec((bsz, PREP_COLS), lambda j: (0, ada_j(j))),
            pl.BlockSpec((n_streams, PREP_COLS), lambda j: (0, ada_j(j))),
            pl.BlockSpec((1, K_BLOCK, 2 * Q_BLOCK), lambda j: (pair_j(j), 0, 0)),
            pl.BlockSpec((1, 2, n_new, K_BLOCK), lambda j: (pair_j(j), 0, 0, 0)),
            pl.BlockSpec((D_MODEL, PREP_COLS), lambda j: (0, j)),
            pl.BlockSpec((D_MODEL, WOUT_COLS), lambda j: (0, pair_j(j))),
            pl.BlockSpec((MLP_CHUNK, A_WIDTH), lambda j: (0, 0)),
        ],
        out_shape=[
            jax.ShapeDtypeStruct((bsz, 3 * D_MODEL), f32),
            jax.ShapeDtypeStruct((n_streams, 3 * D_MODEL), f32),
            jax.ShapeDtypeStruct((N_PAIRS, K_BLOCK, 2 * Q_BLOCK), f32),
            jax.ShapeDtypeStruct((N_PAIRS, 2, n_new, K_BLOCK), f32),
            jax.ShapeDtypeStruct((D_MODEL, D_IN), bf16),
            jax.ShapeDtypeStruct((D_MODEL, D_MODEL), bf16),
            jax.ShapeDtypeStruct((MLP_CHUNK, A_WIDTH), f32),
        ],
        compiler_params=pltpu.CompilerParams(dimension_semantics=("arbitrary",)),
        name="prep",
    )(c_prompt, c_sample, w_ada, b_ada.reshape(1, 3 * D_MODEL), base.reshape(N_PAIRS, 2, K_BLOCK),
      w_in, w_out, b_s)


def _tril_pairs(ws_ref, wt_ref, n):
    row = lax.broadcasted_iota(jnp.int32, (n, n), 0)
    col = lax.broadcasted_iota(jnp.int32, (n, n), 1)
    keep = col <= row
    for g in range(A_GROUPS):
        w = jnp.where(keep, ws_ref[g, :n, :n], 0.0)
        wt_ref[g // 2, (g % 2) * n:(g % 2 + 1) * n, :] = w.astype(jnp.bfloat16)


def _mod_rows(mod_ref, row):
    return tuple(mod_ref[row, i * D_MODEL:(i + 1) * D_MODEL] for i in range(3))


def _rms_rows(x):
    return x * lax.rsqrt(jnp.mean(x * x, axis=-1, keepdims=True) + EPS)


def _layernorm(x, g, b):
    mu = jnp.mean(x, axis=-1, keepdims=True)
    xc = x - mu
    var = jnp.mean(xc * xc, axis=-1, keepdims=True)
    return xc * lax.rsqrt(var + EPS) * g + b


def _pair_select(stacked, n):
    return jnp.where(_even_lane_mask((n, LANES)), stacked[:n], stacked[n:])


def _split_heads(x):
    return x.reshape(x.shape[0], B_HEADS, B_HDIM)


def _to_feature_major(x):
    return x.T.reshape(B_HEADS, B_HDIM, x.shape[0])


def _store_q(q, qe_ref, qo_ref):
    q = (q * Q_SCALE).astype(jnp.bfloat16)
    even = _even_lane_mask(q.shape)
    zero = jnp.zeros((), jnp.bfloat16)
    qe_ref[...] = jnp.where(even, q, zero)
    qo_ref[...] = jnp.where(even, zero, q)


def _prompt_kernel(x_ref, mod_ref, gpre_ref, win_ref, lng_ref, lnb_ref, ws_ref, bfull_ref,
                   biast_ref, wout_ref, gpost_ref,
                   y_ref, klast_ref, vlast_ref,
                   h_s, usg_s, va_s, qe_s, qo_s, k_s, vt_s, sgb_s, o_s, wt_s, st_s, kf_s, vf_s):
    t = pl.program_id(1)
    nt = pl.num_programs(1)
    T = SEQ_TILE

    @pl.when(jnp.logical_and(pl.program_id(0) == 0, t == 0))
    def _():
        _tril_pairs(ws_ref, wt_s, MLP_CHUNK)

    shift, scale, gate = _mod_rows(mod_ref, pl.ds(pl.program_id(0), 1))

    pre_scale = gpre_ref[...] * (1.0 + scale)

    def proj(c0, rows):
        return jnp.dot(h_s[rows, :], win_ref[:, c0:c0 + 512], preferred_element_type=jnp.float32)

    def proj_pieces(lo):
        rb = slice(lo, lo + PROJ_ROWS)
        halves = [slice(lo, lo + PROJ_ROWS // 2), slice(lo + PROJ_ROWS // 2, lo + PROJ_ROWS)]
        hist = slice(KV_WIN + lo, KV_WIN + lo + PROJ_ROWS)
        keep = slice(lo - (T - KV_WIN), lo - (T - KV_WIN) + PROJ_ROWS) if lo >= T - KV_WIN else None

        def pre_norm():
            for rows in halves:
                h_s[rows, :] = (_rms_rows(x_ref[0, rows, :]) * pre_scale + shift).astype(jnp.bfloat16)

        def u_half(rows):
            usg_s[rows, :] = _gelu(proj(C_U, rows))

        def v_block():
            v = proj(C_VV, rb)
            vt_s[:, hist] = v.T.astype(jnp.bfloat16)
            if keep is not None:
                vf_s[keep, :] = v

        def va_block():
            va_s[rb, :] = _layernorm(_gelu(proj(C_V, rb)), lng_ref[...], lnb_ref[...]
                                     ).astype(jnp.bfloat16)

        def q_block():
            _store_q(proj(C_Q, rb), qe_s.at[rb, :], qo_s.at[rb, :])

        def ga_block():
            usg_s[rb, :] = usg_s[rb, :] * _silu(proj(C_GA, rb))

        def gmlp(p):
            lanes = slice(p * LANES, (p + 1) * LANES)
            for c0 in range(lo, lo + PROJ_ROWS, 2 * MLP_CHUNK):
                chunks = [slice(c0, c0 + MLP_CHUNK), slice(c0 + MLP_CHUNK, c0 + 2 * MLP_CHUNK)]
                slabs = jnp.concatenate([va_s[rows, lanes] for rows in chunks], axis=1)
                mix = jnp.dot(wt_s[p], slabs, preferred_element_type=jnp.float32)
                for i, rows in enumerate(chunks):
                    mixed = _pair_select(mix[:, i * LANES:(i + 1) * LANES], MLP_CHUNK)
                    o_s[rows, lanes] = (usg_s[rows, lanes] * (mixed + bfull_ref[:, lanes])
                                        ).astype(jnp.bfloat16)

        def gb_block():
            sgb_s[rb, :] = _silu(proj(C_GB, rb))

        def k_block():
            k = proj(C_K, rb)
            k_s[hist, :] = k.astype(jnp.bfloat16)
            if keep is not None:
                kf_s[keep, :] = k

        g = [functools.partial(gmlp, p) for p in range(N_PAIRS)]
        return [pre_norm, functools.partial(u_half, halves[0]), functools.partial(u_half, halves[1]),
                v_block, va_block, ga_block, g[0], q_block, g[1], gb_block, g[2], k_block, g[3]]

    def key_lo(qb, first_tile):
        return max(KV_WIN - qb * Q_BLOCK, 0) if first_tile else 0

    def scores(qb, p, slot, first_tile):
        r0, lo = qb * Q_BLOCK, key_lo(qb, first_tile)
        lanes = slice(p * LANES, (p + 1) * LANES)
        qs = jnp.concatenate([qe_s[r0:r0 + Q_BLOCK, lanes],
                              qo_s[r0:r0 + Q_BLOCK, lanes]], axis=0)
        half = (K_BLOCK - lo) // 2
        for a in (lo, lo + half):
            st_s[slot, a:a + half, :] = lax.dot_general(
                k_s[r0 + a:r0 + a + half, lanes], qs, _CONTRACT_LANES,
                preferred_element_type=jnp.float32) + biast_ref[p, a:a + half, :]

    def finish(qb, p, slot, first_tile):
        r0, lo = qb * Q_BLOCK, key_lo(qb, first_tile)
        st = st_s[slot, lo:K_BLOCK, :]
        m = jnp.max(st, axis=0, keepdims=True)
        pt = jnp.exp2(st - m).astype(jnp.bfloat16)
        ones = jnp.ones((BF16_ROWS, K_BLOCK - lo), jnp.bfloat16)
        lhs = jnp.concatenate([vt_s[p * LANES:(p + 1) * LANES, r0 + lo:r0 + K_BLOCK], ones], axis=0)
        ot = jnp.dot(lhs, pt, preferred_element_type=jnp.float32)
        inv_l = 1.0 / ot[LANES:LANES + 1, :]
        ots = [ot[par * B_HDIM:(par + 1) * B_HDIM, par * Q_BLOCK:(par + 1) * Q_BLOCK]
               * inv_l[:, par * Q_BLOCK:(par + 1) * Q_BLOCK] for par in range(2)]
        yb = jnp.concatenate(ots, axis=0).T * sgb_s[r0:r0 + Q_BLOCK, p * LANES:(p + 1) * LANES]
        o_s[r0:r0 + Q_BLOCK, B_WIDTH + p * LANES:B_WIDTH + (p + 1) * LANES] = (
            yb.astype(jnp.bfloat16))

    post_scale = gate * gpost_ref[...]

    def out_block(lo, n):
        rows = slice(lo, lo + n)
        o = jnp.dot(o_s[rows, :], wout_ref[...], preferred_element_type=jnp.float32)
        y_ref[0, rows, :] = x_ref[0, rows, :] + _rms_rows(o) * post_scale

    def attend(first_tile):
        items = [(qb, p) for qb in range(T // Q_BLOCK) for p in range(N_PAIRS)]
        per_block = (PROJ_ROWS // Q_BLOCK) * N_PAIRS
        n_blocks = T // PROJ_ROWS
        for piece in proj_pieces(0):
            piece()
        pending = []
        for j in range(SCORE_LEAD):
            scores(*items[j], j % SCORE_SLOTS, first_tile)
        for i, (qb, p) in enumerate(items):
            blk, pos = divmod(i, per_block)
            if pos == 0 and blk + 1 < n_blocks:
                pending = proj_pieces((blk + 1) * PROJ_ROWS)
            if i + SCORE_LEAD < len(items):
                assert pos < per_block - SCORE_LEAD or not pending
                scores(*items[i + SCORE_LEAD], (i + SCORE_LEAD) % SCORE_SLOTS, first_tile)
            finish(qb, p, i % SCORE_SLOTS, first_tile)
            slots_left = per_block - SCORE_LEAD - pos
            if pending and slots_left > 0:
                n_now = -(-len(pending) // slots_left)
                for piece in pending[:n_now]:
                    piece()
                pending = pending[n_now:]
            rows_done = (qb + 1) * Q_BLOCK
            if p == N_PAIRS - 1 and rows_done % POST_ROWS == 0:
                out_block(rows_done - POST_ROWS, POST_ROWS)

    @pl.when(t == 0)
    def _():
        attend(True)

    @pl.when(t > 0)
    def _():
        attend(False)

    k_s[0:KV_WIN, :] = k_s[T:T + KV_WIN, :]
    vt_s[:, 0:KV_WIN] = vt_s[:, T:T + KV_WIN]

    @pl.when(t == nt - 1)
    def _():
        klast_ref[0] = _to_feature_major(kf_s[...])
        vlast_ref[0] = _to_feature_major(vf_s[...])


def _const_spec(shape):
    nd = len(shape)
    return pl.BlockSpec(shape, lambda b, t: (0,) * nd, pipeline_mode=pl.Buffered(1))


def _prompt_call(x, mod, g_pre, w_in, ln_g, ln_b, w_s, b_full, bias_t, w_out, g_post):
    bsz, seq, _ = x.shape
    T = SEQ_TILE
    assert seq % T == 0 and T % PROJ_ROWS == 0 and KV_WIN % PROJ_ROWS == 0 and T % POST_ROWS == 0
    assert PROJ_ROWS % (2 * MLP_CHUNK) == 0
    nt = seq // T
    f32, bf16 = jnp.float32, jnp.bfloat16
    return pl.pallas_call(
        _prompt_kernel,
        grid=(bsz, nt),
        in_specs=[
            pl.BlockSpec((1, T, D_MODEL), lambda b, t: (b, t, 0)),
            _const_spec((bsz, 3 * D_MODEL)),
            _const_spec((1, D_MODEL)),
            _const_spec((D_MODEL, D_IN)),
            _const_spec((1, A_WIDTH)),
            _const_spec((1, A_WIDTH)),
            _const_spec((A_GROUPS, MLP_CHUNK, MLP_CHUNK)),
            _const_spec((MLP_CHUNK, A_WIDTH)),
            _const_spec((N_PAIRS, K_BLOCK, 2 * Q_BLOCK)),
            _const_spec((D_MODEL, D_MODEL)),
            _const_spec((1, D_MODEL)),
        ],
        out_specs=[
            pl.BlockSpec((1, T, D_MODEL), lambda b, t: (b, t, 0)),
            pl.BlockSpec((1, B_HEADS, B_HDIM, KV_WIN), lambda b, t: (b, 0, 0, 0)),
            pl.BlockSpec((1, B_HEADS, B_HDIM, KV_WIN), lambda b, t: (b, 0, 0, 0)),
        ],
        out_shape=[
            jax.ShapeDtypeStruct((bsz, seq, D_MODEL), f32),
            jax.ShapeDtypeStruct((bsz, B_HEADS, B_HDIM, KV_WIN), f32),
            jax.ShapeDtypeStruct((bsz, B_HEADS, B_HDIM, KV_WIN), f32),
        ],
        scratch_shapes=[
            pltpu.VMEM((T, D_MODEL), bf16),
            pltpu.VMEM((T, A_WIDTH), f32),
            pltpu.VMEM((T, A_WIDTH), bf16),
            pltpu.VMEM((T, B_WIDTH), bf16),
            pltpu.VMEM((T, B_WIDTH), bf16),
            pltpu.VMEM((KV_WIN + T, B_WIDTH), bf16),
            pltpu.VMEM((B_WIDTH, KV_WIN + T), bf16),
            pltpu.VMEM((T, B_WIDTH), f32),
            pltpu.VMEM((T, D_MODEL), bf16),
            pltpu.VMEM((N_PAIRS, 2 * MLP_CHUNK, MLP_CHUNK), bf16),
            pltpu.VMEM((SCORE_SLOTS, K_BLOCK, 2 * Q_BLOCK), f32),
            pltpu.VMEM((KV_WIN, B_WIDTH), f32),
            pltpu.VMEM((KV_WIN, B_WIDTH), f32),
        ],
        compiler_params=pltpu.CompilerParams(
            dimension_semantics=("arbitrary", "arbitrary"),
            vmem_limit_bytes=VMEM_LIMIT_BYTES),
        name="prompt",
    )(x, mod, g_pre, w_in, ln_g, ln_b, w_s, b_full, bias_t, w_out, g_post)


def _sample_kernel(x_ref, mod_ref, gpre_ref, win_ref, lng_ref, lnb_ref, ws_ref, bfull_ref,
                   bias_ref, wout_ref, gpost_ref, ck_ref, cv_ref,
                   y_ref, knew_ref, vnew_ref, vanew_ref,
                   usg_s, va_s, qe_s, qo_s, k_s, v_s, sgb_s, o_s, wt_s, *, n_streams, n_new):
    b = pl.program_id(0)
    S = n_new

    @pl.when(b == 0)
    def _():
        _tril_pairs(ws_ref, wt_s, S)
        gpre = gpre_ref[...]
        hs = []
        for i in range(n_streams):
            xi = x_ref[i * S:(i + 1) * S, :]
            shift, scale, _ = _mod_rows(mod_ref, slice(i, i + 1))
            hs.append(_rms_rows(xi) * (gpre * (1.0 + scale)) + shift)
        h = jnp.concatenate(hs, axis=0).astype(jnp.bfloat16)

        def proj(c0):
            return jnp.dot(h, win_ref[:, c0:c0 + 512], preferred_element_type=jnp.float32)

        va = _layernorm(_gelu(proj(C_V)), lng_ref[...], lnb_ref[...])
        vanew_ref[...] = va
        va_s[...] = va.astype(jnp.bfloat16)
        usg_s[...] = _gelu(proj(C_U)) * _silu(proj(C_GA))
        _store_q(proj(C_Q), qe_s, qo_s)
        k = proj(C_K)
        v = proj(C_VV)
        knew_ref[...] = _split_heads(k)
        vnew_ref[...] = _split_heads(v)
        k_s[...] = k.astype(jnp.bfloat16)
        v_s[...] = v.astype(jnp.bfloat16)
        sgb_s[...] = _silu(proj(C_GB))

    r0 = pl.multiple_of(b * S, S)
    rows = pl.ds(r0, S)
    pair_lanes = [slice(p * LANES, (p + 1) * LANES) for p in range(N_PAIRS)]
    for p, lanes in enumerate(pair_lanes):
        mix = jnp.dot(wt_s[p], va_s[rows, lanes], preferred_element_type=jnp.float32)
        mixed = _pair_select(mix, S) + bfull_ref[0:S, lanes]
        o_s[rows, lanes] = (usg_s[rows, lanes] * mixed).astype(jnp.bfloat16)

    scores = []
    for p, lanes in enumerate(pair_lanes):
        qs = jnp.concatenate([qe_s[rows, lanes], qo_s[rows, lanes]], axis=0)
        kct = ck_ref[0, 2 * p:2 * p + 2].reshape(LANES, KV_WIN).astype(jnp.bfloat16)
        bias = jnp.concatenate([bias_ref[p, 0], bias_ref[p, 1]], axis=0)
        s1 = jnp.dot(qs, kct, preferred_element_type=jnp.float32) + bias[:, 0:KV_WIN]
        s2 = lax.dot_general(qs, k_s[rows, lanes], _CONTRACT_LANES,
                             preferred_element_type=jnp.float32) + bias[:, KV_WIN:KV_WIN + S]
        scores.append((s1, s2))
    weights = []
    for s1, s2 in scores:
        m = jnp.maximum(jnp.max(s1, axis=-1, keepdims=True), jnp.max(s2, axis=-1, keepdims=True))
        e1 = jnp.exp2(s1 - m)
        e2 = jnp.exp2(s2 - m)
        l = jnp.sum(e1, axis=-1, keepdims=True) + jnp.sum(e2, axis=-1, keepdims=True)
        weights.append((e1.astype(jnp.bfloat16), e2.astype(jnp.bfloat16), 1.0 / l))
    for p, lanes in enumerate(pair_lanes):
        e1, e2, inv_l = weights[p]
        vct = cv_ref[0, 2 * p:2 * p + 2].reshape(LANES, KV_WIN).astype(jnp.bfloat16)
        o2 = (lax.dot_general(e1, vct, _CONTRACT_LANES, preferred_element_type=jnp.float32)
              + jnp.dot(e2, v_s[rows, lanes], preferred_element_type=jnp.float32))
        yb = _pair_select(o2 * inv_l, S) * sgb_s[rows, lanes]
        o_s[rows, B_WIDTH + p * LANES:B_WIDTH + (p + 1) * LANES] = yb.astype(jnp.bfloat16)

    @pl.when(b == n_streams - 1)
    def _():
        o = jnp.dot(o_s[...], wout_ref[...], preferred_element_type=jnp.float32)
        on = _rms_rows(o) * gpost_ref[...]
        for i in range(n_streams):
            sl = slice(i * S, (i + 1) * S)
            y_ref[sl, :] = x_ref[sl, :] + _mod_rows(mod_ref, slice(i, i + 1))[2] * on[sl, :]


def _sample_call(x, mod, g_pre, w_in, ln_g, ln_b, w_s, b_full, bias_s, w_out, g_post, ck, cv):
    n_streams, n_new, _ = x.shape
    assert n_new % BF16_ROWS == 0 and n_new <= CHUNK
    rows = n_streams * n_new
    f32, bf16 = jnp.float32, jnp.bfloat16

    def const(shape):
        nd = len(shape)
        return pl.BlockSpec(shape, lambda b: (0,) * nd, pipeline_mode=pl.Buffered(1))

    kern = functools.partial(_sample_kernel, n_streams=n_streams, n_new=n_new)
    return pl.pallas_call(
        kern,
        grid=(n_streams,),
        in_specs=[
            const((rows, D_MODEL)),
            const((n_streams, 3 * D_MODEL)),
            const((1, D_MODEL)),
            const((D_MODEL, D_IN)),
            const((1, A_WIDTH)),
            const((1, A_WIDTH)),
            const((A_GROUPS, MLP_CHUNK, MLP_CHUNK)),
            const((MLP_CHUNK, A_WIDTH)),
            const((N_PAIRS, 2, n_new, K_BLOCK)),
            const((D_MODEL, D_MODEL)),
            const((1, D_MODEL)),
            pl.BlockSpec((1, B_HEADS, B_HDIM, KV_WIN), lambda b: (b, 0, 0, 0)),
            pl.BlockSpec((1, B_HEADS, B_HDIM, KV_WIN), lambda b: (b, 0, 0, 0)),
        ],
        out_specs=[
            pl.BlockSpec((rows, D_MODEL), lambda b: (0, 0)),
            pl.BlockSpec((rows, B_HEADS, B_HDIM), lambda b: (0, 0, 0)),
            pl.BlockSpec((rows, B_HEADS, B_HDIM), lambda b: (0, 0, 0)),
            pl.BlockSpec((rows, A_WIDTH), lambda b: (0, 0)),
        ],
        out_shape=[
            jax.ShapeDtypeStruct((rows, D_MODEL), f32),
            jax.ShapeDtypeStruct((rows, B_HEADS, B_HDIM), f32),
            jax.ShapeDtypeStruct((rows, B_HEADS, B_HDIM), f32),
            jax.ShapeDtypeStruct((rows, A_WIDTH), f32),
        ],
        scratch_shapes=[
            pltpu.VMEM((rows, A_WIDTH), f32),
            pltpu.VMEM((rows, A_WIDTH), bf16),
            pltpu.VMEM((rows, B_WIDTH), bf16),
            pltpu.VMEM((rows, B_WIDTH), bf16),
            pltpu.VMEM((rows, B_WIDTH), bf16),
            pltpu.VMEM((rows, B_WIDTH), bf16),
            pltpu.VMEM((rows, B_WIDTH), f32),
            pltpu.VMEM((rows, D_MODEL), bf16),
            pltpu.VMEM((N_PAIRS, 2 * n_new, n_new), bf16),
        ],
        compiler_params=pltpu.CompilerParams(
            dimension_semantics=("arbitrary",),
            vmem_limit_bytes=VMEM_LIMIT_BYTES),
        name="sample",
    )(x.reshape(rows, D_MODEL), mod, g_pre, w_in, ln_g, ln_b, w_s, b_full, bias_s, w_out, g_post,
      ck, cv)


def _bias_base(rel_bias):
    n_far = KV_WIN - REL_CLIP
    far = jnp.broadcast_to(rel_bias[:, N_REL - 1:N_REL], (B_HEADS, n_far))
    near = rel_bias[:, N_REL - 1:0:-1]
    return jnp.concatenate([far, near], axis=1)


def kernel(x_prompt, x_sample, cache_attn_k, cache_attn_v, c_prompt, c_sample, g_pre, w_ada, b_ada,
           w_in, ln_g, ln_b, w_s, b_s, rel_bias, w_out, g_post):
    depth = g_pre.shape[0]
    bsz, seq, _ = x_prompt.shape
    n_streams, n_new, _ = x_sample.shape
    win = cache_attn_k.shape[2]
    assert win == KV_WIN

    yp, ys = x_prompt, x_sample
    kp_rows, vp_rows, ks_rows, vs_rows, va_rows = [], [], [], [], []
    for l in range(depth):
        mod_p, mod_s, bias_t, bias_s, w_in_b, w_out_b, b_full = _prep_call(
            c_prompt, c_sample, w_ada[l], b_ada[l], _bias_base(rel_bias[l]), w_in[l], w_out[l],
            b_s[l], n_new)
        pre = (g_pre[l].reshape(1, D_MODEL), w_in_b, ln_g[l].reshape(1, A_WIDTH),
               ln_b[l].reshape(1, A_WIDTH), w_s[l], b_full)
        post = (w_out_b, g_post[l].reshape(1, D_MODEL))

        yp, k_last, v_last = _prompt_call(yp, mod_p, *pre, bias_t, *post)
        kp_rows.append(jnp.transpose(k_last, _FROM_FEATURE_MAJOR))
        vp_rows.append(jnp.transpose(v_last, _FROM_FEATURE_MAJOR))

        ys2, k_new, v_new, va_new = _sample_call(ys, mod_s, *pre, bias_s, *post,
                                                 jnp.transpose(cache_attn_k[l], _TO_FEATURE_MAJOR),
                                                 jnp.transpose(cache_attn_v[l], _TO_FEATURE_MAJOR))
        ys = ys2.reshape(n_streams, n_new, D_MODEL)
        ks_rows.append(k_new.reshape(n_streams, n_new, B_HEADS, B_HDIM))
        vs_rows.append(v_new.reshape(n_streams, n_new, B_HEADS, B_HDIM))
        va_rows.append(va_new.reshape(n_streams, n_new, A_WIDTH))

    return (yp, ys, jnp.stack(kp_rows), jnp.stack(vp_rows), jnp.stack(ks_rows), jnp.stack(vs_rows),
            jnp.stack(va_rows))
```

```python
import functools

import jax
import jax.numpy as jnp
from jax import lax
from jax.experimental import pallas as pl
from jax.experimental.pallas import tpu as pltpu

LANES = 128
SUBLANES = 8
BF16_ROWS = 16
VMEM_LIMIT_BYTES = 60 * 1024 * 1024

D_MODEL = 1024
A_WIDTH = 512
A_GROUPS = 8
A_GDIM = A_WIDTH // A_GROUPS
MLP_CHUNK = 128
B_WIDTH = 512
B_HEADS = 8
B_HDIM = B_WIDTH // B_HEADS
CHUNK = 64
KV_WIN = 512
REL_CLIP = 128
N_REL = 2 * REL_CLIP + 1
EPS = 1e-6
D_IN = 3 * A_WIDTH + 4 * B_WIDTH
NEG = -1e30
LOG2E = 1.4426950408889634
Q_SCALE = B_HDIM ** -0.5 * LOG2E

N_PAIRS = B_HEADS // 2
Q_BLOCK = 2 * CHUNK
K_BLOCK = KV_WIN + Q_BLOCK
SEQ_TILE = 1024
PROJ_ROWS = 256
POST_ROWS = 256
SCORE_LEAD = 1
SCORE_SLOTS = SCORE_LEAD + 1

C_U, C_V, C_GA, C_Q, C_K, C_VV, C_GB = (i * 512 for i in range(7))

_CONTRACT_LANES = (((1,), (1,)), ((), ()))

_TO_FEATURE_MAJOR = (0, 2, 3, 1)
_FROM_FEATURE_MAJOR = (0, 3, 1, 2)


def _sigmoid_exp2(neg_arg_log2):
    return 1.0 / (1.0 + jnp.exp2(neg_arg_log2))


def _gelu(x):
    c1 = -2.0 * (2.0 / jnp.pi) ** 0.5 * LOG2E
    return x * _sigmoid_exp2(x * (c1 + (c1 * 0.044715) * (x * x)))


def _silu(x):
    return x * _sigmoid_exp2(x * (-LOG2E))


def _even_lane_mask(shape):
    lane = lax.broadcasted_iota(jnp.int32, shape, len(shape) - 1)
    return (lane % LANES) < B_HDIM


def _ada_block(c_ref, w_ref, b_ref, o_ref):
    c = c_ref[...]
    o_ref[...] = jnp.dot(_silu(c), w_ref[...], preferred_element_type=jnp.float32) + b_ref[...]


def _relb_block(base_ref, bt_ref, bs_ref, n_new):
    row = lax.broadcasted_iota(jnp.int32, (Q_BLOCK, K_BLOCK), 0)
    col = lax.broadcasted_iota(jnp.int32, (Q_BLOCK, K_BLOCK), 1)
    band_lo = jnp.where(row < CHUNK, 0, CHUNK)
    rel = col - band_lo
    outside = jnp.logical_or(rel < 0, rel >= KV_WIN + CHUNK)
    for par in range(2):
        base = base_ref[0, par:par + 1, :] * LOG2E
        x = jnp.broadcast_to(base, (Q_BLOCK, K_BLOCK))
        shift = 1
        while shift < Q_BLOCK:
            x = jnp.where((row & shift) != 0, jnp.roll(x, shift, axis=1), x)
            shift *= 2
        far = jnp.broadcast_to(base[:, 0:1], (Q_BLOCK, K_BLOCK))
        x = jnp.where(col < row, far, x)
        bs_ref[0, par] = x[0:n_new, :]
        bt_ref[0, :, par * Q_BLOCK:(par + 1) * Q_BLOCK] = jnp.where(outside, NEG, x).T


PREP_COLS = 512
N_PREP = D_IN // PREP_COLS
N_ADA = 3 * D_MODEL // PREP_COLS
WOUT_COLS = D_MODEL // N_PAIRS


def _prep_kernel(cp_ref, cs_ref, wada_ref, bada_ref, base_ref, win_ref, wout_ref, bs_ref,
                 modp_ref, mods_ref, bt_ref, bsam_ref, winb_ref, woutb_ref, bfull_ref, *, n_new):
    j = pl.program_id(0)
    winb_ref[...] = win_ref[...].astype(jnp.bfloat16)

    @pl.when(j == 0)
    def _():
        bst = bs_ref[...].T
        even = _even_lane_mask((MLP_CHUNK, LANES))
        for p in range(N_PAIRS):
            bfull_ref[:, p * LANES:(p + 1) * LANES] = jnp.where(
                even, bst[:, 2 * p:2 * p + 1], bst[:, 2 * p + 1:2 * p + 2])

    @pl.when(j < N_ADA)
    def _():
        _ada_block(cp_ref, wada_ref, bada_ref, modp_ref)
        _ada_block(cs_ref, wada_ref, bada_ref, mods_ref)

    @pl.when(j < N_PAIRS)
    def _():
        woutb_ref[...] = wout_ref[...].astype(jnp.bfloat16)
        _relb_block(base_ref, bt_ref, bsam_ref, n_new)


def _prep_call(c_prompt, c_sample, w_ada, b_ada, base, w_in, w_out, b_s, n_new):
    f32, bf16 = jnp.float32, jnp.bfloat16
    bsz, n_streams = c_prompt.shape[0], c_sample.shape[0]

    def upto(n):
        return lambda j: jnp.minimum(j, n - 1)

    ada_j, pair_j = upto(N_ADA), upto(N_PAIRS)
    return pl.pallas_call(
        functools.partial(_prep_kernel, n_new=n_new),
        grid=(N_PREP,),
        in_specs=[
            pl.BlockSpec((bsz, D_MODEL), lambda j: (0, 0)),
            pl.BlockSpec((n_streams, D_MODEL), lambda j: (0, 0)),
            pl.BlockSpec((D_MODEL, PREP_COLS), lambda j: (0, ada_j(j))),
            pl.BlockSpec((1, PREP_COLS), lambda j: (0, ada_j(j))),
            pl.BlockSpec((1, 2, K_BLOCK), lambda j: (pair_j(j), 0, 0)),
            pl.BlockSpec((D_MODEL, PREP_COLS), lambda j: (0, j)),
            pl.BlockSpec((D_MODEL, WOUT_COLS), lambda j: (0, pair_j(j))),
            pl.BlockSpec((A_GROUPS, MLP_CHUNK), lambda j: (0, 0)),
        ],
        out_specs=[
            pl.BlockSpec((bsz, PREP_COLS), lambda j: (0, ada_j(j))),
            pl.BlockSpec((n_streams, PREP_COLS), lambda j: (0, ada_j(j))),
            pl.BlockSpec((1, K_BLOCK, 2 * Q_BLOCK), lambda j: (pair_j(j), 0, 0)),
            pl.BlockSpec((1, 2, n_new, K_BLOCK), lambda j: (pair_j(j), 0, 0, 0)),
            pl.BlockSpec((D_MODEL, PREP_COLS), lambda j: (0, j)),
            pl.BlockSpec((D_MODEL, WOUT_COLS), lambda j: (0, pair_j(j))),
            pl.BlockSpec((MLP_CHUNK, A_WIDTH), lambda j: (0, 0)),
        ],
        out_shape=[
            jax.ShapeDtypeStruct((bsz, 3 * D_MODEL), f32),
            jax.ShapeDtypeStruct((n_streams, 3 * D_MODEL), f32),
            jax.ShapeDtypeStruct((N_PAIRS, K_BLOCK, 2 * Q_BLOCK), f32),
            jax.ShapeDtypeStruct((N_PAIRS, 2, n_new, K_BLOCK), f32),
            jax.ShapeDtypeStruct((D_MODEL, D_IN), bf16),
            jax.ShapeDtypeStruct((D_MODEL, D_MODEL), bf16),
            jax.ShapeDtypeStruct((MLP_CHUNK, A_WIDTH), f32),
        ],
        compiler_params=pltpu.CompilerParams(dimension_semantics=("arbitrary",)),
        name="prep",
    )(c_prompt, c_sample, w_ada, b_ada.reshape(1, 3 * D_MODEL), base.reshape(N_PAIRS, 2, K_BLOCK),
      w_in, w_out, b_s)


def _tril_pairs(ws_ref, wt_ref, n):
    row = lax.broadcasted_iota(jnp.int32, (n, n), 0)
    col = lax.broadcasted_iota(jnp.int32, (n, n), 1)
    keep = col <= row
    for g in range(A_GROUPS):
        w = jnp.where(keep, ws_ref[g, :n, :n], 0.0)
        wt_ref[g // 2, (g % 2) * n:(g % 2 + 1) * n, :] = w.astype(jnp.bfloat16)


def _mod_rows(mod_ref, row):
    return tuple(mod_ref[row, i * D_MODEL:(i + 1) * D_MODEL] for i in range(3))


def _rms_rows(x):
    return x * lax.rsqrt(jnp.mean(x * x, axis=-1, keepdims=True) + EPS)


def _layernorm(x, g, b):
    mu = jnp.mean(x, axis=-1, keepdims=True)
    xc = x - mu
    var = jnp.mean(xc * xc, axis=-1, keepdims=True)
    return xc * lax.rsqrt(var + EPS) * g + b


def _pair_select(stacked, n):
    return jnp.where(_even_lane_mask((n, LANES)), stacked[:n], stacked[n:])


def _split_heads(x):
    return x.reshape(x.shape[0], B_HEADS, B_HDIM)


def _to_feature_major(x):
    return x.T.reshape(B_HEADS, B_HDIM, x.shape[0])


def _store_q(q, qe_ref, qo_ref):
    q = (q * Q_SCALE).astype(jnp.bfloat16)
    even = _even_lane_mask(q.shape)
    zero = jnp.zeros((), jnp.bfloat16)
    qe_ref[...] = jnp.where(even, q, zero)
    qo_ref[...] = jnp.where(even, zero, q)


def _prompt_kernel(x_ref, mod_ref, gpre_ref, win_ref, lng_ref, lnb_ref, ws_ref, bfull_ref,
                   biast_ref, wout_ref, gpost_ref,
                   y_ref, klast_ref, vlast_ref,
                   h_s, usg_s, va_s, qe_s, qo_s, k_s, vt_s, sgb_s, o_s, wt_s, st_s, kf_s, vf_s):
    t = pl.program_id(1)
    nt = pl.num_programs(1)
    T = SEQ_TILE

    @pl.when(jnp.logical_and(pl.program_id(0) == 0, t == 0))
    def _():
        _tril_pairs(ws_ref, wt_s, MLP_CHUNK)

    shift, scale, gate = _mod_rows(mod_ref, pl.ds(pl.program_id(0), 1))

    pre_scale = gpre_ref[...] * (1.0 + scale)

    def proj(c0, rows):
        return jnp.dot(h_s[rows, :], win_ref[:, c0:c0 + 512], preferred_element_type=jnp.float32)

    def proj_pieces(lo):
        rb = slice(lo, lo + PROJ_ROWS)
        halves = [slice(lo, lo + PROJ_ROWS // 2), slice(lo + PROJ_ROWS // 2, lo + PROJ_ROWS)]
        hist = slice(KV_WIN + lo, KV_WIN + lo + PROJ_ROWS)
        keep = slice(lo - (T - KV_WIN), lo - (T - KV_WIN) + PROJ_ROWS) if lo >= T - KV_WIN else None

        def pre_norm():
            for rows in halves:
                h_s[rows, :] = (_rms_rows(x_ref[0, rows, :]) * pre_scale + shift).astype(jnp.bfloat16)

        def u_half(rows):
            usg_s[rows, :] = _gelu(proj(C_U, rows))

        def v_block():
            v = proj(C_VV, rb)
            vt_s[:, hist] = v.T.astype(jnp.bfloat16)
            if keep is not None:
                vf_s[keep, :] = v

        def va_block():
            va_s[rb, :] = _layernorm(_gelu(proj(C_V, rb)), lng_ref[...], lnb_ref[...]
                                     ).astype(jnp.bfloat16)

        def q_block():
            _store_q(proj(C_Q, rb), qe_s.at[rb, :], qo_s.at[rb, :])

        def ga_block():
            usg_s[rb, :] = usg_s[rb, :] * _silu(proj(C_GA, rb))

        def gmlp(p):
            lanes = slice(p * LANES, (p + 1) * LANES)
            for c0 in range(lo, lo + PROJ_ROWS, 2 * MLP_CHUNK):
                chunks = [slice(c0, c0 + MLP_CHUNK), slice(c0 + MLP_CHUNK, c0 + 2 * MLP_CHUNK)]
                slabs = jnp.concatenate([va_s[rows, lanes] for rows in chunks], axis=1)
                mix = jnp.dot(wt_s[p], slabs, preferred_element_type=jnp.float32)
                for i, rows in enumerate(chunks):
                    mixed = _pair_select(mix[:, i * LANES:(i + 1) * LANES], MLP_CHUNK)
                    o_s[rows, lanes] = (usg_s[rows, lanes] * (mixed + bfull_ref[:, lanes])
                                        ).astype(jnp.bfloat16)

        def gb_block():
            sgb_s[rb, :] = _silu(proj(C_GB, rb))

        def k_block():
            k = proj(C_K, rb)
            k_s[hist, :] = k.astype(jnp.bfloat16)
            if keep is not None:
                kf_s[keep, :] = k

        g = [functools.partial(gmlp, p) for p in range(N_PAIRS)]
        return [pre_norm, functools.partial(u_half, halves[0]), functools.partial(u_half, halves[1]),
                v_block, va_block, ga_block, g[0], q_block, g[1], gb_block, g[2], k_block, g[3]]

    def key_lo(qb, first_tile):
        return max(KV_WIN - qb * Q_BLOCK, 0) if first_tile else 0

    def scores(qb, p, slot, first_tile):
        r0, lo = qb * Q_BLOCK, key_lo(qb, first_tile)
        lanes = slice(p * LANES, (p + 1) * LANES)
        qs = jnp.concatenate([qe_s[r0:r0 + Q_BLOCK, lanes],
                              qo_s[r0:r0 + Q_BLOCK, lanes]], axis=0)
        half = (K_BLOCK - lo) // 2
        for a in (lo, lo + half):
            st_s[slot, a:a + half, :] = lax.dot_general(
                k_s[r0 + a:r0 + a + half, lanes], qs, _CONTRACT_LANES,
                preferred_element_type=jnp.float32) + biast_ref[p, a:a + half, :]

    def finish(qb, p, slot, first_tile):
        r0, lo = qb * Q_BLOCK, key_lo(qb, first_tile)
        st = st_s[slot, lo:K_BLOCK, :]
        m = jnp.max(st, axis=0, keepdims=True)
        pt = jnp.exp2(st - m).astype(jnp.bfloat16)
        ones = jnp.ones((BF16_ROWS, K_BLOCK - lo), jnp.bfloat16)
        lhs = jnp.concatenate([vt_s[p * LANES:(p + 1) * LANES, r0 + lo:r0 + K_BLOCK], ones], axis=0)
        ot = jnp.dot(lhs, pt, preferred_element_type=jnp.float32)
        inv_l = 1.0 / ot[LANES:LANES + 1, :]
        ots = [ot[par * B_HDIM:(par + 1) * B_HDIM, par * Q_BLOCK:(par + 1) * Q_BLOCK]
               * inv_l[:, par * Q_BLOCK:(par + 1) * Q_BLOCK] for par in range(2)]
        yb = jnp.concatenate(ots, axis=0).T * sgb_s[r0:r0 + Q_BLOCK, p * LANES:(p + 1) * LANES]
        o_s[r0:r0 + Q_BLOCK, B_WIDTH + p * LANES:B_WIDTH + (p + 1) * LANES] = (
            yb.astype(jnp.bfloat16))

    post_scale = gate * gpost_ref[...]

    def out_block(lo, n):
        rows = slice(lo, lo + n)
        o = jnp.dot(o_s[rows, :], wout_ref[...], preferred_element_type=jnp.float32)
        y_ref[0, rows, :] = x_ref[0, rows, :] + _rms_rows(o) * post_scale

    def attend(first_tile):
        items = [(qb, p) for qb in range(T // Q_BLOCK) for p in range(N_PAIRS)]
        per_block = (PROJ_ROWS // Q_BLOCK) * N_PAIRS
        n_blocks = T // PROJ_ROWS
        for piece in proj_pieces(0):
            piece()
        pending = []
        for j in range(SCORE_LEAD):
            scores(*items[j], j % SCORE_SLOTS, first_tile)
        for i, (qb, p) in enumerate(items):
            blk, pos = divmod(i, per_block)
            if pos == 0 and blk + 1 < n_blocks:
                pending = proj_pieces((blk + 1) * PROJ_ROWS)
            if i + SCORE_LEAD < len(items):
                assert pos < per_block - SCORE_LEAD or not pending
                scores(*items[i + SCORE_LEAD], (i + SCORE_LEAD) % SCORE_SLOTS, first_tile)
            finish(qb, p, i % SCORE_SLOTS, first_tile)
            slots_left = per_block - SCORE_LEAD - pos
            if pending and slots_left > 0:
                n_now = -(-len(pending) // slots_left)
                for piece in pending[:n_now]:
                    piece()
                pending = pending[n_now:]
            rows_done = (qb + 1) * Q_BLOCK
            if p == N_PAIRS - 1 and rows_done % POST_ROWS == 0:
                out_block(rows_done - POST_ROWS, POST_ROWS)

    @pl.when(t == 0)
    def _():
        attend(True)

    @pl.when(t > 0)
    def _():
        attend(False)

    k_s[0:KV_WIN, :] = k_s[T:T + KV_WIN, :]
    vt_s[:, 0:KV_WIN] = vt_s[:, T:T + KV_WIN]

    @pl.when(t == nt - 1)
    def _():
        klast_ref[0] = _to_feature_major(kf_s[...])
        vlast_ref[0] = _to_feature_major(vf_s[...])


def _const_spec(shape):
    nd = len(shape)
    return pl.BlockSpec(shape, lambda b, t: (0,) * nd, pipeline_mode=pl.Buffered(1))


def _prompt_call(x, mod, g_pre, w_in, ln_g, ln_b, w_s, b_full, bias_t, w_out, g_post):
    bsz, seq, _ = x.shape
    T = SEQ_TILE
    assert seq % T == 0 and T % PROJ_ROWS == 0 and KV_WIN % PROJ_ROWS == 0 and T % POST_ROWS == 0
    assert PROJ_ROWS % (2 * MLP_CHUNK) == 0
    nt = seq // T
    f32, bf16 = jnp.float32, jnp.bfloat16
    return pl.pallas_call(
        _prompt_kernel,
        grid=(bsz, nt),
        in_specs=[
            pl.BlockSpec((1, T, D_MODEL), lambda b, t: (b, t, 0)),
            _const_spec((bsz, 3 * D_MODEL)),
            _const_spec((1, D_MODEL)),
            _const_spec((D_MODEL, D_IN)),
            _const_spec((1, A_WIDTH)),
            _const_spec((1, A_WIDTH)),
            _const_spec((A_GROUPS, MLP_CHUNK, MLP_CHUNK)),
            _const_spec((MLP_CHUNK, A_WIDTH)),
            _const_spec((N_PAIRS, K_BLOCK, 2 * Q_BLOCK)),
            _const_spec((D_MODEL, D_MODEL)),
            _const_spec((1, D_MODEL)),
        ],
        out_specs=[
            pl.BlockSpec((1, T, D_MODEL), lambda b, t: (b, t, 0)),
            pl.BlockSpec((1, B_HEADS, B_HDIM, KV_WIN), lambda b, t: (b, 0, 0, 0)),
            pl.BlockSpec((1, B_HEADS, B_HDIM, KV_WIN), lambda b, t: (b, 0, 0, 0)),
        ],
        out_shape=[
            jax.ShapeDtypeStruct((bsz, seq, D_MODEL), f32),
            jax.ShapeDtypeStruct((bsz, B_HEADS, B_HDIM, KV_WIN), f32),
            jax.ShapeDtypeStruct((bsz, B_HEADS, B_HDIM, KV_WIN), f32),
        ],
        scratch_shapes=[
            pltpu.VMEM((T, D_MODEL), bf16),
            pltpu.VMEM((T, A_WIDTH), f32),
            pltpu.VMEM((T, A_WIDTH), bf16),
            pltpu.VMEM((T, B_WIDTH), bf16),
            pltpu.VMEM((T, B_WIDTH), bf16),
            pltpu.VMEM((KV_WIN + T, B_WIDTH), bf16),
            pltpu.VMEM((B_WIDTH, KV_WIN + T), bf16),
            pltpu.VMEM((T, B_WIDTH), f32),
            pltpu.VMEM((T, D_MODEL), bf16),
            pltpu.VMEM((N_PAIRS, 2 * MLP_CHUNK, MLP_CHUNK), bf16),
            pltpu.VMEM((SCORE_SLOTS, K_BLOCK, 2 * Q_BLOCK), f32),
            pltpu.VMEM((KV_WIN, B_WIDTH), f32),
            pltpu.VMEM((KV_WIN, B_WIDTH), f32),
        ],
        compiler_params=pltpu.CompilerParams(
            dimension_semantics=("arbitrary", "arbitrary"),
            vmem_limit_bytes=VMEM_LIMIT_BYTES),
        name="prompt",
    )(x, mod, g_pre, w_in, ln_g, ln_b, w_s, b_full, bias_t, w_out, g_post)


def _sample_kernel(x_ref, mod_ref, gpre_ref, win_ref, lng_ref, lnb_ref, ws_ref, bfull_ref,
                   bias_ref, wout_ref, gpost_ref, ck_ref, cv_ref,
                   y_ref, knew_ref, vnew_ref, vanew_ref,
                   usg_s, va_s, qe_s, qo_s, k_s, v_s, sgb_s, o_s, wt_s, *, n_streams, n_new):
    b = pl.program_id(0)
    S = n_new

    @pl.when(b == 0)
    def _():
        _tril_pairs(ws_ref, wt_s, S)
        gpre = gpre_ref[...]
        hs = []
        for i in range(n_streams):
            xi = x_ref[i * S:(i + 1) * S, :]
            shift, scale, _ = _mod_rows(mod_ref, slice(i, i + 1))
            hs.append(_rms_rows(xi) * (gpre * (1.0 + scale)) + shift)
        h = jnp.concatenate(hs, axis=0).astype(jnp.bfloat16)

        def proj(c0):
            return jnp.dot(h, win_ref[:, c0:c0 + 512], preferred_element_type=jnp.float32)

        va = _layernorm(_gelu(proj(C_V)), lng_ref[...], lnb_ref[...])
        vanew_ref[...] = va
        va_s[...] = va.astype(jnp.bfloat16)
        usg_s[...] = _gelu(proj(C_U)) * _silu(proj(C_GA))
        _store_q(proj(C_Q), qe_s, qo_s)
        k = proj(C_K)
        v = proj(C_VV)
        knew_ref[...] = _split_heads(k)
        vnew_ref[...] = _split_heads(v)
        k_s[...] = k.astype(jnp.bfloat16)
        v_s[...] = v.astype(jnp.bfloat16)
        sgb_s[...] = _silu(proj(C_GB))

    r0 = pl.multiple_of(b * S, S)
    rows = pl.ds(r0, S)
    pair_lanes = [slice(p * LANES, (p + 1) * LANES) for p in range(N_PAIRS)]
    for p, lanes in enumerate(pair_lanes):
        mix = jnp.dot(wt_s[p], va_s[rows, lanes], preferred_element_type=jnp.float32)
        mixed = _pair_select(mix, S) + bfull_ref[0:S, lanes]
        o_s[rows, lanes] = (usg_s[rows, lanes] * mixed).astype(jnp.bfloat16)

    scores = []
    for p, lanes in enumerate(pair_lanes):
        qs = jnp.concatenate([qe_s[rows, lanes], qo_s[rows, lanes]], axis=0)
        kct = ck_ref[0, 2 * p:2 * p + 2].reshape(LANES, KV_WIN).astype(jnp.bfloat16)
        bias = jnp.concatenate([bias_ref[p, 0], bias_ref[p, 1]], axis=0)
        s1 = jnp.dot(qs, kct, preferred_element_type=jnp.float32) + bias[:, 0:KV_WIN]
        s2 = lax.dot_general(qs, k_s[rows, lanes], _CONTRACT_LANES,
                             preferred_element_type=jnp.float32) + bias[:, KV_WIN:KV_WIN + S]
        scores.append((s1, s2))
    weights = []
    for s1, s2 in scores:
        m = jnp.maximum(jnp.max(s1, axis=-1, keepdims=True), jnp.max(s2, axis=-1, keepdims=True))
        e1 = jnp.exp2(s1 - m)
        e2 = jnp.exp2(s2 - m)
        l = jnp.sum(e1, axis=-1, keepdims=True) + jnp.sum(e2, axis=-1, keepdims=True)
        weights.append((e1.astype(jnp.bfloat16), e2.astype(jnp.bfloat16), 1.0 / l))
    for p, lanes in enumerate(pair_lanes):
        e1, e2, inv_l = weights[p]
        vct = cv_ref[0, 2 * p:2 * p + 2].reshape(LANES, KV_WIN).astype(jnp.bfloat16)
        o2 = (lax.dot_general(e1, vct, _CONTRACT_LANES, preferred_element_type=jnp.float32)
              + jnp.dot(e2, v_s[rows, lanes], preferred_element_type=jnp.float32))
        yb = _pair_select(o2 * inv_l, S) * sgb_s[rows, lanes]
        o_s[rows, B_WIDTH + p * LANES:B_WIDTH + (p + 1) * LANES] = yb.astype(jnp.bfloat16)

    @pl.when(b == n_streams - 1)
    def _():
        o = jnp.dot(o_s[...], wout_ref[...], preferred_element_type=jnp.float32)
        on = _rms_rows(o) * gpost_ref[...]
        for i in range(n_streams):
            sl = slice(i * S, (i + 1) * S)
            y_ref[sl, :] = x_ref[sl, :] + _mod_rows(mod_ref, slice(i, i + 1))[2] * on[sl, :]


def _sample_call(x, mod, g_pre, w_in, ln_g, ln_b, w_s, b_full, bias_s, w_out, g_post, ck, cv):
    n_streams, n_new, _ = x.shape
    assert n_new % BF16_ROWS == 0 and n_new <= CHUNK
    rows = n_streams * n_new
    f32, bf16 = jnp.float32, jnp.bfloat16

    def const(shape):
        nd = len(shape)
        return pl.BlockSpec(shape, lambda b: (0,) * nd, pipeline_mode=pl.Buffered(1))

    kern = functools.partial(_sample_kernel, n_streams=n_streams, n_new=n_new)
    return pl.pallas_call(
        kern,
        grid=(n_streams,),
        in_specs=[
            const((rows, D_MODEL)),
            const((n_streams, 3 * D_MODEL)),
            const((1, D_MODEL)),
            const((D_MODEL, D_IN)),
            const((1, A_WIDTH)),
            const((1, A_WIDTH)),
            const((A_GROUPS, MLP_CHUNK, MLP_CHUNK)),
            const((MLP_CHUNK, A_WIDTH)),
            const((N_PAIRS, 2, n_new, K_BLOCK)),
            const((D_MODEL, D_MODEL)),
            const((1, D_MODEL)),
            pl.BlockSpec((1, B_HEADS, B_HDIM, KV_WIN), lambda b: (b, 0, 0, 0)),
            pl.BlockSpec((1, B_HEADS, B_HDIM, KV_WIN), lambda b: (b, 0, 0, 0)),
        ],
        out_specs=[
            pl.BlockSpec((rows, D_MODEL), lambda b: (0, 0)),
            pl.BlockSpec((rows, B_HEADS, B_HDIM), lambda b: (0, 0, 0)),
            pl.BlockSpec((rows, B_HEADS, B_HDIM), lambda b: (0, 0, 0)),
            pl.BlockSpec((rows, A_WIDTH), lambda b: (0, 0)),
        ],
        out_shape=[
            jax.ShapeDtypeStruct((rows, D_MODEL), f32),
            jax.ShapeDtypeStruct((rows, B_HEADS, B_HDIM), f32),
            jax.ShapeDtypeStruct((rows, B_HEADS, B_HDIM), f32),
            jax.ShapeDtypeStruct((rows, A_WIDTH), f32),
        ],
        scratch_shapes=[
            pltpu.VMEM((rows, A_WIDTH), f32),
            pltpu.VMEM((rows, A_WIDTH), bf16),
            pltpu.VMEM((rows, B_WIDTH), bf16),
            pltpu.VMEM((rows, B_WIDTH), bf16),
            pltpu.VMEM((rows, B_WIDTH), bf16),
            pltpu.VMEM((rows, B_WIDTH), bf16),
            pltpu.VMEM((rows, B_WIDTH), f32),
            pltpu.VMEM((rows, D_MODEL), bf16),
            pltpu.VMEM((N_PAIRS, 2 * n_new, n_new), bf16),
        ],
        compiler_params=pltpu.CompilerParams(
            dimension_semantics=("arbitrary",),
            vmem_limit_bytes=VMEM_LIMIT_BYTES),
        name="sample",
    )(x.reshape(rows, D_MODEL), mod, g_pre, w_in, ln_g, ln_b, w_s, b_full, bias_s, w_out, g_post,
      ck, cv)


def _bias_base(rel_bias):
    n_far = KV_WIN - REL_CLIP
    far = jnp.broadcast_to(rel_bias[:, N_REL - 1:N_REL], (B_HEADS, n_far))
    near = rel_bias[:, N_REL - 1:0:-1]
    return jnp.concatenate([far, near], axis=1)


def kernel(x_prompt, x_sample, cache_attn_k, cache_attn_v, c_prompt, c_sample, g_pre, w_ada, b_ada,
           w_in, ln_g, ln_b, w_s, b_s, rel_bias, w_out, g_post):
    depth = g_pre.shape[0]
    bsz, seq, _ = x_prompt.shape
    n_streams, n_new, _ = x_sample.shape
    win = cache_attn_k.shape[2]
    assert win == KV_WIN

    yp, ys = x_prompt, x_sample
    kp_rows, vp_rows, ks_rows, vs_rows, va_rows = [], [], [], [], []
    for l in range(depth):
        mod_p, mod_s, bias_t, bias_s, w_in_b, w_out_b, b_full = _prep_call(
            c_prompt, c_sample, w_ada[l], b_ada[l], _bias_base(rel_bias[l]), w_in[l], w_out[l],
            b_s[l], n_new)
        pre = (g_pre[l].reshape(1, D_MODEL), w_in_b, ln_g[l].reshape(1, A_WIDTH),
               ln_b[l].reshape(1, A_WIDTH), w_s[l], b_full)
        post = (w_out_b, g_post[l].reshape(1, D_MODEL))

        yp, k_last, v_last = _prompt_call(yp, mod_p, *pre, bias_t, *post)
        kp_rows.append(jnp.transpose(k_last, _FROM_FEATURE_MAJOR))
        vp_rows.append(jnp.transpose(v_last, _FROM_FEATURE_MAJOR))

        ys2, k_new, v_new, va_new = _sample_call(ys, mod_s, *pre, bias_s, *post,
                                                 jnp.transpose(cache_attn_k[l], _TO_FEATURE_MAJOR),
                                                 jnp.transpose(cache_attn_v[l], _TO_FEATURE_MAJOR))
        ys = ys2.reshape(n_streams, n_new, D_MODEL)
        ks_rows.append(k_new.reshape(n_streams, n_new, B_HEADS, B_HDIM))
        vs_rows.append(v_new.reshape(n_streams, n_new, B_HEADS, B_HDIM))
        va_rows.append(va_new.reshape(n_streams, n_new, A_WIDTH))

    return (yp, ys, jnp.stack(kp_rows), jnp.stack(vp_rows), jnp.stack(ks_rows), jnp.stack(vs_rows),
            jnp.stack(va_rows))
```

```python
import functools

import jax
import jax.numpy as jnp
from jax import lax
from jax.experimental import pallas as pl
from jax.experimental.pallas import tpu as pltpu

LANES = 128
SUBLANES = 8
BF16_ROWS = 16
VMEM_LIMIT_BYTES = 60 * 1024 * 1024

D_MODEL = 1024
A_WIDTH = 512
A_GROUPS = 8
A_GDIM = A_WIDTH // A_GROUPS
MLP_CHUNK = 128
B_WIDTH = 512
B_HEADS = 8
B_HDIM = B_WIDTH // B_HEADS
CHUNK = 64
KV_WIN = 512
REL_CLIP = 128
N_REL = 2 * REL_CLIP + 1
EPS = 1e-6
D_IN = 3 * A_WIDTH + 4 * B_WIDTH
NEG = -1e30
LOG2E = 1.4426950408889634
Q_SCALE = B_HDIM ** -0.5 * LOG2E

N_PAIRS = B_HEADS // 2
Q_BLOCK = 2 * CHUNK
K_BLOCK = KV_WIN + Q_BLOCK
SEQ_TILE = 1024
PROJ_ROWS = 256
POST_ROWS = 256
SCORE_LEAD = 2
SCORE_SLOTS = SCORE_LEAD + 1

C_U, C_V, C_GA, C_Q, C_K, C_VV, C_GB = (i * 512 for i in range(7))

_CONTRACT_LANES = (((1,), (1,)), ((), ()))

_TO_FEATURE_MAJOR = (0, 2, 3, 1)
_FROM_FEATURE_MAJOR = (0, 3, 1, 2)


def _sigmoid_exp2(neg_arg_log2):
    return 1.0 / (1.0 + jnp.exp2(neg_arg_log2))


def _gelu(x):
    c1 = -2.0 * (2.0 / jnp.pi) ** 0.5 * LOG2E
    return x * _sigmoid_exp2(x * (c1 + (c1 * 0.044715) * (x * x)))


def _silu(x):
    return x * _sigmoid_exp2(x * (-LOG2E))


def _even_lane_mask(shape):
    lane = lax.broadcasted_iota(jnp.int32, shape, len(shape) - 1)
    return (lane % LANES) < B_HDIM


def _ada_block(c_ref, w_ref, b_ref, o_ref):
    c = c_ref[...]
    o_ref[...] = jnp.dot(_silu(c), w_ref[...], preferred_element_type=jnp.float32) + b_ref[...]


def _relb_block(base_ref, bt_ref, bs_ref, n_new):
    row = lax.broadcasted_iota(jnp.int32, (Q_BLOCK, K_BLOCK), 0)
    col = lax.broadcasted_iota(jnp.int32, (Q_BLOCK, K_BLOCK), 1)
    band_lo = jnp.where(row < CHUNK, 0, CHUNK)
    rel = col - band_lo
    outside = jnp.logical_or(rel < 0, rel >= KV_WIN + CHUNK)
    for par in range(2):
        base = base_ref[0, par:par + 1, :] * LOG2E
        x = jnp.broadcast_to(base, (Q_BLOCK, K_BLOCK))
        shift = 1
        while shift < Q_BLOCK:
            x = jnp.where((row & shift) != 0, jnp.roll(x, shift, axis=1), x)
            shift *= 2
        far = jnp.broadcast_to(base[:, 0:1], (Q_BLOCK, K_BLOCK))
        x = jnp.where(col < row, far, x)
        bs_ref[0, par] = x[0:n_new, :]
        bt_ref[0, :, par * Q_BLOCK:(par + 1) * Q_BLOCK] = jnp.where(outside, NEG, x).T


PREP_COLS = 512
N_PREP = D_IN // PREP_COLS
N_ADA = 3 * D_MODEL // PREP_COLS
WOUT_COLS = D_MODEL // N_PAIRS


def _prep_kernel(cp_ref, cs_ref, wada_ref, bada_ref, base_ref, win_ref, wout_ref, bs_ref,
                 modp_ref, mods_ref, bt_ref, bsam_ref, winb_ref, woutb_ref, bfull_ref, *, n_new):
    j = pl.program_id(0)
    winb_ref[...] = win_ref[...].astype(jnp.bfloat16)

    @pl.when(j == 0)
    def _():
        bst = bs_ref[...].T
        even = _even_lane_mask((MLP_CHUNK, LANES))
        for p in range(N_PAIRS):
            bfull_ref[:, p * LANES:(p + 1) * LANES] = jnp.where(
                even, bst[:, 2 * p:2 * p + 1], bst[:, 2 * p + 1:2 * p + 2])

    @pl.when(j < N_ADA)
    def _():
        _ada_block(cp_ref, wada_ref, bada_ref, modp_ref)
        _ada_block(cs_ref, wada_ref, bada_ref, mods_ref)

    @pl.when(j < N_PAIRS)
    def _():
        woutb_ref[...] = wout_ref[...].astype(jnp.bfloat16)
        _relb_block(base_ref, bt_ref, bsam_ref, n_new)


def _prep_call(c_prompt, c_sample, w_ada, b_ada, base, w_in, w_out, b_s, n_new):
    f32, bf16 = jnp.float32, jnp.bfloat16
    bsz, n_streams = c_prompt.shape[0], c_sample.shape[0]

    def upto(n):
        return lambda j: jnp.minimum(j, n - 1)

    ada_j, pair_j = upto(N_ADA), upto(N_PAIRS)
    return pl.pallas_call(
        functools.partial(_prep_kernel, n_new=n_new),
        grid=(N_PREP,),
        in_specs=[
            pl.BlockSpec((bsz, D_MODEL), lambda j: (0, 0)),
            pl.BlockSpec((n_streams, D_MODEL), lambda j: (0, 0)),
            pl.BlockSpec((D_MODEL, PREP_COLS), lambda j: (0, ada_j(j))),
            pl.BlockSpec((1, PREP_COLS), lambda j: (0, ada_j(j))),
            pl.BlockSpec((1, 2, K_BLOCK), lambda j: (pair_j(j), 0, 0)),
            pl.BlockSpec((D_MODEL, PREP_COLS), lambda j: (0, j)),
            pl.BlockSpec((D_MODEL, WOUT_COLS), lambda j: (0, pair_j(j))),
            pl.BlockSpec((A_GROUPS, MLP_CHUNK), lambda j: (0, 0)),
        ],
        out_specs=[
            pl.BlockSpec((bsz, PREP_COLS), lambda j: (0, ada_j(j))),
            pl.BlockSpec((n_streams, PREP_COLS), lambda j: (0, ada_j(j))),
            pl.BlockSpec((1, K_BLOCK, 2 * Q_BLOCK), lambda j: (pair_j(j), 0, 0)),
            pl.BlockSpec((1, 2, n_new, K_BLOCK), lambda j: (pair_j(j), 0, 0, 0)),
            pl.BlockSpec((D_MODEL, PREP_COLS), lambda j: (0, j)),
            pl.BlockSpec((D_MODEL, WOUT_COLS), lambda j: (0, pair_j(j))),
            pl.BlockSpec((MLP_CHUNK, A_WIDTH), lambda j: (0, 0)),
        ],
        out_shape=[
            jax.ShapeDtypeStruct((bsz, 3 * D_MODEL), f32),
            jax.ShapeDtypeStruct((n_streams, 3 * D_MODEL), f32),
            jax.ShapeDtypeStruct((N_PAIRS, K_BLOCK, 2 * Q_BLOCK), f32),
            jax.ShapeDtypeStruct((N_PAIRS, 2, n_new, K_BLOCK), f32),
            jax.ShapeDtypeStruct((D_MODEL, D_IN), bf16),
            jax.ShapeDtypeStruct((D_MODEL, D_MODEL), bf16),
            jax.ShapeDtypeStruct((MLP_CHUNK, A_WIDTH), f32),
        ],
        compiler_params=pltpu.CompilerParams(dimension_semantics=("arbitrary",)),
        name="prep",
    )(c_prompt, c_sample, w_ada, b_ada.reshape(1, 3 * D_MODEL), base.reshape(N_PAIRS, 2, K_BLOCK),
      w_in, w_out, b_s)


def _tril_pairs(ws_ref, wt_ref, n):
    row = lax.broadcasted_iota(jnp.int32, (n, n), 0)
    col = lax.broadcasted_iota(jnp.int32, (n, n), 1)
    keep = col <= row
    for g in range(A_GROUPS):
        w = jnp.where(keep, ws_ref[g, :n, :n], 0.0)
        wt_ref[g // 2, (g % 2) * n:(g % 2 + 1) * n, :] = w.astype(jnp.bfloat16)


def _mod_rows(mod_ref, row):
    return tuple(mod_ref[row, i * D_MODEL:(i + 1) * D_MODEL] for i in range(3))


def _rms_rows(x):
    return x * lax.rsqrt(jnp.mean(x * x, axis=-1, keepdims=True) + EPS)


def _layernorm(x, g, b):
    mu = jnp.mean(x, axis=-1, keepdims=True)
    xc = x - mu
    var = jnp.mean(xc * xc, axis=-1, keepdims=True)
    return xc * lax.rsqrt(var + EPS) * g + b


def _pair_select(stacked, n):
    return jnp.where(_even_lane_mask((n, LANES)), stacked[:n], stacked[n:])


def _split_heads(x):
    return x.reshape(x.shape[0], B_HEADS, B_HDIM)


def _to_feature_major(x):
    return x.T.reshape(B_HEADS, B_HDIM, x.shape[0])


def _store_q(q, qe_ref, qo_ref):
    q = (q * Q_SCALE).astype(jnp.bfloat16)
    even = _even_lane_mask(q.shape)
    zero = jnp.zeros((), jnp.bfloat16)
    qe_ref[...] = jnp.where(even, q, zero)
    qo_ref[...] = jnp.where(even, zero, q)


def _prompt_kernel(x_ref, xn_ref, mod_ref, gpre_ref, win_ref, lng_ref, lnb_ref, ws_ref, bfull_ref,
                   biast_ref, wout_ref, gpost_ref,
                   y_ref, klast_ref, vlast_ref,
                   h_s, usg_s, va_s, qe_s, qo_s, k_s, vt_s, sgb_s, o_s, wt_s, st_s, kf_s, vf_s):
    b = pl.program_id(0)
    t = pl.program_id(1)
    nt = pl.num_programs(1)
    T = SEQ_TILE
    first_step = jnp.logical_and(b == 0, t == 0)

    @pl.when(first_step)
    def _():
        _tril_pairs(ws_ref, wt_s, MLP_CHUNK)

    shift, scale, gate = _mod_rows(mod_ref, pl.ds(b, 1))
    b_next = jnp.minimum(b * nt + t + 1, pl.num_programs(0) * nt - 1) // nt
    shift_next, scale_next, _ = _mod_rows(mod_ref, pl.ds(b_next, 1))

    pre_scale = gpre_ref[...] * (1.0 + scale)
    pre_scale_next = gpre_ref[...] * (1.0 + scale_next)

    def proj(c0, rows):
        return jnp.dot(h_s[rows, :], win_ref[:, c0:c0 + 512], preferred_element_type=jnp.float32)

    def proj_pieces(lo, next_tile=False):
        assert not next_tile or (lo == 0 and T - KV_WIN >= PROJ_ROWS)
        rb = slice(lo, lo + PROJ_ROWS)
        halves = [slice(lo, lo + PROJ_ROWS // 2), slice(lo + PROJ_ROWS // 2, lo + PROJ_ROWS)]
        hist = slice(KV_WIN + lo, KV_WIN + lo + PROJ_ROWS)
        keep = (slice(lo - (T - KV_WIN), lo - (T - KV_WIN) + PROJ_ROWS)
                if lo >= T - KV_WIN and not next_tile else None)
        src, a, c = (xn_ref, pre_scale_next, shift_next) if next_tile else (x_ref, pre_scale, shift)

        def pre_norm():
            for rows in halves:
                h_s[rows, :] = (_rms_rows(src[0, rows, :]) * a + c).astype(jnp.bfloat16)

        def u_half(rows):
            usg_s[rows, :] = _gelu(proj(C_U, rows))

        def v_block():
            v = proj(C_VV, rb)
            vt_s[:, hist] = v.T.astype(jnp.bfloat16)
            if keep is not None:
                vf_s[keep, :] = v

        def va_block():
            va_s[rb, :] = _layernorm(_gelu(proj(C_V, rb)), lng_ref[...], lnb_ref[...]
                                     ).astype(jnp.bfloat16)

        def q_block():
            _store_q(proj(C_Q, rb), qe_s.at[rb, :], qo_s.at[rb, :])

        def ga_block():
            usg_s[rb, :] = usg_s[rb, :] * _silu(proj(C_GA, rb))

        def gmlp(p):
            lanes = slice(p * LANES, (p + 1) * LANES)
            for c0 in range(lo, lo + PROJ_ROWS, 2 * MLP_CHUNK):
                chunks = [slice(c0, c0 + MLP_CHUNK), slice(c0 + MLP_CHUNK, c0 + 2 * MLP_CHUNK)]
                slabs = jnp.concatenate([va_s[rows, lanes] for rows in chunks], axis=1)
                mix = jnp.dot(wt_s[p], slabs, preferred_element_type=jnp.float32)
                for i, rows in enumerate(chunks):
                    mixed = _pair_select(mix[:, i * LANES:(i + 1) * LANES], MLP_CHUNK)
                    o_s[rows, lanes] = (usg_s[rows, lanes] * (mixed + bfull_ref[:, lanes])
                                        ).astype(jnp.bfloat16)

        def gb_block():
            sgb_s[rb, :] = _silu(proj(C_GB, rb))

        def k_block():
            k = proj(C_K, rb)
            k_s[hist, :] = k.astype(jnp.bfloat16)
            if keep is not None:
                kf_s[keep, :] = k

        g = [functools.partial(gmlp, p) for p in range(N_PAIRS)]
        return [pre_norm, functools.partial(u_half, halves[0]), functools.partial(u_half, halves[1]),
                v_block, va_block, ga_block, g[0], q_block, g[1], gb_block, g[2], k_block, g[3]]

    def key_lo(qb, first_tile):
        return max(KV_WIN - qb * Q_BLOCK, 0) if first_tile else 0

    def scores(qb, p, slot, first_tile):
        r0, lo = qb * Q_BLOCK, key_lo(qb, first_tile)
        lanes = slice(p * LANES, (p + 1) * LANES)
        qs = jnp.concatenate([qe_s[r0:r0 + Q_BLOCK, lanes],
                              qo_s[r0:r0 + Q_BLOCK, lanes]], axis=0)
        half = (K_BLOCK - lo) // 2
        for a in (lo, lo + half):
            st_s[slot, a:a + half, :] = lax.dot_general(
                k_s[r0 + a:r0 + a + half, lanes], qs, _CONTRACT_LANES,
                preferred_element_type=jnp.float32) + biast_ref[p, a:a + half, :]

    def finish(qb, p, slot, first_tile):
        r0, lo = qb * Q_BLOCK, key_lo(qb, first_tile)
        st = st_s[slot, lo:K_BLOCK, :]
        m = jnp.max(st, axis=0, keepdims=True)
        pt = jnp.exp2(st - m).astype(jnp.bfloat16)
        ones = jnp.ones((BF16_ROWS, K_BLOCK - lo), jnp.bfloat16)
        lhs = jnp.concatenate([vt_s[p * LANES:(p + 1) * LANES, r0 + lo:r0 + K_BLOCK], ones], axis=0)
        ot = jnp.dot(lhs, pt, preferred_element_type=jnp.float32)
        inv_l = 1.0 / ot[LANES:LANES + 1, :]
        ots = [ot[par * B_HDIM:(par + 1) * B_HDIM, par * Q_BLOCK:(par + 1) * Q_BLOCK]
               * inv_l[:, par * Q_BLOCK:(par + 1) * Q_BLOCK] for par in range(2)]
        yb = jnp.concatenate(ots, axis=0).T * sgb_s[r0:r0 + Q_BLOCK, p * LANES:(p + 1) * LANES]
        o_s[r0:r0 + Q_BLOCK, B_WIDTH + p * LANES:B_WIDTH + (p + 1) * LANES] = (
            yb.astype(jnp.bfloat16))

    post_scale = gate * gpost_ref[...]

    def out_block(lo, n):
        rows = slice(lo, lo + n)
        o = jnp.dot(o_s[rows, :], wout_ref[...], preferred_element_type=jnp.float32)
        y_ref[0, rows, :] = x_ref[0, rows, :] + _rms_rows(o) * post_scale

    def attend(first_tile):
        items = [(qb, p) for qb in range(T // Q_BLOCK) for p in range(N_PAIRS)]
        per_block = (PROJ_ROWS // Q_BLOCK) * N_PAIRS
        n_blocks = T // PROJ_ROWS
        pending = []
        for j in range(SCORE_LEAD):
            scores(*items[j], j % SCORE_SLOTS, first_tile)
        for i, (qb, p) in enumerate(items):
            blk, pos = divmod(i, per_block)
            last_block = blk + 1 == n_blocks
            if pos == 0:
                pending = (proj_pieces(0, next_tile=True) if last_block
                           else proj_pieces((blk + 1) * PROJ_ROWS))
            if i + SCORE_LEAD < len(items):
                assert last_block or pos < per_block - SCORE_LEAD or not pending
                scores(*items[i + SCORE_LEAD], (i + SCORE_LEAD) % SCORE_SLOTS, first_tile)
            finish(qb, p, i % SCORE_SLOTS, first_tile)
            slots_left = per_block - pos - (0 if last_block else SCORE_LEAD)
            if pending and slots_left > 0:
                n_now = -(-len(pending) // slots_left)
                for piece in pending[:n_now]:
                    piece()
                pending = pending[n_now:]
            rows_done = (qb + 1) * Q_BLOCK
            if p == N_PAIRS - 1 and rows_done % POST_ROWS == 0:
                out_block(rows_done - POST_ROWS, POST_ROWS)

    @pl.when(first_step)
    def _():
        for piece in proj_pieces(0):
            piece()

    @pl.when(t == 0)
    def _():
        attend(True)

    @pl.when(t > 0)
    def _():
        attend(False)

    k_s[0:KV_WIN, :] = k_s[T:T + KV_WIN, :]
    vt_s[:, 0:KV_WIN] = vt_s[:, T:T + KV_WIN]

    @pl.when(t == nt - 1)
    def _():
        klast_ref[0] = _to_feature_major(kf_s[...])
        vlast_ref[0] = _to_feature_major(vf_s[...])


def _const_spec(shape):
    nd = len(shape)
    return pl.BlockSpec(shape, lambda b, t: (0,) * nd, pipeline_mode=pl.Buffered(1))


def _prompt_call(x, mod, g_pre, w_in, ln_g, ln_b, w_s, b_full, bias_t, w_out, g_post):
    bsz, seq, _ = x.shape
    T = SEQ_TILE
    assert seq % T == 0 and T % PROJ_ROWS == 0 and KV_WIN % PROJ_ROWS == 0 and T % POST_ROWS == 0
    assert PROJ_ROWS % (2 * MLP_CHUNK) == 0
    nt = seq // T
    f32, bf16 = jnp.float32, jnp.bfloat16

    def next_first_block(b, t):
        f = jnp.minimum(b * nt + t + 1, bsz * nt - 1)
        return (f // nt, (f % nt) * (T // PROJ_ROWS), 0)

    return pl.pallas_call(
        _prompt_kernel,
        grid=(bsz, nt),
        in_specs=[
            pl.BlockSpec((1, T, D_MODEL), lambda b, t: (b, t, 0)),
            pl.BlockSpec((1, PROJ_ROWS, D_MODEL), next_first_block),
            _const_spec((bsz, 3 * D_MODEL)),
            _const_spec((1, D_MODEL)),
            _const_spec((D_MODEL, D_IN)),
            _const_spec((1, A_WIDTH)),
            _const_spec((1, A_WIDTH)),
            _const_spec((A_GROUPS, MLP_CHUNK, MLP_CHUNK)),
            _const_spec((MLP_CHUNK, A_WIDTH)),
            _const_spec((N_PAIRS, K_BLOCK, 2 * Q_BLOCK)),
            _const_spec((D_MODEL, D_MODEL)),
            _const_spec((1, D_MODEL)),
        ],
        out_specs=[
            pl.BlockSpec((1, T, D_MODEL), lambda b, t: (b, t, 0)),
            pl.BlockSpec((1, B_HEADS, B_HDIM, KV_WIN), lambda b, t: (b, 0, 0, 0)),
            pl.BlockSpec((1, B_HEADS, B_HDIM, KV_WIN), lambda b, t: (b, 0, 0, 0)),
        ],
        out_shape=[
            jax.ShapeDtypeStruct((bsz, seq, D_MODEL), f32),
            jax.ShapeDtypeStruct((bsz, B_HEADS, B_HDIM, KV_WIN), f32),
            jax.ShapeDtypeStruct((bsz, B_HEADS, B_HDIM, KV_WIN), f32),
        ],
        scratch_shapes=[
            pltpu.VMEM((T, D_MODEL), bf16),
            pltpu.VMEM((T, A_WIDTH), f32),
            pltpu.VMEM((T, A_WIDTH), bf16),
            pltpu.VMEM((T, B_WIDTH), bf16),
            pltpu.VMEM((T, B_WIDTH), bf16),
            pltpu.VMEM((KV_WIN + T, B_WIDTH), bf16),
            pltpu.VMEM((B_WIDTH, KV_WIN + T), bf16),
            pltpu.VMEM((T, B_WIDTH), f32),
            pltpu.VMEM((T, D_MODEL), bf16),
            pltpu.VMEM((N_PAIRS, 2 * MLP_CHUNK, MLP_CHUNK), bf16),
            pltpu.VMEM((SCORE_SLOTS, K_BLOCK, 2 * Q_BLOCK), f32),
            pltpu.VMEM((KV_WIN, B_WIDTH), f32),
            pltpu.VMEM((KV_WIN, B_WIDTH), f32),
        ],
        compiler_params=pltpu.CompilerParams(
            dimension_semantics=("arbitrary", "arbitrary"),
            vmem_limit_bytes=VMEM_LIMIT_BYTES),
        name="prompt",
    )(x, x, mod, g_pre, w_in, ln_g, ln_b, w_s, b_full, bias_t, w_out, g_post)


def _sample_kernel(x_ref, mod_ref, gpre_ref, win_ref, lng_ref, lnb_ref, ws_ref, bfull_ref,
                   bias_ref, wout_ref, gpost_ref, ck_ref, cv_ref,
                   y_ref, knew_ref, vnew_ref, vanew_ref,
                   usg_s, va_s, qe_s, qo_s, k_s, v_s, sgb_s, o_s, wt_s, *, n_streams, n_new):
    b = pl.program_id(0)
    S = n_new

    @pl.when(b == 0)
    def _():
        _tril_pairs(ws_ref, wt_s, S)
        gpre = gpre_ref[...]
        hs = []
        for i in range(n_streams):
            xi = x_ref[i * S:(i + 1) * S, :]
            shift, scale, _ = _mod_rows(mod_ref, slice(i, i + 1))
            hs.append(_rms_rows(xi) * (gpre * (1.0 + scale)) + shift)
        h = jnp.concatenate(hs, axis=0).astype(jnp.bfloat16)

        def proj(c0):
            return jnp.dot(h, win_ref[:, c0:c0 + 512], preferred_element_type=jnp.float32)

        va = _layernorm(_gelu(proj(C_V)), lng_ref[...], lnb_ref[...])
        vanew_ref[...] = va
        va_s[...] = va.astype(jnp.bfloat16)
        usg_s[...] = _gelu(proj(C_U)) * _silu(proj(C_GA))
        _store_q(proj(C_Q), qe_s, qo_s)
        k = proj(C_K)
        v = proj(C_VV)
        knew_ref[...] = _split_heads(k)
        vnew_ref[...] = _split_heads(v)
        k_s[...] = k.astype(jnp.bfloat16)
        v_s[...] = v.astype(jnp.bfloat16)
        sgb_s[...] = _silu(proj(C_GB))

    r0 = pl.multiple_of(b * S, S)
    rows = pl.ds(r0, S)
    pair_lanes = [slice(p * LANES, (p + 1) * LANES) for p in range(N_PAIRS)]
    for p, lanes in enumerate(pair_lanes):
        mix = jnp.dot(wt_s[p], va_s[rows, lanes], preferred_element_type=jnp.float32)
        mixed = _pair_select(mix, S) + bfull_ref[0:S, lanes]
        o_s[rows, lanes] = (usg_s[rows, lanes] * mixed).astype(jnp.bfloat16)

    scores = []
    for p, lanes in enumerate(pair_lanes):
        qs = jnp.concatenate([qe_s[rows, lanes], qo_s[rows, lanes]], axis=0)
        kct = ck_ref[0, 2 * p:2 * p + 2].reshape(LANES, KV_WIN).astype(jnp.bfloat16)
        bias = jnp.concatenate([bias_ref[p, 0], bias_ref[p, 1]], axis=0)
        s1 = jnp.dot(qs, kct, preferred_element_type=jnp.float32) + bias[:, 0:KV_WIN]
        s2 = lax.dot_general(qs, k_s[rows, lanes], _CONTRACT_LANES,
                             preferred_element_type=jnp.float32) + bias[:, KV_WIN:KV_WIN + S]
        scores.append((s1, s2))
    weights = []
    for s1, s2 in scores:
        m = jnp.maximum(jnp.max(s1, axis=-1, keepdims=True), jnp.max(s2, axis=-1, keepdims=True))
        e1 = jnp.exp2(s1 - m)
        e2 = jnp.exp2(s2 - m)
        l = jnp.sum(e1, axis=-1, keepdims=True) + jnp.sum(e2, axis=-1, keepdims=True)
        weights.append((e1.astype(jnp.bfloat16), e2.astype(jnp.bfloat16), 1.0 / l))
    for p, lanes in enumerate(pair_lanes):
        e1, e2, inv_l = weights[p]
        vct = cv_ref[0, 2 * p:2 * p + 2].reshape(LANES, KV_WIN).astype(jnp.bfloat16)
        o2 = (lax.dot_general(e1, vct, _CONTRACT_LANES, preferred_element_type=jnp.float32)
              + jnp.dot(e2, v_s[rows, lanes], preferred_element_type=jnp.float32))
        yb = _pair_select(o2 * inv_l, S) * sgb_s[rows, lanes]
        o_s[rows, B_WIDTH + p * LANES:B_WIDTH + (p + 1) * LANES] = yb.astype(jnp.bfloat16)

    @pl.when(b == n_streams - 1)
    def _():
        o = jnp.dot(o_s[...], wout_ref[...], preferred_element_type=jnp.float32)
        on = _rms_rows(o) * gpost_ref[...]
        for i in range(n_streams):
            sl = slice(i * S, (i + 1) * S)
            y_ref[sl, :] = x_ref[sl, :] + _mod_rows(mod_ref, slice(i, i + 1))[2] * on[sl, :]


def _sample_call(x, mod, g_pre, w_in, ln_g, ln_b, w_s, b_full, bias_s, w_out, g_post, ck, cv):
    n_streams, n_new, _ = x.shape
    assert n_new % BF16_ROWS == 0 and n_new <= CHUNK
    rows = n_streams * n_new
    f32, bf16 = jnp.float32, jnp.bfloat16

    def const(shape):
        nd = len(shape)
        return pl.BlockSpec(shape, lambda b: (0,) * nd, pipeline_mode=pl.Buffered(1))

    kern = functools.partial(_sample_kernel, n_streams=n_streams, n_new=n_new)
    return pl.pallas_call(
        kern,
        grid=(n_streams,),
        in_specs=[
            const((rows, D_MODEL)),
            const((n_streams, 3 * D_MODEL)),
            const((1, D_MODEL)),
            const((D_MODEL, D_IN)),
            const((1, A_WIDTH)),
            const((1, A_WIDTH)),
            const((A_GROUPS, MLP_CHUNK, MLP_CHUNK)),
            const((MLP_CHUNK, A_WIDTH)),
            const((N_PAIRS, 2, n_new, K_BLOCK)),
            const((D_MODEL, D_MODEL)),
            const((1, D_MODEL)),
            pl.BlockSpec((1, B_HEADS, B_HDIM, KV_WIN), lambda b: (b, 0, 0, 0)),
            pl.BlockSpec((1, B_HEADS, B_HDIM, KV_WIN), lambda b: (b, 0, 0, 0)),
        ],
        out_specs=[
            pl.BlockSpec((rows, D_MODEL), lambda b: (0, 0)),
            pl.BlockSpec((rows, B_HEADS, B_HDIM), lambda b: (0, 0, 0)),
            pl.BlockSpec((rows, B_HEADS, B_HDIM), lambda b: (0, 0, 0)),
            pl.BlockSpec((rows, A_WIDTH), lambda b: (0, 0)),
        ],
        out_shape=[
            jax.ShapeDtypeStruct((rows, D_MODEL), f32),
            jax.ShapeDtypeStruct((rows, B_HEADS, B_HDIM), f32),
            jax.ShapeDtypeStruct((rows, B_HEADS, B_HDIM), f32),
            jax.ShapeDtypeStruct((rows, A_WIDTH), f32),
        ],
        scratch_shapes=[
            pltpu.VMEM((rows, A_WIDTH), f32),
            pltpu.VMEM((rows, A_WIDTH), bf16),
            pltpu.VMEM((rows, B_WIDTH), bf16),
            pltpu.VMEM((rows, B_WIDTH), bf16),
            pltpu.VMEM((rows, B_WIDTH), bf16),
            pltpu.VMEM((rows, B_WIDTH), bf16),
            pltpu.VMEM((rows, B_WIDTH), f32),
            pltpu.VMEM((rows, D_MODEL), bf16),
            pltpu.VMEM((N_PAIRS, 2 * n_new, n_new), bf16),
        ],
        compiler_params=pltpu.CompilerParams(
            dimension_semantics=("arbitrary",),
            vmem_limit_bytes=VMEM_LIMIT_BYTES),
        name="sample",
    )(x.reshape(rows, D_MODEL), mod, g_pre, w_in, ln_g, ln_b, w_s, b_full, bias_s, w_out, g_post,
      ck, cv)


def _bias_base(rel_bias):
    n_far = KV_WIN - REL_CLIP
    far = jnp.broadcast_to(rel_bias[:, N_REL - 1:N_REL], (B_HEADS, n_far))
    near = rel_bias[:, N_REL - 1:0:-1]
    return jnp.concatenate([far, near], axis=1)


def kernel(x_prompt, x_sample, cache_attn_k, cache_attn_v, c_prompt, c_sample, g_pre, w_ada, b_ada,
           w_in, ln_g, ln_b, w_s, b_s, rel_bias, w_out, g_post):
    depth = g_pre.shape[0]
    bsz, seq, _ = x_prompt.shape
    n_streams, n_new, _ = x_sample.shape
    win = cache_attn_k.shape[2]
    assert win == KV_WIN

    yp, ys = x_prompt, x_sample
    kp_rows, vp_rows, ks_rows, vs_rows, va_rows = [], [], [], [], []
    for l in range(depth):
        mod_p, mod_s, bias_t, bias_s, w_in_b, w_out_b, b_full = _prep_call(
            c_prompt, c_sample, w_ada[l], b_ada[l], _bias_base(rel_bias[l]), w_in[l], w_out[l],
            b_s[l], n_new)
        pre = (g_pre[l].reshape(1, D_MODEL), w_in_b, ln_g[l].reshape(1, A_WIDTH),
               ln_b[l].reshape(1, A_WIDTH), w_s[l], b_full)
        post = (w_out_b, g_post[l].reshape(1, D_MODEL))

        yp, k_last, v_last = _prompt_call(yp, mod_p, *pre, bias_t, *post)
        kp_rows.append(jnp.transpose(k_last, _FROM_FEATURE_MAJOR))
        vp_rows.append(jnp.transpose(v_last, _FROM_FEATURE_MAJOR))

        ys2, k_new, v_new, va_new = _sample_call(ys, mod_s, *pre, bias_s, *post,
                                                 jnp.transpose(cache_attn_k[l], _TO_FEATURE_MAJOR),
                                                 jnp.transpose(cache_attn_v[l], _TO_FEATURE_MAJOR))
        ys = ys2.reshape(n_streams, n_new, D_MODEL)
        ks_rows.append(k_new.reshape(n_streams, n_new, B_HEADS, B_HDIM))
        vs_rows.append(v_new.reshape(n_streams, n_new, B_HEADS, B_HDIM))
        va_rows.append(va_new.reshape(n_streams, n_new, A_WIDTH))

    return (yp, ys, jnp.stack(kp_rows), jnp.stack(vp_rows), jnp.stack(ks_rows), jnp.stack(vs_rows),
            jnp.stack(va_rows))
```

```python
import functools

import jax
import jax.numpy as jnp
from jax import lax
from jax.experimental import pallas as pl
from jax.experimental.pallas import tpu as pltpu

LANES = 128
SUBLANES = 8
BF16_ROWS = 16
VMEM_LIMIT_BYTES = 60 * 1024 * 1024

D_MODEL = 1024
A_WIDTH = 512
A_GROUPS = 8
A_GDIM = A_WIDTH // A_GROUPS
MLP_CHUNK = 128
B_WIDTH = 512
B_HEADS = 8
B_HDIM = B_WIDTH // B_HEADS
CHUNK = 64
KV_WIN = 512
REL_CLIP = 128
N_REL = 2 * REL_CLIP + 1
EPS = 1e-6
D_IN = 3 * A_WIDTH + 4 * B_WIDTH
NEG = -1e30
LOG2E = 1.4426950408889634
Q_SCALE = B_HDIM ** -0.5 * LOG2E

N_PAIRS = B_HEADS // 2
Q_BLOCK = 2 * CHUNK
K_BLOCK = KV_WIN + Q_BLOCK
SEQ_TILE = 1024
PROJ_ROWS = 512
HALF_COLS = 256
POST_ROWS = 256
SCORE_LEAD = 2
SCORE_SLOTS = SCORE_LEAD + 1

C_U, C_V, C_GA, C_Q, C_K, C_VV, C_GB = (i * 512 for i in range(7))

_CONTRACT_LANES = (((1,), (1,)), ((), ()))

_TO_FEATURE_MAJOR = (0, 2, 3, 1)
_FROM_FEATURE_MAJOR = (0, 3, 1, 2)


def _sigmoid_exp2(neg_arg_log2):
    return 1.0 / (1.0 + jnp.exp2(neg_arg_log2))


def _gelu(x):
    c1 = -2.0 * (2.0 / jnp.pi) ** 0.5 * LOG2E
    return x * _sigmoid_exp2(x * (c1 + (c1 * 0.044715) * (x * x)))


def _silu(x):
    return x * _sigmoid_exp2(x * (-LOG2E))


def _even_lane_mask(shape):
    lane = lax.broadcasted_iota(jnp.int32, shape, len(shape) - 1)
    return (lane % LANES) < B_HDIM


def _ada_block(c_ref, w_ref, b_ref, o_ref):
    c = c_ref[...]
    o_ref[...] = jnp.dot(_silu(c), w_ref[...], preferred_element_type=jnp.float32) + b_ref[...]


def _relb_block(base_ref, bt_ref, bs_ref, n_new):
    row = lax.broadcasted_iota(jnp.int32, (Q_BLOCK, K_BLOCK), 0)
    col = lax.broadcasted_iota(jnp.int32, (Q_BLOCK, K_BLOCK), 1)
    band_lo = jnp.where(row < CHUNK, 0, CHUNK)
    rel = col - band_lo
    outside = jnp.logical_or(rel < 0, rel >= KV_WIN + CHUNK)
    for par in range(2):
        base = base_ref[0, par:par + 1, :] * LOG2E
        x = jnp.broadcast_to(base, (Q_BLOCK, K_BLOCK))
        shift = 1
        while shift < Q_BLOCK:
            x = jnp.where((row & shift) != 0, jnp.roll(x, shift, axis=1), x)
            shift *= 2
        far = jnp.broadcast_to(base[:, 0:1], (Q_BLOCK, K_BLOCK))
        x = jnp.where(col < row, far, x)
        bs_ref[0, par] = x[0:n_new, :]
        bt_ref[0, :, par * Q_BLOCK:(par + 1) * Q_BLOCK] = jnp.where(outside, NEG, x).T


PREP_COLS = 512
N_PREP = D_IN // PREP_COLS
N_ADA = 3 * D_MODEL // PREP_COLS
WOUT_COLS = D_MODEL // N_PAIRS


def _prep_kernel(cp_ref, cs_ref, wada_ref, bada_ref, base_ref, win_ref, wout_ref, bs_ref,
                 modp_ref, mods_ref, bt_ref, bsam_ref, winb_ref, woutb_ref, bfull_ref, *, n_new):
    j = pl.program_id(0)
    winb_ref[...] = win_ref[...].astype(jnp.bfloat16)

    @pl.when(j == 0)
    def _():
        bst = bs_ref[...].T
        even = _even_lane_mask((MLP_CHUNK, LANES))
        for p in range(N_PAIRS):
            bfull_ref[:, p * LANES:(p + 1) * LANES] = jnp.where(
                even, bst[:, 2 * p:2 * p + 1], bst[:, 2 * p + 1:2 * p + 2])

    @pl.when(j < N_ADA)
    def _():
        _ada_block(cp_ref, wada_ref, bada_ref, modp_ref)
        _ada_block(cs_ref, wada_ref, bada_ref, mods_ref)

    @pl.when(j < N_PAIRS)
    def _():
        woutb_ref[...] = wout_ref[...].astype(jnp.bfloat16)
        _relb_block(base_ref, bt_ref, bsam_ref, n_new)


def _prep_call(c_prompt, c_sample, w_ada, b_ada, base, w_in, w_out, b_s, n_new):
    f32, bf16 = jnp.float32, jnp.bfloat16
    bsz, n_streams = c_prompt.shape[0], c_sample.shape[0]

    def upto(n):
        return lambda j: jnp.minimum(j, n - 1)

    ada_j, pair_j = upto(N_ADA), upto(N_PAIRS)
    return pl.pallas_call(
        functools.partial(_prep_kernel, n_new=n_new),
        grid=(N_PREP,),
        in_specs=[
            pl.BlockSpec((bsz, D_MODEL), lambda j: (0, 0)),
            pl.BlockSpec((n_streams, D_MODEL), lambda j: (0, 0)),
            pl.BlockSpec((D_MODEL, PREP_COLS), lambda j: (0, ada_j(j))),
            pl.BlockSpec((1, PREP_COLS), lambda j: (0, ada_j(j))),
            pl.BlockSpec((1, 2, K_BLOCK), lambda j: (pair_j(j), 0, 0)),
            pl.BlockSpec((D_MODEL, PREP_COLS), lambda j: (0, j)),
            pl.BlockSpec((D_MODEL, WOUT_COLS), lambda j: (0, pair_j(j))),
            pl.BlockSpec((A_GROUPS, MLP_CHUNK), lambda j: (0, 0)),
        ],
        out_specs=[
            pl.BlockSpec((bsz, PREP_COLS), lambda j: (0, ada_j(j))),
            pl.BlockSpec((n_streams, PREP_COLS), lambda j: (0, ada_j(j))),
            pl.BlockSpec((1, K_BLOCK, 2 * Q_BLOCK), lambda j: (pair_j(j), 0, 0)),
            pl.BlockSpec((1, 2, n_new, K_BLOCK), lambda j: (pair_j(j), 0, 0, 0)),
            pl.BlockSpec((D_MODEL, PREP_COLS), lambda j: (0, j)),
            pl.BlockSpec((D_MODEL, WOUT_COLS), lambda j: (0, pair_j(j))),
            pl.BlockSpec((MLP_CHUNK, A_WIDTH), lambda j: (0, 0)),
        ],
        out_shape=[
            jax.ShapeDtypeStruct((bsz, 3 * D_MODEL), f32),
            jax.ShapeDtypeStruct((n_streams, 3 * D_MODEL), f32),
            jax.ShapeDtypeStruct((N_PAIRS, K_BLOCK, 2 * Q_BLOCK), f32),
            jax.ShapeDtypeStruct((N_PAIRS, 2, n_new, K_BLOCK), f32),
            jax.ShapeDtypeStruct((D_MODEL, D_IN), bf16),
            jax.ShapeDtypeStruct((D_MODEL, D_MODEL), bf16),
            jax.ShapeDtypeStruct((MLP_CHUNK, A_WIDTH), f32),
        ],
        compiler_params=pltpu.CompilerParams(dimension_semantics=("arbitrary",)),
        name="prep",
    )(c_prompt, c_sample, w_ada, b_ada.reshape(1, 3 * D_MODEL), base.reshape(N_PAIRS, 2, K_BLOCK),
      w_in, w_out, b_s)


def _tril_pairs(ws_ref, wt_ref, n):
    row = lax.broadcasted_iota(jnp.int32, (n, n), 0)
    col = lax.broadcasted_iota(jnp.int32, (n, n), 1)
    keep = col <= row
    for g in range(A_GROUPS):
        w = jnp.where(keep, ws_ref[g, :n, :n], 0.0)
        wt_ref[g // 2, (g % 2) * n:(g % 2 + 1) * n, :] = w.astype(jnp.bfloat16)


def _mod_rows(mod_ref, row):
    return tuple(mod_ref[row, i * D_MODEL:(i + 1) * D_MODEL] for i in range(3))


def _rms_rows(x):
    return x * lax.rsqrt(jnp.mean(x * x, axis=-1, keepdims=True) + EPS)


def _layernorm(x, g, b):
    mu = jnp.mean(x, axis=-1, keepdims=True)
    xc = x - mu
    var = jnp.mean(xc * xc, axis=-1, keepdims=True)
    return xc * lax.rsqrt(var + EPS) * g + b


def _pair_select(stacked, n):
    return jnp.where(_even_lane_mask((n, LANES)), stacked[:n], stacked[n:])


def _split_heads(x):
    return x.reshape(x.shape[0], B_HEADS, B_HDIM)


def _to_feature_major(x):
    return x.T.reshape(B_HEADS, B_HDIM, x.shape[0])


def _store_q(q, qe_ref, qo_ref):
    q = (q * Q_SCALE).astype(jnp.bfloat16)
    even = _even_lane_mask(q.shape)
    zero = jnp.zeros((), jnp.bfloat16)
    qe_ref[...] = jnp.where(even, q, zero)
    qo_ref[...] = jnp.where(even, zero, q)


def _prompt_kernel(x_ref, mod_ref, gpre_ref, win_ref, lng_ref, lnb_ref, ws_ref, bfull_ref,
                   biast_ref, wout_ref, gpost_ref,
                   y_ref, klast_ref, vlast_ref,
                   h_s, usg_s, va_s, qe_s, qo_s, k_s, vt_s, sgb_s, o_s, wt_s, st_s, kf_s, vf_s):
    t = pl.program_id(1)
    nt = pl.num_programs(1)
    T = SEQ_TILE

    @pl.when(jnp.logical_and(pl.program_id(0) == 0, t == 0))
    def _():
        _tril_pairs(ws_ref, wt_s, MLP_CHUNK)

    shift, scale, gate = _mod_rows(mod_ref, pl.ds(pl.program_id(0), 1))

    pre_scale = gpre_ref[...] * (1.0 + scale)

    def proj(c0, rows, width=512):
        return jnp.dot(h_s[rows, :], win_ref[:, c0:c0 + width], preferred_element_type=jnp.float32)

    def proj_pieces(lo):
        rb = slice(lo, lo + PROJ_ROWS)
        halves = [slice(lo, lo + PROJ_ROWS // 2), slice(lo + PROJ_ROWS // 2, lo + PROJ_ROWS)]
        hist = slice(KV_WIN + lo, KV_WIN + lo + PROJ_ROWS)
        keep = slice(lo - (T - KV_WIN), lo - (T - KV_WIN) + PROJ_ROWS) if lo >= T - KV_WIN else None

        def pre_norm():
            for rows in halves:
                h_s[rows, :] = (_rms_rows(x_ref[0, rows, :]) * pre_scale + shift).astype(jnp.bfloat16)

        def cols(ci):
            return slice(ci * HALF_COLS, (ci + 1) * HALF_COLS)

        def u_half(ci):
            usg_s[rb, cols(ci)] = _gelu(proj(C_U + ci * HALF_COLS, rb, HALF_COLS))

        def v_block(ci):
            v = proj(C_VV + ci * HALF_COLS, rb, HALF_COLS)
            vt_s[cols(ci), hist] = v.T.astype(jnp.bfloat16)
            if keep is not None:
                vf_s[keep, cols(ci)] = v

        def va_block(rows):
            va_s[rows, :] = _layernorm(_gelu(proj(C_V, rows)), lng_ref[...], lnb_ref[...]
                                       ).astype(jnp.bfloat16)

        def q_block(ci):
            _store_q(proj(C_Q + ci * HALF_COLS, rb, HALF_COLS),
                     qe_s.at[rb, cols(ci)], qo_s.at[rb, cols(ci)])

        def ga_block(ci):
            usg_s[rb, cols(ci)] = usg_s[rb, cols(ci)] * _silu(proj(C_GA + ci * HALF_COLS, rb, HALF_COLS))

        def gmlp(p):
            lanes = slice(p * LANES, (p + 1) * LANES)
            for c0 in range(lo, lo + PROJ_ROWS, 2 * MLP_CHUNK):
                chunks = [slice(c0, c0 + MLP_CHUNK), slice(c0 + MLP_CHUNK, c0 + 2 * MLP_CHUNK)]
                slabs = jnp.concatenate([va_s[rows, lanes] for rows in chunks], axis=1)
                mix = jnp.dot(wt_s[p], slabs, preferred_element_type=jnp.float32)
                for i, rows in enumerate(chunks):
                    mixed = _pair_select(mix[:, i * LANES:(i + 1) * LANES], MLP_CHUNK)
                    o_s[rows, lanes] = (usg_s[rows, lanes] * (mixed + bfull_ref[:, lanes])
                                        ).astype(jnp.bfloat16)

        def gb_block(ci):
            sgb_s[rb, cols(ci)] = _silu(proj(C_GB + ci * HALF_COLS, rb, HALF_COLS))

        def k_block(ci):
            k = proj(C_K + ci * HALF_COLS, rb, HALF_COLS)
            k_s[hist, cols(ci)] = k.astype(jnp.bfloat16)
            if keep is not None:
                kf_s[keep, cols(ci)] = k

        P = functools.partial
        return [pre_norm, P(u_half, 0), P(u_half, 1), P(v_block, 0), P(va_block, halves[0]),
                P(v_block, 1), P(va_block, halves[1]), P(ga_block, 0), P(gmlp, 0), P(ga_block, 1),
                P(gmlp, 1), P(q_block, 0), P(gmlp, 2), P(q_block, 1), P(gmlp, 3), P(gb_block, 0),
                P(gb_block, 1), P(k_block, 0), P(k_block, 1)]

    def key_lo(qb, first_tile):
        return max(KV_WIN - qb * Q_BLOCK, 0) if first_tile else 0

    def scores(qb, p, slot, first_tile):
        r0, lo = qb * Q_BLOCK, key_lo(qb, first_tile)
        lanes = slice(p * LANES, (p + 1) * LANES)
        qs = jnp.concatenate([qe_s[r0:r0 + Q_BLOCK, lanes],
                              qo_s[r0:r0 + Q_BLOCK, lanes]], axis=0)
        half = (K_BLOCK - lo) // 2
        for a in (lo, lo + half):
            st_s[slot, a:a + half, :] = lax.dot_general(
                k_s[r0 + a:r0 + a + half, lanes], qs, _CONTRACT_LANES,
                preferred_element_type=jnp.float32) + biast_ref[p, a:a + half, :]

    def finish(qb, p, slot, first_tile):
        r0, lo = qb * Q_BLOCK, key_lo(qb, first_tile)
        st = st_s[slot, lo:K_BLOCK, :]
        m = jnp.max(st, axis=0, keepdims=True)
        pt = jnp.exp2(st - m).astype(jnp.bfloat16)
        ones = jnp.ones((BF16_ROWS, K_BLOCK - lo), jnp.bfloat16)
        lhs = jnp.concatenate([vt_s[p * LANES:(p + 1) * LANES, r0 + lo:r0 + K_BLOCK], ones], axis=0)
        ot = jnp.dot(lhs, pt, preferred_element_type=jnp.float32)
        inv_l = 1.0 / ot[LANES:LANES + 1, :]
        ots = [ot[par * B_HDIM:(par + 1) * B_HDIM, par * Q_BLOCK:(par + 1) * Q_BLOCK]
               * inv_l[:, par * Q_BLOCK:(par + 1) * Q_BLOCK] for par in range(2)]
        yb = jnp.concatenate(ots, axis=0).T * sgb_s[r0:r0 + Q_BLOCK, p * LANES:(p + 1) * LANES]
        o_s[r0:r0 + Q_BLOCK, B_WIDTH + p * LANES:B_WIDTH + (p + 1) * LANES] = (
            yb.astype(jnp.bfloat16))

    post_scale = gate * gpost_ref[...]

    def out_block(lo, n):
        rows = slice(lo, lo + n)
        o = jnp.dot(o_s[rows, :], wout_ref[...], preferred_element_type=jnp.float32)
        y_ref[0, rows, :] = x_ref[0, rows, :] + _rms_rows(o) * post_scale

    def attend(first_tile):
        items = [(qb, p) for qb in range(T // Q_BLOCK) for p in range(N_PAIRS)]
        per_block = (PROJ_ROWS // Q_BLOCK) * N_PAIRS
        n_blocks = T // PROJ_ROWS
        for piece in proj_pieces(0):
            piece()
        pending = []
        for j in range(SCORE_LEAD):
            scores(*items[j], j % SCORE_SLOTS, first_tile)
        for i, (qb, p) in enumerate(items):
            blk, pos = divmod(i, per_block)
            if pos == 0 and blk + 1 < n_blocks:
                pending = proj_pieces((blk + 1) * PROJ_ROWS)
            if i + SCORE_LEAD < len(items):
                assert pos < per_block - SCORE_LEAD or not pending
                scores(*items[i + SCORE_LEAD], (i + SCORE_LEAD) % SCORE_SLOTS, first_tile)
            finish(qb, p, i % SCORE_SLOTS, first_tile)
            slots_left = per_block - SCORE_LEAD - pos
            if pending and slots_left > 0:
                n_now = -(-len(pending) // slots_left)
                for piece in pending[:n_now]:
                    piece()
                pending = pending[n_now:]
            rows_done = (qb + 1) * Q_BLOCK
            if p == N_PAIRS - 1 and rows_done % POST_ROWS == 0:
                out_block(rows_done - POST_ROWS, POST_ROWS)

    @pl.when(t == 0)
    def _():
        attend(True)

    @pl.when(t > 0)
    def _():
        attend(False)

    k_s[0:KV_WIN, :] = k_s[T:T + KV_WIN, :]
    vt_s[:, 0:KV_WIN] = vt_s[:, T:T + KV_WIN]

    @pl.when(t == nt - 1)
    def _():
        klast_ref[0] = _to_feature_major(kf_s[...])
        vlast_ref[0] = _to_feature_major(vf_s[...])


def _const_spec(shape):
    nd = len(shape)
    return pl.BlockSpec(shape, lambda b, t: (0,) * nd, pipeline_mode=pl.Buffered(1))


def _prompt_call(x, mod, g_pre, w_in, ln_g, ln_b, w_s, b_full, bias_t, w_out, g_post):
    bsz, seq, _ = x.shape
    T = SEQ_TILE
    assert seq % T == 0 and T % PROJ_ROWS == 0 and KV_WIN % PROJ_ROWS == 0 and T % POST_ROWS == 0
    assert PROJ_ROWS % (2 * MLP_CHUNK) == 0
    nt = seq // T
    f32, bf16 = jnp.float32, jnp.bfloat16
    return pl.pallas_call(
        _prompt_kernel,
        grid=(bsz, nt),
        in_specs=[
            pl.BlockSpec((1, T, D_MODEL), lambda b, t: (b, t, 0)),
            _const_spec((bsz, 3 * D_MODEL)),
            _const_spec((1, D_MODEL)),
            _const_spec((D_MODEL, D_IN)),
            _const_spec((1, A_WIDTH)),
            _const_spec((1, A_WIDTH)),
            _const_spec((A_GROUPS, MLP_CHUNK, MLP_CHUNK)),
            _const_spec((MLP_CHUNK, A_WIDTH)),
            _const_spec((N_PAIRS, K_BLOCK, 2 * Q_BLOCK)),
            _const_spec((D_MODEL, D_MODEL)),
            _const_spec((1, D_MODEL)),
        ],
        out_specs=[
            pl.BlockSpec((1, T, D_MODEL), lambda b, t: (b, t, 0)),
            pl.BlockSpec((1, B_HEADS, B_HDIM, KV_WIN), lambda b, t: (b, 0, 0, 0)),
            pl.BlockSpec((1, B_HEADS, B_HDIM, KV_WIN), lambda b, t: (b, 0, 0, 0)),
        ],
        out_shape=[
            jax.ShapeDtypeStruct((bsz, seq, D_MODEL), f32),
            jax.ShapeDtypeStruct((bsz, B_HEADS, B_HDIM, KV_WIN), f32),
            jax.ShapeDtypeStruct((bsz, B_HEADS, B_HDIM, KV_WIN), f32),
        ],
        scratch_shapes=[
            pltpu.VMEM((T, D_MODEL), bf16),
            pltpu.VMEM((T, A_WIDTH), f32),
            pltpu.VMEM((T, A_WIDTH), bf16),
            pltpu.VMEM((T, B_WIDTH), bf16),
            pltpu.VMEM((T, B_WIDTH), bf16),
            pltpu.VMEM((KV_WIN + T, B_WIDTH), bf16),
            pltpu.VMEM((B_WIDTH, KV_WIN + T), bf16),
            pltpu.VMEM((T, B_WIDTH), f32),
            pltpu.VMEM((T, D_MODEL), bf16),
            pltpu.VMEM((N_PAIRS, 2 * MLP_CHUNK, MLP_CHUNK), bf16),
            pltpu.VMEM((SCORE_SLOTS, K_BLOCK, 2 * Q_BLOCK), f32),
            pltpu.VMEM((KV_WIN, B_WIDTH), f32),
            pltpu.VMEM((KV_WIN, B_WIDTH), f32),
        ],
        compiler_params=pltpu.CompilerParams(
            dimension_semantics=("arbitrary", "arbitrary"),
            vmem_limit_bytes=VMEM_LIMIT_BYTES),
        name="prompt",
    )(x, mod, g_pre, w_in, ln_g, ln_b, w_s, b_full, bias_t, w_out, g_post)


def _sample_kernel(x_ref, mod_ref, gpre_ref, win_ref, lng_ref, lnb_ref, ws_ref, bfull_ref,
                   bias_ref, wout_ref, gpost_ref, ck_ref, cv_ref,
                   y_ref, knew_ref, vnew_ref, vanew_ref,
                   usg_s, va_s, qe_s, qo_s, k_s, v_s, sgb_s, o_s, wt_s, *, n_streams, n_new):
    b = pl.program_id(0)
    S = n_new

    @pl.when(b == 0)
    def _():
        _tril_pairs(ws_ref, wt_s, S)
        gpre = gpre_ref[...]
        hs = []
        for i in range(n_streams):
            xi = x_ref[i * S:(i + 1) * S, :]
            shift, scale, _ = _mod_rows(mod_ref, slice(i, i + 1))
            hs.append(_rms_rows(xi) * (gpre * (1.0 + scale)) + shift)
        h = jnp.concatenate(hs, axis=0).astype(jnp.bfloat16)

        def proj(c0):
            return jnp.dot(h, win_ref[:, c0:c0 + 512], preferred_element_type=jnp.float32)

        va = _layernorm(_gelu(proj(C_V)), lng_ref[...], lnb_ref[...])
        vanew_ref[...] = va
        va_s[...] = va.astype(jnp.bfloat16)
        usg_s[...] = _gelu(proj(C_U)) * _silu(proj(C_GA))
        _store_q(proj(C_Q), qe_s, qo_s)
        k = proj(C_K)
        v = proj(C_VV)
        knew_ref[...] = _split_heads(k)
        vnew_ref[...] = _split_heads(v)
        k_s[...] = k.astype(jnp.bfloat16)
        v_s[...] = v.astype(jnp.bfloat16)
        sgb_s[...] = _silu(proj(C_GB))

    r0 = pl.multiple_of(b * S, S)
    rows = pl.ds(r0, S)
    pair_lanes = [slice(p * LANES, (p + 1) * LANES) for p in range(N_PAIRS)]
    for p, lanes in enumerate(pair_lanes):
        mix = jnp.dot(wt_s[p], va_s[rows, lanes], preferred_element_type=jnp.float32)
        mixed = _pair_select(mix, S) + bfull_ref[0:S, lanes]
        o_s[rows, lanes] = (usg_s[rows, lanes] * mixed).astype(jnp.bfloat16)

    scores = []
    for p, lanes in enumerate(pair_lanes):
        qs = jnp.concatenate([qe_s[rows, lanes], qo_s[rows, lanes]], axis=0)
        kct = ck_ref[0, 2 * p:2 * p + 2].reshape(LANES, KV_WIN).astype(jnp.bfloat16)
        bias = jnp.concatenate([bias_ref[p, 0], bias_ref[p, 1]], axis=0)
        s1 = jnp.dot(qs, kct, preferred_element_type=jnp.float32) + bias[:, 0:KV_WIN]
        s2 = lax.dot_general(qs, k_s[rows, lanes], _CONTRACT_LANES,
                             preferred_element_type=jnp.float32) + bias[:, KV_WIN:KV_WIN + S]
        scores.append((s1, s2))
    weights = []
    for s1, s2 in scores:
        m = jnp.maximum(jnp.max(s1, axis=-1, keepdims=True), jnp.max(s2, axis=-1, keepdims=True))
        e1 = jnp.exp2(s1 - m)
        e2 = jnp.exp2(s2 - m)
        l = jnp.sum(e1, axis=-1, keepdims=True) + jnp.sum(e2, axis=-1, keepdims=True)
        weights.append((e1.astype(jnp.bfloat16), e2.astype(jnp.bfloat16), 1.0 / l))
    for p, lanes in enumerate(pair_lanes):
        e1, e2, inv_l = weights[p]
        vct = cv_ref[0, 2 * p:2 * p + 2].reshape(LANES, KV_WIN).astype(jnp.bfloat16)
        o2 = (lax.dot_general(e1, vct, _CONTRACT_LANES, preferred_element_type=jnp.float32)
              + jnp.dot(e2, v_s[rows, lanes], preferred_element_type=jnp.float32))
        yb = _pair_select(o2 * inv_l, S) * sgb_s[rows, lanes]
        o_s[rows, B_WIDTH + p * LANES:B_WIDTH + (p + 1) * LANES] = yb.astype(jnp.bfloat16)

    @pl.when(b == n_streams - 1)
    def _():
        o = jnp.dot(o_s[...], wout_ref[...], preferred_element_type=jnp.float32)
        on = _rms_rows(o) * gpost_ref[...]
        for i in range(n_streams):
            sl = slice(i * S, (i + 1) * S)
            y_ref[sl, :] = x_ref[sl, :] + _mod_rows(mod_ref, slice(i, i + 1))[2] * on[sl, :]


def _sample_call(x, mod, g_pre, w_in, ln_g, ln_b, w_s, b_full, bias_s, w_out, g_post, ck, cv):
    n_streams, n_new, _ = x.shape
    assert n_new % BF16_ROWS == 0 and n_new <= CHUNK
    rows = n_streams * n_new
    f32, bf16 = jnp.float32, jnp.bfloat16

    def const(shape):
        nd = len(shape)
        return pl.BlockSpec(shape, lambda b: (0,) * nd, pipeline_mode=pl.Buffered(1))

    kern = functools.partial(_sample_kernel, n_streams=n_streams, n_new=n_new)
    return pl.pallas_call(
        kern,
        grid=(n_streams,),
        in_specs=[
            const((rows, D_MODEL)),
            const((n_streams, 3 * D_MODEL)),
            const((1, D_MODEL)),
            const((D_MODEL, D_IN)),
            const((1, A_WIDTH)),
            const((1, A_WIDTH)),
            const((A_GROUPS, MLP_CHUNK, MLP_CHUNK)),
            const((MLP_CHUNK, A_WIDTH)),
            const((N_PAIRS, 2, n_new, K_BLOCK)),
            const((D_MODEL, D_MODEL)),
            const((1, D_MODEL)),
            pl.BlockSpec((1, B_HEADS, B_HDIM, KV_WIN), lambda b: (b, 0, 0, 0)),
            pl.BlockSpec((1, B_HEADS, B_HDIM, KV_WIN), lambda b: (b, 0, 0, 0)),
        ],
        out_specs=[
            pl.BlockSpec((rows, D_MODEL), lambda b: (0, 0)),
            pl.BlockSpec((rows, B_HEADS, B_HDIM), lambda b: (0, 0, 0)),
            pl.BlockSpec((rows, B_HEADS, B_HDIM), lambda b: (0, 0, 0)),
            pl.BlockSpec((rows, A_WIDTH), lambda b: (0, 0)),
        ],
        out_shape=[
            jax.ShapeDtypeStruct((rows, D_MODEL), f32),
            jax.ShapeDtypeStruct((rows, B_HEADS, B_HDIM), f32),
            jax.ShapeDtypeStruct((rows, B_HEADS, B_HDIM), f32),
            jax.ShapeDtypeStruct((rows, A_WIDTH), f32),
        ],
        scratch_shapes=[
            pltpu.VMEM((rows, A_WIDTH), f32),
            pltpu.VMEM((rows, A_WIDTH), bf16),
            pltpu.VMEM((rows, B_WIDTH), bf16),
            pltpu.VMEM((rows, B_WIDTH), bf16),
            pltpu.VMEM((rows, B_WIDTH), bf16),
            pltpu.VMEM((rows, B_WIDTH), bf16),
            pltpu.VMEM((rows, B_WIDTH), f32),
            pltpu.VMEM((rows, D_MODEL), bf16),
            pltpu.VMEM((N_PAIRS, 2 * n_new, n_new), bf16),
        ],
        compiler_params=pltpu.CompilerParams(
            dimension_semantics=("arbitrary",),
            vmem_limit_bytes=VMEM_LIMIT_BYTES),
        name="sample",
    )(x.reshape(rows, D_MODEL), mod, g_pre, w_in, ln_g, ln_b, w_s, b_full, bias_s, w_out, g_post,
      ck, cv)


def _bias_base(rel_bias):
    n_far = KV_WIN - REL_CLIP
    far = jnp.broadcast_to(rel_bias[:, N_REL - 1:N_REL], (B_HEADS, n_far))
    near = rel_bias[:, N_REL - 1:0:-1]
    return jnp.concatenate([far, near], axis=1)


def kernel(x_prompt, x_sample, cache_attn_k, cache_attn_v, c_prompt, c_sample, g_pre, w_ada, b_ada,
           w_in, ln_g, ln_b, w_s, b_s, rel_bias, w_out, g_post):
    depth = g_pre.shape[0]
    bsz, seq, _ = x_prompt.shape
    n_streams, n_new, _ = x_sample.shape
    win = cache_attn_k.shape[2]
    assert win == KV_WIN

    yp, ys = x_prompt, x_sample
    kp_rows, vp_rows, ks_rows, vs_rows, va_rows = [], [], [], [], []
    for l in range(depth):
        mod_p, mod_s, bias_t, bias_s, w_in_b, w_out_b, b_full = _prep_call(
            c_prompt, c_sample, w_ada[l], b_ada[l], _bias_base(rel_bias[l]), w_in[l], w_out[l],
            b_s[l], n_new)
        pre = (g_pre[l].reshape(1, D_MODEL), w_in_b, ln_g[l].reshape(1, A_WIDTH),
               ln_b[l].reshape(1, A_WIDTH), w_s[l], b_full)
        post = (w_out_b, g_post[l].reshape(1, D_MODEL))

        yp, k_last, v_last = _prompt_call(yp, mod_p, *pre, bias_t, *post)
        kp_rows.append(jnp.transpose(k_last, _FROM_FEATURE_MAJOR))
        vp_rows.append(jnp.transpose(v_last, _FROM_FEATURE_MAJOR))

        ys2, k_new, v_new, va_new = _sample_call(ys, mod_s, *pre, bias_s, *post,
                                                 jnp.transpose(cache_attn_k[l], _TO_FEATURE_MAJOR),
                                                 jnp.transpose(cache_attn_v[l], _TO_FEATURE_MAJOR))
        ys = ys2.reshape(n_streams, n_new, D_MODEL)
        ks_rows.append(k_new.reshape(n_streams, n_new, B_HEADS, B_HDIM))
        vs_rows.append(v_new.reshape(n_streams, n_new, B_HEADS, B_HDIM))
        va_rows.append(va_new.reshape(n_streams, n_new, A_WIDTH))

    return (yp, ys, jnp.stack(kp_rows), jnp.stack(vp_rows), jnp.stack(ks_rows), jnp.stack(vs_rows),
            jnp.stack(va_rows))
```

```python
import functools

import jax
import jax.numpy as jnp
from jax import lax
from jax.experimental import pallas as pl
from jax.experimental.pallas import tpu as pltpu

LANES = 128
SUBLANES = 8
BF16_ROWS = 16
VMEM_LIMIT_BYTES = 60 * 1024 * 1024

D_MODEL = 1024
A_WIDTH = 512
A_GROUPS = 8
A_GDIM = A_WIDTH // A_GROUPS
MLP_CHUNK = 128
B_WIDTH = 512
B_HEADS = 8
B_HDIM = B_WIDTH // B_HEADS
CHUNK = 64
KV_WIN = 512
REL_CLIP = 128
N_REL = 2 * REL_CLIP + 1
EPS = 1e-6
D_IN = 3 * A_WIDTH + 4 * B_WIDTH
NEG = -1e30
LOG2E = 1.4426950408889634
Q_SCALE = B_HDIM ** -0.5 * LOG2E

N_PAIRS = B_HEADS // 2
Q_BLOCK = 2 * CHUNK
K_BLOCK = KV_WIN + Q_BLOCK
SEQ_TILE = 1024
PROJ_ROWS = 512
HALF_COLS = 256
POST_ROWS = 256
SCORE_LEAD = 3
SCORE_SLOTS = SCORE_LEAD + 1

C_U, C_V, C_GA, C_Q, C_K, C_VV, C_GB = (i * 512 for i in range(7))

_CONTRACT_LANES = (((1,), (1,)), ((), ()))

_TO_FEATURE_MAJOR = (0, 2, 3, 1)
_FROM_FEATURE_MAJOR = (0, 3, 1, 2)


def _sigmoid_exp2(neg_arg_log2):
    return 1.0 / (1.0 + jnp.exp2(neg_arg_log2))


def _gelu(x):
    c1 = -2.0 * (2.0 / jnp.pi) ** 0.5 * LOG2E
    return x * _sigmoid_exp2(x * (c1 + (c1 * 0.044715) * (x * x)))


def _silu(x):
    return x * _sigmoid_exp2(x * (-LOG2E))


def _even_lane_mask(shape):
    lane = lax.broadcasted_iota(jnp.int32, shape, len(shape) - 1)
    return (lane % LANES) < B_HDIM


def _ada_block(c_ref, w_ref, b_ref, o_ref):
    c = c_ref[...]
    o_ref[...] = jnp.dot(_silu(c), w_ref[...], preferred_element_type=jnp.float32) + b_ref[...]


def _relb_block(base_ref, bt_ref, bs_ref, n_new):
    row = lax.broadcasted_iota(jnp.int32, (Q_BLOCK, K_BLOCK), 0)
    col = lax.broadcasted_iota(jnp.int32, (Q_BLOCK, K_BLOCK), 1)
    band_lo = jnp.where(row < CHUNK, 0, CHUNK)
    rel = col - band_lo
    outside = jnp.logical_or(rel < 0, rel >= KV_WIN + CHUNK)
    for par in range(2):
        base = base_ref[0, par:par + 1, :] * LOG2E
        x = jnp.broadcast_to(base, (Q_BLOCK, K_BLOCK))
        shift = 1
        while shift < Q_BLOCK:
            x = jnp.where((row & shift) != 0, jnp.roll(x, shift, axis=1), x)
            shift *= 2
        far = jnp.broadcast_to(base[:, 0:1], (Q_BLOCK, K_BLOCK))
        x = jnp.where(col < row, far, x)
        bs_ref[0, par] = x[0:n_new, :]
        bt_ref[0, :, par * Q_BLOCK:(par + 1) * Q_BLOCK] = jnp.where(outside, NEG, x).T


PREP_COLS = 512
N_PREP = D_IN // PREP_COLS
N_ADA = 3 * D_MODEL // PREP_COLS
WOUT_COLS = D_MODEL // N_PAIRS


def _prep_kernel(cp_ref, cs_ref, wada_ref, bada_ref, base_ref, win_ref, wout_ref, bs_ref,
                 modp_ref, mods_ref, bt_ref, bsam_ref, winb_ref, woutb_ref, bfull_ref, *, n_new):
    j = pl.program_id(0)
    winb_ref[...] = win_ref[...].astype(jnp.bfloat16)

    @pl.when(j == 0)
    def _():
        bst = bs_ref[...].T
        even = _even_lane_mask((MLP_CHUNK, LANES))
        for p in range(N_PAIRS):
            bfull_ref[:, p * LANES:(p + 1) * LANES] = jnp.where(
                even, bst[:, 2 * p:2 * p + 1], bst[:, 2 * p + 1:2 * p + 2])

    @pl.when(j < N_ADA)
    def _():
        _ada_block(cp_ref, wada_ref, bada_ref, modp_ref)
        _ada_block(cs_ref, wada_ref, bada_ref, mods_ref)

    @pl.when(j < N_PAIRS)
    def _():
        woutb_ref[...] = wout_ref[...].astype(jnp.bfloat16)
        _relb_block(base_ref, bt_ref, bsam_ref, n_new)


def _prep_call(c_prompt, c_sample, w_ada, b_ada, base, w_in, w_out, b_s, n_new):
    f32, bf16 = jnp.float32, jnp.bfloat16
    bsz, n_streams = c_prompt.shape[0], c_sample.shape[0]

    def upto(n):
        return lambda j: jnp.minimum(j, n - 1)

    ada_j, pair_j = upto(N_ADA), upto(N_PAIRS)
    return pl.pallas_call(
        functools.partial(_prep_kernel, n_new=n_new),
        grid=(N_PREP,),
        in_specs=[
            pl.BlockSpec((bsz, D_MODEL), lambda j: (0, 0)),
            pl.BlockSpec((n_streams, D_MODEL), lambda j: (0, 0)),
            pl.BlockSpec((D_MODEL, PREP_COLS), lambda j: (0, ada_j(j))),
            pl.BlockSpec((1, PREP_COLS), lambda j: (0, ada_j(j))),
            pl.BlockSpec((1, 2, K_BLOCK), lambda j: (pair_j(j), 0, 0)),
            pl.BlockSpec((D_MODEL, PREP_COLS), lambda j: (0, j)),
            pl.BlockSpec((D_MODEL, WOUT_COLS), lambda j: (0, pair_j(j))),
            pl.BlockSpec((A_GROUPS, MLP_CHUNK), lambda j: (0, 0)),
        ],
        out_specs=[
            pl.BlockSpec((bsz, PREP_COLS), lambda j: (0, ada_j(j))),
            pl.BlockSpec((n_streams, PREP_COLS), lambda j: (0, ada_j(j))),
            pl.BlockSpec((1, K_BLOCK, 2 * Q_BLOCK), lambda j: (pair_j(j), 0, 0)),
            pl.BlockSpec((1, 2, n_new, K_BLOCK), lambda j: (pair_j(j), 0, 0, 0)),
            pl.BlockSpec((D_MODEL, PREP_COLS), lambda j: (0, j)),
            pl.BlockSpec((D_MODEL, WOUT_COLS), lambda j: (0, pair_j(j))),
            pl.BlockSpec((MLP_CHUNK, A_WIDTH), lambda j: (0, 0)),
        ],
        out_shape=[
            jax.ShapeDtypeStruct((bsz, 3 * D_MODEL), f32),
            jax.ShapeDtypeStruct((n_streams, 3 * D_MODEL), f32),
            jax.ShapeDtypeStruct((N_PAIRS, K_BLOCK, 2 * Q_BLOCK), f32),
            jax.ShapeDtypeStruct((N_PAIRS, 2, n_new, K_BLOCK), f32),
            jax.ShapeDtypeStruct((D_MODEL, D_IN), bf16),
            jax.ShapeDtypeStruct((D_MODEL, D_MODEL), bf16),
            jax.ShapeDtypeStruct((MLP_CHUNK, A_WIDTH), f32),
        ],
        compiler_params=pltpu.CompilerParams(dimension_semantics=("arbitrary",)),
        name="prep",
    )(c_prompt, c_sample, w_ada, b_ada.reshape(1, 3 * D_MODEL), base.reshape(N_PAIRS, 2, K_BLOCK),
      w_in, w_out, b_s)


def _tril_pairs(ws_ref, wt_ref, n):
    row = lax.broadcasted_iota(jnp.int32, (n, n), 0)
    col = lax.broadcasted_iota(jnp.int32, (n, n), 1)
    keep = col <= row
    for g in range(A_GROUPS):
        w = jnp.where(keep, ws_ref[g, :n, :n], 0.0)
        wt_ref[g // 2, (g % 2) * n:(g % 2 + 1) * n, :] = w.astype(jnp.bfloat16)


def _mod_rows(mod_ref, row):
    return tuple(mod_ref[row, i * D_MODEL:(i + 1) * D_MODEL] for i in range(3))


def _rms_rows(x):
    return x * lax.rsqrt(jnp.mean(x * x, axis=-1, keepdims=True) + EPS)


def _layernorm(x, g, b):
    mu = jnp.mean(x, axis=-1, keepdims=True)
    xc = x - mu
    var = jnp.mean(xc * xc, axis=-1, keepdims=True)
    return xc * lax.rsqrt(var + EPS) * g + b


def _pair_select(stacked, n):
    return jnp.where(_even_lane_mask((n, LANES)), stacked[:n], stacked[n:])


def _split_heads(x):
    return x.reshape(x.shape[0], B_HEADS, B_HDIM)


def _to_feature_major(x):
    return x.T.reshape(B_HEADS, B_HDIM, x.shape[0])


def _store_q(q, qe_ref, qo_ref):
    q = (q * Q_SCALE).astype(jnp.bfloat16)
    even = _even_lane_mask(q.shape)
    zero = jnp.zeros((), jnp.bfloat16)
    qe_ref[...] = jnp.where(even, q, zero)
    qo_ref[...] = jnp.where(even, zero, q)


def _prompt_kernel(x_ref, mod_ref, gpre_ref, win_ref, lng_ref, lnb_ref, ws_ref, bfull_ref,
                   biast_ref, wout_ref, gpost_ref,
                   y_ref, klast_ref, vlast_ref,
                   h_s, usg_s, va_s, qe_s, qo_s, k_s, vt_s, sgb_s, o_s, wt_s, st_s, kf_s, vf_s):
    t = pl.program_id(1)
    nt = pl.num_programs(1)
    T = SEQ_TILE

    @pl.when(jnp.logical_and(pl.program_id(0) == 0, t == 0))
    def _():
        _tril_pairs(ws_ref, wt_s, MLP_CHUNK)

    shift, scale, gate = _mod_rows(mod_ref, pl.ds(pl.program_id(0), 1))

    pre_scale = gpre_ref[...] * (1.0 + scale)

    def proj(c0, rows, width=512):
        return jnp.dot(h_s[rows, :], win_ref[:, c0:c0 + width], preferred_element_type=jnp.float32)

    def proj_pieces(lo):
        rb = slice(lo, lo + PROJ_ROWS)
        halves = [slice(lo, lo + PROJ_ROWS // 2), slice(lo + PROJ_ROWS // 2, lo + PROJ_ROWS)]
        hist = slice(KV_WIN + lo, KV_WIN + lo + PROJ_ROWS)
        keep = slice(lo - (T - KV_WIN), lo - (T - KV_WIN) + PROJ_ROWS) if lo >= T - KV_WIN else None

        def pre_norm():
            for rows in halves:
                h_s[rows, :] = (_rms_rows(x_ref[0, rows, :]) * pre_scale + shift).astype(jnp.bfloat16)

        def cols(ci):
            return slice(ci * HALF_COLS, (ci + 1) * HALF_COLS)

        def u_half(ci):
            usg_s[rb, cols(ci)] = _gelu(proj(C_U + ci * HALF_COLS, rb, HALF_COLS))

        def v_block(ci):
            v = proj(C_VV + ci * HALF_COLS, rb, HALF_COLS)
            vt_s[cols(ci), hist] = v.T.astype(jnp.bfloat16)
            if keep is not None:
                vf_s[keep, cols(ci)] = v

        def va_block(rows):
            va_s[rows, :] = _layernorm(_gelu(proj(C_V, rows)), lng_ref[...], lnb_ref[...]
                                       ).astype(jnp.bfloat16)

        def q_block(ci):
            _store_q(proj(C_Q + ci * HALF_COLS, rb, HALF_COLS),
                     qe_s.at[rb, cols(ci)], qo_s.at[rb, cols(ci)])

        def ga_block(ci):
            usg_s[rb, cols(ci)] = usg_s[rb, cols(ci)] * _silu(proj(C_GA + ci * HALF_COLS, rb, HALF_COLS))

        def gmlp(p):
            lanes = slice(p * LANES, (p + 1) * LANES)
            for c0 in range(lo, lo + PROJ_ROWS, 2 * MLP_CHUNK):
                chunks = [slice(c0, c0 + MLP_CHUNK), slice(c0 + MLP_CHUNK, c0 + 2 * MLP_CHUNK)]
                slabs = jnp.concatenate([va_s[rows, lanes] for rows in chunks], axis=1)
                mix = jnp.dot(wt_s[p], slabs, preferred_element_type=jnp.float32)
                for i, rows in enumerate(chunks):
                    mixed = _pair_select(mix[:, i * LANES:(i + 1) * LANES], MLP_CHUNK)
                    o_s[rows, lanes] = (usg_s[rows, lanes] * (mixed + bfull_ref[:, lanes])
                                        ).astype(jnp.bfloat16)

        def gb_block(ci):
            sgb_s[rb, cols(ci)] = _silu(proj(C_GB + ci * HALF_COLS, rb, HALF_COLS))

        def k_block(ci):
            k = proj(C_K + ci * HALF_COLS, rb, HALF_COLS)
            k_s[hist, cols(ci)] = k.astype(jnp.bfloat16)
            if keep is not None:
                kf_s[keep, cols(ci)] = k

        P = functools.partial
        return [pre_norm, P(u_half, 0), P(u_half, 1), P(v_block, 0), P(va_block, halves[0]),
                P(v_block, 1), P(va_block, halves[1]), P(ga_block, 0), P(gmlp, 0), P(ga_block, 1),
                P(gmlp, 1), P(q_block, 0), P(gmlp, 2), P(q_block, 1), P(gmlp, 3), P(gb_block, 0),
                P(gb_block, 1), P(k_block, 0), P(k_block, 1)]

    def key_lo(qb, first_tile):
        return max(KV_WIN - qb * Q_BLOCK, 0) if first_tile else 0

    def scores(qb, p, slot, first_tile):
        r0, lo = qb * Q_BLOCK, key_lo(qb, first_tile)
        lanes = slice(p * LANES, (p + 1) * LANES)
        qs = jnp.concatenate([qe_s[r0:r0 + Q_BLOCK, lanes],
                              qo_s[r0:r0 + Q_BLOCK, lanes]], axis=0)
        half = (K_BLOCK - lo) // 2
        for a in (lo, lo + half):
            st_s[slot, a:a + half, :] = lax.dot_general(
                k_s[r0 + a:r0 + a + half, lanes], qs, _CONTRACT_LANES,
                preferred_element_type=jnp.float32) + biast_ref[p, a:a + half, :]

    def finish(qb, p, slot, first_tile):
        r0, lo = qb * Q_BLOCK, key_lo(qb, first_tile)
        st = st_s[slot, lo:K_BLOCK, :]
        m = jnp.max(st, axis=0, keepdims=True)
        pt = jnp.exp2(st - m).astype(jnp.bfloat16)
        ones = jnp.ones((BF16_ROWS, K_BLOCK - lo), jnp.bfloat16)
        lhs = jnp.concatenate([vt_s[p * LANES:(p + 1) * LANES, r0 + lo:r0 + K_BLOCK], ones], axis=0)
        ot = jnp.dot(lhs, pt, preferred_element_type=jnp.float32)
        inv_l = 1.0 / ot[LANES:LANES + 1, :]
        ots = [ot[par * B_HDIM:(par + 1) * B_HDIM, par * Q_BLOCK:(par + 1) * Q_BLOCK]
               * inv_l[:, par * Q_BLOCK:(par + 1) * Q_BLOCK] for par in range(2)]
        yb = jnp.concatenate(ots, axis=0).T * sgb_s[r0:r0 + Q_BLOCK, p * LANES:(p + 1) * LANES]
        o_s[r0:r0 + Q_BLOCK, B_WIDTH + p * LANES:B_WIDTH + (p + 1) * LANES] = (
            yb.astype(jnp.bfloat16))

    post_scale = gate * gpost_ref[...]

    def out_block(lo, n):
        rows = slice(lo, lo + n)
        o = jnp.dot(o_s[rows, :], wout_ref[...], preferred_element_type=jnp.float32)
        y_ref[0, rows, :] = x_ref[0, rows, :] + _rms_rows(o) * post_scale

    def attend(first_tile):
        items = [(qb, p) for qb in range(T // Q_BLOCK) for p in range(N_PAIRS)]
        per_block = (PROJ_ROWS // Q_BLOCK) * N_PAIRS
        n_blocks = T // PROJ_ROWS
        for piece in proj_pieces(0):
            piece()
        pending = []
        for j in range(SCORE_LEAD):
            scores(*items[j], j % SCORE_SLOTS, first_tile)
        for i, (qb, p) in enumerate(items):
            blk, pos = divmod(i, per_block)
            if pos == 0 and blk + 1 < n_blocks:
                pending = proj_pieces((blk + 1) * PROJ_ROWS)
            if i + SCORE_LEAD < len(items):
                assert pos < per_block - SCORE_LEAD or not pending
                scores(*items[i + SCORE_LEAD], (i + SCORE_LEAD) % SCORE_SLOTS, first_tile)
            finish(qb, p, i % SCORE_SLOTS, first_tile)
            slots_left = per_block - SCORE_LEAD - pos
            if pending and slots_left > 0:
                n_now = -(-len(pending) // slots_left)
                for piece in pending[:n_now]:
                    piece()
                pending = pending[n_now:]
            rows_done = (qb + 1) * Q_BLOCK
            if p == N_PAIRS - 1 and rows_done % POST_ROWS == 0:
                out_block(rows_done - POST_ROWS, POST_ROWS)

    @pl.when(t == 0)
    def _():
        attend(True)

    @pl.when(t > 0)
    def _():
        attend(False)

    k_s[0:KV_WIN, :] = k_s[T:T + KV_WIN, :]
    vt_s[:, 0:KV_WIN] = vt_s[:, T:T + KV_WIN]

    @pl.when(t == nt - 1)
    def _():
        klast_ref[0] = _to_feature_major(kf_s[...])
        vlast_ref[0] = _to_feature_major(vf_s[...])


def _const_spec(shape):
    nd = len(shape)
    return pl.BlockSpec(shape, lambda b, t: (0,) * nd, pipeline_mode=pl.Buffered(1))


def _prompt_call(x, mod, g_pre, w_in, ln_g, ln_b, w_s, b_full, bias_t, w_out, g_post):
    bsz, seq, _ = x.shape
    T = SEQ_TILE
    assert seq % T == 0 and T % PROJ_ROWS == 0 and KV_WIN % PROJ_ROWS == 0 and T % POST_ROWS == 0
    assert PROJ_ROWS % (2 * MLP_CHUNK) == 0
    nt = seq // T
    f32, bf16 = jnp.float32, jnp.bfloat16
    return pl.pallas_call(
        _prompt_kernel,
        grid=(bsz, nt),
        in_specs=[
            pl.BlockSpec((1, T, D_MODEL), lambda b, t: (b, t, 0)),
            _const_spec((bsz, 3 * D_MODEL)),
            _const_spec((1, D_MODEL)),
            _const_spec((D_MODEL, D_IN)),
            _const_spec((1, A_WIDTH)),
            _const_spec((1, A_WIDTH)),
            _const_spec((A_GROUPS, MLP_CHUNK, MLP_CHUNK)),
            _const_spec((MLP_CHUNK, A_WIDTH)),
            _const_spec((N_PAIRS, K_BLOCK, 2 * Q_BLOCK)),
            _const_spec((D_MODEL, D_MODEL)),
            _const_spec((1, D_MODEL)),
        ],
        out_specs=[
            pl.BlockSpec((1, T, D_MODEL), lambda b, t: (b, t, 0)),
            pl.BlockSpec((1, B_HEADS, B_HDIM, KV_WIN), lambda b, t: (b, 0, 0, 0)),
            pl.BlockSpec((1, B_HEADS, B_HDIM, KV_WIN), lambda b, t: (b, 0, 0, 0)),
        ],
        out_shape=[
            jax.ShapeDtypeStruct((bsz, seq, D_MODEL), f32),
            jax.ShapeDtypeStruct((bsz, B_HEADS, B_HDIM, KV_WIN), f32),
            jax.ShapeDtypeStruct((bsz, B_HEADS, B_HDIM, KV_WIN), f32),
        ],
        scratch_shapes=[
            pltpu.VMEM((T, D_MODEL), bf16),
            pltpu.VMEM((T, A_WIDTH), f32),
            pltpu.VMEM((T, A_WIDTH), bf16),
            pltpu.VMEM((T, B_WIDTH), bf16),
            pltpu.VMEM((T, B_WIDTH), bf16),
            pltpu.VMEM((KV_WIN + T, B_WIDTH), bf16),
            pltpu.VMEM((B_WIDTH, KV_WIN + T), bf16),
            pltpu.VMEM((T, B_WIDTH), f32),
            pltpu.VMEM((T, D_MODEL), bf16),
            pltpu.VMEM((N_PAIRS, 2 * MLP_CHUNK, MLP_CHUNK), bf16),
            pltpu.VMEM((SCORE_SLOTS, K_BLOCK, 2 * Q_BLOCK), f32),
            pltpu.VMEM((KV_WIN, B_WIDTH), f32),
            pltpu.VMEM((KV_WIN, B_WIDTH), f32),
        ],
        compiler_params=pltpu.CompilerParams(
            dimension_semantics=("arbitrary", "arbitrary"),
            vmem_limit_bytes=VMEM_LIMIT_BYTES),
        name="prompt",
    )(x, mod, g_pre, w_in, ln_g, ln_b, w_s, b_full, bias_t, w_out, g_post)


def _sample_kernel(x_ref, mod_ref, gpre_ref, win_ref, lng_ref, lnb_ref, ws_ref, bfull_ref,
                   bias_ref, wout_ref, gpost_ref, ck_ref, cv_ref,
                   y_ref, knew_ref, vnew_ref, vanew_ref,
                   usg_s, va_s, qe_s, qo_s, k_s, v_s, sgb_s, o_s, wt_s, *, n_streams, n_new):
    b = pl.program_id(0)
    S = n_new

    @pl.when(b == 0)
    def _():
        _tril_pairs(ws_ref, wt_s, S)
        gpre = gpre_ref[...]
        hs = []
        for i in range(n_streams):
            xi = x_ref[i * S:(i + 1) * S, :]
            shift, scale, _ = _mod_rows(mod_ref, slice(i, i + 1))
            hs.append(_rms_rows(xi) * (gpre * (1.0 + scale)) + shift)
        h = jnp.concatenate(hs, axis=0).astype(jnp.bfloat16)

        def proj(c0):
            return jnp.dot(h, win_ref[:, c0:c0 + 512], preferred_element_type=jnp.float32)

        va = _layernorm(_gelu(proj(C_V)), lng_ref[...], lnb_ref[...])
        vanew_ref[...] = va
        va_s[...] = va.astype(jnp.bfloat16)
        usg_s[...] = _gelu(proj(C_U)) * _silu(proj(C_GA))
        _store_q(proj(C_Q), qe_s, qo_s)
        k = proj(C_K)
        v = proj(C_VV)
        knew_ref[...] = _split_heads(k)
        vnew_ref[...] = _split_heads(v)
        k_s[...] = k.astype(jnp.bfloat16)
        v_s[...] = v.astype(jnp.bfloat16)
        sgb_s[...] = _silu(proj(C_GB))

    r0 = pl.multiple_of(b * S, S)
    rows = pl.ds(r0, S)
    pair_lanes = [slice(p * LANES, (p + 1) * LANES) for p in range(N_PAIRS)]
    for p, lanes in enumerate(pair_lanes):
        mix = jnp.dot(wt_s[p], va_s[rows, lanes], preferred_element_type=jnp.float32)
        mixed = _pair_select(mix, S) + bfull_ref[0:S, lanes]
        o_s[rows, lanes] = (usg_s[rows, lanes] * mixed).astype(jnp.bfloat16)

    scores = []
    for p, lanes in enumerate(pair_lanes):
        qs = jnp.concatenate([qe_s[rows, lanes], qo_s[rows, lanes]], axis=0)
        kct = ck_ref[0, 2 * p:2 * p + 2].reshape(LANES, KV_WIN).astype(jnp.bfloat16)
        bias = jnp.concatenate([bias_ref[p, 0], bias_ref[p, 1]], axis=0)
        s1 = jnp.dot(qs, kct, preferred_element_type=jnp.float32) + bias[:, 0:KV_WIN]
        s2 = lax.dot_general(qs, k_s[rows, lanes], _CONTRACT_LANES,
                             preferred_element_type=jnp.float32) + bias[:, KV_WIN:KV_WIN + S]
        scores.append((s1, s2))
    weights = []
    for s1, s2 in scores:
        m = jnp.maximum(jnp.max(s1, axis=-1, keepdims=True), jnp.max(s2, axis=-1, keepdims=True))
        e1 = jnp.exp2(s1 - m)
        e2 = jnp.exp2(s2 - m)
        l = jnp.sum(e1, axis=-1, keepdims=True) + jnp.sum(e2, axis=-1, keepdims=True)
        weights.append((e1.astype(jnp.bfloat16), e2.astype(jnp.bfloat16), 1.0 / l))
    for p, lanes in enumerate(pair_lanes):
        e1, e2, inv_l = weights[p]
        vct = cv_ref[0, 2 * p:2 * p + 2].reshape(LANES, KV_WIN).astype(jnp.bfloat16)
        o2 = (lax.dot_general(e1, vct, _CONTRACT_LANES, preferred_element_type=jnp.float32)
              + jnp.dot(e2, v_s[rows, lanes], preferred_element_type=jnp.float32))
        yb = _pair_select(o2 * inv_l, S) * sgb_s[rows, lanes]
        o_s[rows, B_WIDTH + p * LANES:B_WIDTH + (p + 1) * LANES] = yb.astype(jnp.bfloat16)

    @pl.when(b == n_streams - 1)
    def _():
        o = jnp.dot(o_s[...], wout_ref[...], preferred_element_type=jnp.float32)
        on = _rms_rows(o) * gpost_ref[...]
        for i in range(n_streams):
            sl = slice(i * S, (i + 1) * S)
            y_ref[sl, :] = x_ref[sl, :] + _mod_rows(mod_ref, slice(i, i + 1))[2] * on[sl, :]


def _sample_call(x, mod, g_pre, w_in, ln_g, ln_b, w_s, b_full, bias_s, w_out, g_post, ck, cv):
    n_streams, n_new, _ = x.shape
    assert n_new % BF16_ROWS == 0 and n_new <= CHUNK
    rows = n_streams * n_new
    f32, bf16 = jnp.float32, jnp.bfloat16

    def const(shape):
        nd = len(shape)
        return pl.BlockSpec(shape, lambda b: (0,) * nd, pipeline_mode=pl.Buffered(1))

    kern = functools.partial(_sample_kernel, n_streams=n_streams, n_new=n_new)
    return pl.pallas_call(
        kern,
        grid=(n_streams,),
        in_specs=[
            const((rows, D_MODEL)),
            const((n_streams, 3 * D_MODEL)),
            const((1, D_MODEL)),
            const((D_MODEL, D_IN)),
            const((1, A_WIDTH)),
            const((1, A_WIDTH)),
            const((A_GROUPS, MLP_CHUNK, MLP_CHUNK)),
            const((MLP_CHUNK, A_WIDTH)),
            const((N_PAIRS, 2, n_new, K_BLOCK)),
            const((D_MODEL, D_MODEL)),
            const((1, D_MODEL)),
            pl.BlockSpec((1, B_HEADS, B_HDIM, KV_WIN), lambda b: (b, 0, 0, 0)),
            pl.BlockSpec((1, B_HEADS, B_HDIM, KV_WIN), lambda b: (b, 0, 0, 0)),
        ],
        out_specs=[
            pl.BlockSpec((rows, D_MODEL), lambda b: (0, 0)),
            pl.BlockSpec((rows, B_HEADS, B_HDIM), lambda b: (0, 0, 0)),
            pl.BlockSpec((rows, B_HEADS, B_HDIM), lambda b: (0, 0, 0)),
            pl.BlockSpec((rows, A_WIDTH), lambda b: (0, 0)),
        ],
        out_shape=[
            jax.ShapeDtypeStruct((rows, D_MODEL), f32),
            jax.ShapeDtypeStruct((rows, B_HEADS, B_HDIM), f32),
            jax.ShapeDtypeStruct((rows, B_HEADS, B_HDIM), f32),
            jax.ShapeDtypeStruct((rows, A_WIDTH), f32),
        ],
        scratch_shapes=[
            pltpu.VMEM((rows, A_WIDTH), f32),
            pltpu.VMEM((rows, A_WIDTH), bf16),
            pltpu.VMEM((rows, B_WIDTH), bf16),
            pltpu.VMEM((rows, B_WIDTH), bf16),
            pltpu.VMEM((rows, B_WIDTH), bf16),
            pltpu.VMEM((rows, B_WIDTH), bf16),
            pltpu.VMEM((rows, B_WIDTH), f32),
            pltpu.VMEM((rows, D_MODEL), bf16),
            pltpu.VMEM((N_PAIRS, 2 * n_new, n_new), bf16),
        ],
        compiler_params=pltpu.CompilerParams(
            dimension_semantics=("arbitrary",),
            vmem_limit_bytes=VMEM_LIMIT_BYTES),
        name="sample",
    )(x.reshape(rows, D_MODEL), mod, g_pre, w_in, ln_g, ln_b, w_s, b_full, bias_s, w_out, g_post,
      ck, cv)


def _bias_base(rel_bias):
    n_far = KV_WIN - REL_CLIP
    far = jnp.broadcast_to(rel_bias[:, N_REL - 1:N_REL], (B_HEADS, n_far))
    near = rel_bias[:, N_REL - 1:0:-1]
    return jnp.concatenate([far, near], axis=1)


def kernel(x_prompt, x_sample, cache_attn_k, cache_attn_v, c_prompt, c_sample, g_pre, w_ada, b_ada,
           w_in, ln_g, ln_b, w_s, b_s, rel_bias, w_out, g_post):
    depth = g_pre.shape[0]
    bsz, seq, _ = x_prompt.shape
    n_streams, n_new, _ = x_sample.shape
    win = cache_attn_k.shape[2]
    assert win == KV_WIN

    yp, ys = x_prompt, x_sample
    kp_rows, vp_rows, ks_rows, vs_rows, va_rows = [], [], [], [], []
    for l in range(depth):
        mod_p, mod_s, bias_t, bias_s, w_in_b, w_out_b, b_full = _prep_call(
            c_prompt, c_sample, w_ada[l], b_ada[l], _bias_base(rel_bias[l]), w_in[l], w_out[l],
            b_s[l], n_new)
        pre = (g_pre[l].reshape(1, D_MODEL), w_in_b, ln_g[l].reshape(1, A_WIDTH),
               ln_b[l].reshape(1, A_WIDTH), w_s[l], b_full)
        post = (w_out_b, g_post[l].reshape(1, D_MODEL))

        yp, k_last, v_last = _prompt_call(yp, mod_p, *pre, bias_t, *post)
        kp_rows.append(jnp.transpose(k_last, _FROM_FEATURE_MAJOR))
        vp_rows.append(jnp.transpose(v_last, _FROM_FEATURE_MAJOR))

        ys2, k_new, v_new, va_new = _sample_call(ys, mod_s, *pre, bias_s, *post,
                                                 jnp.transpose(cache_attn_k[l], _TO_FEATURE_MAJOR),
                                                 jnp.transpose(cache_attn_v[l], _TO_FEATURE_MAJOR))
        ys = ys2.reshape(n_streams, n_new, D_MODEL)
        ks_rows.append(k_new.reshape(n_streams, n_new, B_HEADS, B_HDIM))
        vs_rows.append(v_new.reshape(n_streams, n_new, B_HEADS, B_HDIM))
        va_rows.append(va_new.reshape(n_streams, n_new, A_WIDTH))

    return (yp, ys, jnp.stack(kp_rows), jnp.stack(vp_rows), jnp.stack(ks_rows), jnp.stack(vs_rows),
            jnp.stack(va_rows))
```

```python
import functools

import jax
import jax.numpy as jnp
from jax import lax
from jax.experimental import pallas as pl
from jax.experimental.pallas import tpu as pltpu

LANES = 128
SUBLANES = 8
BF16_ROWS = 16
VMEM_LIMIT_BYTES = 60 * 1024 * 1024

D_MODEL = 1024
A_WIDTH = 512
A_GROUPS = 8
A_GDIM = A_WIDTH // A_GROUPS
MLP_CHUNK = 128
B_WIDTH = 512
B_HEADS = 8
B_HDIM = B_WIDTH // B_HEADS
CHUNK = 64
KV_WIN = 512
REL_CLIP = 128
N_REL = 2 * REL_CLIP + 1
EPS = 1e-6
D_IN = 3 * A_WIDTH + 4 * B_WIDTH
NEG = -1e30
LOG2E = 1.4426950408889634
Q_SCALE = B_HDIM ** -0.5 * LOG2E

N_PAIRS = B_HEADS // 2
Q_BLOCK = 2 * CHUNK
K_BLOCK = KV_WIN + Q_BLOCK
SEQ_TILE = 1024
PROJ_ROWS = 512
HALF_COLS = 256
POST_ROWS = 256
SCORE_LEAD = 3
SCORE_SLOTS = SCORE_LEAD + 1

C_U, C_V, C_GA, C_Q, C_K, C_VV, C_GB = (i * 512 for i in range(7))

_CONTRACT_LANES = (((1,), (1,)), ((), ()))

_TO_FEATURE_MAJOR = (0, 2, 3, 1)
_FROM_FEATURE_MAJOR = (0, 3, 1, 2)


def _sigmoid_exp2(neg_arg_log2):
    return 1.0 / (1.0 + jnp.exp2(neg_arg_log2))


def _gelu(x):
    c1 = -2.0 * (2.0 / jnp.pi) ** 0.5 * LOG2E
    return x * _sigmoid_exp2(x * (c1 + (c1 * 0.044715) * (x * x)))


def _silu(x):
    return x * _sigmoid_exp2(x * (-LOG2E))


def _even_lane_mask(shape):
    lane = lax.broadcasted_iota(jnp.int32, shape, len(shape) - 1)
    return (lane % LANES) < B_HDIM


def _ada_block(c_ref, w_ref, b_ref, o_ref):
    c = c_ref[...]
    o_ref[...] = jnp.dot(_silu(c), w_ref[...], preferred_element_type=jnp.float32) + b_ref[...]


def _relb_block(base_ref, bt_ref, bs_ref, n_new):
    row = lax.broadcasted_iota(jnp.int32, (Q_BLOCK, K_BLOCK), 0)
    col = lax.broadcasted_iota(jnp.int32, (Q_BLOCK, K_BLOCK), 1)
    band_lo = jnp.where(row < CHUNK, 0, CHUNK)
    rel = col - band_lo
    outside = jnp.logical_or(rel < 0, rel >= KV_WIN + CHUNK)
    for par in range(2):
        base = base_ref[0, par:par + 1, :] * LOG2E
        x = jnp.broadcast_to(base, (Q_BLOCK, K_BLOCK))
        shift = 1
        while shift < Q_BLOCK:
            x = jnp.where((row & shift) != 0, jnp.roll(x, shift, axis=1), x)
            shift *= 2
        far = jnp.broadcast_to(base[:, 0:1], (Q_BLOCK, K_BLOCK))
        x = jnp.where(col < row, far, x)
        bs_ref[0, par] = x[0:n_new, :]
        bt_ref[0, :, par * Q_BLOCK:(par + 1) * Q_BLOCK] = jnp.where(outside, NEG, x).T


PREP_COLS = 512
N_PREP = D_IN // PREP_COLS
N_ADA = 3 * D_MODEL // PREP_COLS
WOUT_COLS = D_MODEL // N_PAIRS


def _prep_kernel(cp_ref, cs_ref, wada_ref, bada_ref, base_ref, win_ref, wout_ref, bs_ref,
                 modp_ref, mods_ref, bt_ref, bsam_ref, winb_ref, woutb_ref, bfull_ref, *, n_new):
    j = pl.program_id(0)
    winb_ref[...] = win_ref[...].astype(jnp.bfloat16)

    @pl.when(j == 0)
    def _():
        bst = bs_ref[...].T
        even = _even_lane_mask((MLP_CHUNK, LANES))
        for p in range(N_PAIRS):
            bfull_ref[:, p * LANES:(p + 1) * LANES] = jnp.where(
                even, bst[:, 2 * p:2 * p + 1], bst[:, 2 * p + 1:2 * p + 2])

    @pl.when(j < N_ADA)
    def _():
        _ada_block(cp_ref, wada_ref, bada_ref, modp_ref)
        _ada_block(cs_ref, wada_ref, bada_ref, mods_ref)

    @pl.when(j < N_PAIRS)
    def _():
        woutb_ref[...] = wout_ref[...].astype(jnp.bfloat16)
        _relb_block(base_ref, bt_ref, bsam_ref, n_new)


def _prep_call(c_prompt, c_sample, w_ada, b_ada, base, w_in, w_out, b_s, n_new):
    f32, bf16 = jnp.float32, jnp.bfloat16
    bsz, n_streams = c_prompt.shape[0], c_sample.shape[0]

    def upto(n):
        return lambda j: jnp.minimum(j, n - 1)

    ada_j, pair_j = upto(N_ADA), upto(N_PAIRS)
    return pl.pallas_call(
        functools.partial(_prep_kernel, n_new=n_new),
        grid=(N_PREP,),
        in_specs=[
            pl.BlockSpec((bsz, D_MODEL), lambda j: (0, 0)),
            pl.BlockSpec((n_streams, D_MODEL), lambda j: (0, 0)),
            pl.BlockSpec((D_MODEL, PREP_COLS), lambda j: (0, ada_j(j))),
            pl.BlockSpec((1, PREP_COLS), lambda j: (0, ada_j(j))),
            pl.BlockSpec((1, 2, K_BLOCK), lambda j: (pair_j(j), 0, 0)),
            pl.BlockSpec((D_MODEL, PREP_COLS), lambda j: (0, j)),
            pl.BlockSpec((D_MODEL, WOUT_COLS), lambda j: (0, pair_j(j))),
            pl.BlockSpec((A_GROUPS, MLP_CHUNK), lambda j: (0, 0)),
        ],
        out_specs=[
            pl.BlockSpec((bsz, PREP_COLS), lambda j: (0, ada_j(j))),
            pl.BlockSpec((n_streams, PREP_COLS), lambda j: (0, ada_j(j))),
            pl.BlockSpec((1, K_BLOCK, 2 * Q_BLOCK), lambda j: (pair_j(j), 0, 0)),
            pl.BlockSpec((1, 2, n_new, K_BLOCK), lambda j: (pair_j(j), 0, 0, 0)),
            pl.BlockSpec((D_MODEL, PREP_COLS), lambda j: (0, j)),
            pl.BlockSpec((D_MODEL, WOUT_COLS), lambda j: (0, pair_j(j))),
            pl.BlockSpec((MLP_CHUNK, A_WIDTH), lambda j: (0, 0)),
        ],
        out_shape=[
            jax.ShapeDtypeStruct((bsz, 3 * D_MODEL), f32),
            jax.ShapeDtypeStruct((n_streams, 3 * D_MODEL), f32),
            jax.ShapeDtypeStruct((N_PAIRS, K_BLOCK, 2 * Q_BLOCK), f32),
            jax.ShapeDtypeStruct((N_PAIRS, 2, n_new, K_BLOCK), f32),
            jax.ShapeDtypeStruct((D_MODEL, D_IN), bf16),
            jax.ShapeDtypeStruct((D_MODEL, D_MODEL), bf16),
            jax.ShapeDtypeStruct((MLP_CHUNK, A_WIDTH), f32),
        ],
        compiler_params=pltpu.CompilerParams(dimension_semantics=("arbitrary",)),
        name="prep",
    )(c_prompt, c_sample, w_ada, b_ada.reshape(1, 3 * D_MODEL), base.reshape(N_PAIRS, 2, K_BLOCK),
      w_in, w_out, b_s)


def _tril_pairs(ws_ref, wt_ref, n):
    row = lax.broadcasted_iota(jnp.int32, (n, n), 0)
    col = lax.broadcasted_iota(jnp.int32, (n, n), 1)
    keep = col <= row
    for g in range(A_GROUPS):
        w = jnp.where(keep, ws_ref[g, :n, :n], 0.0)
        wt_ref[g // 2, (g % 2) * n:(g % 2 + 1) * n, :] = w.astype(jnp.bfloat16)


def _mod_rows(mod_ref, row):
    return tuple(mod_ref[row, i * D_MODEL:(i + 1) * D_MODEL] for i in range(3))


def _rms_rows(x):
    return x * lax.rsqrt(jnp.mean(x * x, axis=-1, keepdims=True) + EPS)


def _layernorm(x, g, b):
    mu = jnp.mean(x, axis=-1, keepdims=True)
    xc = x - mu
    var = jnp.mean(xc * xc, axis=-1, keepdims=True)
    return xc * lax.rsqrt(var + EPS) * g + b


def _pair_select(stacked, n):
    return jnp.where(_even_lane_mask((n, LANES)), stacked[:n], stacked[n:])


def _split_heads(x):
    return x.reshape(x.shape[0], B_HEADS, B_HDIM)


def _to_feature_major(x):
    return x.T.reshape(B_HEADS, B_HDIM, x.shape[0])


def _store_q(q, qe_ref, qo_ref):
    q = (q * Q_SCALE).astype(jnp.bfloat16)
    even = _even_lane_mask(q.shape)
    zero = jnp.zeros((), jnp.bfloat16)
    qe_ref[...] = jnp.where(even, q, zero)
    qo_ref[...] = jnp.where(even, zero, q)


def _prompt_kernel(x_ref, mod_ref, gpre_ref, win_ref, lng_ref, lnb_ref, ws_ref, bfull_ref,
                   biast_ref, wout_ref, gpost_ref,
                   y_ref, klast_ref, vlast_ref,
                   h_s, usg_s, va_s, qe_s, qo_s, k_s, vt_s, sgb_s, o_s, wt_s, st_s, kf_s, vf_s):
    t = pl.program_id(1)
    nt = pl.num_programs(1)
    T = SEQ_TILE

    @pl.when(jnp.logical_and(pl.program_id(0) == 0, t == 0))
    def _():
        _tril_pairs(ws_ref, wt_s, MLP_CHUNK)

    shift, scale, gate = _mod_rows(mod_ref, pl.ds(pl.program_id(0), 1))

    pre_scale = gpre_ref[...] * (1.0 + scale)

    def proj(c0, rows, width=512):
        return jnp.dot(h_s[rows, :], win_ref[:, c0:c0 + width], preferred_element_type=jnp.float32)

    def proj_pieces(lo):
        rb = slice(lo, lo + PROJ_ROWS)
        halves = [slice(lo, lo + PROJ_ROWS // 2), slice(lo + PROJ_ROWS // 2, lo + PROJ_ROWS)]
        hist = slice(KV_WIN + lo, KV_WIN + lo + PROJ_ROWS)
        keep = slice(lo - (T - KV_WIN), lo - (T - KV_WIN) + PROJ_ROWS) if lo >= T - KV_WIN else None

        def pre_norm():
            for rows in halves:
                h_s[rows, :] = (_rms_rows(x_ref[0, rows, :]) * pre_scale + shift).astype(jnp.bfloat16)

        def cols(ci):
            return slice(ci * HALF_COLS, (ci + 1) * HALF_COLS)

        def u_half(ci):
            usg_s[rb, cols(ci)] = _gelu(proj(C_U + ci * HALF_COLS, rb, HALF_COLS))

        def v_block(ci):
            v = proj(C_VV + ci * HALF_COLS, rb, HALF_COLS)
            vt_s[cols(ci), hist] = v.T.astype(jnp.bfloat16)
            if keep is not None:
                vf_s[keep, cols(ci)] = v

        def va_block(rows):
            va_s[rows, :] = _layernorm(_gelu(proj(C_V, rows)), lng_ref[...], lnb_ref[...]
                                       ).astype(jnp.bfloat16)

        def q_block(ci):
            _store_q(proj(C_Q + ci * HALF_COLS, rb, HALF_COLS),
                     qe_s.at[rb, cols(ci)], qo_s.at[rb, cols(ci)])

        def ga_block(ci):
            usg_s[rb, cols(ci)] = usg_s[rb, cols(ci)] * _silu(proj(C_GA + ci * HALF_COLS, rb, HALF_COLS))

        def gmlp(p):
            lanes = slice(p * LANES, (p + 1) * LANES)
            for c0 in range(lo, lo + PROJ_ROWS, 2 * MLP_CHUNK):
                chunks = [slice(c0, c0 + MLP_CHUNK), slice(c0 + MLP_CHUNK, c0 + 2 * MLP_CHUNK)]
                slabs = jnp.concatenate([va_s[rows, lanes] for rows in chunks], axis=1)
                mix = jnp.dot(wt_s[p], slabs, preferred_element_type=jnp.float32)
                for i, rows in enumerate(chunks):
                    mixed = _pair_select(mix[:, i * LANES:(i + 1) * LANES], MLP_CHUNK)
                    o_s[rows, lanes] = (usg_s[rows, lanes] * (mixed + bfull_ref[:, lanes])
                                        ).astype(jnp.bfloat16)

        def gb_block(ci):
            sgb_s[rb, cols(ci)] = _silu(proj(C_GB + ci * HALF_COLS, rb, HALF_COLS))

        def k_block(ci):
            k = proj(C_K + ci * HALF_COLS, rb, HALF_COLS)
            k_s[hist, cols(ci)] = k.astype(jnp.bfloat16)
            if keep is not None:
                kf_s[keep, cols(ci)] = k

        P = functools.partial
        return [pre_norm, P(u_half, 0), P(u_half, 1), P(v_block, 0), P(va_block, halves[0]),
                P(v_block, 1), P(va_block, halves[1]), P(ga_block, 0), P(gmlp, 0), P(ga_block, 1),
                P(gmlp, 1), P(q_block, 0), P(gmlp, 2), P(q_block, 1), P(gmlp, 3), P(gb_block, 0),
                P(gb_block, 1), P(k_block, 0), P(k_block, 1)]

    def key_lo(qb, first_tile):
        return max(KV_WIN - qb * Q_BLOCK, 0) if first_tile else 0

    def scores(qb, p, slot, first_tile):
        r0, lo = qb * Q_BLOCK, key_lo(qb, first_tile)
        lanes = slice(p * LANES, (p + 1) * LANES)
        qs = jnp.concatenate([qe_s[r0:r0 + Q_BLOCK, lanes],
                              qo_s[r0:r0 + Q_BLOCK, lanes]], axis=0)
        half = (K_BLOCK - lo) // 2
        for a in (lo, lo + half):
            st_s[slot, a:a + half, :] = lax.dot_general(
                k_s[r0 + a:r0 + a + half, lanes], qs, _CONTRACT_LANES,
                preferred_element_type=jnp.float32) + biast_ref[p, a:a + half, :]

    def finish(qb, p, slot, first_tile):
        r0, lo = qb * Q_BLOCK, key_lo(qb, first_tile)
        st = st_s[slot, lo:K_BLOCK, :]
        m = jnp.max(st, axis=0, keepdims=True)
        pt = jnp.exp2(st - m).astype(jnp.bfloat16)
        ones = jnp.ones((BF16_ROWS, K_BLOCK - lo), jnp.bfloat16)
        lhs = jnp.concatenate([vt_s[p * LANES:(p + 1) * LANES, r0 + lo:r0 + K_BLOCK], ones], axis=0)
        ot = jnp.dot(lhs, pt, preferred_element_type=jnp.float32)
        inv_l = 1.0 / ot[LANES:LANES + 1, :]
        ots = [ot[par * B_HDIM:(par + 1) * B_HDIM, par * Q_BLOCK:(par + 1) * Q_BLOCK]
               * inv_l[:, par * Q_BLOCK:(par + 1) * Q_BLOCK] for par in range(2)]
        yb = jnp.concatenate(ots, axis=0).T * sgb_s[r0:r0 + Q_BLOCK, p * LANES:(p + 1) * LANES]
        o_s[r0:r0 + Q_BLOCK, B_WIDTH + p * LANES:B_WIDTH + (p + 1) * LANES] = (
            yb.astype(jnp.bfloat16))

    post_scale = gate * gpost_ref[...]

    def out_block(lo, n):
        rows = slice(lo, lo + n)
        o = jnp.dot(o_s[rows, :], wout_ref[...], preferred_element_type=jnp.float32)
        y_ref[0, rows, :] = x_ref[0, rows, :] + _rms_rows(o) * post_scale

    def attend(first_tile):
        items = [(qb, p) for qb in range(T // Q_BLOCK) for p in range(N_PAIRS)]
        per_block = (PROJ_ROWS // Q_BLOCK) * N_PAIRS
        n_blocks = T // PROJ_ROWS
        for piece in proj_pieces(0):
            piece()
        pending = []
        for j in range(SCORE_LEAD):
            scores(*items[j], j % SCORE_SLOTS, first_tile)
        for i, (qb, p) in enumerate(items):
            blk, pos = divmod(i, per_block)
            if pos == 0 and blk + 1 < n_blocks:
                pending = proj_pieces((blk + 1) * PROJ_ROWS)
            if i + SCORE_LEAD < len(items):
                assert pos < per_block - SCORE_LEAD or not pending
                scores(*items[i + SCORE_LEAD], (i + SCORE_LEAD) % SCORE_SLOTS, first_tile)
            slots_left = per_block - SCORE_LEAD - pos
            if pending and slots_left > 0:
                n_now = -(-len(pending) // slots_left)
                for piece in pending[:n_now]:
                    piece()
                pending = pending[n_now:]
            finish(qb, p, i % SCORE_SLOTS, first_tile)
            rows_done = (qb + 1) * Q_BLOCK
            if p == N_PAIRS - 1 and rows_done % POST_ROWS == 0:
                out_block(rows_done - POST_ROWS, POST_ROWS)

    @pl.when(t == 0)
    def _():
        attend(True)

    @pl.when(t > 0)
    def _():
        attend(False)

    k_s[0:KV_WIN, :] = k_s[T:T + KV_WIN, :]
    vt_s[:, 0:KV_WIN] = vt_s[:, T:T + KV_WIN]

    @pl.when(t == nt - 1)
    def _():
        klast_ref[0] = _to_feature_major(kf_s[...])
        vlast_ref[0] = _to_feature_major(vf_s[...])


def _const_spec(shape):
    nd = len(shape)
    return pl.BlockSpec(shape, lambda b, t: (0,) * nd, pipeline_mode=pl.Buffered(1))


def _prompt_call(x, mod, g_pre, w_in, ln_g, ln_b, w_s, b_full, bias_t, w_out, g_post):
    bsz, seq, _ = x.shape
    T = SEQ_TILE
    assert seq % T == 0 and T % PROJ_ROWS == 0 and KV_WIN % PROJ_ROWS == 0 and T % POST_ROWS == 0
    assert PROJ_ROWS % (2 * MLP_CHUNK) == 0
    nt = seq // T
    f32, bf16 = jnp.float32, jnp.bfloat16
    return pl.pallas_call(
        _prompt_kernel,
        grid=(bsz, nt),
        in_specs=[
            pl.BlockSpec((1, T, D_MODEL), lambda b, t: (b, t, 0)),
            _const_spec((bsz, 3 * D_MODEL)),
            _const_spec((1, D_MODEL)),
            _const_spec((D_MODEL, D_IN)),
            _const_spec((1, A_WIDTH)),
            _const_spec((1, A_WIDTH)),
            _const_spec((A_GROUPS, MLP_CHUNK, MLP_CHUNK)),
            _const_spec((MLP_CHUNK, A_WIDTH)),
            _const_spec((N_PAIRS, K_BLOCK, 2 * Q_BLOCK)),
            _const_spec((D_MODEL, D_MODEL)),
            _const_spec((1, D_MODEL)),
        ],
        out_specs=[
            pl.BlockSpec((1, T, D_MODEL), lambda b, t: (b, t, 0)),
            pl.BlockSpec((1, B_HEADS, B_HDIM, KV_WIN), lambda b, t: (b, 0, 0, 0)),
            pl.BlockSpec((1, B_HEADS, B_HDIM, KV_WIN), lambda b, t: (b, 0, 0, 0)),
        ],
        out_shape=[
            jax.ShapeDtypeStruct((bsz, seq, D_MODEL), f32),
            jax.ShapeDtypeStruct((bsz, B_HEADS, B_HDIM, KV_WIN), f32),
            jax.ShapeDtypeStruct((bsz, B_HEADS, B_HDIM, KV_WIN), f32),
        ],
        scratch_shapes=[
            pltpu.VMEM((T, D_MODEL), bf16),
            pltpu.VMEM((T, A_WIDTH), f32),
            pltpu.VMEM((T, A_WIDTH), bf16),
            pltpu.VMEM((T, B_WIDTH), bf16),
            pltpu.VMEM((T, B_WIDTH), bf16),
            pltpu.VMEM((KV_WIN + T, B_WIDTH), bf16),
            pltpu.VMEM((B_WIDTH, KV_WIN + T), bf16),
            pltpu.VMEM((T, B_WIDTH), f32),
            pltpu.VMEM((T, D_MODEL), bf16),
            pltpu.VMEM((N_PAIRS, 2 * MLP_CHUNK, MLP_CHUNK), bf16),
            pltpu.VMEM((SCORE_SLOTS, K_BLOCK, 2 * Q_BLOCK), f32),
            pltpu.VMEM((KV_WIN, B_WIDTH), f32),
            pltpu.VMEM((KV_WIN, B_WIDTH), f32),
        ],
        compiler_params=pltpu.CompilerParams(
            dimension_semantics=("arbitrary", "arbitrary"),
            vmem_limit_bytes=VMEM_LIMIT_BYTES),
        name="prompt",
    )(x, mod, g_pre, w_in, ln_g, ln_b, w_s, b_full, bias_t, w_out, g_post)


def _sample_kernel(x_ref, mod_ref, gpre_ref, win_ref, lng_ref, lnb_ref, ws_ref, bfull_ref,
                   bias_ref, wout_ref, gpost_ref, ck_ref, cv_ref,
                   y_ref, knew_ref, vnew_ref, vanew_ref,
                   usg_s, va_s, qe_s, qo_s, k_s, v_s, sgb_s, o_s, wt_s, *, n_streams, n_new):
    b = pl.program_id(0)
    S = n_new

    @pl.when(b == 0)
    def _():
        _tril_pairs(ws_ref, wt_s, S)
        gpre = gpre_ref[...]
        hs = []
        for i in range(n_streams):
            xi = x_ref[i * S:(i + 1) * S, :]
            shift, scale, _ = _mod_rows(mod_ref, slice(i, i + 1))
            hs.append(_rms_rows(xi) * (gpre * (1.0 + scale)) + shift)
        h = jnp.concatenate(hs, axis=0).astype(jnp.bfloat16)

        def proj(c0):
            return jnp.dot(h, win_ref[:, c0:c0 + 512], preferred_element_type=jnp.float32)

        va = _layernorm(_gelu(proj(C_V)), lng_ref[...], lnb_ref[...])
        vanew_ref[...] = va
        va_s[...] = va.astype(jnp.bfloat16)
        usg_s[...] = _gelu(proj(C_U)) * _silu(proj(C_GA))
        _store_q(proj(C_Q), qe_s, qo_s)
        k = proj(C_K)
        v = proj(C_VV)
        knew_ref[...] = _split_heads(k)
        vnew_ref[...] = _split_heads(v)
        k_s[...] = k.astype(jnp.bfloat16)
        v_s[...] = v.astype(jnp.bfloat16)
        sgb_s[...] = _silu(proj(C_GB))

    r0 = pl.multiple_of(b * S, S)
    rows = pl.ds(r0, S)
    pair_lanes = [slice(p * LANES, (p + 1) * LANES) for p in range(N_PAIRS)]
    for p, lanes in enumerate(pair_lanes):
        mix = jnp.dot(wt_s[p], va_s[rows, lanes], preferred_element_type=jnp.float32)
        mixed = _pair_select(mix, S) + bfull_ref[0:S, lanes]
        o_s[rows, lanes] = (usg_s[rows, lanes] * mixed).astype(jnp.bfloat16)

    scores = []
    for p, lanes in enumerate(pair_lanes):
        qs = jnp.concatenate([qe_s[rows, lanes], qo_s[rows, lanes]], axis=0)
        kct = ck_ref[0, 2 * p:2 * p + 2].reshape(LANES, KV_WIN).astype(jnp.bfloat16)
        bias = jnp.concatenate([bias_ref[p, 0], bias_ref[p, 1]], axis=0)
        s1 = jnp.dot(qs, kct, preferred_element_type=jnp.float32) + bias[:, 0:KV_WIN]
        s2 = lax.dot_general(qs, k_s[rows, lanes], _CONTRACT_LANES,
                             preferred_element_type=jnp.float32) + bias[:, KV_WIN:KV_WIN + S]
        scores.append((s1, s2))
    weights = []
    for s1, s2 in scores:
        m = jnp.maximum(jnp.max(s1, axis=-1, keepdims=True), jnp.max(s2, axis=-1, keepdims=True))
        e1 = jnp.exp2(s1 - m)
        e2 = jnp.exp2(s2 - m)
        l = jnp.sum(e1, axis=-1, keepdims=True) + jnp.sum(e2, axis=-1, keepdims=True)
        weights.append((e1.astype(jnp.bfloat16), e2.astype(jnp.bfloat16), 1.0 / l))
    for p, lanes in enumerate(pair_lanes):
        e1, e2, inv_l = weights[p]
        vct = cv_ref[0, 2 * p:2 * p + 2].reshape(LANES, KV_WIN).astype(jnp.bfloat16)
        o2 = (lax.dot_general(e1, vct, _CONTRACT_LANES, preferred_element_type=jnp.float32)
              + jnp.dot(e2, v_s[rows, lanes], preferred_element_type=jnp.float32))
        yb = _pair_select(o2 * inv_l, S) * sgb_s[rows, lanes]
        o_s[rows, B_WIDTH + p * LANES:B_WIDTH + (p + 1) * LANES] = yb.astype(jnp.bfloat16)

    @pl.when(b == n_streams - 1)
    def _():
        o = jnp.dot(o_s[...], wout_ref[...], preferred_element_type=jnp.float32)
        on = _rms_rows(o) * gpost_ref[...]
        for i in range(n_streams):
            sl = slice(i * S, (i + 1) * S)
            y_ref[sl, :] = x_ref[sl, :] + _mod_rows(mod_ref, slice(i, i + 1))[2] * on[sl, :]


def _sample_call(x, mod, g_pre, w_in, ln_g, ln_b, w_s, b_full, bias_s, w_out, g_post, ck, cv):
    n_streams, n_new, _ = x.shape
    assert n_new % BF16_ROWS == 0 and n_new <= CHUNK
    rows = n_streams * n_new
    f32, bf16 = jnp.float32, jnp.bfloat16

    def const(shape):
        nd = len(shape)
        return pl.BlockSpec(shape, lambda b: (0,) * nd, pipeline_mode=pl.Buffered(1))

    kern = functools.partial(_sample_kernel, n_streams=n_streams, n_new=n_new)
    return pl.pallas_call(
        kern,
        grid=(n_streams,),
        in_specs=[
            const((rows, D_MODEL)),
            const((n_streams, 3 * D_MODEL)),
            const((1, D_MODEL)),
            const((D_MODEL, D_IN)),
            const((1, A_WIDTH)),
            const((1, A_WIDTH)),
            const((A_GROUPS, MLP_CHUNK, MLP_CHUNK)),
            const((MLP_CHUNK, A_WIDTH)),
            const((N_PAIRS, 2, n_new, K_BLOCK)),
            const((D_MODEL, D_MODEL)),
            const((1, D_MODEL)),
            pl.BlockSpec((1, B_HEADS, B_HDIM, KV_WIN), lambda b: (b, 0, 0, 0)),
            pl.BlockSpec((1, B_HEADS, B_HDIM, KV_WIN), lambda b: (b, 0, 0, 0)),
        ],
        out_specs=[
            pl.BlockSpec((rows, D_MODEL), lambda b: (0, 0)),
            pl.BlockSpec((rows, B_HEADS, B_HDIM), lambda b: (0, 0, 0)),
            pl.BlockSpec((rows, B_HEADS, B_HDIM), lambda b: (0, 0, 0)),
            pl.BlockSpec((rows, A_WIDTH), lambda b: (0, 0)),
        ],
        out_shape=[
            jax.ShapeDtypeStruct((rows, D_MODEL), f32),
            jax.ShapeDtypeStruct((rows, B_HEADS, B_HDIM), f32),
            jax.ShapeDtypeStruct((rows, B_HEADS, B_HDIM), f32),
            jax.ShapeDtypeStruct((rows, A_WIDTH), f32),
        ],
        scratch_shapes=[
            pltpu.VMEM((rows, A_WIDTH), f32),
            pltpu.VMEM((rows, A_WIDTH), bf16),
            pltpu.VMEM((rows, B_WIDTH), bf16),
            pltpu.VMEM((rows, B_WIDTH), bf16),
            pltpu.VMEM((rows, B_WIDTH), bf16),
            pltpu.VMEM((rows, B_WIDTH), bf16),
            pltpu.VMEM((rows, B_WIDTH), f32),
            pltpu.VMEM((rows, D_MODEL), bf16),
            pltpu.VMEM((N_PAIRS, 2 * n_new, n_new), bf16),
        ],
        compiler_params=pltpu.CompilerParams(
            dimension_semantics=("arbitrary",),
            vmem_limit_bytes=VMEM_LIMIT_BYTES),
        name="sample",
    )(x.reshape(rows, D_MODEL), mod, g_pre, w_in, ln_g, ln_b, w_s, b_full, bias_s, w_out, g_post,
      ck, cv)


def _bias_base(rel_bias):
    n_far = KV_WIN - REL_CLIP
    far = jnp.broadcast_to(rel_bias[:, N_REL - 1:N_REL], (B_HEADS, n_far))
    near = rel_bias[:, N_REL - 1:0:-1]
    return jnp.concatenate([far, near], axis=1)


def kernel(x_prompt, x_sample, cache_attn_k, cache_attn_v, c_prompt, c_sample, g_pre, w_ada, b_ada,
           w_in, ln_g, ln_b, w_s, b_s, rel_bias, w_out, g_post):
    depth = g_pre.shape[0]
    bsz, seq, _ = x_prompt.shape
    n_streams, n_new, _ = x_sample.shape
    win = cache_attn_k.shape[2]
    assert win == KV_WIN

    yp, ys = x_prompt, x_sample
    kp_rows, vp_rows, ks_rows, vs_rows, va_rows = [], [], [], [], []
    for l in range(depth):
        mod_p, mod_s, bias_t, bias_s, w_in_b, w_out_b, b_full = _prep_call(
            c_prompt, c_sample, w_ada[l], b_ada[l], _bias_base(rel_bias[l]), w_in[l], w_out[l],
            b_s[l], n_new)
        pre = (g_pre[l].reshape(1, D_MODEL), w_in_b, ln_g[l].reshape(1, A_WIDTH),
               ln_b[l].reshape(1, A_WIDTH), w_s[l], b_full)
        post = (w_out_b, g_post[l].reshape(1, D_MODEL))

        yp, k_last, v_last = _prompt_call(yp, mod_p, *pre, bias_t, *post)
        kp_rows.append(jnp.transpose(k_last, _FROM_FEATURE_MAJOR))
        vp_rows.append(jnp.transpose(v_last, _FROM_FEATURE_MAJOR))

        ys2, k_new, v_new, va_new = _sample_call(ys, mod_s, *pre, bias_s, *post,
                                                 jnp.transpose(cache_attn_k[l], _TO_FEATURE_MAJOR),
                                                 jnp.transpose(cache_attn_v[l], _TO_FEATURE_MAJOR))
        ys = ys2.reshape(n_streams, n_new, D_MODEL)
        ks_rows.append(k_new.reshape(n_streams, n_new, B_HEADS, B_HDIM))
        vs_rows.append(v_new.reshape(n_streams, n_new, B_HEADS, B_HDIM))
        va_rows.append(va_new.reshape(n_streams, n_new, A_WIDTH))

    return (yp, ys, jnp.stack(kp_rows), jnp.stack(vp_rows), jnp.stack(ks_rows), jnp.stack(vs_rows),
            jnp.stack(va_rows))
```

```python
import functools

import jax
import jax.numpy as jnp
from jax import lax
from jax.experimental import pallas as pl
from jax.experimental.pallas import tpu as pltpu

LANES = 128
SUBLANES = 8
BF16_ROWS = 16
VMEM_LIMIT_BYTES = 60 * 1024 * 1024

D_MODEL = 1024
A_WIDTH = 512
A_GROUPS = 8
A_GDIM = A_WIDTH // A_GROUPS
MLP_CHUNK = 128
B_WIDTH = 512
B_HEADS = 8
B_HDIM = B_WIDTH // B_HEADS
CHUNK = 64
KV_WIN = 512
REL_CLIP = 128
N_REL = 2 * REL_CLIP + 1
EPS = 1e-6
D_IN = 3 * A_WIDTH + 4 * B_WIDTH
NEG = -1e30
LOG2E = 1.4426950408889634
Q_SCALE = B_HDIM ** -0.5 * LOG2E

N_PAIRS = B_HEADS // 2
Q_BLOCK = 2 * CHUNK
K_BLOCK = KV_WIN + Q_BLOCK
SEQ_TILE = 1024
PROJ_ROWS = 512
HALF_COLS = 256
POST_ROWS = 256
OUT_ROWS = 512
OUT_COLS = 256
SCORE_LEAD = 3
SCORE_SLOTS = SCORE_LEAD + 1

C_U, C_V, C_GA, C_Q, C_K, C_VV, C_GB = (i * 512 for i in range(7))

_CONTRACT_LANES = (((1,), (1,)), ((), ()))

_TO_FEATURE_MAJOR = (0, 2, 3, 1)
_FROM_FEATURE_MAJOR = (0, 3, 1, 2)


def _sigmoid_exp2(neg_arg_log2):
    return 1.0 / (1.0 + jnp.exp2(neg_arg_log2))


def _gelu(x):
    c1 = -2.0 * (2.0 / jnp.pi) ** 0.5 * LOG2E
    return x * _sigmoid_exp2(x * (c1 + (c1 * 0.044715) * (x * x)))


def _silu(x):
    return x * _sigmoid_exp2(x * (-LOG2E))


def _even_lane_mask(shape):
    lane = lax.broadcasted_iota(jnp.int32, shape, len(shape) - 1)
    return (lane % LANES) < B_HDIM


def _ada_block(c_ref, w_ref, b_ref, o_ref):
    c = c_ref[...]
    o_ref[...] = jnp.dot(_silu(c), w_ref[...], preferred_element_type=jnp.float32) + b_ref[...]


def _relb_block(base_ref, bt_ref, bs_ref, n_new):
    row = lax.broadcasted_iota(jnp.int32, (Q_BLOCK, K_BLOCK), 0)
    col = lax.broadcasted_iota(jnp.int32, (Q_BLOCK, K_BLOCK), 1)
    band_lo = jnp.where(row < CHUNK, 0, CHUNK)
    rel = col - band_lo
    outside = jnp.logical_or(rel < 0, rel >= KV_WIN + CHUNK)
    for par in range(2):
        base = base_ref[0, par:par + 1, :] * LOG2E
        x = jnp.broadcast_to(base, (Q_BLOCK, K_BLOCK))
        shift = 1
        while shift < Q_BLOCK:
            x = jnp.where((row & shift) != 0, jnp.roll(x, shift, axis=1), x)
            shift *= 2
        far = jnp.broadcast_to(base[:, 0:1], (Q_BLOCK, K_BLOCK))
        x = jnp.where(col < row, far, x)
        bs_ref[0, par] = x[0:n_new, :]
        bt_ref[0, :, par * Q_BLOCK:(par + 1) * Q_BLOCK] = jnp.where(outside, NEG, x).T


PREP_COLS = 512
N_PREP = D_IN // PREP_COLS
N_ADA = 3 * D_MODEL // PREP_COLS
WOUT_COLS = D_MODEL // N_PAIRS


def _prep_kernel(cp_ref, cs_ref, wada_ref, bada_ref, base_ref, win_ref, wout_ref, bs_ref,
                 modp_ref, mods_ref, bt_ref, bsam_ref, winb_ref, woutb_ref, bfull_ref, *, n_new):
    j = pl.program_id(0)
    winb_ref[...] = win_ref[...].astype(jnp.bfloat16)

    @pl.when(j == 0)
    def _():
        bst = bs_ref[...].T
        even = _even_lane_mask((MLP_CHUNK, LANES))
        for p in range(N_PAIRS):
            bfull_ref[:, p * LANES:(p + 1) * LANES] = jnp.where(
                even, bst[:, 2 * p:2 * p + 1], bst[:, 2 * p + 1:2 * p + 2])

    @pl.when(j < N_ADA)
    def _():
        _ada_block(cp_ref, wada_ref, bada_ref, modp_ref)
        _ada_block(cs_ref, wada_ref, bada_ref, mods_ref)

    @pl.when(j < N_PAIRS)
    def _():
        woutb_ref[...] = wout_ref[...].astype(jnp.bfloat16)
        _relb_block(base_ref, bt_ref, bsam_ref, n_new)


def _prep_call(c_prompt, c_sample, w_ada, b_ada, base, w_in, w_out, b_s, n_new):
    f32, bf16 = jnp.float32, jnp.bfloat16
    bsz, n_streams = c_prompt.shape[0], c_sample.shape[0]

    def upto(n):
        return lambda j: jnp.minimum(j, n - 1)

    ada_j, pair_j = upto(N_ADA), upto(N_PAIRS)
    return pl.pallas_call(
        functools.partial(_prep_kernel, n_new=n_new),
        grid=(N_PREP,),
        in_specs=[
            pl.BlockSpec((bsz, D_MODEL), lambda j: (0, 0)),
            pl.BlockSpec((n_streams, D_MODEL), lambda j: (0, 0)),
            pl.BlockSpec((D_MODEL, PREP_COLS), lambda j: (0, ada_j(j))),
            pl.BlockSpec((1, PREP_COLS), lambda j: (0, ada_j(j))),
            pl.BlockSpec((1, 2, K_BLOCK), lambda j: (pair_j(j), 0, 0)),
            pl.BlockSpec((D_MODEL, PREP_COLS), lambda j: (0, j)),
            pl.BlockSpec((D_MODEL, WOUT_COLS), lambda j: (0, pair_j(j))),
            pl.BlockSpec((A_GROUPS, MLP_CHUNK), lambda j: (0, 0)),
        ],
        out_specs=[
            pl.BlockSpec((bsz, PREP_COLS), lambda j: (0, ada_j(j))),
            pl.BlockSpec((n_streams, PREP_COLS), lambda j: (0, ada_j(j))),
            pl.BlockSpec((1, K_BLOCK, 2 * Q_BLOCK), lambda j: (pair_j(j), 0, 0)),
            pl.BlockSpec((1, 2, n_new, K_BLOCK), lambda j: (pair_j(j), 0, 0, 0)),
            pl.BlockSpec((D_MODEL, PREP_COLS), lambda j: (0, j)),
            pl.BlockSpec((D_MODEL, WOUT_COLS), lambda j: (0, pair_j(j))),
            pl.BlockSpec((MLP_CHUNK, A_WIDTH), lambda j: (0, 0)),
        ],
        out_shape=[
            jax.ShapeDtypeStruct((bsz, 3 * D_MODEL), f32),
            jax.ShapeDtypeStruct((n_streams, 3 * D_MODEL), f32),
            jax.ShapeDtypeStruct((N_PAIRS, K_BLOCK, 2 * Q_BLOCK), f32),
            jax.ShapeDtypeStruct((N_PAIRS, 2, n_new, K_BLOCK), f32),
            jax.ShapeDtypeStruct((D_MODEL, D_IN), bf16),
            jax.ShapeDtypeStruct((D_MODEL, D_MODEL), bf16),
            jax.ShapeDtypeStruct((MLP_CHUNK, A_WIDTH), f32),
        ],
        compiler_params=pltpu.CompilerParams(dimension_semantics=("arbitrary",)),
        name="prep",
    )(c_prompt, c_sample, w_ada, b_ada.reshape(1, 3 * D_MODEL), base.reshape(N_PAIRS, 2, K_BLOCK),
      w_in, w_out, b_s)


def _tril_pairs(ws_ref, wt_ref, n):
    row = lax.broadcasted_iota(jnp.int32, (n, n), 0)
    col = lax.broadcasted_iota(jnp.int32, (n, n), 1)
    keep = col <= row
    for g in range(A_GROUPS):
        w = jnp.where(keep, ws_ref[g, :n, :n], 0.0)
        wt_ref[g // 2, (g % 2) * n:(g % 2 + 1) * n, :] = w.astype(jnp.bfloat16)


def _mod_rows(mod_ref, row):
    return tuple(mod_ref[row, i * D_MODEL:(i + 1) * D_MODEL] for i in range(3))


def _rms_rows(x):
    return x * lax.rsqrt(jnp.mean(x * x, axis=-1, keepdims=True) + EPS)


def _layernorm(x, g, b):
    mu = jnp.mean(x, axis=-1, keepdims=True)
    xc = x - mu
    var = jnp.mean(xc * xc, axis=-1, keepdims=True)
    return xc * lax.rsqrt(var + EPS) * g + b


def _pair_select(stacked, n):
    return jnp.where(_even_lane_mask((n, LANES)), stacked[:n], stacked[n:])


def _split_heads(x):
    return x.reshape(x.shape[0], B_HEADS, B_HDIM)


def _to_feature_major(x):
    return x.T.reshape(B_HEADS, B_HDIM, x.shape[0])


def _store_q(q, qe_ref, qo_ref):
    q = (q * Q_SCALE).astype(jnp.bfloat16)
    even = _even_lane_mask(q.shape)
    zero = jnp.zeros((), jnp.bfloat16)
    qe_ref[...] = jnp.where(even, q, zero)
    qo_ref[...] = jnp.where(even, zero, q)


def _prompt_kernel(x_ref, mod_ref, gpre_ref, win_ref, lng_ref, lnb_ref, ws_ref, bfull_ref,
                   biast_ref, wout_ref, gpost_ref,
                   y_ref, klast_ref, vlast_ref,
                   h_s, usg_s, va_s, qe_s, qo_s, k_s, vt_s, sgb_s, o_s, wt_s, st_s, kf_s, vf_s, of_s):
    t = pl.program_id(1)
    nt = pl.num_programs(1)
    T = SEQ_TILE

    @pl.when(jnp.logical_and(pl.program_id(0) == 0, t == 0))
    def _():
        _tril_pairs(ws_ref, wt_s, MLP_CHUNK)

    shift, scale, gate = _mod_rows(mod_ref, pl.ds(pl.program_id(0), 1))

    pre_scale = gpre_ref[...] * (1.0 + scale)

    def proj(c0, rows, width=512):
        return jnp.dot(h_s[rows, :], win_ref[:, c0:c0 + width], preferred_element_type=jnp.float32)

    def proj_pieces(lo):
        rb = slice(lo, lo + PROJ_ROWS)
        halves = [slice(lo, lo + PROJ_ROWS // 2), slice(lo + PROJ_ROWS // 2, lo + PROJ_ROWS)]
        hist = slice(KV_WIN + lo, KV_WIN + lo + PROJ_ROWS)
        keep = slice(lo - (T - KV_WIN), lo - (T - KV_WIN) + PROJ_ROWS) if lo >= T - KV_WIN else None

        def pre_norm():
            for rows in halves:
                h_s[rows, :] = (_rms_rows(x_ref[0, rows, :]) * pre_scale + shift).astype(jnp.bfloat16)

        def cols(ci):
            return slice(ci * HALF_COLS, (ci + 1) * HALF_COLS)

        def u_half(ci):
            usg_s[rb, cols(ci)] = _gelu(proj(C_U + ci * HALF_COLS, rb, HALF_COLS))

        def v_block(ci):
            v = proj(C_VV + ci * HALF_COLS, rb, HALF_COLS)
            vt_s[cols(ci), hist] = v.T.astype(jnp.bfloat16)
            if keep is not None:
                vf_s[keep, cols(ci)] = v

        def va_block(rows):
            va_s[rows, :] = _layernorm(_gelu(proj(C_V, rows)), lng_ref[...], lnb_ref[...]
                                       ).astype(jnp.bfloat16)

        def q_block(ci):
            _store_q(proj(C_Q + ci * HALF_COLS, rb, HALF_COLS),
                     qe_s.at[rb, cols(ci)], qo_s.at[rb, cols(ci)])

        def ga_block(ci):
            usg_s[rb, cols(ci)] = usg_s[rb, cols(ci)] * _silu(proj(C_GA + ci * HALF_COLS, rb, HALF_COLS))

        def gmlp(p):
            lanes = slice(p * LANES, (p + 1) * LANES)
            for c0 in range(lo, lo + PROJ_ROWS, 2 * MLP_CHUNK):
                chunks = [slice(c0, c0 + MLP_CHUNK), slice(c0 + MLP_CHUNK, c0 + 2 * MLP_CHUNK)]
                slabs = jnp.concatenate([va_s[rows, lanes] for rows in chunks], axis=1)
                mix = jnp.dot(wt_s[p], slabs, preferred_element_type=jnp.float32)
                for i, rows in enumerate(chunks):
                    mixed = _pair_select(mix[:, i * LANES:(i + 1) * LANES], MLP_CHUNK)
                    o_s[rows, lanes] = (usg_s[rows, lanes] * (mixed + bfull_ref[:, lanes])
                                        ).astype(jnp.bfloat16)

        def gb_block(ci):
            sgb_s[rb, cols(ci)] = _silu(proj(C_GB + ci * HALF_COLS, rb, HALF_COLS))

        def k_block(ci):
            k = proj(C_K + ci * HALF_COLS, rb, HALF_COLS)
            k_s[hist, cols(ci)] = k.astype(jnp.bfloat16)
            if keep is not None:
                kf_s[keep, cols(ci)] = k

        P = functools.partial
        return [pre_norm, P(u_half, 0), P(u_half, 1), P(v_block, 0), P(va_block, halves[0]),
                P(v_block, 1), P(va_block, halves[1]), P(ga_block, 0), P(gmlp, 0), P(ga_block, 1),
                P(gmlp, 1), P(q_block, 0), P(gmlp, 2), P(q_block, 1), P(gmlp, 3), P(gb_block, 0),
                P(gb_block, 1), P(k_block, 0), P(k_block, 1)]

    def key_lo(qb, first_tile):
        return max(KV_WIN - qb * Q_BLOCK, 0) if first_tile else 0

    def scores(qb, p, slot, first_tile):
        r0, lo = qb * Q_BLOCK, key_lo(qb, first_tile)
        lanes = slice(p * LANES, (p + 1) * LANES)
        qs = jnp.concatenate([qe_s[r0:r0 + Q_BLOCK, lanes],
                              qo_s[r0:r0 + Q_BLOCK, lanes]], axis=0)
        half = (K_BLOCK - lo) // 2
        for a in (lo, lo + half):
            st_s[slot, a:a + half, :] = lax.dot_general(
                k_s[r0 + a:r0 + a + half, lanes], qs, _CONTRACT_LANES,
                preferred_element_type=jnp.float32) + biast_ref[p, a:a + half, :]

    def finish(qb, p, slot, first_tile):
        r0, lo = qb * Q_BLOCK, key_lo(qb, first_tile)
        st = st_s[slot, lo:K_BLOCK, :]
        m = jnp.max(st, axis=0, keepdims=True)
        pt = jnp.exp2(st - m).astype(jnp.bfloat16)
        ones = jnp.ones((BF16_ROWS, K_BLOCK - lo), jnp.bfloat16)
        lhs = jnp.concatenate([vt_s[p * LANES:(p + 1) * LANES, r0 + lo:r0 + K_BLOCK], ones], axis=0)
        ot = jnp.dot(lhs, pt, preferred_element_type=jnp.float32)
        inv_l = 1.0 / ot[LANES:LANES + 1, :]
        ots = [ot[par * B_HDIM:(par + 1) * B_HDIM, par * Q_BLOCK:(par + 1) * Q_BLOCK]
               * inv_l[:, par * Q_BLOCK:(par + 1) * Q_BLOCK] for par in range(2)]
        yb = jnp.concatenate(ots, axis=0).T * sgb_s[r0:r0 + Q_BLOCK, p * LANES:(p + 1) * LANES]
        o_s[r0:r0 + Q_BLOCK, B_WIDTH + p * LANES:B_WIDTH + (p + 1) * LANES] = (
            yb.astype(jnp.bfloat16))

    post_scale = gate * gpost_ref[...]

    def out_pieces(lo):
        rows = slice(lo, lo + OUT_ROWS)

        def col_piece(c0):
            of_s[:, c0:c0 + OUT_COLS] = jnp.dot(o_s[rows, :], wout_ref[:, c0:c0 + OUT_COLS],
                                                preferred_element_type=jnp.float32)

        def norm_piece():
            y_ref[0, rows, :] = x_ref[0, rows, :] + _rms_rows(of_s[...]) * post_scale

        return [functools.partial(col_piece, c0) for c0 in range(0, D_MODEL, OUT_COLS)] + [norm_piece]

    def attend(first_tile):
        items = [(qb, p) for qb in range(T // Q_BLOCK) for p in range(N_PAIRS)]
        per_block = (PROJ_ROWS // Q_BLOCK) * N_PAIRS
        n_blocks = T // PROJ_ROWS
        for piece in proj_pieces(0):
            piece()
        pending = []
        post_pending = []
        for j in range(SCORE_LEAD):
            scores(*items[j], j % SCORE_SLOTS, first_tile)
        for i, (qb, p) in enumerate(items):
            blk, pos = divmod(i, per_block)
            if pos == 0 and blk + 1 < n_blocks:
                pending = proj_pieces((blk + 1) * PROJ_ROWS)
            if i + SCORE_LEAD < len(items):
                assert pos < per_block - SCORE_LEAD or not pending
                scores(*items[i + SCORE_LEAD], (i + SCORE_LEAD) % SCORE_SLOTS, first_tile)
            finish(qb, p, i % SCORE_SLOTS, first_tile)
            slots_left = per_block - SCORE_LEAD - pos
            if pending and slots_left > 0:
                n_now = -(-len(pending) // slots_left)
                for piece in pending[:n_now]:
                    piece()
                pending = pending[n_now:]
            if post_pending:
                post_pending.pop(0)()
            rows_done = (qb + 1) * Q_BLOCK
            if p == N_PAIRS - 1 and rows_done % OUT_ROWS == 0:
                post_pending.extend(out_pieces(rows_done - OUT_ROWS))
        for piece in post_pending:
            piece()

    @pl.when(t == 0)
    def _():
        attend(True)

    @pl.when(t > 0)
    def _():
        attend(False)

    k_s[0:KV_WIN, :] = k_s[T:T + KV_WIN, :]
    vt_s[:, 0:KV_WIN] = vt_s[:, T:T + KV_WIN]

    @pl.when(t == nt - 1)
    def _():
        klast_ref[0] = _to_feature_major(kf_s[...])
        vlast_ref[0] = _to_feature_major(vf_s[...])


def _const_spec(shape):
    nd = len(shape)
    return pl.BlockSpec(shape, lambda b, t: (0,) * nd, pipeline_mode=pl.Buffered(1))


def _prompt_call(x, mod, g_pre, w_in, ln_g, ln_b, w_s, b_full, bias_t, w_out, g_post):
    bsz, seq, _ = x.shape
    T = SEQ_TILE
    assert seq % T == 0 and T % PROJ_ROWS == 0 and KV_WIN % PROJ_ROWS == 0 and T % POST_ROWS == 0
    assert PROJ_ROWS % (2 * MLP_CHUNK) == 0
    nt = seq // T
    f32, bf16 = jnp.float32, jnp.bfloat16
    return pl.pallas_call(
        _prompt_kernel,
        grid=(bsz, nt),
        in_specs=[
            pl.BlockSpec((1, T, D_MODEL), lambda b, t: (b, t, 0)),
            _const_spec((bsz, 3 * D_MODEL)),
            _const_spec((1, D_MODEL)),
            _const_spec((D_MODEL, D_IN)),
            _const_spec((1, A_WIDTH)),
            _const_spec((1, A_WIDTH)),
            _const_spec((A_GROUPS, MLP_CHUNK, MLP_CHUNK)),
            _const_spec((MLP_CHUNK, A_WIDTH)),
            _const_spec((N_PAIRS, K_BLOCK, 2 * Q_BLOCK)),
            _const_spec((D_MODEL, D_MODEL)),
            _const_spec((1, D_MODEL)),
        ],
        out_specs=[
            pl.BlockSpec((1, T, D_MODEL), lambda b, t: (b, t, 0)),
            pl.BlockSpec((1, B_HEADS, B_HDIM, KV_WIN), lambda b, t: (b, 0, 0, 0)),
            pl.BlockSpec((1, B_HEADS, B_HDIM, KV_WIN), lambda b, t: (b, 0, 0, 0)),
        ],
        out_shape=[
            jax.ShapeDtypeStruct((bsz, seq, D_MODEL), f32),
            jax.ShapeDtypeStruct((bsz, B_HEADS, B_HDIM, KV_WIN), f32),
            jax.ShapeDtypeStruct((bsz, B_HEADS, B_HDIM, KV_WIN), f32),
        ],
        scratch_shapes=[
            pltpu.VMEM((T, D_MODEL), bf16),
            pltpu.VMEM((T, A_WIDTH), f32),
            pltpu.VMEM((T, A_WIDTH), bf16),
            pltpu.VMEM((T, B_WIDTH), bf16),
            pltpu.VMEM((T, B_WIDTH), bf16),
            pltpu.VMEM((KV_WIN + T, B_WIDTH), bf16),
            pltpu.VMEM((B_WIDTH, KV_WIN + T), bf16),
            pltpu.VMEM((T, B_WIDTH), f32),
            pltpu.VMEM((T, D_MODEL), bf16),
            pltpu.VMEM((N_PAIRS, 2 * MLP_CHUNK, MLP_CHUNK), bf16),
            pltpu.VMEM((SCORE_SLOTS, K_BLOCK, 2 * Q_BLOCK), f32),
            pltpu.VMEM((KV_WIN, B_WIDTH), f32),
            pltpu.VMEM((KV_WIN, B_WIDTH), f32),
            pltpu.VMEM((OUT_ROWS, D_MODEL), f32),
        ],
        compiler_params=pltpu.CompilerParams(
            dimension_semantics=("arbitrary", "arbitrary"),
            vmem_limit_bytes=VMEM_LIMIT_BYTES),
        name="prompt",
    )(x, mod, g_pre, w_in, ln_g, ln_b, w_s, b_full, bias_t, w_out, g_post)


def _sample_kernel(x_ref, mod_ref, gpre_ref, win_ref, lng_ref, lnb_ref, ws_ref, bfull_ref,
                   bias_ref, wout_ref, gpost_ref, ck_ref, cv_ref,
                   y_ref, knew_ref, vnew_ref, vanew_ref,
                   usg_s, va_s, qe_s, qo_s, k_s, v_s, sgb_s, o_s, wt_s, *, n_streams, n_new):
    b = pl.program_id(0)
    S = n_new

    @pl.when(b == 0)
    def _():
        _tril_pairs(ws_ref, wt_s, S)
        gpre = gpre_ref[...]
        hs = []
        for i in range(n_streams):
            xi = x_ref[i * S:(i + 1) * S, :]
            shift, scale, _ = _mod_rows(mod_ref, slice(i, i + 1))
            hs.append(_rms_rows(xi) * (gpre * (1.0 + scale)) + shift)
        h = jnp.concatenate(hs, axis=0).astype(jnp.bfloat16)

        def proj(c0):
            return jnp.dot(h, win_ref[:, c0:c0 + 512], preferred_element_type=jnp.float32)

        va = _layernorm(_gelu(proj(C_V)), lng_ref[...], lnb_ref[...])
        vanew_ref[...] = va
        va_s[...] = va.astype(jnp.bfloat16)
        usg_s[...] = _gelu(proj(C_U)) * _silu(proj(C_GA))
        _store_q(proj(C_Q), qe_s, qo_s)
        k = proj(C_K)
        v = proj(C_VV)
        knew_ref[...] = _split_heads(k)
        vnew_ref[...] = _split_heads(v)
        k_s[...] = k.astype(jnp.bfloat16)
        v_s[...] = v.astype(jnp.bfloat16)
        sgb_s[...] = _silu(proj(C_GB))

    r0 = pl.multiple_of(b * S, S)
    rows = pl.ds(r0, S)
    pair_lanes = [slice(p * LANES, (p + 1) * LANES) for p in range(N_PAIRS)]
    for p, lanes in enumerate(pair_lanes):
        mix = jnp.dot(wt_s[p], va_s[rows, lanes], preferred_element_type=jnp.float32)
        mixed = _pair_select(mix, S) + bfull_ref[0:S, lanes]
        o_s[rows, lanes] = (usg_s[rows, lanes] * mixed).astype(jnp.bfloat16)

    scores = []
    for p, lanes in enumerate(pair_lanes):
        qs = jnp.concatenate([qe_s[rows, lanes], qo_s[rows, lanes]], axis=0)
        kct = ck_ref[0, 2 * p:2 * p + 2].reshape(LANES, KV_WIN).astype(jnp.bfloat16)
        bias = jnp.concatenate([bias_ref[p, 0], bias_ref[p, 1]], axis=0)
        s1 = jnp.dot(qs, kct, preferred_element_type=jnp.float32) + bias[:, 0:KV_WIN]
        s2 = lax.dot_general(qs, k_s[rows, lanes], _CONTRACT_LANES,
                             preferred_element_type=jnp.float32) + bias[:, KV_WIN:KV_WIN + S]
        scores.append((s1, s2))
    weights = []
    for s1, s2 in scores:
        m = jnp.maximum(jnp.max(s1, axis=-1, keepdims=True), jnp.max(s2, axis=-1, keepdims=True))
        e1 = jnp.exp2(s1 - m)
        e2 = jnp.exp2(s2 - m)
        l = jnp.sum(e1, axis=-1, keepdims=True) + jnp.sum(e2, axis=-1, keepdims=True)
        weights.append((e1.astype(jnp.bfloat16), e2.astype(jnp.bfloat16), 1.0 / l))
    for p, lanes in enumerate(pair_lanes):
        e1, e2, inv_l = weights[p]
        vct = cv_ref[0, 2 * p:2 * p + 2].reshape(LANES, KV_WIN).astype(jnp.bfloat16)
        o2 = (lax.dot_general(e1, vct, _CONTRACT_LANES, preferred_element_type=jnp.float32)
              + jnp.dot(e2, v_s[rows, lanes], preferred_element_type=jnp.float32))
        yb = _pair_select(o2 * inv_l, S) * sgb_s[rows, lanes]
        o_s[rows, B_WIDTH + p * LANES:B_WIDTH + (p + 1) * LANES] = yb.astype(jnp.bfloat16)

    @pl.when(b == n_streams - 1)
    def _():
        o = jnp.dot(o_s[...], wout_ref[...], preferred_element_type=jnp.float32)
        on = _rms_rows(o) * gpost_ref[...]
        for i in range(n_streams):
            sl = slice(i * S, (i + 1) * S)
            y_ref[sl, :] = x_ref[sl, :] + _mod_rows(mod_ref, slice(i, i + 1))[2] * on[sl, :]


def _sample_call(x, mod, g_pre, w_in, ln_g, ln_b, w_s, b_full, bias_s, w_out, g_post, ck, cv):
    n_streams, n_new, _ = x.shape
    assert n_new % BF16_ROWS == 0 and n_new <= CHUNK
    rows = n_streams * n_new
    f32, bf16 = jnp.float32, jnp.bfloat16

    def const(shape):
        nd = len(shape)
        return pl.BlockSpec(shape, lambda b: (0,) * nd, pipeline_mode=pl.Buffered(1))

    kern = functools.partial(_sample_kernel, n_streams=n_streams, n_new=n_new)
    return pl.pallas_call(
        kern,
        grid=(n_streams,),
        in_specs=[
            const((rows, D_MODEL)),
            const((n_streams, 3 * D_MODEL)),
            const((1, D_MODEL)),
            const((D_MODEL, D_IN)),
            const((1, A_WIDTH)),
            const((1, A_WIDTH)),
            const((A_GROUPS, MLP_CHUNK, MLP_CHUNK)),
            const((MLP_CHUNK, A_WIDTH)),
            const((N_PAIRS, 2, n_new, K_BLOCK)),
            const((D_MODEL, D_MODEL)),
            const((1, D_MODEL)),
            pl.BlockSpec((1, B_HEADS, B_HDIM, KV_WIN), lambda b: (b, 0, 0, 0)),
            pl.BlockSpec((1, B_HEADS, B_HDIM, KV_WIN), lambda b: (b, 0, 0, 0)),
        ],
        out_specs=[
            pl.BlockSpec((rows, D_MODEL), lambda b: (0, 0)),
            pl.BlockSpec((rows, B_HEADS, B_HDIM), lambda b: (0, 0, 0)),
            pl.BlockSpec((rows, B_HEADS, B_HDIM), lambda b: (0, 0, 0)),
            pl.BlockSpec((rows, A_WIDTH), lambda b: (0, 0)),
        ],
        out_shape=[
            jax.ShapeDtypeStruct((rows, D_MODEL), f32),
            jax.ShapeDtypeStruct((rows, B_HEADS, B_HDIM), f32),
            jax.ShapeDtypeStruct((rows, B_HEADS, B_HDIM), f32),
            jax.ShapeDtypeStruct((rows, A_WIDTH), f32),
        ],
        scratch_shapes=[
            pltpu.VMEM((rows, A_WIDTH), f32),
            pltpu.VMEM((rows, A_WIDTH), bf16),
            pltpu.VMEM((rows, B_WIDTH), bf16),
            pltpu.VMEM((rows, B_WIDTH), bf16),
            pltpu.VMEM((rows, B_WIDTH), bf16),
            pltpu.VMEM((rows, B_WIDTH), bf16),
            pltpu.VMEM((rows, B_WIDTH), f32),
            pltpu.VMEM((rows, D_MODEL), bf16),
            pltpu.VMEM((N_PAIRS, 2 * n_new, n_new), bf16),
        ],
        compiler_params=pltpu.CompilerParams(
            dimension_semantics=("arbitrary",),
            vmem_limit_bytes=VMEM_LIMIT_BYTES),
        name="sample",
    )(x.reshape(rows, D_MODEL), mod, g_pre, w_in, ln_g, ln_b, w_s, b_full, bias_s, w_out, g_post,
      ck, cv)


def _bias_base(rel_bias):
    n_far = KV_WIN - REL_CLIP
    far = jnp.broadcast_to(rel_bias[:, N_REL - 1:N_REL], (B_HEADS, n_far))
    near = rel_bias[:, N_REL - 1:0:-1]
    return jnp.concatenate([far, near], axis=1)


def kernel(x_prompt, x_sample, cache_attn_k, cache_attn_v, c_prompt, c_sample, g_pre, w_ada, b_ada,
           w_in, ln_g, ln_b, w_s, b_s, rel_bias, w_out, g_post):
    depth = g_pre.shape[0]
    bsz, seq, _ = x_prompt.shape
    n_streams, n_new, _ = x_sample.shape
    win = cache_attn_k.shape[2]
    assert win == KV_WIN

    yp, ys = x_prompt, x_sample
    kp_rows, vp_rows, ks_rows, vs_rows, va_rows = [], [], [], [], []
    for l in range(depth):
        mod_p, mod_s, bias_t, bias_s, w_in_b, w_out_b, b_full = _prep_call(
            c_prompt, c_sample, w_ada[l], b_ada[l], _bias_base(rel_bias[l]), w_in[l], w_out[l],
            b_s[l], n_new)
        pre = (g_pre[l].reshape(1, D_MODEL), w_in_b, ln_g[l].reshape(1, A_WIDTH),
               ln_b[l].reshape(1, A_WIDTH), w_s[l], b_full)
        post = (w_out_b, g_post[l].reshape(1, D_MODEL))

        yp, k_last, v_last = _prompt_call(yp, mod_p, *pre, bias_t, *post)
        kp_rows.append(jnp.transpose(k_last, _FROM_FEATURE_MAJOR))
        vp_rows.append(jnp.transpose(v_last, _FROM_FEATURE_MAJOR))

        ys2, k_new, v_new, va_new = _sample_call(ys, mod_s, *pre, bias_s, *post,
                                                 jnp.transpose(cache_attn_k[l], _TO_FEATURE_MAJOR),
                                                 jnp.transpose(cache_attn_v[l], _TO_FEATURE_MAJOR))
        ys = ys2.reshape(n_streams, n_new, D_MODEL)
        ks_rows.append(k_new.reshape(n_streams, n_new, B_HEADS, B_HDIM))
        vs_rows.append(v_new.reshape(n_streams, n_new, B_HEADS, B_HDIM))
        va_rows.append(va_new.reshape(n_streams, n_new, A_WIDTH))

    return (yp, ys, jnp.stack(kp_rows), jnp.stack(vp_rows), jnp.stack(ks_rows), jnp.stack(vs_rows),
            jnp.stack(va_rows))
```
